```python
import math
import jax, jax.numpy as jnp
from jax import lax
import numpy as np

D_MODEL = 1024
BATCH = 8
SEQ = 2048
DEPTH = 2

GRID_W = 64
CTX_LEN = 256
EPS = 1e-6
CHUNK = 128
D_A = D_MODEL
G_A = 8
D_B = D_MODEL
CONV_W = 3
D_C = D_MODEL // 2
S5_GROUP = 16
G_C = D_C // S5_GROUP
S5_STATE = 64
N_HEADS = 8
QK_NOPE = 64
QK_ROPE = 32
QK_HEAD = QK_NOPE + QK_ROPE
V_HEAD = 64
Q_LORA = 768
KV_LORA = 256
AXIS_PAIRS = QK_ROPE // 4
ROPE_BASE = 10000.0
Q_BLOCK = 128
N_GROUPS = 4
EXPERTS_PER_GROUP = 8
N_EXPERTS = N_GROUPS * EXPERTS_PER_GROUP
TOP_K = 2
D_EXPERT = 512

D_IN_AB = 2 * D_A + 3 * D_B
D_IN_CD = D_C + Q_LORA + KV_LORA + QK_ROPE
D_MIX_AB = D_A + D_B
D_MIX_CD = D_C + N_HEADS * V_HEAD

kernel_name = 'hybrid_gmlp_conv_s5_mla_hmoe_dit'


def rms(x):
    xf = x.astype(jnp.float32)
    return (xf * lax.rsqrt(jnp.mean(xf * xf, axis=-1, keepdims=True) + EPS)).astype(x.dtype)


def modulate(x, shift, scale):
    return rms(x) * (1 + scale) + shift


def chunk_spatial_gating(u, v, sgu_norm, sgu_w, sgu_b):
    b, l, _ = v.shape
    vc = (rms(v) * sgu_norm).reshape(b, l // CHUNK, CHUNK, G_A, D_A // G_A)
    s = jnp.einsum('gpq,bnqgc->bnpgc', sgu_w, vc) + sgu_b.T[None, None, :, :, None]
    return u * s.reshape(b, l, D_A)


def short_conv(z, conv_w):
    return lax.conv_general_dilated(
        z, conv_w[:, None, :].astype(z.dtype), window_strides=(1,),
        padding=[(CONV_W // 2, CONV_W // 2)], dimension_numbers=('NWC', 'WIO', 'NWC'),
        feature_group_count=z.shape[-1])


def mixer_ab(h, w_in, sgu_norm, sgu_w, sgu_b, conv_w, w_out):
    z = h @ w_in
    u, v, gate_b, gate_c, xb = jnp.split(z, [D_A, 2 * D_A, 2 * D_A + D_B, 2 * D_A + 2 * D_B], axis=-1)
    y_a = chunk_spatial_gating(jax.nn.gelu(u), jax.nn.gelu(v), sgu_norm, sgu_w, sgu_b)
    y_b = gate_b * short_conv(gate_c * xb, conv_w)
    return jnp.concatenate([y_a, y_b], axis=-1) @ w_out


def cmul(ar, ai, br, bi):
    return ar * br - ai * bi, ar * bi + ai * br


def s5_discretise(a_re, a_im, log_dt, b_re, b_im):
    dt = jnp.exp(log_dt)[:, None]
    mag = jnp.exp(dt * a_re)
    ab_re, ab_im = mag * jnp.cos(dt * a_im), mag * jnp.sin(dt * a_im)
    den = a_re * a_re + a_im * a_im
    nr = ab_re - 1.0
    f_re = (nr * a_re + ab_im * a_im) / den
    f_im = (ab_im * a_re - nr * a_im) / den
    bb_re, bb_im = cmul(f_re[..., None], f_im[..., None], b_re, b_im)
    return ab_re, ab_im, bb_re, bb_im


def s5_scan(u, ab_re, ab_im, bb_re, bb_im, h0, reverse):
    bu_re = jnp.einsum('gpc,blgc->blgp', bb_re, u)
    bu_im = jnp.einsum('gpc,blgc->blgp', bb_im, u)
    if h0 is not None:
        first = -1 if reverse else 0
        i_re, i_im = cmul(ab_re, ab_im, h0[0], h0[1])
        bu_re = bu_re.at[:, first].add(i_re)
        bu_im = bu_im.at[:, first].add(i_im)
    shape = (1, u.shape[1]) + ab_re.shape
    a_re = jnp.broadcast_to(ab_re, shape)
    a_im = jnp.broadcast_to(ab_im, shape)

    def combine(e1, e2):
        a1r, a1i, b1r, b1i = e1
        a2r, a2i, b2r, b2i = e2
        ar, ai = cmul(a2r, a2i, a1r, a1i)
        br, bi = cmul(a2r, a2i, b1r, b1i)
        return ar, ai, br + b2r, bi + b2i

    _, _, h_re, h_im = lax.associative_scan(combine, (a_re, a_im, bu_re, bu_im), reverse=reverse, axis=1)
    return h_re, h_im


def s5_readout(c_re, c_im, h_re, h_im):
    return jnp.einsum('gcp,blgp->blgc', c_re, h_re) - jnp.einsum('gcp,blgp->blgc', c_im, h_im)


def s5_mixer(u_lat, u_ctx, a_re, a_im, log_dt, b_re, b_im, c_re, c_im, d_skip, w_glu, need_ctx_out):
    dtype = u_lat.dtype
    f = lambda t: t.astype(jnp.float32)
    ul = f(u_lat).reshape(u_lat.shape[0], u_lat.shape[1], G_C, S5_GROUP)
    uc = f(u_ctx).reshape(u_ctx.shape[0], u_ctx.shape[1], G_C, S5_GROUP)
    d = f(d_skip).reshape(G_C, S5_GROUP)
    y_l = ul * d
    y_c = uc * d if need_ctx_out else None
    for direction in range(2):
        rev = direction == 1
        ab_re, ab_im, bb_re, bb_im = s5_discretise(
            f(a_re[direction]), f(a_im[direction]), f(log_dt[direction]), f(b_re[direction]), f(b_im[direction]))
        hc_re, hc_im = s5_scan(uc, ab_re, ab_im, bb_re, bb_im, None, rev)
        end = 0 if rev else -1
        hl_re, hl_im = s5_scan(ul, ab_re, ab_im, bb_re, bb_im, (hc_re[:, end], hc_im[:, end]), rev)
        cr, ci = f(c_re[direction]), f(c_im[direction])
        y_l = y_l + s5_readout(cr, ci, hl_re, hl_im)
        if need_ctx_out:
            y_c = y_c + s5_readout(cr, ci, hc_re, hc_im)

    def glu(y):
        g = jax.nn.gelu(y.reshape(y.shape[0], y.shape[1], D_C)).astype(dtype)
        return g * jax.nn.sigmoid(g @ w_glu)

    return glu(y_l), (glu(y_c) if need_ctx_out else None)


def axial_angles(l):
    rows = l // GRID_W
    row = jnp.repeat(jnp.arange(rows, dtype=jnp.float32), GRID_W)
    col = jnp.tile(jnp.arange(GRID_W, dtype=jnp.float32), rows)
    inv_freq = ROPE_BASE ** (-jnp.arange(AXIS_PAIRS, dtype=jnp.float32) / AXIS_PAIRS)
    return row[:, None] * inv_freq, col[:, None] * inv_freq


def rotate(x, ang):
    cos = jnp.cos(ang)[None, :, None, :]
    sin = jnp.sin(ang)[None, :, None, :]
    xf = x.astype(jnp.float32)
    x1, x2 = xf[..., :AXIS_PAIRS], xf[..., AXIS_PAIRS:]
    return jnp.concatenate([x1 * cos - x2 * sin, x2 * cos + x1 * sin], axis=-1).astype(x.dtype)


def axial_rope(t, ang_row, ang_col):
    nope, r_row, r_col = jnp.split(t, [QK_NOPE, QK_NOPE + 2 * AXIS_PAIRS], axis=-1)
    return jnp.concatenate([nope, rotate(r_row, ang_row), rotate(r_col, ang_col)], axis=-1)


def mla_q(cq, q_norm, w_uq, qn_gain, angles):
    b, l, _ = cq.shape
    q = ((rms(cq) * q_norm) @ w_uq).reshape(b, l, N_HEADS, QK_HEAD)
    q = rms(q) * qn_gain
    return axial_rope(q, *angles) if angles is not None else q


def mla_kv(ckv, k_rope, kv_norm, w_uk, w_uv, kn_gain, angles):
    b, l, _ = ckv.shape
    ckv_n = rms(ckv) * kv_norm
    k_nope = (ckv_n @ w_uk).reshape(b, l, N_HEADS, QK_NOPE)
    v = (ckv_n @ w_uv).reshape(b, l, N_HEADS, V_HEAD)
    k_r = jnp.broadcast_to(k_rope[:, :, None, :], (b, l, N_HEADS, QK_ROPE))
    k = rms(jnp.concatenate([k_nope, k_r], axis=-1)) * kn_gain
    k = axial_rope(k, *angles) if angles is not None else k
    return k, v


def attend(q, k, v):
    s = jnp.einsum('bqhd,bkhd->bhqk', q, k).astype(jnp.float32) * (QK_HEAD ** -0.5)
    p = jax.nn.softmax(s, axis=-1).astype(v.dtype)
    return jnp.einsum('bhqk,bkhd->bqhd', p, v)


def blockwise_attend(q, k, v):
    b, l, h, d = q.shape
    qb = q.reshape(b, l // Q_BLOCK, Q_BLOCK, h, d).transpose(1, 0, 2, 3, 4)
    ob = lax.map(lambda qi: attend(qi, k, v), qb)
    return ob.transpose(1, 0, 2, 3, 4).reshape(b, l, h, v.shape[-1])


def mixer_cd(h, hc, w_in, a_re, a_im, log_dt, b_re, b_im, c_re, c_im, d_skip, w_glu,
             q_norm, kv_norm, w_uq, w_uk, w_uv, qn_gain, kn_gain, w_out, angles, need_ctx_out):
    idx = [D_C, D_C + Q_LORA, D_C + Q_LORA + KV_LORA]
    u_l, cq_l, ckv_l, kr_l = jnp.split(h @ w_in, idx, axis=-1)
    u_c, cq_c, ckv_c, kr_c = jnp.split(hc @ w_in, idx, axis=-1)
    s5_l, s5_c = s5_mixer(u_l, u_c, a_re, a_im, log_dt, b_re, b_im, c_re, c_im, d_skip, w_glu, need_ctx_out)
    k_c, v_c = mla_kv(ckv_c, kr_c, kv_norm, w_uk, w_uv, kn_gain, None)
    k_l, v_l = mla_kv(ckv_l, kr_l, kv_norm, w_uk, w_uv, kn_gain, angles)
    q_l = mla_q(cq_l, q_norm, w_uq, qn_gain, angles)
    o_l = blockwise_attend(q_l, jnp.concatenate([k_c, k_l], axis=1), jnp.concatenate([v_c, v_l], axis=1))
    b, l = h.shape[:2]
    y_l = jnp.concatenate([s5_l, o_l.reshape(b, l, N_HEADS * V_HEAD)], axis=-1) @ w_out
    if not need_ctx_out:
        return y_l, None
    q_c = mla_q(cq_c, q_norm, w_uq, qn_gain, None)
    o_c = attend(q_c, k_c, v_c)
    y_c = jnp.concatenate([s5_c, o_c.reshape(b, hc.shape[1], N_HEADS * V_HEAD)], axis=-1) @ w_out
    return y_l, y_c


def hier_moe(x, w_grp, b_grp, w_exp, b_exp, w_gate, w_up, w_down):
    shape = x.shape
    t = x.reshape(-1, shape[-1])
    n = t.shape[0]
    p_grp = jax.nn.softmax((t @ w_grp).astype(jnp.float32) + b_grp.astype(jnp.float32), axis=-1)
    p_top, grp = lax.top_k(p_grp, 1)
    e_logits = ((t @ w_exp).astype(jnp.float32) + b_exp.astype(jnp.float32)).reshape(n, N_GROUPS, EXPERTS_PER_GROUP)
    sel = jnp.take_along_axis(e_logits, jnp.broadcast_to(grp[:, :, None], (n, 1, EXPERTS_PER_GROUP)), axis=1)[:, 0]
    top_l, top_i = lax.top_k(sel, TOP_K)
    w_top = jax.nn.softmax(top_l, axis=-1) * p_top
    eid = grp * EXPERTS_PER_GROUP + top_i
    gates = jnp.sum(jax.nn.one_hot(eid, N_EXPERTS, dtype=jnp.float32) * w_top[..., None], axis=1).astype(t.dtype)
    y = jnp.zeros_like(t)
    for e in range(N_EXPERTS):
        he = jax.nn.silu(t @ w_gate[e]) * (t @ w_up[e])
        y = y + gates[:, e:e + 1] * (he @ w_down[e])
    return y.reshape(shape)


def setup_inputs(seed: int = 0) -> dict:
    key = jax.random.key(seed)
    ks = iter(jax.random.split(key, 48))
    nrm = lambda shape, scale: scale * jax.random.normal(next(ks), shape, jnp.float32)
    ne, no = (DEPTH + 1) // 2, DEPTH // 2
    D = D_MODEL
    n_idx = jnp.arange(S5_STATE, dtype=jnp.float32)
    return {
        'x': nrm((BATCH, SEQ, D), 1.0),
        'c': nrm((BATCH, D), 1.0),
        'ctx': nrm((BATCH, CTX_LEN, D), 1.0),
        'c_ctx': nrm((D,), 1.0),
        'ada_w': nrm((DEPTH, D, 6 * D), 0.5 * D ** -0.5),
        'ada_b': nrm((DEPTH, 6 * D), 0.02),
        'ab_w_in': nrm((ne, D, D_IN_AB), D ** -0.5),
        'sgu_norm': 1.0 + nrm((ne, D_A), 0.02),
        'sgu_w': nrm((ne, G_A, CHUNK, CHUNK), CHUNK ** -0.5),
        'sgu_b': 1.0 + nrm((ne, G_A, CHUNK), 0.02),
        'conv_w': nrm((ne, CONV_W, D_B), CONV_W ** -0.5),
        'ab_w_out': nrm((ne, D_MIX_AB, D), D_MIX_AB ** -0.5),
        'cd_w_in': nrm((no, D, D_IN_CD), D ** -0.5),
        's5_a_re': -0.5 + nrm((no, 2, G_C, S5_STATE), 0.01),
        's5_a_im': math.pi * n_idx + nrm((no, 2, G_C, S5_STATE), 0.01),
        's5_log_dt': jax.random.uniform(next(ks), (no, 2, G_C), jnp.float32, math.log(1e-3), math.log(1e-1)),
        's5_b_re': nrm((no, 2, G_C, S5_STATE, S5_GROUP), (2 * S5_GROUP) ** -0.5),
        's5_b_im': nrm((no, 2, G_C, S5_STATE, S5_GROUP), (2 * S5_GROUP) ** -0.5),
        's5_c_re': nrm((no, 2, G_C, S5_GROUP, S5_STATE), (S5_STATE / 2) ** -0.5),
        's5_c_im': nrm((no, 2, G_C, S5_GROUP, S5_STATE), (S5_STATE / 2) ** -0.5),
        's5_d': nrm((no, D_C), 1.0),
        's5_w_glu': nrm((no, D_C, D_C), D_C ** -0.5),
        'mla_q_norm': 1.0 + nrm((no, Q_LORA), 0.02),
        'mla_kv_norm': 1.0 + nrm((no, KV_LORA), 0.02),
        'mla_w_uq': nrm((no, Q_LORA, N_HEADS * QK_HEAD), Q_LORA ** -0.5),
        'mla_w_uk': nrm((no, KV_LORA, N_HEADS * QK_NOPE), KV_LORA ** -0.5),
        'mla_w_uv': nrm((no, KV_LORA, N_HEADS * V_HEAD), KV_LORA ** -0.5),
        'mla_qn_gain': 1.0 + nrm((no, QK_HEAD), 0.02),
        'mla_kn_gain': 1.0 + nrm((no, QK_HEAD), 0.02),
        'cd_w_out': nrm((no, D_MIX_CD, D), D_MIX_CD ** -0.5),
        'moe_w_grp': nrm((DEPTH, D, N_GROUPS), D ** -0.5),
        'moe_b_grp': nrm((DEPTH, N_GROUPS), 0.01),
        'moe_w_exp': nrm((DEPTH, D, N_EXPERTS), D ** -0.5),
        'moe_b_exp': nrm((DEPTH, N_EXPERTS), 0.01),
        'moe_w_gate': nrm((DEPTH, N_EXPERTS, D, D_EXPERT), D ** -0.5),
        'moe_w_up': nrm((DEPTH, N_EXPERTS, D, D_EXPERT), D ** -0.5),
        'moe_w_down': nrm((DEPTH, N_EXPERTS, D_EXPERT, D), D_EXPERT ** -0.5),
    }


def reference(x, c, ctx, c_ctx, ada_w, ada_b, ab_w_in, sgu_norm, sgu_w, sgu_b, conv_w, ab_w_out,
              cd_w_in, s5_a_re, s5_a_im, s5_log_dt, s5_b_re, s5_b_im, s5_c_re, s5_c_im, s5_d, s5_w_glu,
              mla_q_norm, mla_kv_norm, mla_w_uq, mla_w_uk, mla_w_uv, mla_qn_gain, mla_kn_gain, cd_w_out,
              moe_w_grp, moe_b_grp, moe_w_exp, moe_b_exp, moe_w_gate, moe_w_up, moe_w_down):
    angles = axial_angles(x.shape[1])
    xl, xc = x, ctx
    for i in range(DEPTH):
        last = i == DEPTH - 1
        j = i // 2
        mod = (jax.nn.silu(c) @ ada_w[i] + ada_b[i])[:, None, :]
        mod_c = jax.nn.silu(c_ctx) @ ada_w[i] + ada_b[i]
        sh1, sc1, g1, sh2, sc2, g2 = jnp.split(mod, 6, axis=-1)
        csh1, csc1, cg1, csh2, csc2, cg2 = jnp.split(mod_c, 6, axis=-1)
        h = modulate(xl, sh1, sc1)
        hc = modulate(xc, csh1, csc1)
        if i % 2 == 0:
            y = mixer_ab(h, ab_w_in[j], sgu_norm[j], sgu_w[j], sgu_b[j], conv_w[j], ab_w_out[j])
            yc = None if last else mixer_ab(hc, ab_w_in[j], sgu_norm[j], sgu_w[j], sgu_b[j], conv_w[j], ab_w_out[j])
        else:
            y, yc = mixer_cd(h, hc, cd_w_in[j], s5_a_re[j], s5_a_im[j], s5_log_dt[j], s5_b_re[j], s5_b_im[j],
                             s5_c_re[j], s5_c_im[j], s5_d[j], s5_w_glu[j], mla_q_norm[j], mla_kv_norm[j],
                             mla_w_uq[j], mla_w_uk[j], mla_w_uv[j], mla_qn_gain[j], mla_kn_gain[j], cd_w_out[j],
                             angles, not last)
        xl = xl + g1 * y
        xl = xl + g2 * hier_moe(modulate(xl, sh2, sc2), moe_w_grp[i], moe_b_grp[i], moe_w_exp[i], moe_b_exp[i],
                                moe_w_gate[i], moe_w_up[i], moe_w_down[i])
        if not last:
            xc = xc + cg1 * yc
            xc = xc + cg2 * hier_moe(modulate(xc, csh2, csc2), moe_w_grp[i], moe_b_grp[i], moe_w_exp[i],
                                     moe_b_exp[i], moe_w_gate[i], moe_w_up[i], moe_w_down[i])
    return xl
```

```python
import functools
import math

import jax
import jax.numpy as jnp
from jax import lax
from jax.experimental import pallas as pl
from jax.experimental.pallas import tpu as pltpu

F32 = jnp.float32
BF16 = jnp.bfloat16
I32 = jnp.int32
HIGHEST = lax.Precision.HIGHEST

EPS = 1e-6
GRID_W = 64
CHUNK = 128
G_A = 8
CONV_W = 3
S5_GROUP = 16
S5_STATE = 64
N_HEADS = 8
QK_NOPE = 64
QK_ROPE = 32
QK_HEAD = QK_NOPE + QK_ROPE
V_HEAD = 64
AXIS_PAIRS = QK_ROPE // 4
ROPE_BASE = 10000.0
N_GROUPS = 4
EXPERTS_PER_GROUP = 8
N_EXPERTS = N_GROUPS * EXPERTS_PER_GROUP
TOP_K = 2

LANE = 128
SUBLANE = 8
HEAD_PAD = LANE
TOKEN_TILE = 512
MOE_TILE = 256
COMBINE_TILE = 256
S5_STEPS = 64
S5_SUPER = LANE // S5_GROUP
VMEM_LIMIT = 56 * 1024 * 1024


def _cparams(sem):
    return pltpu.CompilerParams(dimension_semantics=sem, vmem_limit_bytes=VMEM_LIMIT)


def _rms(x):
    return x * lax.rsqrt(jnp.mean(x * x, axis=-1, keepdims=True) + EPS)


def _sigmoid(x):
    return 1.0 / (1.0 + jnp.exp(-x))


def _dot(a, b):
    return jnp.dot(a, b, preferred_element_type=F32)


def _ada_kernel(c_ref, w_ref, b_ref, o_ref):
    c = c_ref[...]
    s = c * _sigmoid(c)
    o_ref[...] = jnp.dot(s, w_ref[...], precision=HIGHEST, preferred_element_type=F32) + b_ref[...]


def _ada(cvec, ada_w, ada_b):
    depth, d, n = ada_w.shape
    rows = cvec.shape[0]
    tn = 1024
    return pl.pallas_call(
        _ada_kernel,
        grid=(depth, n // tn),
        in_specs=[
            pl.BlockSpec((rows, d), lambda i, j: (0, 0)),
            pl.BlockSpec((None, d, tn), lambda i, j: (i, 0, j)),
            pl.BlockSpec((None, 1, tn), lambda i, j: (i, 0, j)),
        ],
        out_specs=pl.BlockSpec((None, rows, tn), lambda i, j: (i, 0, j)),
        out_shape=jax.ShapeDtypeStruct((depth, rows, n), F32),
        compiler_params=_cparams(("arbitrary", "arbitrary")),
        name="ada_mod",
    )(cvec, ada_w, ada_b.reshape(depth, 1, n))


def _route(xm, wr_ref, br_ref):
    logits = jnp.dot(xm, wr_ref[...], precision=HIGHEST, preferred_element_type=F32) + br_ref[...]
    lane = lax.broadcasted_iota(I32, logits.shape, 1).astype(F32)
    neg = jnp.float32(-1e30)
    big = jnp.float32(1e6)
    is_grp = lane < N_GROUPS
    gl = jnp.where(is_grp, logits, neg)
    gmax = jnp.max(gl, axis=-1, keepdims=True)
    gsum = jnp.sum(jnp.where(is_grp, jnp.exp(gl - gmax), 0.0), axis=-1, keepdims=True)
    p_top = 1.0 / gsum
    grp = jnp.min(jnp.where(gl == gmax, lane, big), axis=-1, keepdims=True)
    eidx = lane - N_GROUPS
    in_grp = (eidx >= grp * EXPERTS_PER_GROUP) & (eidx < (grp + 1.0) * EXPERTS_PER_GROUP)
    el = jnp.where(in_grp, logits, neg)
    m1 = jnp.max(el, axis=-1, keepdims=True)
    i1 = jnp.min(jnp.where(el == m1, lane, big), axis=-1, keepdims=True)
    el2 = jnp.where(lane == i1, neg, el)
    m2 = jnp.max(el2, axis=-1, keepdims=True)
    i2 = jnp.min(jnp.where(el2 == m2, lane, big), axis=-1, keepdims=True)
    t = jnp.exp(m2 - m1)
    w1 = p_top / (1.0 + t)
    w2 = p_top * t / (1.0 + t)
    e1 = i1 - N_GROUPS
    e2 = i2 - N_GROUPS
    return jnp.where(lane == 0, e1, jnp.where(lane == 1, e2, jnp.where(lane == 2, w1, jnp.where(lane == 3, w2, 0.0))))


def _residual_and_route(x, y, g1, sh2, sc2, wr_ref, br_ref, xl_ref, xm_ref, rt_ref):
    xl = x + g1 * y
    xl_ref[...] = xl
    xm = _rms(xl) * (1.0 + sc2) + sh2
    xm_ref[...] = xm
    rt_ref[...] = _route(xm, wr_ref, br_ref)


def _inproj_ab_kernel(x_ref, sh_ref, sc_ref, w_ref, z_ref, p_ref):
    d = x_ref.shape[-1]
    h = (_rms(x_ref[...]) * (1.0 + sc_ref[...]) + sh_ref[...]).astype(BF16)
    for j in range(3):
        z_ref[:, j * d:(j + 1) * d] = _dot(h, w_ref[:, j * d:(j + 1) * d]).astype(BF16)
    gate_c = _dot(h, w_ref[:, 3 * d:4 * d])
    xb = _dot(h, w_ref[:, 4 * d:5 * d])
    p_ref[...] = gate_c * xb


def _inproj_ab(x, sh, sc, w_bf16, tm):
    b, l, d = x.shape
    n = w_bf16.shape[1]
    return pl.pallas_call(
        _inproj_ab_kernel,
        grid=(b, l // tm),
        in_specs=[
            pl.BlockSpec((None, tm, d), lambda i, j: (i, j, 0)),
            pl.BlockSpec((None, 1, d), lambda i, j: (i, 0, 0)),
            pl.BlockSpec((None, 1, d), lambda i, j: (i, 0, 0)),
            pl.BlockSpec((d, n), lambda i, j: (0, 0)),
        ],
        out_specs=[
            pl.BlockSpec((None, tm, 3 * d), lambda i, j: (i, j, 0)),
            pl.BlockSpec((None, tm, d), lambda i, j: (i, j, 0)),
        ],
        out_shape=[
            jax.ShapeDtypeStruct((b, l, 3 * d), BF16),
            jax.ShapeDtypeStruct((b, l, d), F32),
        ],
        compiler_params=_cparams(("arbitrary", "arbitrary")),
        name="inproj_ab",
    )(x, sh, sc, w_bf16)


def _mix_ab_kernel(z_ref, p_ref, pprev_ref, pnext_ref, x_ref, g1_ref, sh2_ref, sc2_ref,
                   sgun_ref, sguw_ref, sgub_ref, convw_ref, wout_ref, wr_ref, br_ref,
                   xl_ref, xm_ref, rt_ref, pbuf, ycat):
    i = pl.program_id(1)
    nt = pl.num_programs(1)
    tm, d = x_ref.shape
    cg = d // G_A
    v = jax.nn.gelu(z_ref[:, d:2 * d].astype(F32))
    vc = (_rms(v) * sgun_ref[...]).astype(BF16)
    for c in range(tm // CHUNK):
        r0 = c * CHUNK
        cols = [_dot(sguw_ref[g], vc[r0:r0 + CHUNK, g * cg:(g + 1) * cg]) for g in range(G_A)]
        s = jnp.concatenate(cols, axis=1) + sgub_ref[...]
        u = jax.nn.gelu(z_ref[r0:r0 + CHUNK, 0:d].astype(F32))
        ycat[r0:r0 + CHUNK, 0:d] = (u * s).astype(BF16)
    pbuf[SUBLANE:SUBLANE + tm, :] = p_ref[...]
    pbuf[0:SUBLANE, :] = jnp.where(i > 0, pprev_ref[...], 0.0)
    pbuf[SUBLANE + tm:2 * SUBLANE + tm, :] = jnp.where(i < nt - 1, pnext_ref[...], 0.0)
    conv = (convw_ref[0:1, :] * pbuf[SUBLANE - 1:SUBLANE - 1 + tm, :]
            + convw_ref[1:2, :] * pbuf[SUBLANE:SUBLANE + tm, :]
            + convw_ref[2:3, :] * pbuf[SUBLANE + 1:SUBLANE + 1 + tm, :])
    ycat[:, d:2 * d] = (z_ref[:, 2 * d:3 * d].astype(F32) * conv).astype(BF16)
    y = _dot(ycat[...], wout_ref[...])
    _residual_and_route(x_ref[...], y, g1_ref[...], sh2_ref[...], sc2_ref[...], wr_ref, br_ref,
                        xl_ref, xm_ref, rt_ref)


def _mix_ab(z, p, x, g1, sh2, sc2, sgun, sguw, sgub, convw, wout, wr, br, tm):
    b, l, d = x.shape
    hb = tm // SUBLANE
    nhb = l // SUBLANE
    tok = lambda i, j: (i, j, 0)
    per_b = lambda i, j: (i, 0, 0)
    const2 = lambda i, j: (0, 0)
    return pl.pallas_call(
        _mix_ab_kernel,
        grid=(b, l // tm),
        in_specs=[
            pl.BlockSpec((None, tm, 3 * d), tok),
            pl.BlockSpec((None, tm, d), tok),
            pl.BlockSpec((None, SUBLANE, d), lambda i, j: (i, jnp.maximum(j * hb - 1, 0), 0)),
            pl.BlockSpec((None, SUBLANE, d), lambda i, j: (i, jnp.minimum((j + 1) * hb, nhb - 1), 0)),
            pl.BlockSpec((None, tm, d), tok),
            pl.BlockSpec((None, 1, d), per_b),
            pl.BlockSpec((None, 1, d), per_b),
            pl.BlockSpec((None, 1, d), per_b),
            pl.BlockSpec((1, d), const2),
            pl.BlockSpec((G_A, CHUNK, CHUNK), lambda i, j: (0, 0, 0)),
            pl.BlockSpec((CHUNK, d), const2),
            pl.BlockSpec((CONV_W, d), const2),
            pl.BlockSpec((2 * d, d), const2),
            pl.BlockSpec((d, LANE), const2),
            pl.BlockSpec((1, LANE), const2),
        ],
        out_specs=[
            pl.BlockSpec((None, tm, d), tok),
            pl.BlockSpec((None, tm, d), tok),
            pl.BlockSpec((None, tm, LANE), tok),
        ],
        out_shape=[
            jax.ShapeDtypeStruct((b, l, d), F32),
            jax.ShapeDtypeStruct((b, l, d), F32),
            jax.ShapeDtypeStruct((b, l, LANE), F32),
        ],
        scratch_shapes=[
            pltpu.VMEM((tm + 2 * SUBLANE, d), F32),
            pltpu.VMEM((tm, 2 * d), BF16),
        ],
        compiler_params=_cparams(("arbitrary", "arbitrary")),
        name="mix_ab",
    )(z, p, p, p, x, g1, sh2, sc2, sgun, sguw, sgub, convw, wout, wr, br)


def _moe_meta(e_flat, tm, n_tiles):
    a = e_flat.shape[0]
    onehot = (e_flat[:, None] == jnp.arange(N_EXPERTS, dtype=I32)[None, :]).astype(I32)
    csum = jnp.cumsum(onehot, axis=0)
    counts = csum[-1]
    rank = jnp.take_along_axis(csum, e_flat[:, None], axis=1)[:, 0] - 1
    padded = ((counts + tm - 1) // tm) * tm
    starts = jnp.cumsum(padded) - padded
    pos = starts[e_flat] + rank
    tile_end = jnp.cumsum(padded // tm)
    t = jnp.arange(n_tiles, dtype=I32)
    tile_e = jnp.sum((t[:, None] >= tile_end[None, :]).astype(I32), axis=1)
    valid = (t < tile_end[-1]).astype(I32)
    tile_e = jnp.minimum(tile_e, N_EXPERTS - 1)
    return pos, tile_e, valid


def _moe_kernel(te_ref, tv_ref, src_ref, wrow_ref, xm_hbm, wg_ref, wu_ref, wd_ref, y_ref,
                xbuf, wg_s, wu_s, wd_s, sem):
    t = pl.program_id(0)
    tm = xbuf.shape[0]

    @pl.when(tv_ref[t] == 1)
    def _():
        def issue(r, carry):
            tok = src_ref[0, r]
            pltpu.make_async_copy(xm_hbm.at[pl.ds(tok, 1)], xbuf.at[pl.ds(r, 1)], sem).start()
            return carry
        lax.fori_loop(0, tm, issue, 0)

        changed = jnp.logical_or(t == 0, te_ref[t] != te_ref[jnp.maximum(t - 1, 0)])

        @pl.when(changed)
        def _():
            wg_s[...] = wg_ref[...].astype(BF16)
            wu_s[...] = wu_ref[...].astype(BF16)
            wd_s[...] = wd_ref[...].astype(BF16)

        pltpu.make_async_copy(xm_hbm.at[pl.ds(0, tm)], xbuf, sem).wait()
        x = xbuf[...].astype(BF16)
        g = _dot(x, wg_s[...])
        u = _dot(x, wu_s[...])
        h = (g * _sigmoid(g) * u).astype(BF16)
        y_ref[...] = _dot(h, wd_s[...]) * wrow_ref[...]

    @pl.when(tv_ref[t] != 1)
    def _():
        y_ref[...] = jnp.zeros_like(y_ref)


def _moe_experts(xm, src, wrow, tile_e, valid, w_gate, w_up, w_down, tm):
    n_tiles = tile_e.shape[0]
    d = xm.shape[-1]
    de = w_gate.shape[-1]
    grid_spec = pltpu.PrefetchScalarGridSpec(
        num_scalar_prefetch=2,
        grid=(n_tiles,),
        in_specs=[
            pl.BlockSpec((None, 1, tm), lambda t, te, tv: (t, 0, 0), memory_space=pltpu.SMEM),
            pl.BlockSpec((tm, 1), lambda t, te, tv: (t, 0)),
            pl.BlockSpec(memory_space=pl.ANY),
            pl.BlockSpec((None, d, de), lambda t, te, tv: (te[t], 0, 0)),
            pl.BlockSpec((None, d, de), lambda t, te, tv: (te[t], 0, 0)),
            pl.BlockSpec((None, de, d), lambda t, te, tv: (te[t], 0, 0)),
        ],
        out_specs=pl.BlockSpec((tm, d), lambda t, te, tv: (t, 0)),
        scratch_shapes=[
            pltpu.VMEM((tm, d), F32),
            pltpu.VMEM((d, de), BF16),
            pltpu.VMEM((d, de), BF16),
            pltpu.VMEM((de, d), BF16),
            pltpu.SemaphoreType.DMA(()),
        ],
    )
    return pl.pallas_call(
        _moe_kernel,
        grid_spec=grid_spec,
        out_shape=jax.ShapeDtypeStruct((n_tiles * tm, d), F32),
        compiler_params=_cparams(("arbitrary",)),
        name="moe_experts",
    )(tile_e, valid, src.reshape(n_tiles, 1, tm), wrow, xm, w_gate, w_up, w_down)


def _combine_kernel(p0_ref, p1_ref, x_ref, g_ref, ys_hbm, o_ref, buf0, buf1, sem):
    tt = buf0.shape[0]

    def issue(r, carry):
        pltpu.make_async_copy(ys_hbm.at[pl.ds(p0_ref[0, r], 1)], buf0.at[pl.ds(r, 1)], sem.at[0]).start()
        pltpu.make_async_copy(ys_hbm.at[pl.ds(p1_ref[0, r], 1)], buf1.at[pl.ds(r, 1)], sem.at[1]).start()
        return carry
    lax.fori_loop(0, tt, issue, 0)
    pltpu.make_async_copy(ys_hbm.at[pl.ds(0, tt)], buf0, sem.at[0]).wait()
    pltpu.make_async_copy(ys_hbm.at[pl.ds(0, tt)], buf1, sem.at[1]).wait()
    o_ref[...] = x_ref[...] + g_ref[...] * (buf0[...] + buf1[...])


def _moe_combine(x, g2, pos0, pos1, ys, tt):
    b, l, d = x.shape
    nt = l // tt
    idx = lambda i, j: (i * nt + j, 0, 0)
    return pl.pallas_call(
        _combine_kernel,
        grid=(b, nt),
        in_specs=[
            pl.BlockSpec((None, 1, tt), idx, memory_space=pltpu.SMEM),
            pl.BlockSpec((None, 1, tt), idx, memory_space=pltpu.SMEM),
            pl.BlockSpec((None, tt, d), lambda i, j: (i, j, 0)),
            pl.BlockSpec((None, 1, d), lambda i, j: (i, 0, 0)),
            pl.BlockSpec(memory_space=pl.ANY),
        ],
        out_specs=pl.BlockSpec((None, tt, d), lambda i, j: (i, j, 0)),
        out_shape=jax.ShapeDtypeStruct((b, l, d), F32),
        scratch_shapes=[
            pltpu.VMEM((tt, d), F32),
            pltpu.VMEM((tt, d), F32),
            pltpu.SemaphoreType.DMA((2,)),
        ],
        compiler_params=_cparams(("arbitrary", "arbitrary")),
        name="moe_combine",
    )(pos0.reshape(b * nt, 1, tt), pos1.reshape(b * nt, 1, tt), x, g2, ys)


def _hier_moe(streams, w_gate, w_up, w_down):
    d = streams[0][0].shape[-1]
    sizes = [s[0].shape[0] * s[0].shape[1] for s in streams]
    n = sum(sizes)
    xm = jnp.concatenate([s[1].reshape(-1, d) for s in streams], axis=0)
    rt = jnp.concatenate([s[2].reshape(-1, LANE) for s in streams], axis=0)
    e_flat = jnp.concatenate([rt[:, 0], rt[:, 1]]).astype(I32)
    w_flat = jnp.concatenate([rt[:, 2], rt[:, 3]])
    tok = jnp.concatenate([jnp.arange(n, dtype=I32)] * TOP_K)
    tm = MOE_TILE
    n_tiles = (TOP_K * n) // tm + N_EXPERTS
    pos, tile_e, valid = _moe_meta(e_flat, tm, n_tiles)
    src = jnp.zeros((n_tiles * tm,), I32).at[pos].set(tok, unique_indices=True)
    wrow = jnp.zeros((n_tiles * tm,), F32).at[pos].set(w_flat, unique_indices=True)
    ys = _moe_experts(xm, src, wrow.reshape(-1, 1), tile_e, valid, w_gate, w_up, w_down, tm)
    outs = []
    off = 0
    for (xl, _, _, g2), sz in zip(streams, sizes):
        p0 = pos[off:off + sz]
        p1 = pos[n + off:n + off + sz]
        outs.append(_moe_combine(xl, g2, p0, p1, ys, min(COMBINE_TILE, xl.shape[1])))
        off += sz
    return outs


def _inproj_cd_kernel(x_ref, sh_ref, sc_ref, w_ref, u_ref, cq_ref, ckv_ref, kr_ref):
    h = (_rms(x_ref[...]) * (1.0 + sc_ref[...]) + sh_ref[...]).astype(BF16)
    off = 0
    for ref in (u_ref, cq_ref, ckv_ref, kr_ref):
        n = ref.shape[-1]
        ref[...] = _dot(h, w_ref[:, off:off + n]).astype(ref.dtype)
        off += n


def _inproj_cd(x, sh, sc, w_bf16, widths, tm):
    b, l, d = x.shape
    n = w_bf16.shape[1]
    tok = lambda i, j: (i, j, 0)
    return pl.pallas_call(
        _inproj_cd_kernel,
        grid=(b, l // tm),
        in_specs=[
            pl.BlockSpec((None, tm, d), tok),
            pl.BlockSpec((None, 1, d), lambda i, j: (i, 0, 0)),
            pl.BlockSpec((None, 1, d), lambda i, j: (i, 0, 0)),
            pl.BlockSpec((d, n), lambda i, j: (0, 0)),
        ],
        out_specs=[pl.BlockSpec((None, tm, w), tok) for w in widths],
        out_shape=[jax.ShapeDtypeStruct((b, l, w), F32) for w in widths],
        compiler_params=_cparams(("arbitrary", "arbitrary")),
        name="inproj_cd",
    )(x, sh, sc, w_bf16)


def _s5_param_kernel(are_ref, aim_ref, ldt_ref, bre_ref, bim_ref, abre_ref, abim_ref, bbre_ref, bbim_ref):
    a_re = are_ref[...]
    a_im = aim_ref[...]
    dt = jnp.exp(ldt_ref[...])
    mag = jnp.exp(dt * a_re)
    ab_re = mag * jnp.cos(dt * a_im)
    ab_im = mag * jnp.sin(dt * a_im)
    den = a_re * a_re + a_im * a_im
    nr = ab_re - 1.0
    f_re = (nr * a_re + ab_im * a_im) / den
    f_im = (ab_im * a_re - nr * a_im) / den
    abre_ref[...] = ab_re
    abim_ref[...] = ab_im
    b_re = bre_ref[...]
    b_im = bim_ref[...]
    bbre_ref[...] = f_re[None] * b_re - f_im[None] * b_im
    bbim_ref[...] = f_re[None] * b_im + f_im[None] * b_re


def _s5_params(a_re, a_im, log_dt, b_re, b_im):
    nd, g, p = a_re.shape
    c = b_re.shape[-1]
    rows = nd * g * p // LANE
    flat = lambda t: t.reshape(rows, LANE)
    chan_major = lambda t: jnp.moveaxis(t, -1, 0).reshape(c, rows, LANE)
    ldt = jnp.broadcast_to(log_dt[:, :, None], (nd, g, p))
    shapes = [jax.ShapeDtypeStruct((rows, LANE), F32)] * 2 + [jax.ShapeDtypeStruct((c, rows, LANE), F32)] * 2
    ab_re, ab_im, bb_re, bb_im = pl.pallas_call(
        _s5_param_kernel, out_shape=shapes, name="s5_params",
    )(flat(a_re), flat(a_im), flat(ldt), chan_major(b_re), chan_major(b_im))
    unflat = lambda t: jnp.moveaxis(t.reshape(c, nd, g, p), 0, -1)
    return ab_re.reshape(nd, g, p), ab_im.reshape(nd, g, p), unflat(bb_re), unflat(bb_im)


def _s5_kernel(uf_ref, ub_ref, a_ref, bsb_ref, csb_ref, yf_ref, yb_ref, hf, hb, cf, cb):
    nb = cf.shape[0]
    steps = hf.shape[0] // nb
    n_super = bsb_ref.shape[1]
    cw = bsb_ref.shape[2]
    sw = bsb_ref.shape[3]
    half = sw // 2

    @pl.when(pl.program_id(0) == 0)
    def _():
        cf[...] = jnp.zeros_like(cf)
        cb[...] = jnp.zeros_like(cb)

    for s in range(n_super):
        hf[:, s * sw:(s + 1) * sw] = _dot(uf_ref[:, s * cw:(s + 1) * cw].astype(BF16), bsb_ref[0, s])
        hb[:, s * sw:(s + 1) * sw] = _dot(ub_ref[:, s * cw:(s + 1) * cw].astype(BF16), bsb_ref[1, s])

    for s in range(n_super):
        re = slice(s * sw, s * sw + half)
        im = slice(s * sw + half, (s + 1) * sw)
        st = slice(s * half, (s + 1) * half)
        arf, aif = a_ref[0, 0, :, st], a_ref[0, 1, :, st]
        arb, aib = a_ref[1, 0, :, st], a_ref[1, 1, :, st]

        def step(t, carry):
            hfr, hfi, hbr, hbi = carry
            rf = pl.multiple_of(t * nb, nb)
            nfr = arf * hfr - aif * hfi + hf[pl.ds(rf, nb), re]
            nfi = arf * hfi + aif * hfr + hf[pl.ds(rf, nb), im]
            hf[pl.ds(rf, nb), re] = nfr
            hf[pl.ds(rf, nb), im] = nfi
            rb = pl.multiple_of((steps - 1 - t) * nb, nb)
            nbr = arb * hbr - aib * hbi + hb[pl.ds(rb, nb), re]
            nbi = arb * hbi + aib * hbr + hb[pl.ds(rb, nb), im]
            hb[pl.ds(rb, nb), re] = nbr
            hb[pl.ds(rb, nb), im] = nbi
            return nfr, nfi, nbr, nbi

        out = lax.fori_loop(0, steps, step, (cf[:, re], cf[:, im], cb[:, re], cb[:, im]), unroll=4)
        cf[:, re], cf[:, im], cb[:, re], cb[:, im] = out

    for s in range(n_super):
        yf_ref[:, s * cw:(s + 1) * cw] = _dot(hf[:, s * sw:(s + 1) * sw].astype(BF16), csb_ref[0, s])
        yb_ref[:, s * cw:(s + 1) * cw] = _dot(hb[:, s * sw:(s + 1) * sw].astype(BF16), csb_ref[1, s])


def _s5_scan(u_tm, n_ctx_blocks, a_bc, bsb, csb, nb):
    rows, c = u_tm.shape
    blk = S5_STEPS * nb
    n_blocks = rows // blk
    n_state = bsb.shape[1] * bsb.shape[3]
    fwd = lambda i: (i, 0)
    bwd = lambda i: (jnp.where(i < n_ctx_blocks, n_ctx_blocks - 1 - i, n_blocks - 1 - (i - n_ctx_blocks)), 0)
    return pl.pallas_call(
        _s5_kernel,
        grid=(n_blocks,),
        in_specs=[
            pl.BlockSpec((blk, c), fwd),
            pl.BlockSpec((blk, c), bwd),
            pl.BlockSpec(a_bc.shape, lambda i: (0, 0, 0, 0)),
            pl.BlockSpec(bsb.shape, lambda i: (0, 0, 0, 0)),
            pl.BlockSpec(csb.shape, lambda i: (0, 0, 0, 0)),
        ],
        out_specs=[pl.BlockSpec((blk, c), fwd), pl.BlockSpec((blk, c), bwd)],
        out_shape=[jax.ShapeDtypeStruct((rows, c), F32)] * 2,
        scratch_shapes=[
            pltpu.VMEM((blk, n_state), F32),
            pltpu.VMEM((blk, n_state), F32),
            pltpu.VMEM((nb, n_state), F32),
            pltpu.VMEM((nb, n_state), F32),
        ],
        compiler_params=_cparams(("arbitrary",)),
        name="s5_scan",
    )(u_tm, u_tm, a_bc, bsb, csb)


def _s5_block_matrices(ab_re, ab_im, bb_re, bb_im, c_re, c_im, nb):
    nd, g, p = ab_re.shape
    c = bb_re.shape[-1]
    ns = g // S5_SUPER
    eye = jnp.eye(S5_SUPER, dtype=F32)

    def in_mat(bb):
        t = bb.reshape(nd, ns, S5_SUPER, p, c)
        return jnp.einsum('dsgpc,gh->dsgchp', t, eye).reshape(nd, ns, S5_SUPER * c, S5_SUPER * p)

    def out_mat(cc):
        t = cc.reshape(nd, ns, S5_SUPER, c, p)
        return jnp.einsum('dsgcp,gh->dsgphc', t, eye).reshape(nd, ns, S5_SUPER * p, S5_SUPER * c)

    bsb = jnp.concatenate([in_mat(bb_re), in_mat(bb_im)], axis=-1).astype(BF16)
    csb = jnp.concatenate([out_mat(c_re), out_mat(-c_im)], axis=-2).astype(BF16)
    a_bc = jnp.stack([ab_re.reshape(nd, g * p), ab_im.reshape(nd, g * p)], axis=1)
    a_bc = jnp.broadcast_to(a_bc[:, :, None, :], (nd, 2, nb, g * p))
    return a_bc, bsb, csb


def _rope(x, cos, sin_a, sin_b):
    return (x * cos + pltpu.roll(x, HEAD_PAD - AXIS_PAIRS, axis=1) * sin_a
            + pltpu.roll(x, AXIS_PAIRS, axis=1) * sin_b)


def _head_norm(x, gain):
    return x * lax.rsqrt(jnp.sum(x * x, axis=-1, keepdims=True) * (1.0 / QK_HEAD) + EPS) * gain


def _mla_prep_kernel(*refs, with_q, with_rope):
    if with_q:
        cq_ref, qn_ref, wq_ref, qg_ref = refs[:4]
        refs = refs[4:]
    ckv_ref, krp_ref, kvn_ref, wk_ref, wv_ref, kg_ref = refs[:6]
    refs = refs[6:]
    if with_rope:
        cos_ref, sa_ref, sb_ref = refs[:3]
        refs = refs[3:]
        cos, sa, sb = cos_ref[...], sa_ref[...], sb_ref[...]
    if with_q:
        q_ref, k_ref, v_ref = refs
    else:
        k_ref, v_ref = refs
    ckvn = (_rms(ckv_ref[...]) * kvn_ref[...]).astype(BF16)
    v_ref[...] = _dot(ckvn, wv_ref[...]).astype(v_ref.dtype)
    kn = _dot(ckvn, wk_ref[...])
    krp = krp_ref[...]
    for h in range(N_HEADS):
        sl = slice(h * HEAD_PAD, (h + 1) * HEAD_PAD)
        kh = _head_norm(kn[:, sl] + krp, kg_ref[...])
        if with_rope:
            kh = _rope(kh, cos, sa, sb)
        k_ref[:, sl] = kh.astype(k_ref.dtype)
    if with_q:
        cqn = (_rms(cq_ref[...]) * qn_ref[...]).astype(BF16)
        qn = _dot(cqn, wq_ref[...])
        for h in range(N_HEADS):
            sl = slice(h * HEAD_PAD, (h + 1) * HEAD_PAD)
            qh = _head_norm(qn[:, sl], qg_ref[...])
            if with_rope:
                qh = _rope(qh, cos, sa, sb)
            q_ref[:, sl] = (qh * (QK_HEAD ** -0.5)).astype(q_ref.dtype)


def _mla_prep(cq, ckv, krp, q_norm, wq, q_gain, kv_norm, wk, wv, k_gain, tables, tm):
    b, l, _ = ckv.shape
    with_q = cq is not None
    with_rope = tables is not None
    tok = lambda i, j: (i, j, 0)
    const = lambda i, j: (0, 0)
    args, specs = [], []

    def add(arr, spec):
        args.append(arr)
        specs.append(spec)

    if with_q:
        add(cq, pl.BlockSpec((None, tm, cq.shape[-1]), tok))
        add(q_norm, pl.BlockSpec(q_norm.shape, const))
        add(wq, pl.BlockSpec(wq.shape, const))
        add(q_gain, pl.BlockSpec(q_gain.shape, const))
    add(ckv, pl.BlockSpec((None, tm, ckv.shape[-1]), tok))
    add(krp, pl.BlockSpec((None, tm, HEAD_PAD), tok))
    add(kv_norm, pl.BlockSpec(kv_norm.shape, const))
    add(wk, pl.BlockSpec(wk.shape, const))
    add(wv, pl.BlockSpec(wv.shape, const))
    add(k_gain, pl.BlockSpec(k_gain.shape, const))
    if with_rope:
        for tbl in tables:
            add(tbl, pl.BlockSpec((tm, HEAD_PAD), lambda i, j: (j, 0)))
    hk = N_HEADS * HEAD_PAD
    hv = wv.shape[1]
    out_specs = [pl.BlockSpec((None, tm, hk), tok), pl.BlockSpec((None, tm, hv), tok)]
    out_shape = [jax.ShapeDtypeStruct((b, l, hk), BF16), jax.ShapeDtypeStruct((b, l, hv), BF16)]
    if with_q:
        out_specs = [pl.BlockSpec((None, tm, hk), tok)] + out_specs
        out_shape = [jax.ShapeDtypeStruct((b, l, hk), BF16)] + out_shape
    return pl.pallas_call(
        functools.partial(_mla_prep_kernel, with_q=with_q, with_rope=with_rope),
        grid=(b, l // tm),
        in_specs=specs,
        out_specs=out_specs,
        out_shape=out_shape,
        compiler_params=_cparams(("arbitrary", "arbitrary")),
        name="mla_prep_q" if with_q else "mla_prep_ctx",
    )(*args)


def _attn_kernel(q_ref, kl_ref, kc_ref, vl_ref, vc_ref, o_ref):
    nt = (((1,), (1,)), ((), ()))
    outs = []
    for hh in range(2):
        sl = slice(hh * HEAD_PAD, (hh + 1) * HEAD_PAD)
        q = q_ref[:, sl]
        s_l = lax.dot_general(q, kl_ref[:, sl], nt, preferred_element_type=F32)
        s_c = lax.dot_general(q, kc_ref[:, sl], nt, preferred_element_type=F32)
        m = jnp.maximum(jnp.max(s_l, axis=-1, keepdims=True), jnp.max(s_c, axis=-1, keepdims=True))
        p_l = jnp.exp(s_l - m)
        p_c = jnp.exp(s_c - m)
        den = jnp.sum(p_l, axis=-1, keepdims=True) + jnp.sum(p_c, axis=-1, keepdims=True)
        o = _dot(p_l.astype(BF16), vl_ref[...]) + _dot(p_c.astype(BF16), vc_ref[...])
        outs.append(o / den)
    lane = lax.broadcasted_iota(I32, outs[0].shape, 1)
    o_ref[...] = jnp.where(lane < V_HEAD, outs[0], outs[1]).astype(o_ref.dtype)


def _attention(q, k_l, k_c, v_l, v_c, tq):
    b, l, _ = q.shape
    lc = k_c.shape[1]
    hp = N_HEADS // 2
    kw = 2 * HEAD_PAD
    vw = 2 * V_HEAD
    return pl.pallas_call(
        _attn_kernel,
        grid=(b, hp, l // tq),
        in_specs=[
            pl.BlockSpec((None, tq, kw), lambda i, h, j: (i, j, h)),
            pl.BlockSpec((None, l, kw), lambda i, h, j: (i, 0, h)),
            pl.BlockSpec((None, lc, kw), lambda i, h, j: (i, 0, h)),
            pl.BlockSpec((None, l, vw), lambda i, h, j: (i, 0, h)),
            pl.BlockSpec((None, lc, vw), lambda i, h, j: (i, 0, h)),
        ],
        out_specs=pl.BlockSpec((None, tq, vw), lambda i, h, j: (i, j, h)),
        out_shape=jax.ShapeDtypeStruct((b, l, N_HEADS * V_HEAD), BF16),
        compiler_params=_cparams(("arbitrary", "arbitrary", "arbitrary")),
        name="mla_attention",
    )(q, k_l, k_c, v_l, v_c)


def _mix_cd_kernel(yf_ref, yb_ref, u_ref, o_ref, x_ref, g1_ref, sh2_ref, sc2_ref,
                   dskip_ref, wglu_ref, wout_ref, wr_ref, br_ref, xl_ref, xm_ref, rt_ref):
    dc = u_ref.shape[-1]
    y = yf_ref[...] + yb_ref[...] + u_ref[...] * dskip_ref[...]
    g = jax.nn.gelu(y)
    s5 = (g * _sigmoid(_dot(g.astype(BF16), wglu_ref[...]))).astype(BF16)
    y_mix = _dot(s5, wout_ref[0:dc, :]) + _dot(o_ref[...], wout_ref[dc:, :])
    _residual_and_route(x_ref[...], y_mix, g1_ref[...], sh2_ref[...], sc2_ref[...], wr_ref, br_ref,
                        xl_ref, xm_ref, rt_ref)


def _mix_cd(yf, yb, u, o, x, g1, sh2, sc2, dskip, wglu, wout, wr, br, tm):
    b, l, d = x.shape
    dc = u.shape[-1]
    tok = lambda i, j: (i, j, 0)
    per_b = lambda i, j: (i, 0, 0)
    const2 = lambda i, j: (0, 0)
    return pl.pallas_call(
        _mix_cd_kernel,
        grid=(b, l // tm),
        in_specs=[
            pl.BlockSpec((None, tm, dc), tok),
            pl.BlockSpec((None, tm, dc), tok),
            pl.BlockSpec((None, tm, dc), tok),
            pl.BlockSpec((None, tm, o.shape[-1]), tok),
            pl.BlockSpec((None, tm, d), tok),
            pl.BlockSpec((None, 1, d), per_b),
            pl.BlockSpec((None, 1, d), per_b),
            pl.BlockSpec((None, 1, d), per_b),
            pl.BlockSpec((1, dc), const2),
            pl.BlockSpec(wglu.shape, const2),
            pl.BlockSpec(wout.shape, const2),
            pl.BlockSpec((d, LANE), const2),
            pl.BlockSpec((1, LANE), const2),
        ],
        out_specs=[
            pl.BlockSpec((None, tm, d), tok),
            pl.BlockSpec((None, tm, d), tok),
            pl.BlockSpec((None, tm, LANE), tok),
        ],
        out_shape=[
            jax.ShapeDtypeStruct((b, l, d), F32),
            jax.ShapeDtypeStruct((b, l, d), F32),
            jax.ShapeDtypeStruct((b, l, LANE), F32),
        ],
        compiler_params=_cparams(("arbitrary", "arbitrary")),
        name="mix_cd",
    )(yf, yb, u, o, x, g1, sh2, sc2, dskip, wglu, wout, wr, br)


def _router_weights(w_grp, b_grp, w_exp, b_exp):
    d = w_grp.shape[0]
    pad = LANE - N_GROUPS - N_EXPERTS
    wr = jnp.concatenate([w_grp, w_exp, jnp.zeros((d, pad), F32)], axis=1)
    br = jnp.concatenate([b_grp, b_exp, jnp.zeros((pad,), F32)]).reshape(1, LANE)
    return wr, br


def _pad_heads(w, width):
    k = w.shape[0]
    w = w.reshape(k, N_HEADS, width)
    return jnp.pad(w, ((0, 0), (0, 0), (0, HEAD_PAD - width))).reshape(k, N_HEADS * HEAD_PAD)


def _rope_tables(l):
    rows = l // GRID_W
    row = jnp.repeat(jnp.arange(rows, dtype=F32), GRID_W)
    col = jnp.tile(jnp.arange(GRID_W, dtype=F32), rows)
    inv_freq = ROPE_BASE ** (-jnp.arange(AXIS_PAIRS, dtype=F32) / AXIS_PAIRS)
    ar, ac = row[:, None] * inv_freq, col[:, None] * inv_freq
    one = jnp.ones((l, QK_NOPE), F32)
    zn = jnp.zeros((l, QK_NOPE), F32)
    zp = jnp.zeros((l, HEAD_PAD - QK_HEAD), F32)
    z8 = jnp.zeros((l, AXIS_PAIRS), F32)
    cos = jnp.concatenate([one, jnp.cos(ar), jnp.cos(ar), jnp.cos(ac), jnp.cos(ac), zp], axis=1)
    sin_a = jnp.concatenate([zn, -jnp.sin(ar), z8, -jnp.sin(ac), z8, zp], axis=1)
    sin_b = jnp.concatenate([zn, z8, jnp.sin(ar), z8, jnp.sin(ac), zp], axis=1)
    return cos, sin_a, sin_b


def _split_mod(mod_row, b, d):
    parts = jnp.split(mod_row, 6, axis=-1)
    return [jnp.broadcast_to(p.reshape(-1, 1, d), (b, 1, d)) for p in parts]


def kernel(x, c, ctx, c_ctx, ada_w, ada_b, ab_w_in, sgu_norm, sgu_w, sgu_b, conv_w, ab_w_out, cd_w_in, s5_a_re, s5_a_im, s5_log_dt, s5_b_re, s5_b_im, s5_c_re, s5_c_im, s5_d, s5_w_glu, mla_q_norm, mla_kv_norm, mla_w_uq, mla_w_uk, mla_w_uv, mla_qn_gain, mla_kn_gain, cd_w_out, moe_w_grp, moe_b_grp, moe_w_exp, moe_b_exp, moe_w_gate, moe_w_up, moe_w_down):
    b, l, d = x.shape
    lc = ctx.shape[1]
    depth = ada_w.shape[0]
    assert depth == 2, "layer 0 = gated-MLP/conv mixers, layer 1 = S5/attention mixers"
    tm = min(TOKEN_TILE, l)
    tmc = min(TOKEN_TILE, lc)

    mod_rows = 2 * SUBLANE
    cvec = jnp.zeros((mod_rows, d), F32).at[:b].set(c).at[b].set(c_ctx)
    mod = _ada(cvec, ada_w, ada_b)

    sh1, sc1, g1, sh2, sc2, g2 = _split_mod(mod[0, :b], b, d)
    csh1, csc1, cg1, csh2, csc2, cg2 = _split_mod(mod[0, b:b + 1], b, d)
    w_in = ab_w_in[0].astype(BF16)
    wr, br = _router_weights(moe_w_grp[0], moe_b_grp[0], moe_w_exp[0], moe_b_exp[0])
    sgub = jnp.repeat(sgu_b[0].T, d // G_A, axis=1)
    mix_args = (sgu_norm[0].reshape(1, d), sgu_w[0].astype(BF16), sgub, conv_w[0],
                ab_w_out[0].astype(BF16), wr, br)
    z, p = _inproj_ab(x, sh1, sc1, w_in, tm)
    xl, xm, rt = _mix_ab(z, p, x, g1, sh2, sc2, *mix_args, tm)
    zc, pc = _inproj_ab(ctx, csh1, csc1, w_in, tmc)
    xc, xmc, rtc = _mix_ab(zc, pc, ctx, cg1, csh2, csc2, *mix_args, tmc)
    xl, xc = _hier_moe([(xl, xm, rt, g2), (xc, xmc, rtc, cg2)], moe_w_gate[0], moe_w_up[0], moe_w_down[0])

    sh1, sc1, g1, sh2, sc2, g2 = _split_mod(mod[1, :b], b, d)
    csh1, csc1, _, _, _, _ = _split_mod(mod[1, b:b + 1], b, d)
    d_c = s5_d.shape[-1]
    q_lora = mla_q_norm.shape[-1]
    kv_lora = mla_kv_norm.shape[-1]
    w_cd = cd_w_in[0]
    o_kr = d_c + q_lora + kv_lora
    w_kr = jnp.pad(w_cd[:, o_kr:], ((0, 0), (QK_NOPE, HEAD_PAD - QK_HEAD)))
    w_cd = jnp.concatenate([w_cd[:, :o_kr], w_kr], axis=1).astype(BF16)
    widths = (d_c, q_lora, kv_lora, HEAD_PAD)
    u_l, cq_l, ckv_l, krp_l = _inproj_cd(xl, sh1, sc1, w_cd, widths, tm)
    u_c, _, ckv_c, krp_c = _inproj_cd(xc, csh1, csc1, w_cd, widths, tmc)

    ab_re, ab_im, bb_re, bb_im = _s5_params(s5_a_re[0], s5_a_im[0], s5_log_dt[0], s5_b_re[0], s5_b_im[0])
    a_bc, bsb, csb = _s5_block_matrices(ab_re, ab_im, bb_re, bb_im, s5_c_re[0], s5_c_im[0], b)
    u_tm = jnp.concatenate([u_c.transpose(1, 0, 2), u_l.transpose(1, 0, 2)], axis=0).reshape((lc + l) * b, d_c)
    yf, yb = _s5_scan(u_tm, lc // S5_STEPS, a_bc, bsb, csb, b)
    to_bm = lambda y: y.reshape(lc + l, b, d_c)[lc:].transpose(1, 0, 2)
    yf, yb = to_bm(yf), to_bm(yb)

    wq = _pad_heads(mla_w_uq[0], QK_HEAD).astype(BF16)
    wk = _pad_heads(mla_w_uk[0], QK_NOPE).astype(BF16)
    wv = mla_w_uv[0].astype(BF16)
    qg = jnp.pad(mla_qn_gain[0], (0, HEAD_PAD - QK_HEAD)).reshape(1, HEAD_PAD)
    kg = jnp.pad(mla_kn_gain[0], (0, HEAD_PAD - QK_HEAD)).reshape(1, HEAD_PAD)
    qn = mla_q_norm[0].reshape(1, q_lora)
    kvn = mla_kv_norm[0].reshape(1, kv_lora)
    q_l, k_l, v_l = _mla_prep(cq_l, ckv_l, krp_l, qn, wq, qg, kvn, wk, wv, kg, _rope_tables(l), tm)
    k_c, v_c = _mla_prep(None, ckv_c, krp_c, None, None, None, kvn, wk, wv, kg, None, tmc)
    o_l = _attention(q_l, k_l, k_c, v_l, v_c, tm)

    wr, br = _router_weights(moe_w_grp[1], moe_b_grp[1], moe_w_exp[1], moe_b_exp[1])
    xl, xm, rt = _mix_cd(yf, yb, u_l, o_l, xl, g1, sh2, sc2, s5_d[0].reshape(1, d_c),
                         s5_w_glu[0].astype(BF16), cd_w_out[0].astype(BF16), wr, br, tm)
    (xl,) = _hier_moe([(xl, xm, rt, g2)], moe_w_gate[1], moe_w_up[1], moe_w_down[1])
    return xl
```

```python
import functools
import math

import jax
import jax.numpy as jnp
from jax import lax
from jax.experimental import pallas as pl
from jax.experimental.pallas import tpu as pltpu

F32 = jnp.float32
BF16 = jnp.bfloat16
I32 = jnp.int32
HIGHEST = lax.Precision.HIGHEST

EPS = 1e-6
GRID_W = 64
CHUNK = 128
G_A = 8
CONV_W = 3
S5_GROUP = 16
S5_STATE = 64
N_HEADS = 8
QK_NOPE = 64
QK_ROPE = 32
QK_HEAD = QK_NOPE + QK_ROPE
V_HEAD = 64
AXIS_PAIRS = QK_ROPE // 4
ROPE_BASE = 10000.0
N_GROUPS = 4
EXPERTS_PER_GROUP = 8
N_EXPERTS = N_GROUPS * EXPERTS_PER_GROUP
TOP_K = 2

LANE = 128
SUBLANE = 8
HEAD_PAD = LANE
TOKEN_TILE = 512
MOE_TILE = 256
COMBINE_TILE = 256
PLAN_BLOCK = 1024
MAX_TILES = 256
S5_STEPS = 64
S5_SUPER = LANE // S5_GROUP
VMEM_LIMIT = 56 * 1024 * 1024


def _cparams(sem):
    return pltpu.CompilerParams(dimension_semantics=sem, vmem_limit_bytes=VMEM_LIMIT)


def _rms(x):
    return x * lax.rsqrt(jnp.mean(x * x, axis=-1, keepdims=True) + EPS)


def _sigmoid(x):
    return 1.0 / (1.0 + jnp.exp(-x))


def _dot(a, b):
    return jnp.dot(a, b, preferred_element_type=F32)


def _ada_kernel(c_ref, w_ref, b_ref, o_ref):
    c = c_ref[...]
    s = c * _sigmoid(c)
    o_ref[...] = jnp.dot(s, w_ref[...], precision=HIGHEST, preferred_element_type=F32) + b_ref[...]


def _ada(cvec, ada_w, ada_b):
    depth, d, n = ada_w.shape
    rows = cvec.shape[0]
    tn = 1024
    return pl.pallas_call(
        _ada_kernel,
        grid=(depth, n // tn),
        in_specs=[
            pl.BlockSpec((rows, d), lambda i, j: (0, 0)),
            pl.BlockSpec((None, d, tn), lambda i, j: (i, 0, j)),
            pl.BlockSpec((None, 1, tn), lambda i, j: (i, 0, j)),
        ],
        out_specs=pl.BlockSpec((None, rows, tn), lambda i, j: (i, 0, j)),
        out_shape=jax.ShapeDtypeStruct((depth, rows, n), F32),
        compiler_params=_cparams(("arbitrary", "arbitrary")),
        name="ada_mod",
    )(cvec, ada_w, ada_b.reshape(depth, 1, n))


def _route(xm, wr_ref, br_ref):
    logits = jnp.dot(xm, wr_ref[...], precision=HIGHEST, preferred_element_type=F32) + br_ref[...]
    lane = lax.broadcasted_iota(I32, logits.shape, 1).astype(F32)
    neg = jnp.float32(-1e30)
    big = jnp.float32(1e6)
    is_grp = lane < N_GROUPS
    gl = jnp.where(is_grp, logits, neg)
    gmax = jnp.max(gl, axis=-1, keepdims=True)
    gsum = jnp.sum(jnp.where(is_grp, jnp.exp(gl - gmax), 0.0), axis=-1, keepdims=True)
    p_top = 1.0 / gsum
    grp = jnp.min(jnp.where(gl == gmax, lane, big), axis=-1, keepdims=True)
    eidx = lane - N_GROUPS
    in_grp = (eidx >= grp * EXPERTS_PER_GROUP) & (eidx < (grp + 1.0) * EXPERTS_PER_GROUP)
    el = jnp.where(in_grp, logits, neg)
    m1 = jnp.max(el, axis=-1, keepdims=True)
    i1 = jnp.min(jnp.where(el == m1, lane, big), axis=-1, keepdims=True)
    el2 = jnp.where(lane == i1, neg, el)
    m2 = jnp.max(el2, axis=-1, keepdims=True)
    i2 = jnp.min(jnp.where(el2 == m2, lane, big), axis=-1, keepdims=True)
    t = jnp.exp(m2 - m1)
    w1 = p_top / (1.0 + t)
    w2 = p_top * t / (1.0 + t)
    e1 = i1 - N_GROUPS
    e2 = i2 - N_GROUPS
    return jnp.where(lane == 0, e1, jnp.where(lane == 1, e2, jnp.where(lane == 2, w1, jnp.where(lane == 3, w2, 0.0))))


def _residual_and_route(x, y, g1, sh2, sc2, wr_ref, br_ref, xl_ref, xm_ref, rt_ref):
    xl = x + g1 * y
    xl_ref[...] = xl
    xm = _rms(xl) * (1.0 + sc2) + sh2
    _store_token_tiles(xm_ref, xm)
    rt_ref[...] = _route(xm, wr_ref, br_ref)


def _inproj_ab_kernel(x_ref, sh_ref, sc_ref, w_ref, z_ref, p_ref):
    d = x_ref.shape[-1]
    h = (_rms(x_ref[...]) * (1.0 + sc_ref[...]) + sh_ref[...]).astype(BF16)
    for j in range(3):
        z_ref[:, j * d:(j + 1) * d] = _dot(h, w_ref[:, j * d:(j + 1) * d]).astype(BF16)
    gate_c = _dot(h, w_ref[:, 3 * d:4 * d])
    xb = _dot(h, w_ref[:, 4 * d:5 * d])
    p_ref[...] = gate_c * xb


def _inproj_ab(x, sh, sc, w_bf16, tm):
    b, l, d = x.shape
    n = w_bf16.shape[1]
    return pl.pallas_call(
        _inproj_ab_kernel,
        grid=(b, l // tm),
        in_specs=[
            pl.BlockSpec((None, tm, d), lambda i, j: (i, j, 0)),
            pl.BlockSpec((None, 1, d), lambda i, j: (i, 0, 0)),
            pl.BlockSpec((None, 1, d), lambda i, j: (i, 0, 0)),
            pl.BlockSpec((d, n), lambda i, j: (0, 0)),
        ],
        out_specs=[
            pl.BlockSpec((None, tm, 3 * d), lambda i, j: (i, j, 0)),
            pl.BlockSpec((None, tm, d), lambda i, j: (i, j, 0)),
        ],
        out_shape=[
            jax.ShapeDtypeStruct((b, l, 3 * d), BF16),
            jax.ShapeDtypeStruct((b, l, d), F32),
        ],
        compiler_params=_cparams(("arbitrary", "arbitrary")),
        name="inproj_ab",
    )(x, sh, sc, w_bf16)


def _mix_ab_kernel(z_ref, p_ref, pprev_ref, pnext_ref, x_ref, g1_ref, sh2_ref, sc2_ref,
                   sgun_ref, sguw_ref, sgub_ref, convw_ref, wout_ref, wr_ref, br_ref,
                   xl_ref, xm_ref, rt_ref, pbuf, ycat):
    i = pl.program_id(1)
    nt = pl.num_programs(1)
    tm, d = x_ref.shape
    cg = d // G_A
    v = jax.nn.gelu(z_ref[:, d:2 * d].astype(F32))
    vc = (_rms(v) * sgun_ref[...]).astype(BF16)
    for c in range(tm // CHUNK):
        r0 = c * CHUNK
        cols = [_dot(sguw_ref[g], vc[r0:r0 + CHUNK, g * cg:(g + 1) * cg]) for g in range(G_A)]
        s = jnp.concatenate(cols, axis=1) + sgub_ref[...]
        u = jax.nn.gelu(z_ref[r0:r0 + CHUNK, 0:d].astype(F32))
        ycat[r0:r0 + CHUNK, 0:d] = (u * s).astype(BF16)
    pbuf[SUBLANE:SUBLANE + tm, :] = p_ref[...]
    pbuf[0:SUBLANE, :] = jnp.where(i > 0, pprev_ref[...], 0.0)
    pbuf[SUBLANE + tm:2 * SUBLANE + tm, :] = jnp.where(i < nt - 1, pnext_ref[...], 0.0)
    conv = (convw_ref[0:1, :] * pbuf[SUBLANE - 1:SUBLANE - 1 + tm, :]
            + convw_ref[1:2, :] * pbuf[SUBLANE:SUBLANE + tm, :]
            + convw_ref[2:3, :] * pbuf[SUBLANE + 1:SUBLANE + 1 + tm, :])
    ycat[:, d:2 * d] = (z_ref[:, 2 * d:3 * d].astype(F32) * conv).astype(BF16)
    y = _dot(ycat[...], wout_ref[...])
    _residual_and_route(x_ref[...], y, g1_ref[...], sh2_ref[...], sc2_ref[...], wr_ref, br_ref,
                        xl_ref, xm_ref, rt_ref)


def _mix_ab(z, p, x, g1, sh2, sc2, sgun, sguw, sgub, convw, wout, wr, br, tm):
    b, l, d = x.shape
    hb = tm // SUBLANE
    nhb = l // SUBLANE
    tok = lambda i, j: (i, j, 0)
    per_b = lambda i, j: (i, 0, 0)
    const2 = lambda i, j: (0, 0)
    return pl.pallas_call(
        _mix_ab_kernel,
        grid=(b, l // tm),
        in_specs=[
            pl.BlockSpec((None, tm, 3 * d), tok),
            pl.BlockSpec((None, tm, d), tok),
            pl.BlockSpec((None, SUBLANE, d), lambda i, j: (i, jnp.maximum(j * hb - 1, 0), 0)),
            pl.BlockSpec((None, SUBLANE, d), lambda i, j: (i, jnp.minimum((j + 1) * hb, nhb - 1), 0)),
            pl.BlockSpec((None, tm, d), tok),
            pl.BlockSpec((None, 1, d), per_b),
            pl.BlockSpec((None, 1, d), per_b),
            pl.BlockSpec((None, 1, d), per_b),
            pl.BlockSpec((1, d), const2),
            pl.BlockSpec((G_A, CHUNK, CHUNK), lambda i, j: (0, 0, 0)),
            pl.BlockSpec((CHUNK, d), const2),
            pl.BlockSpec((CONV_W, d), const2),
            pl.BlockSpec((2 * d, d), const2),
            pl.BlockSpec((d, LANE), const2),
            pl.BlockSpec((1, LANE), const2),
        ],
        out_specs=[
            pl.BlockSpec((None, tm, d), tok),
            pl.BlockSpec((tm * (d // LANE), LANE), lambda i, j: (i * (l // tm) + j, 0)),
            pl.BlockSpec((None, tm, LANE), tok),
        ],
        out_shape=[
            jax.ShapeDtypeStruct((b, l, d), F32),
            jax.ShapeDtypeStruct((b * l * (d // LANE), LANE), F32),
            jax.ShapeDtypeStruct((b, l, LANE), F32),
        ],
        scratch_shapes=[
            pltpu.VMEM((tm + 2 * SUBLANE, d), F32),
            pltpu.VMEM((tm, 2 * d), BF16),
        ],
        compiler_params=_cparams(("arbitrary", "arbitrary")),
        name="mix_ab",
    )(z, p, p, p, x, g1, sh2, sc2, sgun, sguw, sgub, convw, wout, wr, br)


def _store_token_tiles(ref, val):
    tm, d = val.shape
    rpt = d // LANE
    for j in range(rpt):
        ref[pl.ds(j, tm, stride=rpt), :] = val[:, j * LANE:(j + 1) * LANE]


def _load_token_tile_col(ref, tm, rpt, j):
    return ref[pl.ds(j, tm, stride=rpt), :]


def _plan_kernel(rt_ref, tri_ref, pos_ref, te_ref, tv_ref, pad_ref, run, *, tm):
    ps = pl.program_id(0)
    i = pl.program_id(1)
    blk = rt_ref.shape[0]
    ne = N_EXPERTS

    @pl.when(jnp.logical_and(ps == 0, i == 0))
    def _():
        run[...] = jnp.zeros_like(run)

    @pl.when(jnp.logical_and(ps == 1, i == 0))
    def _():
        counts = run[...]
        padded = jnp.floor((counts + (tm - 1)) * (1.0 / tm)) * tm
        r = lax.broadcasted_iota(I32, (ne, ne), 0)
        c = lax.broadcasted_iota(I32, (ne, ne), 1)
        starts = jnp.dot((c < r).astype(F32), padded, precision=HIGHEST, preferred_element_type=F32)
        nt = te_ref.shape[1]
        ends = jnp.concatenate([(starts + padded) * (1.0 / tm)] * (nt // LANE), axis=1)
        t = lax.broadcasted_iota(I32, (ne, nt), 1).astype(F32)
        te = jnp.sum((t >= ends).astype(F32), axis=0, keepdims=True)
        total = jnp.max(ends, axis=0, keepdims=True)
        te_ref[...] = jnp.minimum(te, ne - 1.0).astype(I32)
        tv_ref[...] = (t[0:1, :] < total).astype(I32)
        lane = lax.broadcasted_iota(I32, (ne, LANE), 1)
        pad_ref[...] = jnp.where(lane == 0, starts + counts,
                                 jnp.where(lane == 1, padded - counts, 0.0)).astype(I32)
        run[...] = starts

    slab_t = rt_ref[...].T
    ex = lax.broadcasted_iota(I32, (ne, blk), 0).astype(F32)
    oh1 = ex == slab_t[0:1, :]
    oh2 = ex == slab_t[1:2, :]
    oh = jnp.where(jnp.logical_or(oh1, oh2), 1.0, 0.0)
    prefix = _dot(oh.astype(BF16), tri_ref[...]) + run[:, 0:1]
    pos1 = jnp.sum(jnp.where(oh1, prefix, 0.0), axis=0, keepdims=True)
    pos2 = jnp.sum(jnp.where(oh2, prefix, 0.0), axis=0, keepdims=True)
    pos_ref[...] = jnp.concatenate([pos1, pos2], axis=0).astype(I32)
    run[...] = run[...] + jnp.sum(oh, axis=1, keepdims=True)


def _moe_plan(rt_all, tm):
    n = rt_all.shape[0]
    blk = PLAN_BLOCK
    nb = n // blk
    tri = jnp.triu(jnp.ones((blk, blk), BF16), k=1)
    return pl.pallas_call(
        functools.partial(_plan_kernel, tm=tm),
        grid=(2, nb),
        in_specs=[
            pl.BlockSpec((blk, LANE), lambda p, i: (i, 0)),
            pl.BlockSpec((blk, blk), lambda p, i: (0, 0)),
        ],
        out_specs=[
            pl.BlockSpec((None, TOP_K, blk), lambda p, i: (jnp.where(p == 0, nb, i), 0, 0)),
            pl.BlockSpec((1, MAX_TILES), lambda p, i: (0, 0)),
            pl.BlockSpec((1, MAX_TILES), lambda p, i: (0, 0)),
            pl.BlockSpec((N_EXPERTS, LANE), lambda p, i: (0, 0)),
        ],
        out_shape=[
            jax.ShapeDtypeStruct((nb + 1, TOP_K, blk), I32),
            jax.ShapeDtypeStruct((1, MAX_TILES), I32),
            jax.ShapeDtypeStruct((1, MAX_TILES), I32),
            jax.ShapeDtypeStruct((N_EXPERTS, LANE), I32),
        ],
        scratch_shapes=[pltpu.VMEM((N_EXPERTS, LANE), F32)],
        compiler_params=_cparams(("arbitrary", "arbitrary")),
        name="moe_plan",
    )(rt_all, tri)


def _dispatch_kernel(ps_ref, pl_ref, tv_ref, pos_ref, *rest, blocks, n_tiles):
    xm_refs = rest[:len(blocks)]
    xs_hbm, zbuf, sem = rest[len(blocks):]
    i = pl.program_id(0)
    tt = pos_ref.shape[1]
    rpt = xm_refs[0].shape[0] // tt
    tile_rows = zbuf.shape[0]

    @pl.when(i == 0)
    def _():
        zbuf[...] = jnp.zeros_like(zbuf)
        zrow = zbuf.at[pl.ds(0, rpt)]

        def per_expert(e, carry):
            base = ps_ref[e]
            n = pl_ref[e]

            def start(r, c2):
                pltpu.make_async_copy(zrow, xs_hbm.at[pl.ds((base + r) * rpt, rpt)], sem.at[1]).start()
                return c2
            lax.fori_loop(0, n, start, 0)

            def wait(r, c2):
                pltpu.make_async_copy(zrow, xs_hbm.at[pl.ds(0, rpt)], sem.at[1]).wait()
                return c2
            lax.fori_loop(0, n, wait, 0)
            return carry
        lax.fori_loop(0, N_EXPERTS, per_expert, 0)

        def per_tile(t, carry):
            @pl.when(tv_ref[t] != 1)
            def _():
                dst = xs_hbm.at[pl.ds(pl.multiple_of(t * tile_rows, tile_rows), tile_rows)]
                cp = pltpu.make_async_copy(zbuf, dst, sem.at[1])
                cp.start()
                cp.wait()
            return carry
        lax.fori_loop(0, n_tiles, per_tile, 0)

    def scatter(xm_ref):
        def issue(r, carry):
            src = xm_ref.at[pl.ds(pl.multiple_of(r * rpt, rpt), rpt)]
            for k in range(TOP_K):
                dst = xs_hbm.at[pl.ds(pl.multiple_of(pos_ref[k, r] * rpt, rpt), rpt)]
                pltpu.make_async_copy(src, dst, sem.at[0]).start()
            return carry
        lax.fori_loop(0, tt, issue, 0, unroll=8)
        for k in range(TOP_K):
            pltpu.make_async_copy(xm_ref, xs_hbm.at[pl.ds(0, tt * rpt)], sem.at[0]).wait()

    lo = 0
    for xm_ref, nblk in zip(xm_refs, blocks):
        pl.when(jnp.logical_and(i >= lo, i < lo + nblk))(functools.partial(scatter, xm_ref))
        lo += nblk


def _dispatch(pad_start, pad_len, valid, pos, xm_streams, n_tiles, tm):
    tt = pos.shape[-1]
    rpt = SUBLANE
    blocks = tuple(xm.shape[0] // (rpt * tt) for xm in xm_streams)
    in_specs = [pl.BlockSpec((None, TOP_K, tt), lambda i, a, b, c: (i, 0, 0), memory_space=pltpu.SMEM)]
    lo = 0
    for nblk in blocks:
        in_specs.append(pl.BlockSpec(
            (tt * rpt, LANE), lambda i, a, b, c, lo=lo, nblk=nblk: (jnp.clip(i - lo, 0, nblk - 1), 0)))
        lo += nblk
    return pl.pallas_call(
        functools.partial(_dispatch_kernel, blocks=blocks, n_tiles=n_tiles),
        grid_spec=pltpu.PrefetchScalarGridSpec(
            num_scalar_prefetch=3,
            grid=(sum(blocks),),
            in_specs=in_specs,
            out_specs=pl.BlockSpec(memory_space=pl.ANY),
            scratch_shapes=[pltpu.VMEM((tm * rpt, LANE), F32), pltpu.SemaphoreType.DMA((2,))],
        ),
        out_shape=jax.ShapeDtypeStruct((n_tiles * tm * rpt, LANE), F32),
        compiler_params=_cparams(("arbitrary",)),
        name="moe_dispatch",
    )(pad_start, pad_len, valid, pos, *xm_streams)


def _moe_kernel(te_ref, tv_ref, x_ref, wg_ref, wu_ref, wd_ref, y_ref, wg_s, wu_s, wd_s):
    t = pl.program_id(0)
    rpt = wg_ref.shape[0] // LANE
    tm = x_ref.shape[0] // rpt

    @pl.when(tv_ref[t] == 1)
    def _():
        changed = jnp.logical_or(t == 0, te_ref[t] != te_ref[jnp.maximum(t - 1, 0)])

        @pl.when(changed)
        def _():
            wg_s[...] = wg_ref[...].astype(BF16)
            wu_s[...] = wu_ref[...].astype(BF16)
            wd_s[...] = wd_ref[...].astype(BF16)

        x = jnp.concatenate([_load_token_tile_col(x_ref, tm, rpt, j) for j in range(rpt)], axis=1).astype(BF16)
        g = _dot(x, wg_s[...])
        u = _dot(x, wu_s[...])
        h = (g * _sigmoid(g) * u).astype(BF16)
        _store_token_tiles(y_ref, _dot(h, wd_s[...]))

    @pl.when(tv_ref[t] != 1)
    def _():
        y_ref[...] = jnp.zeros_like(y_ref)


def _moe_experts(xs, tile_e, valid, n_tiles, w_gate, w_up, w_down, tm):
    _, d, de = w_gate.shape
    rpt = d // LANE
    grid_spec = pltpu.PrefetchScalarGridSpec(
        num_scalar_prefetch=2,
        grid=(n_tiles,),
        in_specs=[
            pl.BlockSpec((tm * rpt, LANE), lambda t, te, tv: (t * tv[t], 0)),
            pl.BlockSpec((None, d, de), lambda t, te, tv: (te[t], 0, 0)),
            pl.BlockSpec((None, d, de), lambda t, te, tv: (te[t], 0, 0)),
            pl.BlockSpec((None, de, d), lambda t, te, tv: (te[t], 0, 0)),
        ],
        out_specs=pl.BlockSpec((tm * rpt, LANE), lambda t, te, tv: (t, 0)),
        scratch_shapes=[
            pltpu.VMEM((d, de), BF16),
            pltpu.VMEM((d, de), BF16),
            pltpu.VMEM((de, d), BF16),
        ],
    )
    return pl.pallas_call(
        _moe_kernel,
        grid_spec=grid_spec,
        out_shape=jax.ShapeDtypeStruct((n_tiles * tm * rpt, LANE), F32),
        compiler_params=_cparams(("arbitrary",)),
        name="moe_experts",
    )(tile_e, valid, xs, w_gate, w_up, w_down)


def _combine_kernel(pos0_ref, posn_ref, x_ref, g_ref, rt_ref, ys_hbm, o_ref, buf, sem, *, n):
    i = pl.program_id(0)
    tt, d = x_ref.shape
    rpt = d // LANE

    def gather(pos_ref, slot):
        def issue(r, carry):
            for k in range(TOP_K):
                src = ys_hbm.at[pl.ds(pl.multiple_of(pos_ref[k, r] * rpt, rpt), rpt)]
                dst = buf.at[slot, k, pl.ds(pl.multiple_of(r * rpt, rpt), rpt)]
                pltpu.make_async_copy(src, dst, sem.at[slot, k]).start()
            return carry
        lax.fori_loop(0, tt, issue, 0, unroll=8)

    slot = lax.rem(i, 2)

    @pl.when(i == 0)
    def _():
        gather(pos0_ref, 0)

    @pl.when(i + 1 < n)
    def _():
        gather(posn_ref, 1 - slot)

    for k in range(TOP_K):
        pltpu.make_async_copy(ys_hbm.at[pl.ds(0, tt * rpt)], buf.at[slot, k], sem.at[slot, k]).wait()
    w1 = rt_ref[:, 2:3]
    w2 = rt_ref[:, 3:4]
    for j in range(rpt):
        r1 = buf[slot, 0, pl.ds(j, tt, stride=rpt), :]
        r2 = buf[slot, 1, pl.ds(j, tt, stride=rpt), :]
        sl = slice(j * LANE, (j + 1) * LANE)
        o_ref[:, sl] = x_ref[:, sl] + g_ref[:, sl] * (w1 * r1 + w2 * r2)


def _moe_combine(x, g2, rt, pos, tok_off, ys, tt):
    b, l, d = x.shape
    rpt = d // LANE
    n = b * l
    per_b = l // tt
    per_blk = PLAN_BLOCK // tt
    blk0 = tok_off // PLAN_BLOCK

    def pos_map(step):
        return lambda i: (blk0 + step(i) // per_blk, 0, step(i) % per_blk)

    return pl.pallas_call(
        functools.partial(_combine_kernel, n=n // tt),
        grid=(n // tt,),
        in_specs=[
            pl.BlockSpec((None, TOP_K, tt), pos_map(lambda i: i), memory_space=pltpu.SMEM),
            pl.BlockSpec((None, TOP_K, tt), pos_map(lambda i: jnp.minimum(i + 1, n // tt - 1)),
                         memory_space=pltpu.SMEM),
            pl.BlockSpec((tt, d), lambda i: (i, 0)),
            pl.BlockSpec((None, 1, d), lambda i: (i // per_b, 0, 0)),
            pl.BlockSpec((tt, LANE), lambda i: (i, 0)),
            pl.BlockSpec(memory_space=pl.ANY),
        ],
        out_specs=pl.BlockSpec((tt, d), lambda i: (i, 0)),
        out_shape=jax.ShapeDtypeStruct((n, d), F32),
        scratch_shapes=[
            pltpu.VMEM((2, TOP_K, tt * rpt, LANE), F32),
            pltpu.SemaphoreType.DMA((2, TOP_K)),
        ],
        compiler_params=_cparams(("arbitrary",)),
        name="moe_combine",
    )(pos, pos, x.reshape(n, d), g2, rt.reshape(n, LANE), ys).reshape(b, l, d)


def _hier_moe(streams, w_gate, w_up, w_down):
    sizes = [s[0].shape[0] * s[0].shape[1] for s in streams]
    n = sum(sizes)
    tm = MOE_TILE
    n_tiles = (TOP_K * n) // tm + N_EXPERTS
    assert n_tiles <= MAX_TILES and all(sz % PLAN_BLOCK == 0 for sz in sizes)
    rt_all = jnp.concatenate([s[2].reshape(-1, LANE) for s in streams], axis=0)
    pos, tile_e, valid, pad = _moe_plan(rt_all, tm)
    tile_e, valid = tile_e.reshape(-1), valid.reshape(-1)
    xs = _dispatch(pad[:, 0], pad[:, 1], valid, pos, [s[1] for s in streams], n_tiles, tm)
    ys = _moe_experts(xs, tile_e, valid, n_tiles, w_gate, w_up, w_down, tm)
    outs = []
    off = 0
    for (xl, _, rt, g2), sz in zip(streams, sizes):
        outs.append(_moe_combine(xl, g2, rt, pos, off, ys, COMBINE_TILE))
        off += sz
    return outs


def _inproj_cd_kernel(x_ref, sh_ref, sc_ref, w_ref, u_ref, cq_ref, ckv_ref, kr_ref):
    h = (_rms(x_ref[...]) * (1.0 + sc_ref[...]) + sh_ref[...]).astype(BF16)
    off = 0
    for ref in (u_ref, cq_ref, ckv_ref, kr_ref):
        n = ref.shape[-1]
        ref[...] = _dot(h, w_ref[:, off:off + n]).astype(ref.dtype)
        off += n


def _inproj_cd(x, sh, sc, w_bf16, widths, tm):
    b, l, d = x.shape
    n = w_bf16.shape[1]
    tok = lambda i, j: (i, j, 0)
    return pl.pallas_call(
        _inproj_cd_kernel,
        grid=(b, l // tm),
        in_specs=[
            pl.BlockSpec((None, tm, d), tok),
            pl.BlockSpec((None, 1, d), lambda i, j: (i, 0, 0)),
            pl.BlockSpec((None, 1, d), lambda i, j: (i, 0, 0)),
            pl.BlockSpec((d, n), lambda i, j: (0, 0)),
        ],
        out_specs=[pl.BlockSpec((None, tm, w), tok) for w in widths],
        out_shape=[jax.ShapeDtypeStruct((b, l, w), F32) for w in widths],
        compiler_params=_cparams(("arbitrary", "arbitrary")),
        name="inproj_cd",
    )(x, sh, sc, w_bf16)


def _s5_param_kernel(are_ref, aim_ref, ldt_ref, bre_ref, bim_ref, abre_ref, abim_ref, bbre_ref, bbim_ref):
    a_re = are_ref[...]
    a_im = aim_ref[...]
    dt = jnp.exp(ldt_ref[...])
    mag = jnp.exp(dt * a_re)
    ab_re = mag * jnp.cos(dt * a_im)
    ab_im = mag * jnp.sin(dt * a_im)
    den = a_re * a_re + a_im * a_im
    nr = ab_re - 1.0
    f_re = (nr * a_re + ab_im * a_im) / den
    f_im = (ab_im * a_re - nr * a_im) / den
    abre_ref[...] = ab_re
    abim_ref[...] = ab_im
    b_re = bre_ref[...]
    b_im = bim_ref[...]
    bbre_ref[...] = f_re[None] * b_re - f_im[None] * b_im
    bbim_ref[...] = f_re[None] * b_im + f_im[None] * b_re


def _s5_params(a_re, a_im, log_dt, b_re, b_im):
    nd, g, p = a_re.shape
    c = b_re.shape[-1]
    rows = nd * g * p // LANE
    flat = lambda t: t.reshape(rows, LANE)
    chan_major = lambda t: jnp.moveaxis(t, -1, 0).reshape(c, rows, LANE)
    ldt = jnp.broadcast_to(log_dt[:, :, None], (nd, g, p))
    shapes = [jax.ShapeDtypeStruct((rows, LANE), F32)] * 2 + [jax.ShapeDtypeStruct((c, rows, LANE), F32)] * 2
    ab_re, ab_im, bb_re, bb_im = pl.pallas_call(
        _s5_param_kernel, out_shape=shapes, name="s5_params",
    )(flat(a_re), flat(a_im), flat(ldt), chan_major(b_re), chan_major(b_im))
    unflat = lambda t: jnp.moveaxis(t.reshape(c, nd, g, p), 0, -1)
    return ab_re.reshape(nd, g, p), ab_im.reshape(nd, g, p), unflat(bb_re), unflat(bb_im)


def _s5_kernel(uf_ref, ub_ref, a_ref, bsb_ref, csb_ref, yf_ref, yb_ref, hf, hb, cf, cb):
    nb = cf.shape[0]
    steps = hf.shape[0] // nb
    n_super = bsb_ref.shape[1]
    cw = bsb_ref.shape[2]
    sw = bsb_ref.shape[3]
    half = sw // 2

    @pl.when(pl.program_id(0) == 0)
    def _():
        cf[...] = jnp.zeros_like(cf)
        cb[...] = jnp.zeros_like(cb)

    for s in range(n_super):
        hf[:, s * sw:(s + 1) * sw] = _dot(uf_ref[:, s * cw:(s + 1) * cw].astype(BF16), bsb_ref[0, s])
        hb[:, s * sw:(s + 1) * sw] = _dot(ub_ref[:, s * cw:(s + 1) * cw].astype(BF16), bsb_ref[1, s])

    for s in range(n_super):
        re = slice(s * sw, s * sw + half)
        im = slice(s * sw + half, (s + 1) * sw)
        st = slice(s * half, (s + 1) * half)
        arf, aif = a_ref[0, 0, :, st], a_ref[0, 1, :, st]
        arb, aib = a_ref[1, 0, :, st], a_ref[1, 1, :, st]

        def step(t, carry):
            hfr, hfi, hbr, hbi = carry
            rf = pl.multiple_of(t * nb, nb)
            nfr = arf * hfr - aif * hfi + hf[pl.ds(rf, nb), re]
            nfi = arf * hfi + aif * hfr + hf[pl.ds(rf, nb), im]
            hf[pl.ds(rf, nb), re] = nfr
            hf[pl.ds(rf, nb), im] = nfi
            rb = pl.multiple_of((steps - 1 - t) * nb, nb)
            nbr = arb * hbr - aib * hbi + hb[pl.ds(rb, nb), re]
            nbi = arb * hbi + aib * hbr + hb[pl.ds(rb, nb), im]
            hb[pl.ds(rb, nb), re] = nbr
            hb[pl.ds(rb, nb), im] = nbi
            return nfr, nfi, nbr, nbi

        out = lax.fori_loop(0, steps, step, (cf[:, re], cf[:, im], cb[:, re], cb[:, im]), unroll=4)
        cf[:, re], cf[:, im], cb[:, re], cb[:, im] = out

    for s in range(n_super):
        yf_ref[:, s * cw:(s + 1) * cw] = _dot(hf[:, s * sw:(s + 1) * sw].astype(BF16), csb_ref[0, s])
        yb_ref[:, s * cw:(s + 1) * cw] = _dot(hb[:, s * sw:(s + 1) * sw].astype(BF16), csb_ref[1, s])


def _s5_scan(u_tm, n_ctx_blocks, a_bc, bsb, csb, nb):
    rows, c = u_tm.shape
    blk = S5_STEPS * nb
    n_blocks = rows // blk
    n_state = bsb.shape[1] * bsb.shape[3]
    fwd = lambda i: (i, 0)
    bwd = lambda i: (jnp.where(i < n_ctx_blocks, n_ctx_blocks - 1 - i, n_blocks - 1 - (i - n_ctx_blocks)), 0)
    return pl.pallas_call(
        _s5_kernel,
        grid=(n_blocks,),
        in_specs=[
            pl.BlockSpec((blk, c), fwd),
            pl.BlockSpec((blk, c), bwd),
            pl.BlockSpec(a_bc.shape, lambda i: (0, 0, 0, 0)),
            pl.BlockSpec(bsb.shape, lambda i: (0, 0, 0, 0)),
            pl.BlockSpec(csb.shape, lambda i: (0, 0, 0, 0)),
        ],
        out_specs=[pl.BlockSpec((blk, c), fwd), pl.BlockSpec((blk, c), bwd)],
        out_shape=[jax.ShapeDtypeStruct((rows, c), F32)] * 2,
        scratch_shapes=[
            pltpu.VMEM((blk, n_state), F32),
            pltpu.VMEM((blk, n_state), F32),
            pltpu.VMEM((nb, n_state), F32),
            pltpu.VMEM((nb, n_state), F32),
        ],
        compiler_params=_cparams(("arbitrary",)),
        name="s5_scan",
    )(u_tm, u_tm, a_bc, bsb, csb)


def _s5_block_matrices(ab_re, ab_im, bb_re, bb_im, c_re, c_im, nb):
    nd, g, p = ab_re.shape
    c = bb_re.shape[-1]
    ns = g // S5_SUPER
    eye = jnp.eye(S5_SUPER, dtype=F32)

    def in_mat(bb):
        t = bb.reshape(nd, ns, S5_SUPER, p, c)
        return jnp.einsum('dsgpc,gh->dsgchp', t, eye).reshape(nd, ns, S5_SUPER * c, S5_SUPER * p)

    def out_mat(cc):
        t = cc.reshape(nd, ns, S5_SUPER, c, p)
        return jnp.einsum('dsgcp,gh->dsgphc', t, eye).reshape(nd, ns, S5_SUPER * p, S5_SUPER * c)

    bsb = jnp.concatenate([in_mat(bb_re), in_mat(bb_im)], axis=-1).astype(BF16)
    csb = jnp.concatenate([out_mat(c_re), out_mat(-c_im)], axis=-2).astype(BF16)
    a_bc = jnp.stack([ab_re.reshape(nd, g * p), ab_im.reshape(nd, g * p)], axis=1)
    a_bc = jnp.broadcast_to(a_bc[:, :, None, :], (nd, 2, nb, g * p))
    return a_bc, bsb, csb


def _rope(x, cos, sin_a, sin_b):
    return (x * cos + pltpu.roll(x, HEAD_PAD - AXIS_PAIRS, axis=1) * sin_a
            + pltpu.roll(x, AXIS_PAIRS, axis=1) * sin_b)


def _head_norm(x, gain):
    return x * lax.rsqrt(jnp.sum(x * x, axis=-1, keepdims=True) * (1.0 / QK_HEAD) + EPS) * gain


def _mla_prep_kernel(*refs, with_q, with_rope):
    if with_q:
        cq_ref, qn_ref, wq_ref, qg_ref = refs[:4]
        refs = refs[4:]
    ckv_ref, krp_ref, kvn_ref, wk_ref, wv_ref, kg_ref = refs[:6]
    refs = refs[6:]
    if with_rope:
        cos_ref, sa_ref, sb_ref = refs[:3]
        refs = refs[3:]
        cos, sa, sb = cos_ref[...], sa_ref[...], sb_ref[...]
    if with_q:
        q_ref, k_ref, v_ref = refs
    else:
        k_ref, v_ref = refs
    ckvn = (_rms(ckv_ref[...]) * kvn_ref[...]).astype(BF16)
    v_ref[...] = _dot(ckvn, wv_ref[...]).astype(v_ref.dtype)
    kn = _dot(ckvn, wk_ref[...])
    krp = krp_ref[...]
    for h in range(N_HEADS):
        sl = slice(h * HEAD_PAD, (h + 1) * HEAD_PAD)
        kh = _head_norm(kn[:, sl] + krp, kg_ref[...])
        if with_rope:
            kh = _rope(kh, cos, sa, sb)
        k_ref[:, sl] = kh.astype(k_ref.dtype)
    if with_q:
        cqn = (_rms(cq_ref[...]) * qn_ref[...]).astype(BF16)
        qn = _dot(cqn, wq_ref[...])
        for h in range(N_HEADS):
            sl = slice(h * HEAD_PAD, (h + 1) * HEAD_PAD)
            qh = _head_norm(qn[:, sl], qg_ref[...])
            if with_rope:
                qh = _rope(qh, cos, sa, sb)
            q_ref[:, sl] = (qh * (QK_HEAD ** -0.5)).astype(q_ref.dtype)


def _mla_prep(cq, ckv, krp, q_norm, wq, q_gain, kv_norm, wk, wv, k_gain, tables, tm):
    b, l, _ = ckv.shape
    with_q = cq is not None
    with_rope = tables is not None
    tok = lambda i, j: (i, j, 0)
    const = lambda i, j: (0, 0)
    args, specs = [], []

    def add(arr, spec):
        args.append(arr)
        specs.append(spec)

    if with_q:
        add(cq, pl.BlockSpec((None, tm, cq.shape[-1]), tok))
        add(q_norm, pl.BlockSpec(q_norm.shape, const))
        add(wq, pl.BlockSpec(wq.shape, const))
        add(q_gain, pl.BlockSpec(q_gain.shape, const))
    add(ckv, pl.BlockSpec((None, tm, ckv.shape[-1]), tok))
    add(krp, pl.BlockSpec((None, tm, HEAD_PAD), tok))
    add(kv_norm, pl.BlockSpec(kv_norm.shape, const))
    add(wk, pl.BlockSpec(wk.shape, const))
    add(wv, pl.BlockSpec(wv.shape, const))
    add(k_gain, pl.BlockSpec(k_gain.shape, const))
    if with_rope:
        for tbl in tables:
            add(tbl, pl.BlockSpec((tm, HEAD_PAD), lambda i, j: (j, 0)))
    hk = N_HEADS * HEAD_PAD
    hv = wv.shape[1]
    out_specs = [pl.BlockSpec((None, tm, hk), tok), pl.BlockSpec((None, tm, hv), tok)]
    out_shape = [jax.ShapeDtypeStruct((b, l, hk), BF16), jax.ShapeDtypeStruct((b, l, hv), BF16)]
    if with_q:
        out_specs = [pl.BlockSpec((None, tm, hk), tok)] + out_specs
        out_shape = [jax.ShapeDtypeStruct((b, l, hk), BF16)] + out_shape
    return pl.pallas_call(
        functools.partial(_mla_prep_kernel, with_q=with_q, with_rope=with_rope),
        grid=(b, l // tm),
        in_specs=specs,
        out_specs=out_specs,
        out_shape=out_shape,
        compiler_params=_cparams(("arbitrary", "arbitrary")),
        name="mla_prep_q" if with_q else "mla_prep_ctx",
    )(*args)


def _attn_kernel(q_ref, kl_ref, kc_ref, vl_ref, vc_ref, o_ref):
    nt = (((1,), (1,)), ((), ()))
    outs = []
    for hh in range(2):
        sl = slice(hh * HEAD_PAD, (hh + 1) * HEAD_PAD)
        q = q_ref[:, sl]
        s_l = lax.dot_general(q, kl_ref[:, sl], nt, preferred_element_type=F32)
        s_c = lax.dot_general(q, kc_ref[:, sl], nt, preferred_element_type=F32)
        m = jnp.maximum(jnp.max(s_l, axis=-1, keepdims=True), jnp.max(s_c, axis=-1, keepdims=True))
        p_l = jnp.exp(s_l - m)
        p_c = jnp.exp(s_c - m)
        den = jnp.sum(p_l, axis=-1, keepdims=True) + jnp.sum(p_c, axis=-1, keepdims=True)
        o = _dot(p_l.astype(BF16), vl_ref[...]) + _dot(p_c.astype(BF16), vc_ref[...])
        outs.append(o / den)
    lane = lax.broadcasted_iota(I32, outs[0].shape, 1)
    o_ref[...] = jnp.where(lane < V_HEAD, outs[0], outs[1]).astype(o_ref.dtype)


def _attention(q, k_l, k_c, v_l, v_c, tq):
    b, l, _ = q.shape
    lc = k_c.shape[1]
    hp = N_HEADS // 2
    kw = 2 * HEAD_PAD
    vw = 2 * V_HEAD
    return pl.pallas_call(
        _attn_kernel,
        grid=(b, hp, l // tq),
        in_specs=[
            pl.BlockSpec((None, tq, kw), lambda i, h, j: (i, j, h)),
            pl.BlockSpec((None, l, kw), lambda i, h, j: (i, 0, h)),
            pl.BlockSpec((None, lc, kw), lambda i, h, j: (i, 0, h)),
            pl.BlockSpec((None, l, vw), lambda i, h, j: (i, 0, h)),
            pl.BlockSpec((None, lc, vw), lambda i, h, j: (i, 0, h)),
        ],
        out_specs=pl.BlockSpec((None, tq, vw), lambda i, h, j: (i, j, h)),
        out_shape=jax.ShapeDtypeStruct((b, l, N_HEADS * V_HEAD), BF16),
        compiler_params=_cparams(("arbitrary", "arbitrary", "arbitrary")),
        name="mla_attention",
    )(q, k_l, k_c, v_l, v_c)


def _mix_cd_kernel(yf_ref, yb_ref, u_ref, o_ref, x_ref, g1_ref, sh2_ref, sc2_ref,
                   dskip_ref, wglu_ref, wout_ref, wr_ref, br_ref, xl_ref, xm_ref, rt_ref):
    dc = u_ref.shape[-1]
    y = yf_ref[...] + yb_ref[...] + u_ref[...] * dskip_ref[...]
    g = jax.nn.gelu(y)
    s5 = (g * _sigmoid(_dot(g.astype(BF16), wglu_ref[...]))).astype(BF16)
    y_mix = _dot(s5, wout_ref[0:dc, :]) + _dot(o_ref[...], wout_ref[dc:, :])
    _residual_and_route(x_ref[...], y_mix, g1_ref[...], sh2_ref[...], sc2_ref[...], wr_ref, br_ref,
                        xl_ref, xm_ref, rt_ref)


def _mix_cd(yf, yb, u, o, x, g1, sh2, sc2, dskip, wglu, wout, wr, br, tm):
    b, l, d = x.shape
    dc = u.shape[-1]
    tok = lambda i, j: (i, j, 0)
    per_b = lambda i, j: (i, 0, 0)
    const2 = lambda i, j: (0, 0)
    return pl.pallas_call(
        _mix_cd_kernel,
        grid=(b, l // tm),
        in_specs=[
            pl.BlockSpec((None, tm, dc), tok),
            pl.BlockSpec((None, tm, dc), tok),
            pl.BlockSpec((None, tm, dc), tok),
            pl.BlockSpec((None, tm, o.shape[-1]), tok),
            pl.BlockSpec((None, tm, d), tok),
            pl.BlockSpec((None, 1, d), per_b),
            pl.BlockSpec((None, 1, d), per_b),
            pl.BlockSpec((None, 1, d), per_b),
            pl.BlockSpec((1, dc), const2),
            pl.BlockSpec(wglu.shape, const2),
            pl.BlockSpec(wout.shape, const2),
            pl.BlockSpec((d, LANE), const2),
            pl.BlockSpec((1, LANE), const2),
        ],
        out_specs=[
            pl.BlockSpec((None, tm, d), tok),
            pl.BlockSpec((tm * (d // LANE), LANE), lambda i, j: (i * (l // tm) + j, 0)),
            pl.BlockSpec((None, tm, LANE), tok),
        ],
        out_shape=[
            jax.ShapeDtypeStruct((b, l, d), F32),
            jax.ShapeDtypeStruct((b * l * (d // LANE), LANE), F32),
            jax.ShapeDtypeStruct((b, l, LANE), F32),
        ],
        compiler_params=_cparams(("arbitrary", "arbitrary")),
        name="mix_cd",
    )(yf, yb, u, o, x, g1, sh2, sc2, dskip, wglu, wout, wr, br)


def _router_weights(w_grp, b_grp, w_exp, b_exp):
    d = w_grp.shape[0]
    pad = LANE - N_GROUPS - N_EXPERTS
    wr = jnp.concatenate([w_grp, w_exp, jnp.zeros((d, pad), F32)], axis=1)
    br = jnp.concatenate([b_grp, b_exp, jnp.zeros((pad,), F32)]).reshape(1, LANE)
    return wr, br


def _pad_heads(w, width):
    k = w.shape[0]
    w = w.reshape(k, N_HEADS, width)
    return jnp.pad(w, ((0, 0), (0, 0), (0, HEAD_PAD - width))).reshape(k, N_HEADS * HEAD_PAD)


def _rope_tables(l):
    rows = l // GRID_W
    row = jnp.repeat(jnp.arange(rows, dtype=F32), GRID_W)
    col = jnp.tile(jnp.arange(GRID_W, dtype=F32), rows)
    inv_freq = ROPE_BASE ** (-jnp.arange(AXIS_PAIRS, dtype=F32) / AXIS_PAIRS)
    ar, ac = row[:, None] * inv_freq, col[:, None] * inv_freq
    one = jnp.ones((l, QK_NOPE), F32)
    zn = jnp.zeros((l, QK_NOPE), F32)
    zp = jnp.zeros((l, HEAD_PAD - QK_HEAD), F32)
    z8 = jnp.zeros((l, AXIS_PAIRS), F32)
    cos = jnp.concatenate([one, jnp.cos(ar), jnp.cos(ar), jnp.cos(ac), jnp.cos(ac), zp], axis=1)
    sin_a = jnp.concatenate([zn, -jnp.sin(ar), z8, -jnp.sin(ac), z8, zp], axis=1)
    sin_b = jnp.concatenate([zn, z8, jnp.sin(ar), z8, jnp.sin(ac), zp], axis=1)
    return cos, sin_a, sin_b


def _split_mod(mod_row, b, d):
    parts = jnp.split(mod_row, 6, axis=-1)
    return [jnp.broadcast_to(p.reshape(-1, 1, d), (b, 1, d)) for p in parts]


def kernel(x, c, ctx, c_ctx, ada_w, ada_b, ab_w_in, sgu_norm, sgu_w, sgu_b, conv_w, ab_w_out, cd_w_in, s5_a_re, s5_a_im, s5_log_dt, s5_b_re, s5_b_im, s5_c_re, s5_c_im, s5_d, s5_w_glu, mla_q_norm, mla_kv_norm, mla_w_uq, mla_w_uk, mla_w_uv, mla_qn_gain, mla_kn_gain, cd_w_out, moe_w_grp, moe_b_grp, moe_w_exp, moe_b_exp, moe_w_gate, moe_w_up, moe_w_down):
    b, l, d = x.shape
    lc = ctx.shape[1]
    depth = ada_w.shape[0]
    assert depth == 2, "layer 0 = gated-MLP/conv mixers, layer 1 = S5/attention mixers"
    tm = min(TOKEN_TILE, l)
    tmc = min(TOKEN_TILE, lc)

    mod_rows = 2 * SUBLANE
    cvec = jnp.zeros((mod_rows, d), F32).at[:b].set(c).at[b].set(c_ctx)
    mod = _ada(cvec, ada_w, ada_b)

    sh1, sc1, g1, sh2, sc2, g2 = _split_mod(mod[0, :b], b, d)
    csh1, csc1, cg1, csh2, csc2, cg2 = _split_mod(mod[0, b:b + 1], b, d)
    w_in = ab_w_in[0].astype(BF16)
    wr, br = _router_weights(moe_w_grp[0], moe_b_grp[0], moe_w_exp[0], moe_b_exp[0])
    sgub = jnp.repeat(sgu_b[0].T, d // G_A, axis=1)
    mix_args = (sgu_norm[0].reshape(1, d), sgu_w[0].astype(BF16), sgub, conv_w[0],
                ab_w_out[0].astype(BF16), wr, br)
    z, p = _inproj_ab(x, sh1, sc1, w_in, tm)
    xl, xm, rt = _mix_ab(z, p, x, g1, sh2, sc2, *mix_args, tm)
    zc, pc = _inproj_ab(ctx, csh1, csc1, w_in, tmc)
    xc, xmc, rtc = _mix_ab(zc, pc, ctx, cg1, csh2, csc2, *mix_args, tmc)
    xl, xc = _hier_moe([(xl, xm, rt, g2), (xc, xmc, rtc, cg2)], moe_w_gate[0], moe_w_up[0], moe_w_down[0])

    sh1, sc1, g1, sh2, sc2, g2 = _split_mod(mod[1, :b], b, d)
    csh1, csc1, _, _, _, _ = _split_mod(mod[1, b:b + 1], b, d)
    d_c = s5_d.shape[-1]
    q_lora = mla_q_norm.shape[-1]
    kv_lora = mla_kv_norm.shape[-1]
    w_cd = cd_w_in[0]
    o_kr = d_c + q_lora + kv_lora
    w_kr = jnp.pad(w_cd[:, o_kr:], ((0, 0), (QK_NOPE, HEAD_PAD - QK_HEAD)))
    w_cd = jnp.concatenate([w_cd[:, :o_kr], w_kr], axis=1).astype(BF16)
    widths = (d_c, q_lora, kv_lora, HEAD_PAD)
    u_l, cq_l, ckv_l, krp_l = _inproj_cd(xl, sh1, sc1, w_cd, widths, tm)
    u_c, _, ckv_c, krp_c = _inproj_cd(xc, csh1, csc1, w_cd, widths, tmc)

    ab_re, ab_im, bb_re, bb_im = _s5_params(s5_a_re[0], s5_a_im[0], s5_log_dt[0], s5_b_re[0], s5_b_im[0])
    a_bc, bsb, csb = _s5_block_matrices(ab_re, ab_im, bb_re, bb_im, s5_c_re[0], s5_c_im[0], b)
    u_tm = jnp.concatenate([u_c.transpose(1, 0, 2), u_l.transpose(1, 0, 2)], axis=0).reshape((lc + l) * b, d_c)
    yf, yb = _s5_scan(u_tm, lc // S5_STEPS, a_bc, bsb, csb, b)
    to_bm = lambda y: y.reshape(lc + l, b, d_c)[lc:].transpose(1, 0, 2)
    yf, yb = to_bm(yf), to_bm(yb)

    wq = _pad_heads(mla_w_uq[0], QK_HEAD).astype(BF16)
    wk = _pad_heads(mla_w_uk[0], QK_NOPE).astype(BF16)
    wv = mla_w_uv[0].astype(BF16)
    qg = jnp.pad(mla_qn_gain[0], (0, HEAD_PAD - QK_HEAD)).reshape(1, HEAD_PAD)
    kg = jnp.pad(mla_kn_gain[0], (0, HEAD_PAD - QK_HEAD)).reshape(1, HEAD_PAD)
    qn = mla_q_norm[0].reshape(1, q_lora)
    kvn = mla_kv_norm[0].reshape(1, kv_lora)
    q_l, k_l, v_l = _mla_prep(cq_l, ckv_l, krp_l, qn, wq, qg, kvn, wk, wv, kg, _rope_tables(l), tm)
    k_c, v_c = _mla_prep(None, ckv_c, krp_c, None, None, None, kvn, wk, wv, kg, None, tmc)
    o_l = _attention(q_l, k_l, k_c, v_l, v_c, tm)

    wr, br = _router_weights(moe_w_grp[1], moe_b_grp[1], moe_w_exp[1], moe_b_exp[1])
    xl, xm, rt = _mix_cd(yf, yb, u_l, o_l, xl, g1, sh2, sc2, s5_d[0].reshape(1, d_c),
                         s5_w_glu[0].astype(BF16), cd_w_out[0].astype(BF16), wr, br, tm)
    (xl,) = _hier_moe([(xl, xm, rt, g2)], moe_w_gate[1], moe_w_up[1], moe_w_down[1])
    return xl
```

```python
import functools
import math

import jax
import jax.numpy as jnp
from jax import lax
from jax.experimental import pallas as pl
from jax.experimental.pallas import tpu as pltpu

F32 = jnp.float32
BF16 = jnp.bfloat16
I32 = jnp.int32
HIGHEST = lax.Precision.HIGHEST

EPS = 1e-6
GRID_W = 64
CHUNK = 128
G_A = 8
CONV_W = 3
S5_GROUP = 16
S5_STATE = 64
N_HEADS = 8
QK_NOPE = 64
QK_ROPE = 32
QK_HEAD = QK_NOPE + QK_ROPE
V_HEAD = 64
AXIS_PAIRS = QK_ROPE // 4
ROPE_BASE = 10000.0
N_GROUPS = 4
EXPERTS_PER_GROUP = 8
N_EXPERTS = N_GROUPS * EXPERTS_PER_GROUP
TOP_K = 2

LANE = 128
SUBLANE = 8
HEAD_PAD = LANE
TOKEN_TILE = 512
MOE_TILE = 256
COMBINE_TILE = 256
PLAN_BLOCK = 1024
MAX_TILES = 256
S5_STEPS = 64
S5_SUPER = LANE // S5_GROUP
VMEM_LIMIT = 56 * 1024 * 1024


def _cparams(sem):
    return pltpu.CompilerParams(dimension_semantics=sem, vmem_limit_bytes=VMEM_LIMIT)


def _rms(x):
    return x * lax.rsqrt(jnp.mean(x * x, axis=-1, keepdims=True) + EPS)


def _sigmoid(x):
    return 1.0 / (1.0 + jnp.exp(-x))


def _dot(a, b):
    return jnp.dot(a, b, preferred_element_type=F32)


def _ada_kernel(c_ref, w_ref, b_ref, o_ref):
    c = c_ref[...]
    s = c * _sigmoid(c)
    o_ref[...] = jnp.dot(s, w_ref[...], precision=HIGHEST, preferred_element_type=F32) + b_ref[...]


def _ada(cvec, ada_w, ada_b):
    depth, d, n = ada_w.shape
    rows = cvec.shape[0]
    tn = 1024
    return pl.pallas_call(
        _ada_kernel,
        grid=(depth, n // tn),
        in_specs=[
            pl.BlockSpec((rows, d), lambda i, j: (0, 0)),
            pl.BlockSpec((None, d, tn), lambda i, j: (i, 0, j)),
            pl.BlockSpec((None, 1, tn), lambda i, j: (i, 0, j)),
        ],
        out_specs=pl.BlockSpec((None, rows, tn), lambda i, j: (i, 0, j)),
        out_shape=jax.ShapeDtypeStruct((depth, rows, n), F32),
        compiler_params=_cparams(("arbitrary", "arbitrary")),
        name="ada_mod",
    )(cvec, ada_w, ada_b.reshape(depth, 1, n))


def _route(xm, wr_ref, br_ref):
    logits = jnp.dot(xm, wr_ref[...], precision=HIGHEST, preferred_element_type=F32) + br_ref[...]
    lane = lax.broadcasted_iota(I32, logits.shape, 1).astype(F32)
    neg = jnp.float32(-1e30)
    big = jnp.float32(1e6)
    is_grp = lane < N_GROUPS
    gl = jnp.where(is_grp, logits, neg)
    gmax = jnp.max(gl, axis=-1, keepdims=True)
    gsum = jnp.sum(jnp.where(is_grp, jnp.exp(gl - gmax), 0.0), axis=-1, keepdims=True)
    p_top = 1.0 / gsum
    grp = jnp.min(jnp.where(gl == gmax, lane, big), axis=-1, keepdims=True)
    eidx = lane - N_GROUPS
    in_grp = (eidx >= grp * EXPERTS_PER_GROUP) & (eidx < (grp + 1.0) * EXPERTS_PER_GROUP)
    el = jnp.where(in_grp, logits, neg)
    m1 = jnp.max(el, axis=-1, keepdims=True)
    i1 = jnp.min(jnp.where(el == m1, lane, big), axis=-1, keepdims=True)
    el2 = jnp.where(lane == i1, neg, el)
    m2 = jnp.max(el2, axis=-1, keepdims=True)
    i2 = jnp.min(jnp.where(el2 == m2, lane, big), axis=-1, keepdims=True)
    t = jnp.exp(m2 - m1)
    w1 = p_top / (1.0 + t)
    w2 = p_top * t / (1.0 + t)
    e1 = i1 - N_GROUPS
    e2 = i2 - N_GROUPS
    return jnp.where(lane == 0, e1, jnp.where(lane == 1, e2, jnp.where(lane == 2, w1, jnp.where(lane == 3, w2, 0.0))))


def _residual_and_route(x, y, g1, sh2, sc2, wr_ref, br_ref, xl_ref, xm_ref, rt_ref):
    xl = x + g1 * y
    xl_ref[...] = xl
    xm = _rms(xl) * (1.0 + sc2) + sh2
    _store_token_tiles(xm_ref, xm)
    rt_ref[...] = _route(xm, wr_ref, br_ref)


def _inproj_ab_kernel(x_ref, sh_ref, sc_ref, w_ref, z_ref, p_ref):
    d = x_ref.shape[-1]
    h = (_rms(x_ref[...]) * (1.0 + sc_ref[...]) + sh_ref[...]).astype(BF16)
    for j in range(3):
        z_ref[:, j * d:(j + 1) * d] = _dot(h, w_ref[:, j * d:(j + 1) * d]).astype(BF16)
    gate_c = _dot(h, w_ref[:, 3 * d:4 * d])
    xb = _dot(h, w_ref[:, 4 * d:5 * d])
    p_ref[...] = gate_c * xb


def _inproj_ab(x, sh, sc, w_bf16, tm):
    b, l, d = x.shape
    n = w_bf16.shape[1]
    return pl.pallas_call(
        _inproj_ab_kernel,
        grid=(b, l // tm),
        in_specs=[
            pl.BlockSpec((None, tm, d), lambda i, j: (i, j, 0)),
            pl.BlockSpec((None, 1, d), lambda i, j: (i, 0, 0)),
            pl.BlockSpec((None, 1, d), lambda i, j: (i, 0, 0)),
            pl.BlockSpec((d, n), lambda i, j: (0, 0)),
        ],
        out_specs=[
            pl.BlockSpec((None, tm, 3 * d), lambda i, j: (i, j, 0)),
            pl.BlockSpec((None, tm, d), lambda i, j: (i, j, 0)),
        ],
        out_shape=[
            jax.ShapeDtypeStruct((b, l, 3 * d), BF16),
            jax.ShapeDtypeStruct((b, l, d), F32),
        ],
        compiler_params=_cparams(("arbitrary", "arbitrary")),
        name="inproj_ab",
    )(x, sh, sc, w_bf16)


def _mix_ab_kernel(z_ref, p_ref, pprev_ref, pnext_ref, x_ref, g1_ref, sh2_ref, sc2_ref,
                   sgun_ref, sguw_ref, sgub_ref, convw_ref, wout_ref, wr_ref, br_ref,
                   xl_ref, xm_ref, rt_ref, pbuf, ycat):
    i = pl.program_id(1)
    nt = pl.num_programs(1)
    tm, d = x_ref.shape
    cg = d // G_A
    v = jax.nn.gelu(z_ref[:, d:2 * d].astype(F32))
    vc = (_rms(v) * sgun_ref[...]).astype(BF16)
    for c in range(tm // CHUNK):
        r0 = c * CHUNK
        cols = [_dot(sguw_ref[g], vc[r0:r0 + CHUNK, g * cg:(g + 1) * cg]) for g in range(G_A)]
        s = jnp.concatenate(cols, axis=1) + sgub_ref[...]
        u = jax.nn.gelu(z_ref[r0:r0 + CHUNK, 0:d].astype(F32))
        ycat[r0:r0 + CHUNK, 0:d] = (u * s).astype(BF16)
    pbuf[SUBLANE:SUBLANE + tm, :] = p_ref[...]
    pbuf[0:SUBLANE, :] = jnp.where(i > 0, pprev_ref[...], 0.0)
    pbuf[SUBLANE + tm:2 * SUBLANE + tm, :] = jnp.where(i < nt - 1, pnext_ref[...], 0.0)
    conv = (convw_ref[0:1, :] * pbuf[SUBLANE - 1:SUBLANE - 1 + tm, :]
            + convw_ref[1:2, :] * pbuf[SUBLANE:SUBLANE + tm, :]
            + convw_ref[2:3, :] * pbuf[SUBLANE + 1:SUBLANE + 1 + tm, :])
    ycat[:, d:2 * d] = (z_ref[:, 2 * d:3 * d].astype(F32) * conv).astype(BF16)
    y = _dot(ycat[...], wout_ref[...])
    _residual_and_route(x_ref[...], y, g1_ref[...], sh2_ref[...], sc2_ref[...], wr_ref, br_ref,
                        xl_ref, xm_ref, rt_ref)


def _mix_ab(z, p, x, g1, sh2, sc2, sgun, sguw, sgub, convw, wout, wr, br, tm):
    b, l, d = x.shape
    hb = tm // SUBLANE
    nhb = l // SUBLANE
    tok = lambda i, j: (i, j, 0)
    per_b = lambda i, j: (i, 0, 0)
    const2 = lambda i, j: (0, 0)
    return pl.pallas_call(
        _mix_ab_kernel,
        grid=(b, l // tm),
        in_specs=[
            pl.BlockSpec((None, tm, 3 * d), tok),
            pl.BlockSpec((None, tm, d), tok),
            pl.BlockSpec((None, SUBLANE, d), lambda i, j: (i, jnp.maximum(j * hb - 1, 0), 0)),
            pl.BlockSpec((None, SUBLANE, d), lambda i, j: (i, jnp.minimum((j + 1) * hb, nhb - 1), 0)),
            pl.BlockSpec((None, tm, d), tok),
            pl.BlockSpec((None, 1, d), per_b),
            pl.BlockSpec((None, 1, d), per_b),
            pl.BlockSpec((None, 1, d), per_b),
            pl.BlockSpec((1, d), const2),
            pl.BlockSpec((G_A, CHUNK, CHUNK), lambda i, j: (0, 0, 0)),
            pl.BlockSpec((CHUNK, d), const2),
            pl.BlockSpec((CONV_W, d), const2),
            pl.BlockSpec((2 * d, d), const2),
            pl.BlockSpec((d, LANE), const2),
            pl.BlockSpec((1, LANE), const2),
        ],
        out_specs=[
            pl.BlockSpec((None, tm, d), tok),
            pl.BlockSpec((tm * (d // LANE), LANE), lambda i, j: (i * (l // tm) + j, 0)),
            pl.BlockSpec((None, tm, LANE), tok),
        ],
        out_shape=[
            jax.ShapeDtypeStruct((b, l, d), F32),
            jax.ShapeDtypeStruct((b * l * (d // LANE), LANE), F32),
            jax.ShapeDtypeStruct((b, l, LANE), F32),
        ],
        scratch_shapes=[
            pltpu.VMEM((tm + 2 * SUBLANE, d), F32),
            pltpu.VMEM((tm, 2 * d), BF16),
        ],
        compiler_params=_cparams(("arbitrary", "arbitrary")),
        name="mix_ab",
    )(z, p, p, p, x, g1, sh2, sc2, sgun, sguw, sgub, convw, wout, wr, br)


def _store_token_tiles(ref, val):
    tm, d = val.shape
    rpt = d // LANE
    for j in range(rpt):
        ref[pl.ds(j, tm, stride=rpt), :] = val[:, j * LANE:(j + 1) * LANE]


def _load_token_tile_col(ref, tm, rpt, j):
    return ref[pl.ds(j, tm, stride=rpt), :]


def _plan_kernel(rt_ref, tri_ref, pos_ref, te_ref, tv_ref, pad_ref, run, *, tm):
    ps = pl.program_id(0)
    i = pl.program_id(1)
    blk = rt_ref.shape[0]
    ne = N_EXPERTS

    @pl.when(jnp.logical_and(ps == 0, i == 0))
    def _():
        run[...] = jnp.zeros_like(run)

    @pl.when(jnp.logical_and(ps == 1, i == 0))
    def _():
        counts = run[...]
        padded = jnp.floor((counts + (tm - 1)) * (1.0 / tm)) * tm
        r = lax.broadcasted_iota(I32, (ne, ne), 0)
        c = lax.broadcasted_iota(I32, (ne, ne), 1)
        starts = jnp.dot((c < r).astype(F32), padded, precision=HIGHEST, preferred_element_type=F32)
        nt = te_ref.shape[1]
        ends = jnp.concatenate([(starts + padded) * (1.0 / tm)] * (nt // LANE), axis=1)
        t = lax.broadcasted_iota(I32, (ne, nt), 1).astype(F32)
        te = jnp.sum((t >= ends).astype(F32), axis=0, keepdims=True)
        total = jnp.max(ends, axis=0, keepdims=True)
        te_ref[...] = jnp.minimum(te, ne - 1.0).astype(I32)
        tv_ref[...] = (t[0:1, :] < total).astype(I32)
        lane = lax.broadcasted_iota(I32, (ne, LANE), 1)
        pad_ref[...] = jnp.where(lane == 0, starts + counts,
                                 jnp.where(lane == 1, padded - counts, 0.0)).astype(I32)
        run[...] = starts

    slab_t = rt_ref[...].T
    ex = lax.broadcasted_iota(I32, (ne, blk), 0).astype(F32)
    oh1 = ex == slab_t[0:1, :]
    oh2 = ex == slab_t[1:2, :]
    oh = jnp.where(jnp.logical_or(oh1, oh2), 1.0, 0.0)
    prefix = _dot(oh.astype(BF16), tri_ref[...]) + run[:, 0:1]
    pos1 = jnp.sum(jnp.where(oh1, prefix, 0.0), axis=0, keepdims=True)
    pos2 = jnp.sum(jnp.where(oh2, prefix, 0.0), axis=0, keepdims=True)
    pos_ref[...] = jnp.concatenate([pos1, pos2], axis=0).astype(I32)
    run[...] = run[...] + jnp.sum(oh, axis=1, keepdims=True)


def _moe_plan(rt_all, tm):
    n = rt_all.shape[0]
    blk = PLAN_BLOCK
    nb = n // blk
    tri = jnp.triu(jnp.ones((blk, blk), BF16), k=1)
    return pl.pallas_call(
        functools.partial(_plan_kernel, tm=tm),
        grid=(2, nb),
        in_specs=[
            pl.BlockSpec((blk, LANE), lambda p, i: (i, 0)),
            pl.BlockSpec((blk, blk), lambda p, i: (0, 0)),
        ],
        out_specs=[
            pl.BlockSpec((None, TOP_K, blk), lambda p, i: (jnp.where(p == 0, nb, i), 0, 0)),
            pl.BlockSpec((1, MAX_TILES), lambda p, i: (0, 0)),
            pl.BlockSpec((1, MAX_TILES), lambda p, i: (0, 0)),
            pl.BlockSpec((N_EXPERTS, LANE), lambda p, i: (0, 0)),
        ],
        out_shape=[
            jax.ShapeDtypeStruct((nb + 1, TOP_K, blk), I32),
            jax.ShapeDtypeStruct((1, MAX_TILES), I32),
            jax.ShapeDtypeStruct((1, MAX_TILES), I32),
            jax.ShapeDtypeStruct((N_EXPERTS, LANE), I32),
        ],
        scratch_shapes=[pltpu.VMEM((N_EXPERTS, LANE), F32)],
        compiler_params=_cparams(("arbitrary", "arbitrary")),
        name="moe_plan",
    )(rt_all, tri)


def _dispatch_kernel(ps_ref, pl_ref, tv_ref, pos_ref, *rest, blocks, n_tiles):
    xm_refs = rest[:len(blocks)]
    xs_hbm, zbuf, sem = rest[len(blocks):]
    i = pl.program_id(0)
    tt = pos_ref.shape[1]
    rpt = xm_refs[0].shape[0] // tt
    tile_rows = zbuf.shape[0]

    @pl.when(i == 0)
    def _():
        zbuf[...] = jnp.zeros_like(zbuf)
        zrow = zbuf.at[pl.ds(0, rpt)]

        def per_expert(e, carry):
            base = ps_ref[e]
            n = pl_ref[e]

            def start(r, c2):
                pltpu.make_async_copy(zrow, xs_hbm.at[pl.ds((base + r) * rpt, rpt)], sem.at[1]).start()
                return c2
            lax.fori_loop(0, n, start, 0)

            def wait(r, c2):
                pltpu.make_async_copy(zrow, xs_hbm.at[pl.ds(0, rpt)], sem.at[1]).wait()
                return c2
            lax.fori_loop(0, n, wait, 0)
            return carry
        lax.fori_loop(0, N_EXPERTS, per_expert, 0)

        def per_tile(t, carry):
            @pl.when(tv_ref[t] != 1)
            def _():
                dst = xs_hbm.at[pl.ds(pl.multiple_of(t * tile_rows, tile_rows), tile_rows)]
                cp = pltpu.make_async_copy(zbuf, dst, sem.at[1])
                cp.start()
                cp.wait()
            return carry
        lax.fori_loop(0, n_tiles, per_tile, 0)

    def scatter(xm_ref):
        def issue(r, carry):
            src = xm_ref.at[pl.ds(pl.multiple_of(r * rpt, rpt), rpt)]
            for k in range(TOP_K):
                dst = xs_hbm.at[pl.ds(pl.multiple_of(pos_ref[k, r] * rpt, rpt), rpt)]
                pltpu.make_async_copy(src, dst, sem.at[0]).start(priority=k)
            return carry
        lax.fori_loop(0, tt, issue, 0, unroll=8)
        for k in range(TOP_K):
            pltpu.make_async_copy(xm_ref, xs_hbm.at[pl.ds(0, tt * rpt)], sem.at[0]).wait()

    lo = 0
    for xm_ref, nblk in zip(xm_refs, blocks):
        pl.when(jnp.logical_and(i >= lo, i < lo + nblk))(functools.partial(scatter, xm_ref))
        lo += nblk


def _dispatch(pad_start, pad_len, valid, pos, xm_streams, n_tiles, tm):
    tt = pos.shape[-1]
    rpt = SUBLANE
    blocks = tuple(xm.shape[0] // (rpt * tt) for xm in xm_streams)
    in_specs = [pl.BlockSpec((None, TOP_K, tt), lambda i, a, b, c: (i, 0, 0), memory_space=pltpu.SMEM)]
    lo = 0
    for nblk in blocks:
        in_specs.append(pl.BlockSpec(
            (tt * rpt, LANE), lambda i, a, b, c, lo=lo, nblk=nblk: (jnp.clip(i - lo, 0, nblk - 1), 0)))
        lo += nblk
    return pl.pallas_call(
        functools.partial(_dispatch_kernel, blocks=blocks, n_tiles=n_tiles),
        grid_spec=pltpu.PrefetchScalarGridSpec(
            num_scalar_prefetch=3,
            grid=(sum(blocks),),
            in_specs=in_specs,
            out_specs=pl.BlockSpec(memory_space=pl.ANY),
            scratch_shapes=[pltpu.VMEM((tm * rpt, LANE), F32), pltpu.SemaphoreType.DMA((2,))],
        ),
        out_shape=jax.ShapeDtypeStruct((n_tiles * tm * rpt, LANE), F32),
        compiler_params=_cparams(("arbitrary",)),
        name="moe_dispatch",
    )(pad_start, pad_len, valid, pos, *xm_streams)


def _moe_kernel(te_ref, tv_ref, x_ref, wg_ref, wu_ref, wd_ref, y_ref, wg_s, wu_s, wd_s):
    t = pl.program_id(0)
    rpt = wg_ref.shape[0] // LANE
    tm = x_ref.shape[0] // rpt

    @pl.when(tv_ref[t] == 1)
    def _():
        changed = jnp.logical_or(t == 0, te_ref[t] != te_ref[jnp.maximum(t - 1, 0)])

        @pl.when(changed)
        def _():
            wg_s[...] = wg_ref[...].astype(BF16)
            wu_s[...] = wu_ref[...].astype(BF16)
            wd_s[...] = wd_ref[...].astype(BF16)

        x = jnp.concatenate([_load_token_tile_col(x_ref, tm, rpt, j) for j in range(rpt)], axis=1).astype(BF16)
        g = _dot(x, wg_s[...])
        u = _dot(x, wu_s[...])
        h = (g * _sigmoid(g) * u).astype(BF16)
        _store_token_tiles(y_ref, _dot(h, wd_s[...]))

    @pl.when(tv_ref[t] != 1)
    def _():
        y_ref[...] = jnp.zeros_like(y_ref)


def _moe_experts(xs, tile_e, valid, n_tiles, layer, w_gate, w_up, w_down, tm):
    _, _, d, de = w_gate.shape
    rpt = d // LANE
    grid_spec = pltpu.PrefetchScalarGridSpec(
        num_scalar_prefetch=2,
        grid=(n_tiles,),
        in_specs=[
            pl.BlockSpec((tm * rpt, LANE), lambda t, te, tv: (t * tv[t], 0)),
            pl.BlockSpec((None, None, d, de), lambda t, te, tv: (layer, te[t], 0, 0)),
            pl.BlockSpec((None, None, d, de), lambda t, te, tv: (layer, te[t], 0, 0)),
            pl.BlockSpec((None, None, de, d), lambda t, te, tv: (layer, te[t], 0, 0)),
        ],
        out_specs=pl.BlockSpec((tm * rpt, LANE), lambda t, te, tv: (t, 0)),
        scratch_shapes=[
            pltpu.VMEM((d, de), BF16),
            pltpu.VMEM((d, de), BF16),
            pltpu.VMEM((de, d), BF16),
        ],
    )
    return pl.pallas_call(
        _moe_kernel,
        grid_spec=grid_spec,
        out_shape=jax.ShapeDtypeStruct((n_tiles * tm * rpt, LANE), F32),
        compiler_params=_cparams(("arbitrary",)),
        name="moe_experts",
    )(tile_e, valid, xs, w_gate, w_up, w_down)


def _combine_kernel(pos0_ref, posn_ref, x_ref, g_ref, rt_ref, ys_hbm, o_ref, buf, sem, *, n):
    i = pl.program_id(0)
    tt, d = x_ref.shape
    rpt = d // LANE

    def gather(pos_ref, slot):
        def issue(r, carry):
            for k in range(TOP_K):
                src = ys_hbm.at[pl.ds(pl.multiple_of(pos_ref[k, r] * rpt, rpt), rpt)]
                dst = buf.at[slot, k, pl.ds(pl.multiple_of(r * rpt, rpt), rpt)]
                pltpu.make_async_copy(src, dst, sem.at[slot, k]).start(priority=k)
            return carry
        lax.fori_loop(0, tt, issue, 0, unroll=8)

    slot = lax.rem(i, 2)

    @pl.when(i == 0)
    def _():
        gather(pos0_ref, 0)

    @pl.when(i + 1 < n)
    def _():
        gather(posn_ref, 1 - slot)

    for k in range(TOP_K):
        pltpu.make_async_copy(ys_hbm.at[pl.ds(0, tt * rpt)], buf.at[slot, k], sem.at[slot, k]).wait()
    w1 = rt_ref[:, 2:3]
    w2 = rt_ref[:, 3:4]
    for j in range(rpt):
        r1 = buf[slot, 0, pl.ds(j, tt, stride=rpt), :]
        r2 = buf[slot, 1, pl.ds(j, tt, stride=rpt), :]
        sl = slice(j * LANE, (j + 1) * LANE)
        o_ref[:, sl] = x_ref[:, sl] + g_ref[:, sl] * (w1 * r1 + w2 * r2)


def _moe_combine(x, g2, rt, pos, tok_off, ys, tt):
    b, l, d = x.shape
    rpt = d // LANE
    n = b * l
    per_b = l // tt
    per_blk = PLAN_BLOCK // tt
    blk0 = tok_off // PLAN_BLOCK

    def pos_map(step):
        return lambda i: (blk0 + step(i) // per_blk, 0, step(i) % per_blk)

    return pl.pallas_call(
        functools.partial(_combine_kernel, n=n // tt),
        grid=(n // tt,),
        in_specs=[
            pl.BlockSpec((None, TOP_K, tt), pos_map(lambda i: i), memory_space=pltpu.SMEM),
            pl.BlockSpec((None, TOP_K, tt), pos_map(lambda i: jnp.minimum(i + 1, n // tt - 1)),
                         memory_space=pltpu.SMEM),
            pl.BlockSpec((tt, d), lambda i: (i, 0)),
            pl.BlockSpec((None, 1, d), lambda i: (i // per_b, 0, 0)),
            pl.BlockSpec((tt, LANE), lambda i: (i, 0)),
            pl.BlockSpec(memory_space=pl.ANY),
        ],
        out_specs=pl.BlockSpec((tt, d), lambda i: (i, 0)),
        out_shape=jax.ShapeDtypeStruct((n, d), F32),
        scratch_shapes=[
            pltpu.VMEM((2, TOP_K, tt * rpt, LANE), F32),
            pltpu.SemaphoreType.DMA((2, TOP_K)),
        ],
        compiler_params=_cparams(("arbitrary",)),
        name="moe_combine",
    )(pos, pos, x.reshape(n, d), g2, rt.reshape(n, LANE), ys).reshape(b, l, d)


def _hier_moe(streams, layer, w_gate, w_up, w_down):
    sizes = [s[0].shape[0] * s[0].shape[1] for s in streams]
    n = sum(sizes)
    tm = MOE_TILE
    n_tiles = (TOP_K * n) // tm + N_EXPERTS
    assert n_tiles <= MAX_TILES and all(sz % PLAN_BLOCK == 0 for sz in sizes)
    rt_all = jnp.concatenate([s[2].reshape(-1, LANE) for s in streams], axis=0)
    pos, tile_e, valid, pad = _moe_plan(rt_all, tm)
    tile_e, valid = tile_e.reshape(-1), valid.reshape(-1)
    xs = _dispatch(pad[:, 0], pad[:, 1], valid, pos, [s[1] for s in streams], n_tiles, tm)
    ys = _moe_experts(xs, tile_e, valid, n_tiles, layer, w_gate, w_up, w_down, tm)
    outs = []
    off = 0
    for (xl, _, rt, g2), sz in zip(streams, sizes):
        outs.append(_moe_combine(xl, g2, rt, pos, off, ys, COMBINE_TILE))
        off += sz
    return outs


def _inproj_cd_kernel(x_ref, sh_ref, sc_ref, w_ref, u_ref, cq_ref, ckv_ref, kr_ref):
    h = (_rms(x_ref[...]) * (1.0 + sc_ref[...]) + sh_ref[...]).astype(BF16)
    off = 0
    for ref in (u_ref, cq_ref, ckv_ref, kr_ref):
        n = ref.shape[-1]
        ref[...] = _dot(h, w_ref[:, off:off + n]).astype(ref.dtype)
        off += n


def _inproj_cd(x, sh, sc, w_bf16, widths, tm):
    b, l, d = x.shape
    n = w_bf16.shape[1]
    tok = lambda i, j: (i, j, 0)
    return pl.pallas_call(
        _inproj_cd_kernel,
        grid=(b, l // tm),
        in_specs=[
            pl.BlockSpec((None, tm, d), tok),
            pl.BlockSpec((None, 1, d), lambda i, j: (i, 0, 0)),
            pl.BlockSpec((None, 1, d), lambda i, j: (i, 0, 0)),
            pl.BlockSpec((d, n), lambda i, j: (0, 0)),
        ],
        out_specs=[pl.BlockSpec((None, tm, w), tok) for w in widths],
        out_shape=[jax.ShapeDtypeStruct((b, l, w), F32) for w in widths],
        compiler_params=_cparams(("arbitrary", "arbitrary")),
        name="inproj_cd",
    )(x, sh, sc, w_bf16)


def _s5_param_kernel(are_ref, aim_ref, ldt_ref, bre_ref, bim_ref, abre_ref, abim_ref, bbre_ref, bbim_ref):
    a_re = are_ref[...]
    a_im = aim_ref[...]
    dt = jnp.exp(ldt_ref[...])
    mag = jnp.exp(dt * a_re)
    ab_re = mag * jnp.cos(dt * a_im)
    ab_im = mag * jnp.sin(dt * a_im)
    den = a_re * a_re + a_im * a_im
    nr = ab_re - 1.0
    f_re = (nr * a_re + ab_im * a_im) / den
    f_im = (ab_im * a_re - nr * a_im) / den
    abre_ref[...] = ab_re
    abim_ref[...] = ab_im
    b_re = bre_ref[...]
    b_im = bim_ref[...]
    bbre_ref[...] = f_re[None] * b_re - f_im[None] * b_im
    bbim_ref[...] = f_re[None] * b_im + f_im[None] * b_re


def _s5_params(a_re, a_im, log_dt, b_re, b_im):
    nd, g, p = a_re.shape
    c = b_re.shape[-1]
    rows = nd * g * p // LANE
    flat = lambda t: t.reshape(rows, LANE)
    chan_major = lambda t: jnp.moveaxis(t, -1, 0).reshape(c, rows, LANE)
    ldt = jnp.broadcast_to(log_dt[:, :, None], (nd, g, p))
    shapes = [jax.ShapeDtypeStruct((rows, LANE), F32)] * 2 + [jax.ShapeDtypeStruct((c, rows, LANE), F32)] * 2
    ab_re, ab_im, bb_re, bb_im = pl.pallas_call(
        _s5_param_kernel, out_shape=shapes, name="s5_params",
    )(flat(a_re), flat(a_im), flat(ldt), chan_major(b_re), chan_major(b_im))
    unflat = lambda t: jnp.moveaxis(t.reshape(c, nd, g, p), 0, -1)
    return ab_re.reshape(nd, g, p), ab_im.reshape(nd, g, p), unflat(bb_re), unflat(bb_im)


def _s5_kernel(uf_ref, ub_ref, a_ref, bsb_ref, csb_ref, yf_ref, yb_ref, hf, hb, cf, cb):
    nb = cf.shape[0]
    steps = hf.shape[0] // nb
    n_super = bsb_ref.shape[1]
    cw = bsb_ref.shape[2]
    sw = bsb_ref.shape[3]
    half = sw // 2

    @pl.when(pl.program_id(0) == 0)
    def _():
        cf[...] = jnp.zeros_like(cf)
        cb[...] = jnp.zeros_like(cb)

    for s in range(n_super):
        hf[:, s * sw:(s + 1) * sw] = _dot(uf_ref[:, s * cw:(s + 1) * cw].astype(BF16), bsb_ref[0, s])
        hb[:, s * sw:(s + 1) * sw] = _dot(ub_ref[:, s * cw:(s + 1) * cw].astype(BF16), bsb_ref[1, s])

    for s in range(n_super):
        re = slice(s * sw, s * sw + half)
        im = slice(s * sw + half, (s + 1) * sw)
        st = slice(s * half, (s + 1) * half)
        arf, aif = a_ref[0, 0, :, st], a_ref[0, 1, :, st]
        arb, aib = a_ref[1, 0, :, st], a_ref[1, 1, :, st]

        def step(t, carry):
            hfr, hfi, hbr, hbi = carry
            rf = pl.multiple_of(t * nb, nb)
            nfr = arf * hfr - aif * hfi + hf[pl.ds(rf, nb), re]
            nfi = arf * hfi + aif * hfr + hf[pl.ds(rf, nb), im]
            hf[pl.ds(rf, nb), re] = nfr
            hf[pl.ds(rf, nb), im] = nfi
            rb = pl.multiple_of((steps - 1 - t) * nb, nb)
            nbr = arb * hbr - aib * hbi + hb[pl.ds(rb, nb), re]
            nbi = arb * hbi + aib * hbr + hb[pl.ds(rb, nb), im]
            hb[pl.ds(rb, nb), re] = nbr
            hb[pl.ds(rb, nb), im] = nbi
            return nfr, nfi, nbr, nbi

        out = lax.fori_loop(0, steps, step, (cf[:, re], cf[:, im], cb[:, re], cb[:, im]), unroll=4)
        cf[:, re], cf[:, im], cb[:, re], cb[:, im] = out

    for s in range(n_super):
        yf_ref[:, s * cw:(s + 1) * cw] = _dot(hf[:, s * sw:(s + 1) * sw].astype(BF16), csb_ref[0, s])
        yb_ref[:, s * cw:(s + 1) * cw] = _dot(hb[:, s * sw:(s + 1) * sw].astype(BF16), csb_ref[1, s])


def _s5_scan(u_tm, n_ctx_blocks, a_bc, bsb, csb, nb):
    rows, c = u_tm.shape
    blk = S5_STEPS * nb
    n_blocks = rows // blk
    n_state = bsb.shape[1] * bsb.shape[3]
    fwd = lambda i: (i, 0)
    bwd = lambda i: (jnp.where(i < n_ctx_blocks, n_ctx_blocks - 1 - i, n_blocks - 1 - (i - n_ctx_blocks)), 0)
    return pl.pallas_call(
        _s5_kernel,
        grid=(n_blocks,),
        in_specs=[
            pl.BlockSpec((blk, c), fwd),
            pl.BlockSpec((blk, c), bwd),
            pl.BlockSpec(a_bc.shape, lambda i: (0, 0, 0, 0)),
            pl.BlockSpec(bsb.shape, lambda i: (0, 0, 0, 0)),
            pl.BlockSpec(csb.shape, lambda i: (0, 0, 0, 0)),
        ],
        out_specs=[pl.BlockSpec((blk, c), fwd), pl.BlockSpec((blk, c), bwd)],
        out_shape=[jax.ShapeDtypeStruct((rows, c), F32)] * 2,
        scratch_shapes=[
            pltpu.VMEM((blk, n_state), F32),
            pltpu.VMEM((blk, n_state), F32),
            pltpu.VMEM((nb, n_state), F32),
            pltpu.VMEM((nb, n_state), F32),
        ],
        compiler_params=_cparams(("arbitrary",)),
        name="s5_scan",
    )(u_tm, u_tm, a_bc, bsb, csb)


def _s5_block_matrices(ab_re, ab_im, bb_re, bb_im, c_re, c_im, nb):
    nd, g, p = ab_re.shape
    c = bb_re.shape[-1]
    ns = g // S5_SUPER
    eye = jnp.eye(S5_SUPER, dtype=F32)

    def in_mat(bb):
        t = bb.reshape(nd, ns, S5_SUPER, p, c)
        return jnp.einsum('dsgpc,gh->dsgchp', t, eye).reshape(nd, ns, S5_SUPER * c, S5_SUPER * p)

    def out_mat(cc):
        t = cc.reshape(nd, ns, S5_SUPER, c, p)
        return jnp.einsum('dsgcp,gh->dsgphc', t, eye).reshape(nd, ns, S5_SUPER * p, S5_SUPER * c)

    bsb = jnp.concatenate([in_mat(bb_re), in_mat(bb_im)], axis=-1).astype(BF16)
    csb = jnp.concatenate([out_mat(c_re), out_mat(-c_im)], axis=-2).astype(BF16)
    a_bc = jnp.stack([ab_re.reshape(nd, g * p), ab_im.reshape(nd, g * p)], axis=1)
    a_bc = jnp.broadcast_to(a_bc[:, :, None, :], (nd, 2, nb, g * p))
    return a_bc, bsb, csb


def _rope(x, cos, sin_a, sin_b):
    return (x * cos + pltpu.roll(x, HEAD_PAD - AXIS_PAIRS, axis=1) * sin_a
            + pltpu.roll(x, AXIS_PAIRS, axis=1) * sin_b)


def _head_norm(x, gain):
    return x * lax.rsqrt(jnp.sum(x * x, axis=-1, keepdims=True) * (1.0 / QK_HEAD) + EPS) * gain


def _mla_prep_kernel(*refs, with_q, with_rope):
    if with_q:
        cq_ref, qn_ref, wq_ref, qg_ref = refs[:4]
        refs = refs[4:]
    ckv_ref, krp_ref, kvn_ref, wk_ref, wv_ref, kg_ref = refs[:6]
    refs = refs[6:]
    if with_rope:
        cos_ref, sa_ref, sb_ref = refs[:3]
        refs = refs[3:]
        cos, sa, sb = cos_ref[...], sa_ref[...], sb_ref[...]
    if with_q:
        q_ref, k_ref, v_ref = refs
    else:
        k_ref, v_ref = refs
    ckvn = (_rms(ckv_ref[...]) * kvn_ref[...]).astype(BF16)
    v_ref[...] = _dot(ckvn, wv_ref[...]).astype(v_ref.dtype)
    kn = _dot(ckvn, wk_ref[...])
    krp = krp_ref[...]
    for h in range(N_HEADS):
        sl = slice(h * HEAD_PAD, (h + 1) * HEAD_PAD)
        kh = _head_norm(kn[:, sl] + krp, kg_ref[...])
        if with_rope:
            kh = _rope(kh, cos, sa, sb)
        k_ref[:, sl] = kh.astype(k_ref.dtype)
    if with_q:
        cqn = (_rms(cq_ref[...]) * qn_ref[...]).astype(BF16)
        qn = _dot(cqn, wq_ref[...])
        for h in range(N_HEADS):
            sl = slice(h * HEAD_PAD, (h + 1) * HEAD_PAD)
            qh = _head_norm(qn[:, sl], qg_ref[...])
            if with_rope:
                qh = _rope(qh, cos, sa, sb)
            q_ref[:, sl] = (qh * (QK_HEAD ** -0.5)).astype(q_ref.dtype)


def _mla_prep(cq, ckv, krp, q_norm, wq, q_gain, kv_norm, wk, wv, k_gain, tables, tm):
    b, l, _ = ckv.shape
    with_q = cq is not None
    with_rope = tables is not None
    tok = lambda i, j: (i, j, 0)
    const = lambda i, j: (0, 0)
    args, specs = [], []

    def add(arr, spec):
        args.append(arr)
        specs.append(spec)

    if with_q:
        add(cq, pl.BlockSpec((None, tm, cq.shape[-1]), tok))
        add(q_norm, pl.BlockSpec(q_norm.shape, const))
        add(wq, pl.BlockSpec(wq.shape, const))
        add(q_gain, pl.BlockSpec(q_gain.shape, const))
    add(ckv, pl.BlockSpec((None, tm, ckv.shape[-1]), tok))
    add(krp, pl.BlockSpec((None, tm, HEAD_PAD), tok))
    add(kv_norm, pl.BlockSpec(kv_norm.shape, const))
    add(wk, pl.BlockSpec(wk.shape, const))
    add(wv, pl.BlockSpec(wv.shape, const))
    add(k_gain, pl.BlockSpec(k_gain.shape, const))
    if with_rope:
        for tbl in tables:
            add(tbl, pl.BlockSpec((tm, HEAD_PAD), lambda i, j: (j, 0)))
    hk = N_HEADS * HEAD_PAD
    hv = wv.shape[1]
    out_specs = [pl.BlockSpec((None, tm, hk), tok), pl.BlockSpec((None, tm, hv), tok)]
    out_shape = [jax.ShapeDtypeStruct((b, l, hk), BF16), jax.ShapeDtypeStruct((b, l, hv), BF16)]
    if with_q:
        out_specs = [pl.BlockSpec((None, tm, hk), tok)] + out_specs
        out_shape = [jax.ShapeDtypeStruct((b, l, hk), BF16)] + out_shape
    return pl.pallas_call(
        functools.partial(_mla_prep_kernel, with_q=with_q, with_rope=with_rope),
        grid=(b, l // tm),
        in_specs=specs,
        out_specs=out_specs,
        out_shape=out_shape,
        compiler_params=_cparams(("arbitrary", "arbitrary")),
        name="mla_prep_q" if with_q else "mla_prep_ctx",
    )(*args)


def _attn_kernel(q_ref, kl_ref, kc_ref, vl_ref, vc_ref, o_ref):
    nt = (((1,), (1,)), ((), ()))
    outs = []
    for hh in range(2):
        sl = slice(hh * HEAD_PAD, (hh + 1) * HEAD_PAD)
        q = q_ref[:, sl]
        s_l = lax.dot_general(q, kl_ref[:, sl], nt, preferred_element_type=F32)
        s_c = lax.dot_general(q, kc_ref[:, sl], nt, preferred_element_type=F32)
        m = jnp.maximum(jnp.max(s_l, axis=-1, keepdims=True), jnp.max(s_c, axis=-1, keepdims=True))
        p_l = jnp.exp(s_l - m)
        p_c = jnp.exp(s_c - m)
        den = jnp.sum(p_l, axis=-1, keepdims=True) + jnp.sum(p_c, axis=-1, keepdims=True)
        o = _dot(p_l.astype(BF16), vl_ref[...]) + _dot(p_c.astype(BF16), vc_ref[...])
        outs.append(o / den)
    lane = lax.broadcasted_iota(I32, outs[0].shape, 1)
    o_ref[...] = jnp.where(lane < V_HEAD, outs[0], outs[1]).astype(o_ref.dtype)


def _attention(q, k_l, k_c, v_l, v_c, tq):
    b, l, _ = q.shape
    lc = k_c.shape[1]
    hp = N_HEADS // 2
    kw = 2 * HEAD_PAD
    vw = 2 * V_HEAD
    return pl.pallas_call(
        _attn_kernel,
        grid=(b, hp, l // tq),
        in_specs=[
            pl.BlockSpec((None, tq, kw), lambda i, h, j: (i, j, h)),
            pl.BlockSpec((None, l, kw), lambda i, h, j: (i, 0, h)),
            pl.BlockSpec((None, lc, kw), lambda i, h, j: (i, 0, h)),
            pl.BlockSpec((None, l, vw), lambda i, h, j: (i, 0, h)),
            pl.BlockSpec((None, lc, vw), lambda i, h, j: (i, 0, h)),
        ],
        out_specs=pl.BlockSpec((None, tq, vw), lambda i, h, j: (i, j, h)),
        out_shape=jax.ShapeDtypeStruct((b, l, N_HEADS * V_HEAD), BF16),
        compiler_params=_cparams(("arbitrary", "arbitrary", "arbitrary")),
        name="mla_attention",
    )(q, k_l, k_c, v_l, v_c)


def _mix_cd_kernel(yf_ref, yb_ref, u_ref, o_ref, x_ref, g1_ref, sh2_ref, sc2_ref,
                   dskip_ref, wglu_ref, wout_ref, wr_ref, br_ref, xl_ref, xm_ref, rt_ref):
    dc = u_ref.shape[-1]
    y = yf_ref[...] + yb_ref[...] + u_ref[...] * dskip_ref[...]
    g = jax.nn.gelu(y)
    s5 = (g * _sigmoid(_dot(g.astype(BF16), wglu_ref[...]))).astype(BF16)
    y_mix = _dot(s5, wout_ref[0:dc, :]) + _dot(o_ref[...], wout_ref[dc:, :])
    _residual_and_route(x_ref[...], y_mix, g1_ref[...], sh2_ref[...], sc2_ref[...], wr_ref, br_ref,
                        xl_ref, xm_ref, rt_ref)


def _mix_cd(yf, yb, u, o, x, g1, sh2, sc2, dskip, wglu, wout, wr, br, tm):
    b, l, d = x.shape
    dc = u.shape[-1]
    tok = lambda i, j: (i, j, 0)
    per_b = lambda i, j: (i, 0, 0)
    const2 = lambda i, j: (0, 0)
    return pl.pallas_call(
        _mix_cd_kernel,
        grid=(b, l // tm),
        in_specs=[
            pl.BlockSpec((None, tm, dc), tok),
            pl.BlockSpec((None, tm, dc), tok),
            pl.BlockSpec((None, tm, dc), tok),
            pl.BlockSpec((None, tm, o.shape[-1]), tok),
            pl.BlockSpec((None, tm, d), tok),
            pl.BlockSpec((None, 1, d), per_b),
            pl.BlockSpec((None, 1, d), per_b),
            pl.BlockSpec((None, 1, d), per_b),
            pl.BlockSpec((1, dc), const2),
            pl.BlockSpec(wglu.shape, const2),
            pl.BlockSpec(wout.shape, const2),
            pl.BlockSpec((d, LANE), const2),
            pl.BlockSpec((1, LANE), const2),
        ],
        out_specs=[
            pl.BlockSpec((None, tm, d), tok),
            pl.BlockSpec((tm * (d // LANE), LANE), lambda i, j: (i * (l // tm) + j, 0)),
            pl.BlockSpec((None, tm, LANE), tok),
        ],
        out_shape=[
            jax.ShapeDtypeStruct((b, l, d), F32),
            jax.ShapeDtypeStruct((b * l * (d // LANE), LANE), F32),
            jax.ShapeDtypeStruct((b, l, LANE), F32),
        ],
        compiler_params=_cparams(("arbitrary", "arbitrary")),
        name="mix_cd",
    )(yf, yb, u, o, x, g1, sh2, sc2, dskip, wglu, wout, wr, br)


def _router_weights(w_grp, b_grp, w_exp, b_exp):
    d = w_grp.shape[0]
    pad = LANE - N_GROUPS - N_EXPERTS
    wr = jnp.concatenate([w_grp, w_exp, jnp.zeros((d, pad), F32)], axis=1)
    br = jnp.concatenate([b_grp, b_exp, jnp.zeros((pad,), F32)]).reshape(1, LANE)
    return wr, br


def _pad_heads(w, width):
    k = w.shape[0]
    w = w.reshape(k, N_HEADS, width)
    return jnp.pad(w, ((0, 0), (0, 0), (0, HEAD_PAD - width))).reshape(k, N_HEADS * HEAD_PAD)


def _rope_tables(l):
    rows = l // GRID_W
    row = jnp.repeat(jnp.arange(rows, dtype=F32), GRID_W)
    col = jnp.tile(jnp.arange(GRID_W, dtype=F32), rows)
    inv_freq = ROPE_BASE ** (-jnp.arange(AXIS_PAIRS, dtype=F32) / AXIS_PAIRS)
    ar, ac = row[:, None] * inv_freq, col[:, None] * inv_freq
    one = jnp.ones((l, QK_NOPE), F32)
    zn = jnp.zeros((l, QK_NOPE), F32)
    zp = jnp.zeros((l, HEAD_PAD - QK_HEAD), F32)
    z8 = jnp.zeros((l, AXIS_PAIRS), F32)
    cos = jnp.concatenate([one, jnp.cos(ar), jnp.cos(ar), jnp.cos(ac), jnp.cos(ac), zp], axis=1)
    sin_a = jnp.concatenate([zn, -jnp.sin(ar), z8, -jnp.sin(ac), z8, zp], axis=1)
    sin_b = jnp.concatenate([zn, z8, jnp.sin(ar), z8, jnp.sin(ac), zp], axis=1)
    return cos, sin_a, sin_b


def _split_mod(mod_row, b, d):
    parts = jnp.split(mod_row, 6, axis=-1)
    return [jnp.broadcast_to(p.reshape(-1, 1, d), (b, 1, d)) for p in parts]


def kernel(x, c, ctx, c_ctx, ada_w, ada_b, ab_w_in, sgu_norm, sgu_w, sgu_b, conv_w, ab_w_out, cd_w_in, s5_a_re, s5_a_im, s5_log_dt, s5_b_re, s5_b_im, s5_c_re, s5_c_im, s5_d, s5_w_glu, mla_q_norm, mla_kv_norm, mla_w_uq, mla_w_uk, mla_w_uv, mla_qn_gain, mla_kn_gain, cd_w_out, moe_w_grp, moe_b_grp, moe_w_exp, moe_b_exp, moe_w_gate, moe_w_up, moe_w_down):
    b, l, d = x.shape
    lc = ctx.shape[1]
    depth = ada_w.shape[0]
    assert depth == 2, "layer 0 = gated-MLP/conv mixers, layer 1 = S5/attention mixers"
    tm = min(TOKEN_TILE, l)
    tmc = min(TOKEN_TILE, lc)

    mod_rows = 2 * SUBLANE
    cvec = jnp.zeros((mod_rows, d), F32).at[:b].set(c).at[b].set(c_ctx)
    mod = _ada(cvec, ada_w, ada_b)

    sh1, sc1, g1, sh2, sc2, g2 = _split_mod(mod[0, :b], b, d)
    csh1, csc1, cg1, csh2, csc2, cg2 = _split_mod(mod[0, b:b + 1], b, d)
    w_in = ab_w_in[0].astype(BF16)
    wr, br = _router_weights(moe_w_grp[0], moe_b_grp[0], moe_w_exp[0], moe_b_exp[0])
    sgub = jnp.repeat(sgu_b[0].T, d // G_A, axis=1)
    mix_args = (sgu_norm[0].reshape(1, d), sgu_w[0].astype(BF16), sgub, conv_w[0],
                ab_w_out[0].astype(BF16), wr, br)
    z, p = _inproj_ab(x, sh1, sc1, w_in, tm)
    xl, xm, rt = _mix_ab(z, p, x, g1, sh2, sc2, *mix_args, tm)
    zc, pc = _inproj_ab(ctx, csh1, csc1, w_in, tmc)
    xc, xmc, rtc = _mix_ab(zc, pc, ctx, cg1, csh2, csc2, *mix_args, tmc)
    xl, xc = _hier_moe([(xl, xm, rt, g2), (xc, xmc, rtc, cg2)], 0, moe_w_gate, moe_w_up, moe_w_down)

    sh1, sc1, g1, sh2, sc2, g2 = _split_mod(mod[1, :b], b, d)
    csh1, csc1, _, _, _, _ = _split_mod(mod[1, b:b + 1], b, d)
    d_c = s5_d.shape[-1]
    q_lora = mla_q_norm.shape[-1]
    kv_lora = mla_kv_norm.shape[-1]
    w_cd = cd_w_in[0]
    o_kr = d_c + q_lora + kv_lora
    w_kr = jnp.pad(w_cd[:, o_kr:], ((0, 0), (QK_NOPE, HEAD_PAD - QK_HEAD)))
    w_cd = jnp.concatenate([w_cd[:, :o_kr], w_kr], axis=1).astype(BF16)
    widths = (d_c, q_lora, kv_lora, HEAD_PAD)
    u_l, cq_l, ckv_l, krp_l = _inproj_cd(xl, sh1, sc1, w_cd, widths, tm)
    u_c, _, ckv_c, krp_c = _inproj_cd(xc, csh1, csc1, w_cd, widths, tmc)

    ab_re, ab_im, bb_re, bb_im = _s5_params(s5_a_re[0], s5_a_im[0], s5_log_dt[0], s5_b_re[0], s5_b_im[0])
    a_bc, bsb, csb = _s5_block_matrices(ab_re, ab_im, bb_re, bb_im, s5_c_re[0], s5_c_im[0], b)
    u_tm = jnp.concatenate([u_c.transpose(1, 0, 2), u_l.transpose(1, 0, 2)], axis=0).reshape((lc + l) * b, d_c)
    yf, yb = _s5_scan(u_tm, lc // S5_STEPS, a_bc, bsb, csb, b)
    to_bm = lambda y: y.reshape(lc + l, b, d_c)[lc:].transpose(1, 0, 2)
    yf, yb = to_bm(yf), to_bm(yb)

    wq = _pad_heads(mla_w_uq[0], QK_HEAD).astype(BF16)
    wk = _pad_heads(mla_w_uk[0], QK_NOPE).astype(BF16)
    wv = mla_w_uv[0].astype(BF16)
    qg = jnp.pad(mla_qn_gain[0], (0, HEAD_PAD - QK_HEAD)).reshape(1, HEAD_PAD)
    kg = jnp.pad(mla_kn_gain[0], (0, HEAD_PAD - QK_HEAD)).reshape(1, HEAD_PAD)
    qn = mla_q_norm[0].reshape(1, q_lora)
    kvn = mla_kv_norm[0].reshape(1, kv_lora)
    q_l, k_l, v_l = _mla_prep(cq_l, ckv_l, krp_l, qn, wq, qg, kvn, wk, wv, kg, _rope_tables(l), tm)
    k_c, v_c = _mla_prep(None, ckv_c, krp_c, None, None, None, kvn, wk, wv, kg, None, tmc)
    o_l = _attention(q_l, k_l, k_c, v_l, v_c, tm)

    wr, br = _router_weights(moe_w_grp[1], moe_b_grp[1], moe_w_exp[1], moe_b_exp[1])
    xl, xm, rt = _mix_cd(yf, yb, u_l, o_l, xl, g1, sh2, sc2, s5_d[0].reshape(1, d_c),
                         s5_w_glu[0].astype(BF16), cd_w_out[0].astype(BF16), wr, br, tm)
    (xl,) = _hier_moe([(xl, xm, rt, g2)], 1, moe_w_gate, moe_w_up, moe_w_down)
    return xl
```

```python
import functools
import math

import jax
import jax.numpy as jnp
from jax import lax
from jax.experimental import pallas as pl
from jax.experimental.pallas import tpu as pltpu

F32 = jnp.float32
BF16 = jnp.bfloat16
I32 = jnp.int32
HIGHEST = lax.Precision.HIGHEST

EPS = 1e-6
GRID_W = 64
CHUNK = 128
G_A = 8
CONV_W = 3
S5_GROUP = 16
S5_STATE = 64
N_HEADS = 8
QK_NOPE = 64
QK_ROPE = 32
QK_HEAD = QK_NOPE + QK_ROPE
V_HEAD = 64
AXIS_PAIRS = QK_ROPE // 4
ROPE_BASE = 10000.0
N_GROUPS = 4
EXPERTS_PER_GROUP = 8
N_EXPERTS = N_GROUPS * EXPERTS_PER_GROUP
TOP_K = 2

LANE = 128
SUBLANE = 8
HEAD_PAD = LANE
TOKEN_TILE = 512
MOE_TILE = 512
COMBINE_TILE = 256
PLAN_BLOCK = 1024
MAX_TILES = 256
ATTN_Q_TILE = 256
ATTN_HEADS = 8
S5_STEPS = 64
S5_SUPER = LANE // S5_GROUP
VMEM_LIMIT = 56 * 1024 * 1024


def _cparams(sem):
    return pltpu.CompilerParams(dimension_semantics=sem, vmem_limit_bytes=VMEM_LIMIT)


def _rms(x):
    return x * lax.rsqrt(jnp.mean(x * x, axis=-1, keepdims=True) + EPS)


def _sigmoid(x):
    return 1.0 / (1.0 + jnp.exp(-x))


def _dot(a, b):
    return jnp.dot(a, b, preferred_element_type=F32)


def _ada_kernel(c_ref, w_ref, b_ref, o_ref):
    c = c_ref[...]
    s = c * _sigmoid(c)
    o_ref[...] = jnp.dot(s, w_ref[...], precision=HIGHEST, preferred_element_type=F32) + b_ref[...]


def _ada(cvec, ada_w, ada_b):
    depth, d, n = ada_w.shape
    rows = cvec.shape[0]
    tn = 1024
    return pl.pallas_call(
        _ada_kernel,
        grid=(depth, n // tn),
        in_specs=[
            pl.BlockSpec((rows, d), lambda i, j: (0, 0)),
            pl.BlockSpec((None, d, tn), lambda i, j: (i, 0, j)),
            pl.BlockSpec((None, 1, tn), lambda i, j: (i, 0, j)),
        ],
        out_specs=pl.BlockSpec((None, rows, tn), lambda i, j: (i, 0, j)),
        out_shape=jax.ShapeDtypeStruct((depth, rows, n), F32),
        compiler_params=_cparams(("arbitrary", "arbitrary")),
        name="ada_mod",
    )(cvec, ada_w, ada_b.reshape(depth, 1, n))


def _route(xm, wr_ref, br_ref):
    logits = jnp.dot(xm, wr_ref[...], precision=HIGHEST, preferred_element_type=F32) + br_ref[...]
    lane = lax.broadcasted_iota(I32, logits.shape, 1).astype(F32)
    neg = jnp.float32(-1e30)
    big = jnp.float32(1e6)
    is_grp = lane < N_GROUPS
    gl = jnp.where(is_grp, logits, neg)
    gmax = jnp.max(gl, axis=-1, keepdims=True)
    gsum = jnp.sum(jnp.where(is_grp, jnp.exp(gl - gmax), 0.0), axis=-1, keepdims=True)
    p_top = 1.0 / gsum
    grp = jnp.min(jnp.where(gl == gmax, lane, big), axis=-1, keepdims=True)
    eidx = lane - N_GROUPS
    in_grp = (eidx >= grp * EXPERTS_PER_GROUP) & (eidx < (grp + 1.0) * EXPERTS_PER_GROUP)
    el = jnp.where(in_grp, logits, neg)
    m1 = jnp.max(el, axis=-1, keepdims=True)
    i1 = jnp.min(jnp.where(el == m1, lane, big), axis=-1, keepdims=True)
    el2 = jnp.where(lane == i1, neg, el)
    m2 = jnp.max(el2, axis=-1, keepdims=True)
    i2 = jnp.min(jnp.where(el2 == m2, lane, big), axis=-1, keepdims=True)
    t = jnp.exp(m2 - m1)
    w1 = p_top / (1.0 + t)
    w2 = p_top * t / (1.0 + t)
    e1 = i1 - N_GROUPS
    e2 = i2 - N_GROUPS
    return jnp.where(lane == 0, e1, jnp.where(lane == 1, e2, jnp.where(lane == 2, w1, jnp.where(lane == 3, w2, 0.0))))


def _residual_and_route(x, y, g1, sh2, sc2, wr_ref, br_ref, xl_ref, xm_ref, rt_ref):
    xl = x + g1 * y
    xl_ref[...] = xl
    xm = _rms(xl) * (1.0 + sc2) + sh2
    _store_token_tiles(xm_ref, xm)
    rt_ref[...] = _route(xm, wr_ref, br_ref)


def _inproj_ab_kernel(x_ref, sh_ref, sc_ref, w_ref, z_ref, p_ref):
    d = x_ref.shape[-1]
    h = (_rms(x_ref[...]) * (1.0 + sc_ref[...]) + sh_ref[...]).astype(BF16)
    for j in range(3):
        z_ref[:, j * d:(j + 1) * d] = _dot(h, w_ref[:, j * d:(j + 1) * d]).astype(BF16)
    gate_c = _dot(h, w_ref[:, 3 * d:4 * d])
    xb = _dot(h, w_ref[:, 4 * d:5 * d])
    p_ref[...] = gate_c * xb


def _inproj_ab(x, sh, sc, w_bf16, tm):
    b, l, d = x.shape
    n = w_bf16.shape[1]
    return pl.pallas_call(
        _inproj_ab_kernel,
        grid=(b, l // tm),
        in_specs=[
            pl.BlockSpec((None, tm, d), lambda i, j: (i, j, 0)),
            pl.BlockSpec((None, 1, d), lambda i, j: (i, 0, 0)),
            pl.BlockSpec((None, 1, d), lambda i, j: (i, 0, 0)),
            pl.BlockSpec((d, n), lambda i, j: (0, 0)),
        ],
        out_specs=[
            pl.BlockSpec((None, tm, 3 * d), lambda i, j: (i, j, 0)),
            pl.BlockSpec((None, tm, d), lambda i, j: (i, j, 0)),
        ],
        out_shape=[
            jax.ShapeDtypeStruct((b, l, 3 * d), BF16),
            jax.ShapeDtypeStruct((b, l, d), F32),
        ],
        compiler_params=_cparams(("arbitrary", "arbitrary")),
        name="inproj_ab",
    )(x, sh, sc, w_bf16)


def _mix_ab_kernel(z_ref, p_ref, pprev_ref, pnext_ref, x_ref, g1_ref, sh2_ref, sc2_ref,
                   sgun_ref, sguw_ref, sgub_ref, convw_ref, wout_ref, wr_ref, br_ref,
                   xl_ref, xm_ref, rt_ref, pbuf, ycat):
    i = pl.program_id(1)
    nt = pl.num_programs(1)
    tm, d = x_ref.shape
    cg = d // G_A
    v = jax.nn.gelu(z_ref[:, d:2 * d].astype(F32))
    vc = (_rms(v) * sgun_ref[...]).astype(BF16)
    for c in range(tm // CHUNK):
        r0 = c * CHUNK
        cols = [_dot(sguw_ref[g], vc[r0:r0 + CHUNK, g * cg:(g + 1) * cg]) for g in range(G_A)]
        s = jnp.concatenate(cols, axis=1) + sgub_ref[...]
        u = jax.nn.gelu(z_ref[r0:r0 + CHUNK, 0:d].astype(F32))
        ycat[r0:r0 + CHUNK, 0:d] = (u * s).astype(BF16)
    pbuf[SUBLANE:SUBLANE + tm, :] = p_ref[...]
    pbuf[0:SUBLANE, :] = jnp.where(i > 0, pprev_ref[...], 0.0)
    pbuf[SUBLANE + tm:2 * SUBLANE + tm, :] = jnp.where(i < nt - 1, pnext_ref[...], 0.0)
    conv = (convw_ref[0:1, :] * pbuf[SUBLANE - 1:SUBLANE - 1 + tm, :]
            + convw_ref[1:2, :] * pbuf[SUBLANE:SUBLANE + tm, :]
            + convw_ref[2:3, :] * pbuf[SUBLANE + 1:SUBLANE + 1 + tm, :])
    ycat[:, d:2 * d] = (z_ref[:, 2 * d:3 * d].astype(F32) * conv).astype(BF16)
    y = _dot(ycat[...], wout_ref[...])
    _residual_and_route(x_ref[...], y, g1_ref[...], sh2_ref[...], sc2_ref[...], wr_ref, br_ref,
                        xl_ref, xm_ref, rt_ref)


def _mix_ab(z, p, x, g1, sh2, sc2, sgun, sguw, sgub, convw, wout, wr, br, tm):
    b, l, d = x.shape
    hb = tm // SUBLANE
    nhb = l // SUBLANE
    tok = lambda i, j: (i, j, 0)
    per_b = lambda i, j: (i, 0, 0)
    const2 = lambda i, j: (0, 0)
    return pl.pallas_call(
        _mix_ab_kernel,
        grid=(b, l // tm),
        in_specs=[
            pl.BlockSpec((None, tm, 3 * d), tok),
            pl.BlockSpec((None, tm, d), tok),
            pl.BlockSpec((None, SUBLANE, d), lambda i, j: (i, jnp.maximum(j * hb - 1, 0), 0)),
            pl.BlockSpec((None, SUBLANE, d), lambda i, j: (i, jnp.minimum((j + 1) * hb, nhb - 1), 0)),
            pl.BlockSpec((None, tm, d), tok),
            pl.BlockSpec((None, 1, d), per_b),
            pl.BlockSpec((None, 1, d), per_b),
            pl.BlockSpec((None, 1, d), per_b),
            pl.BlockSpec((1, d), const2),
            pl.BlockSpec((G_A, CHUNK, CHUNK), lambda i, j: (0, 0, 0)),
            pl.BlockSpec((CHUNK, d), const2),
            pl.BlockSpec((CONV_W, d), const2),
            pl.BlockSpec((2 * d, d), const2),
            pl.BlockSpec((d, LANE), const2),
            pl.BlockSpec((1, LANE), const2),
        ],
        out_specs=[
            pl.BlockSpec((None, tm, d), tok),
            pl.BlockSpec((tm * (d // LANE), LANE), lambda i, j: (i * (l // tm) + j, 0)),
            pl.BlockSpec((None, tm, LANE), tok),
        ],
        out_shape=[
            jax.ShapeDtypeStruct((b, l, d), F32),
            jax.ShapeDtypeStruct((b * l * (d // LANE), LANE), F32),
            jax.ShapeDtypeStruct((b, l, LANE), F32),
        ],
        scratch_shapes=[
            pltpu.VMEM((tm + 2 * SUBLANE, d), F32),
            pltpu.VMEM((tm, 2 * d), BF16),
        ],
        compiler_params=_cparams(("arbitrary", "arbitrary")),
        name="mix_ab",
    )(z, p, p, p, x, g1, sh2, sc2, sgun, sguw, sgub, convw, wout, wr, br)


def _store_token_tiles(ref, val):
    tm, d = val.shape
    rpt = d // LANE
    for j in range(rpt):
        ref[pl.ds(j, tm, stride=rpt), :] = val[:, j * LANE:(j + 1) * LANE]


def _load_token_tile_col(ref, tm, rpt, j):
    return ref[pl.ds(j, tm, stride=rpt), :]


def _plan_kernel(rt_ref, tri_ref, pos_ref, te_ref, tv_ref, pad_ref, run, *, tm):
    ps = pl.program_id(0)
    i = pl.program_id(1)
    blk = rt_ref.shape[0]
    ne = N_EXPERTS

    @pl.when(jnp.logical_and(ps == 0, i == 0))
    def _():
        run[...] = jnp.zeros_like(run)

    @pl.when(jnp.logical_and(ps == 1, i == 0))
    def _():
        counts = run[...]
        padded = jnp.floor((counts + (tm - 1)) * (1.0 / tm)) * tm
        r = lax.broadcasted_iota(I32, (ne, ne), 0)
        c = lax.broadcasted_iota(I32, (ne, ne), 1)
        starts = jnp.dot((c < r).astype(F32), padded, precision=HIGHEST, preferred_element_type=F32)
        nt = te_ref.shape[1]
        ends = jnp.concatenate([(starts + padded) * (1.0 / tm)] * (nt // LANE), axis=1)
        t = lax.broadcasted_iota(I32, (ne, nt), 1).astype(F32)
        te = jnp.sum((t >= ends).astype(F32), axis=0, keepdims=True)
        total = jnp.max(ends, axis=0, keepdims=True)
        te_ref[...] = jnp.minimum(te, ne - 1.0).astype(I32)
        tv_ref[...] = (t[0:1, :] < total).astype(I32)
        lane = lax.broadcasted_iota(I32, (ne, LANE), 1)
        pad_ref[...] = jnp.where(lane == 0, starts + counts,
                                 jnp.where(lane == 1, padded - counts, 0.0)).astype(I32)
        run[...] = starts

    slab_t = rt_ref[...].T
    ex = lax.broadcasted_iota(I32, (ne, blk), 0).astype(F32)
    oh1 = ex == slab_t[0:1, :]
    oh2 = ex == slab_t[1:2, :]
    oh = jnp.where(jnp.logical_or(oh1, oh2), 1.0, 0.0)
    prefix = _dot(oh.astype(BF16), tri_ref[...]) + run[:, 0:1]
    pos1 = jnp.sum(jnp.where(oh1, prefix, 0.0), axis=0, keepdims=True)
    pos2 = jnp.sum(jnp.where(oh2, prefix, 0.0), axis=0, keepdims=True)
    pos_ref[...] = jnp.concatenate([pos1, pos2], axis=0).astype(I32)
    run[...] = run[...] + jnp.sum(oh, axis=1, keepdims=True)


def _moe_plan(rt_all, tm):
    n = rt_all.shape[0]
    blk = PLAN_BLOCK
    nb = n // blk
    tri = jnp.triu(jnp.ones((blk, blk), BF16), k=1)
    return pl.pallas_call(
        functools.partial(_plan_kernel, tm=tm),
        grid=(2, nb),
        in_specs=[
            pl.BlockSpec((blk, LANE), lambda p, i: (i, 0)),
            pl.BlockSpec((blk, blk), lambda p, i: (0, 0)),
        ],
        out_specs=[
            pl.BlockSpec((None, TOP_K, blk), lambda p, i: (jnp.where(p == 0, nb, i), 0, 0)),
            pl.BlockSpec((1, MAX_TILES), lambda p, i: (0, 0)),
            pl.BlockSpec((1, MAX_TILES), lambda p, i: (0, 0)),
            pl.BlockSpec((N_EXPERTS, LANE), lambda p, i: (0, 0)),
        ],
        out_shape=[
            jax.ShapeDtypeStruct((nb + 1, TOP_K, blk), I32),
            jax.ShapeDtypeStruct((1, MAX_TILES), I32),
            jax.ShapeDtypeStruct((1, MAX_TILES), I32),
            jax.ShapeDtypeStruct((N_EXPERTS, LANE), I32),
        ],
        scratch_shapes=[pltpu.VMEM((N_EXPERTS, LANE), F32)],
        compiler_params=_cparams(("arbitrary", "arbitrary")),
        name="moe_plan",
    )(rt_all, tri)


def _dispatch_kernel(ps_ref, pl_ref, tv_ref, pos_ref, *rest, blocks, n_tiles):
    xm_refs = rest[:len(blocks)]
    xs_hbm, zbuf, sem = rest[len(blocks):]
    i = pl.program_id(0)
    tt = pos_ref.shape[1]
    rpt = xm_refs[0].shape[0] // tt
    tile_rows = zbuf.shape[0]

    @pl.when(i == 0)
    def _():
        zbuf[...] = jnp.zeros_like(zbuf)

        def per_expert(e, carry):
            base = ps_ref[e]
            n = pl_ref[e]
            bit = tile_rows // (2 * rpt)
            while bit >= 1:
                lower = n & (bit - 1)

                @pl.when((n & bit) != 0)
                def _(lower=lower, bit=bit):
                    dst = xs_hbm.at[pl.ds((base + lower) * rpt, bit * rpt)]
                    cp = pltpu.make_async_copy(zbuf.at[pl.ds(0, bit * rpt)], dst, sem.at[1])
                    cp.start()
                    cp.wait()
                bit //= 2
            return carry
        lax.fori_loop(0, N_EXPERTS, per_expert, 0)

        def per_tile(t, carry):
            @pl.when(tv_ref[t] != 1)
            def _():
                dst = xs_hbm.at[pl.ds(pl.multiple_of(t * tile_rows, tile_rows), tile_rows)]
                cp = pltpu.make_async_copy(zbuf, dst, sem.at[1])
                cp.start()
                cp.wait()
            return carry
        lax.fori_loop(0, n_tiles, per_tile, 0)

    def scatter(xm_ref):
        def issue(r, carry):
            src = xm_ref.at[pl.ds(pl.multiple_of(r * rpt, rpt), rpt)]
            for k in range(TOP_K):
                dst = xs_hbm.at[pl.ds(pl.multiple_of(pos_ref[k, r] * rpt, rpt), rpt)]
                pltpu.make_async_copy(src, dst, sem.at[0]).start(priority=k)
            return carry
        lax.fori_loop(0, tt, issue, 0, unroll=8)
        for k in range(TOP_K):
            pltpu.make_async_copy(xm_ref, xs_hbm.at[pl.ds(0, tt * rpt)], sem.at[0]).wait()

    lo = 0
    for xm_ref, nblk in zip(xm_refs, blocks):
        pl.when(jnp.logical_and(i >= lo, i < lo + nblk))(functools.partial(scatter, xm_ref))
        lo += nblk


def _dispatch(pad_start, pad_len, valid, pos, xm_streams, n_tiles, tm):
    tt = pos.shape[-1]
    rpt = SUBLANE
    blocks = tuple(xm.shape[0] // (rpt * tt) for xm in xm_streams)
    in_specs = [pl.BlockSpec((None, TOP_K, tt), lambda i, a, b, c: (i, 0, 0), memory_space=pltpu.SMEM)]
    lo = 0
    for nblk in blocks:
        in_specs.append(pl.BlockSpec(
            (tt * rpt, LANE), lambda i, a, b, c, lo=lo, nblk=nblk: (jnp.clip(i - lo, 0, nblk - 1), 0)))
        lo += nblk
    return pl.pallas_call(
        functools.partial(_dispatch_kernel, blocks=blocks, n_tiles=n_tiles),
        grid_spec=pltpu.PrefetchScalarGridSpec(
            num_scalar_prefetch=3,
            grid=(sum(blocks),),
            in_specs=in_specs,
            out_specs=pl.BlockSpec(memory_space=pl.ANY),
            scratch_shapes=[pltpu.VMEM((tm * rpt, LANE), F32), pltpu.SemaphoreType.DMA((2,))],
        ),
        out_shape=jax.ShapeDtypeStruct((n_tiles * tm * rpt, LANE), F32),
        compiler_params=_cparams(("arbitrary",)),
        name="moe_dispatch",
    )(pad_start, pad_len, valid, pos, *xm_streams)


def _moe_kernel(te_ref, tv_ref, x_ref, wg_ref, wu_ref, wd_ref, y_ref, wg_s, wu_s, wd_s):
    t = pl.program_id(0)
    rpt = wg_ref.shape[0] // LANE
    tm = x_ref.shape[0] // rpt

    @pl.when(tv_ref[t] == 1)
    def _():
        changed = jnp.logical_or(t == 0, te_ref[t] != te_ref[jnp.maximum(t - 1, 0)])

        @pl.when(changed)
        def _():
            wg_s[...] = wg_ref[...].astype(BF16)
            wu_s[...] = wu_ref[...].astype(BF16)
            wd_s[...] = wd_ref[...].astype(BF16)

        x = jnp.concatenate([_load_token_tile_col(x_ref, tm, rpt, j) for j in range(rpt)], axis=1).astype(BF16)
        g = _dot(x, wg_s[...])
        u = _dot(x, wu_s[...])
        h = (g * _sigmoid(g) * u).astype(BF16)
        _store_token_tiles(y_ref, _dot(h, wd_s[...]))

    @pl.when(tv_ref[t] != 1)
    def _():
        y_ref[...] = jnp.zeros_like(y_ref)


def _moe_experts(xs, tile_e, valid, n_tiles, layer, w_gate, w_up, w_down, tm):
    _, _, d, de = w_gate.shape
    rpt = d // LANE
    grid_spec = pltpu.PrefetchScalarGridSpec(
        num_scalar_prefetch=2,
        grid=(n_tiles,),
        in_specs=[
            pl.BlockSpec((tm * rpt, LANE), lambda t, te, tv: (t * tv[t], 0)),
            pl.BlockSpec((None, None, d, de), lambda t, te, tv: (layer, te[t], 0, 0)),
            pl.BlockSpec((None, None, d, de), lambda t, te, tv: (layer, te[t], 0, 0)),
            pl.BlockSpec((None, None, de, d), lambda t, te, tv: (layer, te[t], 0, 0)),
        ],
        out_specs=pl.BlockSpec((tm * rpt, LANE), lambda t, te, tv: (t, 0)),
        scratch_shapes=[
            pltpu.VMEM((d, de), BF16),
            pltpu.VMEM((d, de), BF16),
            pltpu.VMEM((de, d), BF16),
        ],
    )
    return pl.pallas_call(
        _moe_kernel,
        grid_spec=grid_spec,
        out_shape=jax.ShapeDtypeStruct((n_tiles * tm * rpt, LANE), F32),
        compiler_params=_cparams(("arbitrary",)),
        name="moe_experts",
    )(tile_e, valid, xs, w_gate, w_up, w_down)


def _combine_kernel(pos0_ref, posn_ref, x_ref, g_ref, rt_ref, ys_hbm, o_ref, buf, sem, *, n):
    i = pl.program_id(0)
    tt, d = x_ref.shape
    rpt = d // LANE

    def gather(pos_ref, slot):
        def issue(r, carry):
            for k in range(TOP_K):
                src = ys_hbm.at[pl.ds(pl.multiple_of(pos_ref[k, r] * rpt, rpt), rpt)]
                dst = buf.at[slot, k, pl.ds(pl.multiple_of(r * rpt, rpt), rpt)]
                pltpu.make_async_copy(src, dst, sem.at[slot, k]).start(priority=k)
            return carry
        lax.fori_loop(0, tt, issue, 0, unroll=8)

    slot = lax.rem(i, 2)

    @pl.when(i == 0)
    def _():
        gather(pos0_ref, 0)

    @pl.when(i + 1 < n)
    def _():
        gather(posn_ref, 1 - slot)

    for k in range(TOP_K):
        pltpu.make_async_copy(ys_hbm.at[pl.ds(0, tt * rpt)], buf.at[slot, k], sem.at[slot, k]).wait()
    w1 = rt_ref[:, 2:3]
    w2 = rt_ref[:, 3:4]
    for j in range(rpt):
        r1 = buf[slot, 0, pl.ds(j, tt, stride=rpt), :]
        r2 = buf[slot, 1, pl.ds(j, tt, stride=rpt), :]
        sl = slice(j * LANE, (j + 1) * LANE)
        o_ref[:, sl] = x_ref[:, sl] + g_ref[:, sl] * (w1 * r1 + w2 * r2)


def _moe_combine(x, g2, rt, pos, tok_off, ys, tt):
    b, l, d = x.shape
    rpt = d // LANE
    n = b * l
    per_b = l // tt
    per_blk = PLAN_BLOCK // tt
    blk0 = tok_off // PLAN_BLOCK

    def pos_map(step):
        return lambda i: (blk0 + step(i) // per_blk, 0, step(i) % per_blk)

    return pl.pallas_call(
        functools.partial(_combine_kernel, n=n // tt),
        grid=(n // tt,),
        in_specs=[
            pl.BlockSpec((None, TOP_K, tt), pos_map(lambda i: i), memory_space=pltpu.SMEM),
            pl.BlockSpec((None, TOP_K, tt), pos_map(lambda i: jnp.minimum(i + 1, n // tt - 1)),
                         memory_space=pltpu.SMEM),
            pl.BlockSpec((tt, d), lambda i: (i, 0)),
            pl.BlockSpec((None, 1, d), lambda i: (i // per_b, 0, 0)),
            pl.BlockSpec((tt, LANE), lambda i: (i, 0)),
            pl.BlockSpec(memory_space=pl.ANY),
        ],
        out_specs=pl.BlockSpec((tt, d), lambda i: (i, 0)),
        out_shape=jax.ShapeDtypeStruct((n, d), F32),
        scratch_shapes=[
            pltpu.VMEM((2, TOP_K, tt * rpt, LANE), F32),
            pltpu.SemaphoreType.DMA((2, TOP_K)),
        ],
        compiler_params=_cparams(("arbitrary",)),
        name="moe_combine",
    )(pos, pos, x.reshape(n, d), g2, rt.reshape(n, LANE), ys).reshape(b, l, d)


def _hier_moe(streams, layer, w_gate, w_up, w_down):
    sizes = [s[0].shape[0] * s[0].shape[1] for s in streams]
    n = sum(sizes)
    tm = MOE_TILE
    n_tiles = (TOP_K * n) // tm + N_EXPERTS
    assert n_tiles <= MAX_TILES and all(sz % PLAN_BLOCK == 0 for sz in sizes)
    rt_all = jnp.concatenate([s[2].reshape(-1, LANE) for s in streams], axis=0)
    pos, tile_e, valid, pad = _moe_plan(rt_all, tm)
    tile_e, valid = tile_e.reshape(-1), valid.reshape(-1)
    xs = _dispatch(pad[:, 0], pad[:, 1], valid, pos, [s[1] for s in streams], n_tiles, tm)
    ys = _moe_experts(xs, tile_e, valid, n_tiles, layer, w_gate, w_up, w_down, tm)
    outs = []
    off = 0
    for (xl, _, rt, g2), sz in zip(streams, sizes):
        outs.append(_moe_combine(xl, g2, rt, pos, off, ys, COMBINE_TILE))
        off += sz
    return outs


def _inproj_cd_kernel(x_ref, sh_ref, sc_ref, w_ref, u_ref, cq_ref, ckv_ref, kr_ref):
    h = (_rms(x_ref[...]) * (1.0 + sc_ref[...]) + sh_ref[...]).astype(BF16)
    off = 0
    for ref in (u_ref, cq_ref, ckv_ref, kr_ref):
        n = ref.shape[-1]
        ref[...] = _dot(h, w_ref[:, off:off + n]).astype(ref.dtype)
        off += n


def _inproj_cd(x, sh, sc, w_bf16, widths, tm):
    b, l, d = x.shape
    n = w_bf16.shape[1]
    tok = lambda i, j: (i, j, 0)
    return pl.pallas_call(
        _inproj_cd_kernel,
        grid=(b, l // tm),
        in_specs=[
            pl.BlockSpec((None, tm, d), tok),
            pl.BlockSpec((None, 1, d), lambda i, j: (i, 0, 0)),
            pl.BlockSpec((None, 1, d), lambda i, j: (i, 0, 0)),
            pl.BlockSpec((d, n), lambda i, j: (0, 0)),
        ],
        out_specs=[pl.BlockSpec((None, tm, w), tok) for w in widths],
        out_shape=[jax.ShapeDtypeStruct((b, l, w), F32) for w in widths],
        compiler_params=_cparams(("arbitrary", "arbitrary")),
        name="inproj_cd",
    )(x, sh, sc, w_bf16)


def _s5_param_kernel(are_ref, aim_ref, ldt_ref, bre_ref, bim_ref, abre_ref, abim_ref, bbre_ref, bbim_ref):
    a_re = are_ref[...]
    a_im = aim_ref[...]
    dt = jnp.exp(ldt_ref[...])
    mag = jnp.exp(dt * a_re)
    ab_re = mag * jnp.cos(dt * a_im)
    ab_im = mag * jnp.sin(dt * a_im)
    den = a_re * a_re + a_im * a_im
    nr = ab_re - 1.0
    f_re = (nr * a_re + ab_im * a_im) / den
    f_im = (ab_im * a_re - nr * a_im) / den
    abre_ref[...] = ab_re
    abim_ref[...] = ab_im
    b_re = bre_ref[...]
    b_im = bim_ref[...]
    bbre_ref[...] = f_re[None] * b_re - f_im[None] * b_im
    bbim_ref[...] = f_re[None] * b_im + f_im[None] * b_re


def _s5_params(a_re, a_im, log_dt, b_re, b_im):
    nd, g, p = a_re.shape
    c = b_re.shape[-1]
    rows = nd * g * p // LANE
    flat = lambda t: t.reshape(rows, LANE)
    chan_major = lambda t: jnp.moveaxis(t, -1, 0).reshape(c, rows, LANE)
    ldt = jnp.broadcast_to(log_dt[:, :, None], (nd, g, p))
    shapes = [jax.ShapeDtypeStruct((rows, LANE), F32)] * 2 + [jax.ShapeDtypeStruct((c, rows, LANE), F32)] * 2
    ab_re, ab_im, bb_re, bb_im = pl.pallas_call(
        _s5_param_kernel, out_shape=shapes, name="s5_params",
    )(flat(a_re), flat(a_im), flat(ldt), chan_major(b_re), chan_major(b_im))
    unflat = lambda t: jnp.moveaxis(t.reshape(c, nd, g, p), 0, -1)
    return ab_re.reshape(nd, g, p), ab_im.reshape(nd, g, p), unflat(bb_re), unflat(bb_im)


def _s5_kernel(uf_ref, ub_ref, a_ref, bsb_ref, csb_ref, yf_ref, yb_ref, hf, hb, cf, cb):
    nb = cf.shape[0]
    steps = hf.shape[0] // nb
    n_super = bsb_ref.shape[1]
    cw = bsb_ref.shape[2]
    sw = bsb_ref.shape[3]
    half = sw // 2

    @pl.when(pl.program_id(0) == 0)
    def _():
        cf[...] = jnp.zeros_like(cf)
        cb[...] = jnp.zeros_like(cb)

    for s in range(n_super):
        hf[:, s * sw:(s + 1) * sw] = _dot(uf_ref[:, s * cw:(s + 1) * cw].astype(BF16), bsb_ref[0, s])
        hb[:, s * sw:(s + 1) * sw] = _dot(ub_ref[:, s * cw:(s + 1) * cw].astype(BF16), bsb_ref[1, s])

    for s in range(n_super):
        re = slice(s * sw, s * sw + half)
        im = slice(s * sw + half, (s + 1) * sw)
        st = slice(s * half, (s + 1) * half)
        arf, aif = a_ref[0, 0, :, st], a_ref[0, 1, :, st]
        arb, aib = a_ref[1, 0, :, st], a_ref[1, 1, :, st]

        def step(t, carry):
            hfr, hfi, hbr, hbi = carry
            rf = pl.multiple_of(t * nb, nb)
            nfr = arf * hfr - aif * hfi + hf[pl.ds(rf, nb), re]
            nfi = arf * hfi + aif * hfr + hf[pl.ds(rf, nb), im]
            hf[pl.ds(rf, nb), re] = nfr
            hf[pl.ds(rf, nb), im] = nfi
            rb = pl.multiple_of((steps - 1 - t) * nb, nb)
            nbr = arb * hbr - aib * hbi + hb[pl.ds(rb, nb), re]
            nbi = arb * hbi + aib * hbr + hb[pl.ds(rb, nb), im]
            hb[pl.ds(rb, nb), re] = nbr
            hb[pl.ds(rb, nb), im] = nbi
            return nfr, nfi, nbr, nbi

        out = lax.fori_loop(0, steps, step, (cf[:, re], cf[:, im], cb[:, re], cb[:, im]), unroll=4)
        cf[:, re], cf[:, im], cb[:, re], cb[:, im] = out

    for s in range(n_super):
        yf_ref[:, s * cw:(s + 1) * cw] = _dot(hf[:, s * sw:(s + 1) * sw].astype(BF16), csb_ref[0, s])
        yb_ref[:, s * cw:(s + 1) * cw] = _dot(hb[:, s * sw:(s + 1) * sw].astype(BF16), csb_ref[1, s])


def _s5_scan(u_tm, n_ctx_blocks, a_bc, bsb, csb, nb):
    rows, c = u_tm.shape
    blk = S5_STEPS * nb
    n_blocks = rows // blk
    n_state = bsb.shape[1] * bsb.shape[3]
    fwd = lambda i: (i, 0)
    bwd = lambda i: (jnp.where(i < n_ctx_blocks, n_ctx_blocks - 1 - i, n_blocks - 1 - (i - n_ctx_blocks)), 0)
    return pl.pallas_call(
        _s5_kernel,
        grid=(n_blocks,),
        in_specs=[
            pl.BlockSpec((blk, c), fwd),
            pl.BlockSpec((blk, c), bwd),
            pl.BlockSpec(a_bc.shape, lambda i: (0, 0, 0, 0)),
            pl.BlockSpec(bsb.shape, lambda i: (0, 0, 0, 0)),
            pl.BlockSpec(csb.shape, lambda i: (0, 0, 0, 0)),
        ],
        out_specs=[pl.BlockSpec((blk, c), fwd), pl.BlockSpec((blk, c), bwd)],
        out_shape=[jax.ShapeDtypeStruct((rows, c), F32)] * 2,
        scratch_shapes=[
            pltpu.VMEM((blk, n_state), F32),
            pltpu.VMEM((blk, n_state), F32),
            pltpu.VMEM((nb, n_state), F32),
            pltpu.VMEM((nb, n_state), F32),
        ],
        compiler_params=_cparams(("arbitrary",)),
        name="s5_scan",
    )(u_tm, u_tm, a_bc, bsb, csb)


def _s5_block_matrices(ab_re, ab_im, bb_re, bb_im, c_re, c_im, nb):
    nd, g, p = ab_re.shape
    c = bb_re.shape[-1]
    ns = g // S5_SUPER
    eye = jnp.eye(S5_SUPER, dtype=F32)

    def in_mat(bb):
        t = bb.reshape(nd, ns, S5_SUPER, p, c)
        return jnp.einsum('dsgpc,gh->dsgchp', t, eye).reshape(nd, ns, S5_SUPER * c, S5_SUPER * p)

    def out_mat(cc):
        t = cc.reshape(nd, ns, S5_SUPER, c, p)
        return jnp.einsum('dsgcp,gh->dsgphc', t, eye).reshape(nd, ns, S5_SUPER * p, S5_SUPER * c)

    bsb = jnp.concatenate([in_mat(bb_re), in_mat(bb_im)], axis=-1).astype(BF16)
    csb = jnp.concatenate([out_mat(c_re), out_mat(-c_im)], axis=-2).astype(BF16)
    a_bc = jnp.stack([ab_re.reshape(nd, g * p), ab_im.reshape(nd, g * p)], axis=1)
    a_bc = jnp.broadcast_to(a_bc[:, :, None, :], (nd, 2, nb, g * p))
    return a_bc, bsb, csb


def _rope(x, cos, sin_a, sin_b):
    return (x * cos + pltpu.roll(x, HEAD_PAD - AXIS_PAIRS, axis=1) * sin_a
            + pltpu.roll(x, AXIS_PAIRS, axis=1) * sin_b)


def _head_norm(x, gain):
    return x * lax.rsqrt(jnp.sum(x * x, axis=-1, keepdims=True) * (1.0 / QK_HEAD) + EPS) * gain


def _mla_prep_kernel(*refs, with_q, with_rope):
    if with_q:
        cq_ref, qn_ref, wq_ref, qg_ref = refs[:4]
        refs = refs[4:]
    ckv_ref, krp_ref, kvn_ref, wk_ref, wv_ref, kg_ref = refs[:6]
    refs = refs[6:]
    if with_rope:
        cos_ref, sa_ref, sb_ref = refs[:3]
        refs = refs[3:]
        cos, sa, sb = cos_ref[...], sa_ref[...], sb_ref[...]
    if with_q:
        q_ref, k_ref, v_ref = refs
    else:
        k_ref, v_ref = refs
    ckvn = (_rms(ckv_ref[...]) * kvn_ref[...]).astype(BF16)
    v_ref[...] = _dot(ckvn, wv_ref[...]).astype(v_ref.dtype)
    kn = _dot(ckvn, wk_ref[...])
    krp = krp_ref[...]
    for h in range(N_HEADS):
        sl = slice(h * HEAD_PAD, (h + 1) * HEAD_PAD)
        kh = _head_norm(kn[:, sl] + krp, kg_ref[...])
        if with_rope:
            kh = _rope(kh, cos, sa, sb)
        k_ref[:, sl] = kh.astype(k_ref.dtype)
    if with_q:
        cqn = (_rms(cq_ref[...]) * qn_ref[...]).astype(BF16)
        qn = _dot(cqn, wq_ref[...])
        for h in range(N_HEADS):
            sl = slice(h * HEAD_PAD, (h + 1) * HEAD_PAD)
            qh = _head_norm(qn[:, sl], qg_ref[...])
            if with_rope:
                qh = _rope(qh, cos, sa, sb)
            q_ref[:, sl] = (qh * (QK_HEAD ** -0.5)).astype(q_ref.dtype)


def _mla_prep(cq, ckv, krp, q_norm, wq, q_gain, kv_norm, wk, wv, k_gain, tables, tm):
    b, l, _ = ckv.shape
    with_q = cq is not None
    with_rope = tables is not None
    tok = lambda i, j: (i, j, 0)
    const = lambda i, j: (0, 0)
    args, specs = [], []

    def add(arr, spec):
        args.append(arr)
        specs.append(spec)

    if with_q:
        add(cq, pl.BlockSpec((None, tm, cq.shape[-1]), tok))
        add(q_norm, pl.BlockSpec(q_norm.shape, const))
        add(wq, pl.BlockSpec(wq.shape, const))
        add(q_gain, pl.BlockSpec(q_gain.shape, const))
    add(ckv, pl.BlockSpec((None, tm, ckv.shape[-1]), tok))
    add(krp, pl.BlockSpec((None, tm, HEAD_PAD), tok))
    add(kv_norm, pl.BlockSpec(kv_norm.shape, const))
    add(wk, pl.BlockSpec(wk.shape, const))
    add(wv, pl.BlockSpec(wv.shape, const))
    add(k_gain, pl.BlockSpec(k_gain.shape, const))
    if with_rope:
        for tbl in tables:
            add(tbl, pl.BlockSpec((tm, HEAD_PAD), lambda i, j: (j, 0)))
    hk = N_HEADS * HEAD_PAD
    hv = wv.shape[1]
    out_specs = [pl.BlockSpec((None, tm, hk), tok), pl.BlockSpec((None, tm, hv), tok)]
    out_shape = [jax.ShapeDtypeStruct((b, l, hk), BF16), jax.ShapeDtypeStruct((b, l, hv), BF16)]
    if with_q:
        out_specs = [pl.BlockSpec((None, tm, hk), tok)] + out_specs
        out_shape = [jax.ShapeDtypeStruct((b, l, hk), BF16)] + out_shape
    return pl.pallas_call(
        functools.partial(_mla_prep_kernel, with_q=with_q, with_rope=with_rope),
        grid=(b, l // tm),
        in_specs=specs,
        out_specs=out_specs,
        out_shape=out_shape,
        compiler_params=_cparams(("arbitrary", "arbitrary")),
        name="mla_prep_q" if with_q else "mla_prep_ctx",
    )(*args)


def _attn_kernel(q_ref, kl_ref, kc_ref, vl_ref, vc_ref, o_ref, *, heads):
    nt = (((1,), (1,)), ((), ()))
    for pr in range(heads // 2):
        vsl = slice(pr * 2 * V_HEAD, (pr + 1) * 2 * V_HEAD)
        outs = []
        for hh in range(2):
            h = 2 * pr + hh
            sl = slice(h * HEAD_PAD, (h + 1) * HEAD_PAD)
            q = q_ref[:, sl]
            s_l = lax.dot_general(q, kl_ref[:, sl], nt, preferred_element_type=F32)
            s_c = lax.dot_general(q, kc_ref[:, sl], nt, preferred_element_type=F32)
            m = jnp.maximum(jnp.max(s_l, axis=-1, keepdims=True), jnp.max(s_c, axis=-1, keepdims=True))
            p_l = jnp.exp(s_l - m)
            p_c = jnp.exp(s_c - m)
            den = jnp.sum(p_l, axis=-1, keepdims=True) + jnp.sum(p_c, axis=-1, keepdims=True)
            o = _dot(p_l.astype(BF16), vl_ref[:, vsl]) + _dot(p_c.astype(BF16), vc_ref[:, vsl])
            outs.append(o / den)
        lane = lax.broadcasted_iota(I32, outs[0].shape, 1)
        o_ref[:, vsl] = jnp.where(lane < V_HEAD, outs[0], outs[1]).astype(o_ref.dtype)


def _attention(q, k_l, k_c, v_l, v_c, tq, heads):
    b, l, _ = q.shape
    lc = k_c.shape[1]
    hp = N_HEADS // heads
    kw = heads * HEAD_PAD
    vw = heads * V_HEAD
    return pl.pallas_call(
        functools.partial(_attn_kernel, heads=heads),
        grid=(b, hp, l // tq),
        in_specs=[
            pl.BlockSpec((None, tq, kw), lambda i, h, j: (i, j, h)),
            pl.BlockSpec((None, l, kw), lambda i, h, j: (i, 0, h)),
            pl.BlockSpec((None, lc, kw), lambda i, h, j: (i, 0, h)),
            pl.BlockSpec((None, l, vw), lambda i, h, j: (i, 0, h)),
            pl.BlockSpec((None, lc, vw), lambda i, h, j: (i, 0, h)),
        ],
        out_specs=pl.BlockSpec((None, tq, vw), lambda i, h, j: (i, j, h)),
        out_shape=jax.ShapeDtypeStruct((b, l, N_HEADS * V_HEAD), BF16),
        compiler_params=_cparams(("arbitrary", "arbitrary", "arbitrary")),
        name="mla_attention",
    )(q, k_l, k_c, v_l, v_c)


def _mix_cd_kernel(yf_ref, yb_ref, u_ref, o_ref, x_ref, g1_ref, sh2_ref, sc2_ref,
                   dskip_ref, wglu_ref, wout_ref, wr_ref, br_ref, xl_ref, xm_ref, rt_ref):
    dc = u_ref.shape[-1]
    y = yf_ref[...] + yb_ref[...] + u_ref[...] * dskip_ref[...]
    g = jax.nn.gelu(y)
    s5 = (g * _sigmoid(_dot(g.astype(BF16), wglu_ref[...]))).astype(BF16)
    y_mix = _dot(s5, wout_ref[0:dc, :]) + _dot(o_ref[...], wout_ref[dc:, :])
    _residual_and_route(x_ref[...], y_mix, g1_ref[...], sh2_ref[...], sc2_ref[...], wr_ref, br_ref,
                        xl_ref, xm_ref, rt_ref)


def _mix_cd(yf, yb, u, o, x, g1, sh2, sc2, dskip, wglu, wout, wr, br, tm):
    b, l, d = x.shape
    dc = u.shape[-1]
    tok = lambda i, j: (i, j, 0)
    per_b = lambda i, j: (i, 0, 0)
    const2 = lambda i, j: (0, 0)
    return pl.pallas_call(
        _mix_cd_kernel,
        grid=(b, l // tm),
        in_specs=[
            pl.BlockSpec((None, tm, dc), tok),
            pl.BlockSpec((None, tm, dc), tok),
            pl.BlockSpec((None, tm, dc), tok),
            pl.BlockSpec((None, tm, o.shape[-1]), tok),
            pl.BlockSpec((None, tm, d), tok),
            pl.BlockSpec((None, 1, d), per_b),
            pl.BlockSpec((None, 1, d), per_b),
            pl.BlockSpec((None, 1, d), per_b),
            pl.BlockSpec((1, dc), const2),
            pl.BlockSpec(wglu.shape, const2),
            pl.BlockSpec(wout.shape, const2),
            pl.BlockSpec((d, LANE), const2),
            pl.BlockSpec((1, LANE), const2),
        ],
        out_specs=[
            pl.BlockSpec((None, tm, d), tok),
            pl.BlockSpec((tm * (d // LANE), LANE), lambda i, j: (i * (l // tm) + j, 0)),
            pl.BlockSpec((None, tm, LANE), tok),
        ],
        out_shape=[
            jax.ShapeDtypeStruct((b, l, d), F32),
            jax.ShapeDtypeStruct((b * l * (d // LANE), LANE), F32),
            jax.ShapeDtypeStruct((b, l, LANE), F32),
        ],
        compiler_params=_cparams(("arbitrary", "arbitrary")),
        name="mix_cd",
    )(yf, yb, u, o, x, g1, sh2, sc2, dskip, wglu, wout, wr, br)


def _router_weights(w_grp, b_grp, w_exp, b_exp):
    d = w_grp.shape[0]
    pad = LANE - N_GROUPS - N_EXPERTS
    wr = jnp.concatenate([w_grp, w_exp, jnp.zeros((d, pad), F32)], axis=1)
    br = jnp.concatenate([b_grp, b_exp, jnp.zeros((pad,), F32)]).reshape(1, LANE)
    return wr, br


def _pad_heads(w, width):
    k = w.shape[0]
    w = w.reshape(k, N_HEADS, width)
    return jnp.pad(w, ((0, 0), (0, 0), (0, HEAD_PAD - width))).reshape(k, N_HEADS * HEAD_PAD)


def _rope_tables(l):
    rows = l // GRID_W
    row = jnp.repeat(jnp.arange(rows, dtype=F32), GRID_W)
    col = jnp.tile(jnp.arange(GRID_W, dtype=F32), rows)
    inv_freq = ROPE_BASE ** (-jnp.arange(AXIS_PAIRS, dtype=F32) / AXIS_PAIRS)
    ar, ac = row[:, None] * inv_freq, col[:, None] * inv_freq
    one = jnp.ones((l, QK_NOPE), F32)
    zn = jnp.zeros((l, QK_NOPE), F32)
    zp = jnp.zeros((l, HEAD_PAD - QK_HEAD), F32)
    z8 = jnp.zeros((l, AXIS_PAIRS), F32)
    cos = jnp.concatenate([one, jnp.cos(ar), jnp.cos(ar), jnp.cos(ac), jnp.cos(ac), zp], axis=1)
    sin_a = jnp.concatenate([zn, -jnp.sin(ar), z8, -jnp.sin(ac), z8, zp], axis=1)
    sin_b = jnp.concatenate([zn, z8, jnp.sin(ar), z8, jnp.sin(ac), zp], axis=1)
    return cos, sin_a, sin_b


def _split_mod(mod_row, b, d):
    parts = jnp.split(mod_row, 6, axis=-1)
    return [jnp.broadcast_to(p.reshape(-1, 1, d), (b, 1, d)) for p in parts]


def kernel(x, c, ctx, c_ctx, ada_w, ada_b, ab_w_in, sgu_norm, sgu_w, sgu_b, conv_w, ab_w_out, cd_w_in, s5_a_re, s5_a_im, s5_log_dt, s5_b_re, s5_b_im, s5_c_re, s5_c_im, s5_d, s5_w_glu, mla_q_norm, mla_kv_norm, mla_w_uq, mla_w_uk, mla_w_uv, mla_qn_gain, mla_kn_gain, cd_w_out, moe_w_grp, moe_b_grp, moe_w_exp, moe_b_exp, moe_w_gate, moe_w_up, moe_w_down):
    b, l, d = x.shape
    lc = ctx.shape[1]
    depth = ada_w.shape[0]
    assert depth == 2, "layer 0 = gated-MLP/conv mixers, layer 1 = S5/attention mixers"
    tm = min(TOKEN_TILE, l)
    tmc = min(TOKEN_TILE, lc)

    mod_rows = 2 * SUBLANE
    cvec = jnp.zeros((mod_rows, d), F32).at[:b].set(c).at[b].set(c_ctx)
    mod = _ada(cvec, ada_w, ada_b)

    sh1, sc1, g1, sh2, sc2, g2 = _split_mod(mod[0, :b], b, d)
    csh1, csc1, cg1, csh2, csc2, cg2 = _split_mod(mod[0, b:b + 1], b, d)
    w_in = ab_w_in[0].astype(BF16)
    wr, br = _router_weights(moe_w_grp[0], moe_b_grp[0], moe_w_exp[0], moe_b_exp[0])
    sgub = jnp.repeat(sgu_b[0].T, d // G_A, axis=1)
    mix_args = (sgu_norm[0].reshape(1, d), sgu_w[0].astype(BF16), sgub, conv_w[0],
                ab_w_out[0].astype(BF16), wr, br)
    z, p = _inproj_ab(x, sh1, sc1, w_in, tm)
    xl, xm, rt = _mix_ab(z, p, x, g1, sh2, sc2, *mix_args, tm)
    zc, pc = _inproj_ab(ctx, csh1, csc1, w_in, tmc)
    xc, xmc, rtc = _mix_ab(zc, pc, ctx, cg1, csh2, csc2, *mix_args, tmc)
    xl, xc = _hier_moe([(xl, xm, rt, g2), (xc, xmc, rtc, cg2)], 0, moe_w_gate, moe_w_up, moe_w_down)

    sh1, sc1, g1, sh2, sc2, g2 = _split_mod(mod[1, :b], b, d)
    csh1, csc1, _, _, _, _ = _split_mod(mod[1, b:b + 1], b, d)
    d_c = s5_d.shape[-1]
    q_lora = mla_q_norm.shape[-1]
    kv_lora = mla_kv_norm.shape[-1]
    w_cd = cd_w_in[0]
    o_kr = d_c + q_lora + kv_lora
    w_kr = jnp.pad(w_cd[:, o_kr:], ((0, 0), (QK_NOPE, HEAD_PAD - QK_HEAD)))
    w_cd = jnp.concatenate([w_cd[:, :o_kr], w_kr], axis=1).astype(BF16)
    widths = (d_c, q_lora, kv_lora, HEAD_PAD)
    u_l, cq_l, ckv_l, krp_l = _inproj_cd(xl, sh1, sc1, w_cd, widths, tm)
    u_c, _, ckv_c, krp_c = _inproj_cd(xc, csh1, csc1, w_cd, widths, tmc)

    ab_re, ab_im, bb_re, bb_im = _s5_params(s5_a_re[0], s5_a_im[0], s5_log_dt[0], s5_b_re[0], s5_b_im[0])
    a_bc, bsb, csb = _s5_block_matrices(ab_re, ab_im, bb_re, bb_im, s5_c_re[0], s5_c_im[0], b)
    u_tm = jnp.concatenate([u_c.transpose(1, 0, 2), u_l.transpose(1, 0, 2)], axis=0).reshape((lc + l) * b, d_c)
    yf, yb = _s5_scan(u_tm, lc // S5_STEPS, a_bc, bsb, csb, b)
    to_bm = lambda y: y.reshape(lc + l, b, d_c)[lc:].transpose(1, 0, 2)
    yf, yb = to_bm(yf), to_bm(yb)

    wq = _pad_heads(mla_w_uq[0], QK_HEAD).astype(BF16)
    wk = _pad_heads(mla_w_uk[0], QK_NOPE).astype(BF16)
    wv = mla_w_uv[0].astype(BF16)
    qg = jnp.pad(mla_qn_gain[0], (0, HEAD_PAD - QK_HEAD)).reshape(1, HEAD_PAD)
    kg = jnp.pad(mla_kn_gain[0], (0, HEAD_PAD - QK_HEAD)).reshape(1, HEAD_PAD)
    qn = mla_q_norm[0].reshape(1, q_lora)
    kvn = mla_kv_norm[0].reshape(1, kv_lora)
    q_l, k_l, v_l = _mla_prep(cq_l, ckv_l, krp_l, qn, wq, qg, kvn, wk, wv, kg, _rope_tables(l), tm)
    k_c, v_c = _mla_prep(None, ckv_c, krp_c, None, None, None, kvn, wk, wv, kg, None, tmc)
    o_l = _attention(q_l, k_l, k_c, v_l, v_c, ATTN_Q_TILE, ATTN_HEADS)

    wr, br = _router_weights(moe_w_grp[1], moe_b_grp[1], moe_w_exp[1], moe_b_exp[1])
    xl, xm, rt = _mix_cd(yf, yb, u_l, o_l, xl, g1, sh2, sc2, s5_d[0].reshape(1, d_c),
                         s5_w_glu[0].astype(BF16), cd_w_out[0].astype(BF16), wr, br, tm)
    (xl,) = _hier_moe([(xl, xm, rt, g2)], 1, moe_w_gate, moe_w_up, moe_w_down)
    return xl
```

```python
import functools
import math

import jax
import jax.numpy as jnp
from jax import lax
from jax.experimental import pallas as pl
from jax.experimental.pallas import tpu as pltpu

F32 = jnp.float32
BF16 = jnp.bfloat16
I32 = jnp.int32
HIGHEST = lax.Precision.HIGHEST

EPS = 1e-6
GRID_W = 64
CHUNK = 128
G_A = 8
CONV_W = 3
S5_GROUP = 16
S5_STATE = 64
N_HEADS = 8
QK_NOPE = 64
QK_ROPE = 32
QK_HEAD = QK_NOPE + QK_ROPE
V_HEAD = 64
AXIS_PAIRS = QK_ROPE // 4
ROPE_BASE = 10000.0
N_GROUPS = 4
EXPERTS_PER_GROUP = 8
N_EXPERTS = N_GROUPS * EXPERTS_PER_GROUP
TOP_K = 2

LANE = 128
SUBLANE = 8
HEAD_PAD = LANE
TOKEN_TILE = 512
MOE_TILE = 512
COMBINE_TILE = 256
PLAN_BLOCK = 1024
MAX_TILES = 256
ATTN_Q_TILE = 256
ATTN_HEADS = 8
S5_STEPS = 64
S5_SUPER = LANE // S5_GROUP
VMEM_LIMIT = 56 * 1024 * 1024


def _cparams(sem):
    return pltpu.CompilerParams(dimension_semantics=sem, vmem_limit_bytes=VMEM_LIMIT)


def _rms(x):
    return x * lax.rsqrt(jnp.mean(x * x, axis=-1, keepdims=True) + EPS)


def _sigmoid(x):
    return 1.0 / (1.0 + jnp.exp(-x))


def _dot(a, b):
    return jnp.dot(a, b, preferred_element_type=F32)


def _ada_kernel(c_ref, w_ref, b_ref, o_ref):
    c = c_ref[...]
    s = c * _sigmoid(c)
    o_ref[...] = jnp.dot(s, w_ref[...], precision=HIGHEST, preferred_element_type=F32) + b_ref[...]


def _ada(cvec, ada_w, ada_b):
    depth, d, n = ada_w.shape
    rows = cvec.shape[0]
    tn = 1024
    return pl.pallas_call(
        _ada_kernel,
        grid=(depth, n // tn),
        in_specs=[
            pl.BlockSpec((rows, d), lambda i, j: (0, 0)),
            pl.BlockSpec((None, d, tn), lambda i, j: (i, 0, j)),
            pl.BlockSpec((None, 1, tn), lambda i, j: (i, 0, j)),
        ],
        out_specs=pl.BlockSpec((None, rows, tn), lambda i, j: (i, 0, j)),
        out_shape=jax.ShapeDtypeStruct((depth, rows, n), F32),
        compiler_params=_cparams(("arbitrary", "arbitrary")),
        name="ada_mod",
    )(cvec, ada_w, ada_b.reshape(depth, 1, n))


def _route(xm, wr_ref, br_ref):
    logits = jnp.dot(xm, wr_ref[...], precision=HIGHEST, preferred_element_type=F32) + br_ref[...]
    lane = lax.broadcasted_iota(I32, logits.shape, 1).astype(F32)
    neg = jnp.float32(-1e30)
    big = jnp.float32(1e6)
    is_grp = lane < N_GROUPS
    gl = jnp.where(is_grp, logits, neg)
    gmax = jnp.max(gl, axis=-1, keepdims=True)
    gsum = jnp.sum(jnp.where(is_grp, jnp.exp(gl - gmax), 0.0), axis=-1, keepdims=True)
    p_top = 1.0 / gsum
    grp = jnp.min(jnp.where(gl == gmax, lane, big), axis=-1, keepdims=True)
    eidx = lane - N_GROUPS
    in_grp = (eidx >= grp * EXPERTS_PER_GROUP) & (eidx < (grp + 1.0) * EXPERTS_PER_GROUP)
    el = jnp.where(in_grp, logits, neg)
    m1 = jnp.max(el, axis=-1, keepdims=True)
    i1 = jnp.min(jnp.where(el == m1, lane, big), axis=-1, keepdims=True)
    el2 = jnp.where(lane == i1, neg, el)
    m2 = jnp.max(el2, axis=-1, keepdims=True)
    i2 = jnp.min(jnp.where(el2 == m2, lane, big), axis=-1, keepdims=True)
    t = jnp.exp(m2 - m1)
    w1 = p_top / (1.0 + t)
    w2 = p_top * t / (1.0 + t)
    e1 = i1 - N_GROUPS
    e2 = i2 - N_GROUPS
    return jnp.where(lane == 0, e1, jnp.where(lane == 1, e2, jnp.where(lane == 2, w1, jnp.where(lane == 3, w2, 0.0))))


def _residual_and_route(x, y, g1, sh2, sc2, wr_ref, br_ref, xl_ref, xm_ref, rt_ref):
    xl = x + g1 * y
    xl_ref[...] = xl
    xm = _rms(xl) * (1.0 + sc2) + sh2
    _store_token_tiles(xm_ref, xm)
    rt_ref[...] = _route(xm, wr_ref, br_ref)


def _inproj_ab_kernel(x_ref, sh_ref, sc_ref, w_ref, z_ref, p_ref):
    d = x_ref.shape[-1]
    h = (_rms(x_ref[...]) * (1.0 + sc_ref[...]) + sh_ref[...]).astype(BF16)
    for j in range(3):
        z_ref[:, j * d:(j + 1) * d] = _dot(h, w_ref[:, j * d:(j + 1) * d]).astype(BF16)
    gate_c = _dot(h, w_ref[:, 3 * d:4 * d])
    xb = _dot(h, w_ref[:, 4 * d:5 * d])
    p_ref[...] = gate_c * xb


def _inproj_ab(x, sh, sc, w_bf16, tm):
    b, l, d = x.shape
    n = w_bf16.shape[1]
    return pl.pallas_call(
        _inproj_ab_kernel,
        grid=(b, l // tm),
        in_specs=[
            pl.BlockSpec((None, tm, d), lambda i, j: (i, j, 0)),
            pl.BlockSpec((None, 1, d), lambda i, j: (i, 0, 0)),
            pl.BlockSpec((None, 1, d), lambda i, j: (i, 0, 0)),
            pl.BlockSpec((d, n), lambda i, j: (0, 0)),
        ],
        out_specs=[
            pl.BlockSpec((None, tm, 3 * d), lambda i, j: (i, j, 0)),
            pl.BlockSpec((None, tm, d), lambda i, j: (i, j, 0)),
        ],
        out_shape=[
            jax.ShapeDtypeStruct((b, l, 3 * d), BF16),
            jax.ShapeDtypeStruct((b, l, d), F32),
        ],
        compiler_params=_cparams(("arbitrary", "arbitrary")),
        name="inproj_ab",
    )(x, sh, sc, w_bf16)


def _mix_ab_kernel(z_ref, p_ref, pprev_ref, pnext_ref, x_ref, g1_ref, sh2_ref, sc2_ref,
                   sgun_ref, sguw_ref, sgub_ref, convw_ref, wout_ref, wr_ref, br_ref,
                   xl_ref, xm_ref, rt_ref, pbuf, ycat):
    i = pl.program_id(1)
    nt = pl.num_programs(1)
    tm, d = x_ref.shape
    cg = d // G_A
    v = jax.nn.gelu(z_ref[:, d:2 * d].astype(F32))
    vc = (_rms(v) * sgun_ref[...]).astype(BF16)
    for c in range(tm // CHUNK):
        r0 = c * CHUNK
        cols = [_dot(sguw_ref[g], vc[r0:r0 + CHUNK, g * cg:(g + 1) * cg]) for g in range(G_A)]
        s = jnp.concatenate(cols, axis=1) + sgub_ref[...]
        u = jax.nn.gelu(z_ref[r0:r0 + CHUNK, 0:d].astype(F32))
        ycat[r0:r0 + CHUNK, 0:d] = (u * s).astype(BF16)
    pbuf[SUBLANE:SUBLANE + tm, :] = p_ref[...]
    pbuf[0:SUBLANE, :] = jnp.where(i > 0, pprev_ref[...], 0.0)
    pbuf[SUBLANE + tm:2 * SUBLANE + tm, :] = jnp.where(i < nt - 1, pnext_ref[...], 0.0)
    conv = (convw_ref[0:1, :] * pbuf[SUBLANE - 1:SUBLANE - 1 + tm, :]
            + convw_ref[1:2, :] * pbuf[SUBLANE:SUBLANE + tm, :]
            + convw_ref[2:3, :] * pbuf[SUBLANE + 1:SUBLANE + 1 + tm, :])
    ycat[:, d:2 * d] = (z_ref[:, 2 * d:3 * d].astype(F32) * conv).astype(BF16)
    y = _dot(ycat[...], wout_ref[...])
    _residual_and_route(x_ref[...], y, g1_ref[...], sh2_ref[...], sc2_ref[...], wr_ref, br_ref,
                        xl_ref, xm_ref, rt_ref)


def _mix_ab(z, p, x, g1, sh2, sc2, sgun, sguw, sgub, convw, wout, wr, br, tm):
    b, l, d = x.shape
    hb = tm // SUBLANE
    nhb = l // SUBLANE
    tok = lambda i, j: (i, j, 0)
    per_b = lambda i, j: (i, 0, 0)
    const2 = lambda i, j: (0, 0)
    return pl.pallas_call(
        _mix_ab_kernel,
        grid=(b, l // tm),
        in_specs=[
            pl.BlockSpec((None, tm, 3 * d), tok),
            pl.BlockSpec((None, tm, d), tok),
            pl.BlockSpec((None, SUBLANE, d), lambda i, j: (i, jnp.maximum(j * hb - 1, 0), 0)),
            pl.BlockSpec((None, SUBLANE, d), lambda i, j: (i, jnp.minimum((j + 1) * hb, nhb - 1), 0)),
            pl.BlockSpec((None, tm, d), tok),
            pl.BlockSpec((None, 1, d), per_b),
            pl.BlockSpec((None, 1, d), per_b),
            pl.BlockSpec((None, 1, d), per_b),
            pl.BlockSpec((1, d), const2),
            pl.BlockSpec((G_A, CHUNK, CHUNK), lambda i, j: (0, 0, 0)),
            pl.BlockSpec((CHUNK, d), const2),
            pl.BlockSpec((CONV_W, d), const2),
            pl.BlockSpec((2 * d, d), const2),
            pl.BlockSpec((d, LANE), const2),
            pl.BlockSpec((1, LANE), const2),
        ],
        out_specs=[
            pl.BlockSpec((None, tm, d), tok),
            pl.BlockSpec((tm * (d // LANE), LANE), lambda i, j: (i * (l // tm) + j, 0)),
            pl.BlockSpec((None, tm, LANE), tok),
        ],
        out_shape=[
            jax.ShapeDtypeStruct((b, l, d), F32),
            jax.ShapeDtypeStruct((b * l * (d // LANE), LANE), F32),
            jax.ShapeDtypeStruct((b, l, LANE), F32),
        ],
        scratch_shapes=[
            pltpu.VMEM((tm + 2 * SUBLANE, d), F32),
            pltpu.VMEM((tm, 2 * d), BF16),
        ],
        compiler_params=_cparams(("arbitrary", "arbitrary")),
        name="mix_ab",
    )(z, p, p, p, x, g1, sh2, sc2, sgun, sguw, sgub, convw, wout, wr, br)


def _store_token_tiles(ref, val):
    tm, d = val.shape
    rpt = d // LANE
    for j in range(rpt):
        ref[pl.ds(j, tm, stride=rpt), :] = val[:, j * LANE:(j + 1) * LANE]


def _load_token_tile_col(ref, tm, rpt, j):
    return ref[pl.ds(j, tm, stride=rpt), :]


def _plan_kernel(rt_ref, tri_ref, pos_ref, te_ref, tv_ref, pad_ref, run, *, tm):
    ps = pl.program_id(0)
    i = pl.program_id(1)
    blk = rt_ref.shape[0]
    ne = N_EXPERTS

    @pl.when(jnp.logical_and(ps == 0, i == 0))
    def _():
        run[...] = jnp.zeros_like(run)

    @pl.when(jnp.logical_and(ps == 1, i == 0))
    def _():
        counts = run[...]
        padded = jnp.floor((counts + (tm - 1)) * (1.0 / tm)) * tm
        r = lax.broadcasted_iota(I32, (ne, ne), 0)
        c = lax.broadcasted_iota(I32, (ne, ne), 1)
        starts = jnp.dot((c < r).astype(F32), padded, precision=HIGHEST, preferred_element_type=F32)
        nt = te_ref.shape[1]
        ends = jnp.concatenate([(starts + padded) * (1.0 / tm)] * (nt // LANE), axis=1)
        t = lax.broadcasted_iota(I32, (ne, nt), 1).astype(F32)
        te = jnp.sum((t >= ends).astype(F32), axis=0, keepdims=True)
        total = jnp.max(ends, axis=0, keepdims=True)
        te_ref[...] = jnp.minimum(te, ne - 1.0).astype(I32)
        tv_ref[...] = (t[0:1, :] < total).astype(I32)
        lane = lax.broadcasted_iota(I32, (ne, LANE), 1)
        pad_ref[...] = jnp.where(lane == 0, starts + counts,
                                 jnp.where(lane == 1, padded - counts, 0.0)).astype(I32)
        run[...] = starts

    slab_t = rt_ref[...].T
    ex = lax.broadcasted_iota(I32, (ne, blk), 0).astype(F32)
    oh1 = ex == slab_t[0:1, :]
    oh2 = ex == slab_t[1:2, :]
    oh = jnp.where(jnp.logical_or(oh1, oh2), 1.0, 0.0)
    prefix = _dot(oh.astype(BF16), tri_ref[...]) + run[:, 0:1]
    pos1 = jnp.sum(jnp.where(oh1, prefix, 0.0), axis=0, keepdims=True)
    pos2 = jnp.sum(jnp.where(oh2, prefix, 0.0), axis=0, keepdims=True)
    pos_ref[...] = jnp.concatenate([pos1, pos2], axis=0).astype(I32)
    run[...] = run[...] + jnp.sum(oh, axis=1, keepdims=True)


def _moe_plan(rt_all, tm):
    n = rt_all.shape[0]
    blk = PLAN_BLOCK
    nb = n // blk
    tri = jnp.triu(jnp.ones((blk, blk), BF16), k=1)
    return pl.pallas_call(
        functools.partial(_plan_kernel, tm=tm),
        grid=(2, nb),
        in_specs=[
            pl.BlockSpec((blk, LANE), lambda p, i: (i, 0)),
            pl.BlockSpec((blk, blk), lambda p, i: (0, 0)),
        ],
        out_specs=[
            pl.BlockSpec((None, TOP_K, blk), lambda p, i: (jnp.where(p == 0, nb, i), 0, 0)),
            pl.BlockSpec((1, MAX_TILES), lambda p, i: (0, 0)),
            pl.BlockSpec((1, MAX_TILES), lambda p, i: (0, 0)),
            pl.BlockSpec((N_EXPERTS, LANE), lambda p, i: (0, 0)),
        ],
        out_shape=[
            jax.ShapeDtypeStruct((nb + 1, TOP_K, blk), I32),
            jax.ShapeDtypeStruct((1, MAX_TILES), I32),
            jax.ShapeDtypeStruct((1, MAX_TILES), I32),
            jax.ShapeDtypeStruct((N_EXPERTS, LANE), I32),
        ],
        scratch_shapes=[pltpu.VMEM((N_EXPERTS, LANE), F32)],
        compiler_params=_cparams(("arbitrary", "arbitrary")),
        name="moe_plan",
    )(rt_all, tri)


def _dispatch_kernel(ps_ref, pl_ref, tv_ref, pos_ref, *rest, blocks, n_tiles):
    xm_refs = rest[:len(blocks)]
    xs_hbm, zbuf, sem = rest[len(blocks):]
    i = pl.program_id(0)
    tt = pos_ref.shape[1]
    rpt = xm_refs[0].shape[0] // tt
    tile_rows = zbuf.shape[0]

    @pl.when(i == 0)
    def _():
        zbuf[...] = jnp.zeros_like(zbuf)

        def gap_copies(e, wait):
            base = ps_ref[e]
            n = pl_ref[e]
            bit = tile_rows // (2 * rpt)
            while bit >= 1:
                lower = n & (bit - 1)

                @pl.when((n & bit) != 0)
                def _(lower=lower, bit=bit):
                    dst = xs_hbm.at[pl.ds((base + lower) * rpt, bit * rpt)]
                    cp = pltpu.make_async_copy(zbuf.at[pl.ds(0, bit * rpt)], dst, sem.at[1])
                    cp.wait() if wait else cp.start()
                bit //= 2

        def tile_copy(t, wait):
            @pl.when(tv_ref[t] != 1)
            def _():
                dst = xs_hbm.at[pl.ds(pl.multiple_of(t * tile_rows, tile_rows), tile_rows)]
                cp = pltpu.make_async_copy(zbuf, dst, sem.at[1])
                cp.wait() if wait else cp.start()

        for wait in (False, True):
            lax.fori_loop(0, N_EXPERTS, lambda e, c, w=wait: (gap_copies(e, w), c)[1], 0)
            lax.fori_loop(0, n_tiles, lambda t, c, w=wait: (tile_copy(t, w), c)[1], 0)

    def scatter(xm_ref):
        def issue(r, carry):
            src = xm_ref.at[pl.ds(pl.multiple_of(r * rpt, rpt), rpt)]
            for k in range(TOP_K):
                dst = xs_hbm.at[pl.ds(pl.multiple_of(pos_ref[k, r] * rpt, rpt), rpt)]
                pltpu.make_async_copy(src, dst, sem.at[0]).start(priority=k)
            return carry
        lax.fori_loop(0, tt, issue, 0, unroll=8)
        for k in range(TOP_K):
            pltpu.make_async_copy(xm_ref, xs_hbm.at[pl.ds(0, tt * rpt)], sem.at[0]).wait()

    lo = 0
    for xm_ref, nblk in zip(xm_refs, blocks):
        pl.when(jnp.logical_and(i >= lo, i < lo + nblk))(functools.partial(scatter, xm_ref))
        lo += nblk


def _dispatch(pad_start, pad_len, valid, pos, xm_streams, n_tiles, tm):
    tt = pos.shape[-1]
    rpt = SUBLANE
    blocks = tuple(xm.shape[0] // (rpt * tt) for xm in xm_streams)
    in_specs = [pl.BlockSpec((None, TOP_K, tt), lambda i, a, b, c: (i, 0, 0), memory_space=pltpu.SMEM)]
    lo = 0
    for nblk in blocks:
        in_specs.append(pl.BlockSpec(
            (tt * rpt, LANE), lambda i, a, b, c, lo=lo, nblk=nblk: (jnp.clip(i - lo, 0, nblk - 1), 0)))
        lo += nblk
    return pl.pallas_call(
        functools.partial(_dispatch_kernel, blocks=blocks, n_tiles=n_tiles),
        grid_spec=pltpu.PrefetchScalarGridSpec(
            num_scalar_prefetch=3,
            grid=(sum(blocks),),
            in_specs=in_specs,
            out_specs=pl.BlockSpec(memory_space=pl.ANY),
            scratch_shapes=[pltpu.VMEM((tm * rpt, LANE), F32), pltpu.SemaphoreType.DMA((2,))],
        ),
        out_shape=jax.ShapeDtypeStruct((n_tiles * tm * rpt, LANE), F32),
        compiler_params=_cparams(("arbitrary",)),
        name="moe_dispatch",
    )(pad_start, pad_len, valid, pos, *xm_streams)


def _moe_kernel(te_ref, tv_ref, x_ref, wg_ref, wu_ref, wd_ref, y_ref, wg_s, wu_s, wd_s):
    t = pl.program_id(0)
    rpt = wg_ref.shape[0] // LANE
    tm = x_ref.shape[0] // rpt

    @pl.when(tv_ref[t] == 1)
    def _():
        changed = jnp.logical_or(t == 0, te_ref[t] != te_ref[jnp.maximum(t - 1, 0)])

        @pl.when(changed)
        def _():
            wg_s[...] = wg_ref[...].astype(BF16)
            wu_s[...] = wu_ref[...].astype(BF16)
            wd_s[...] = wd_ref[...].astype(BF16)

        x = jnp.concatenate([_load_token_tile_col(x_ref, tm, rpt, j) for j in range(rpt)], axis=1).astype(BF16)
        g = _dot(x, wg_s[...])
        u = _dot(x, wu_s[...])
        h = (g * _sigmoid(g) * u).astype(BF16)
        _store_token_tiles(y_ref, _dot(h, wd_s[...]))

    @pl.when(tv_ref[t] != 1)
    def _():
        y_ref[...] = jnp.zeros_like(y_ref)


def _moe_experts(xs, tile_e, valid, n_tiles, layer, w_gate, w_up, w_down, tm):
    _, _, d, de = w_gate.shape
    rpt = d // LANE
    grid_spec = pltpu.PrefetchScalarGridSpec(
        num_scalar_prefetch=2,
        grid=(n_tiles,),
        in_specs=[
            pl.BlockSpec((tm * rpt, LANE), lambda t, te, tv: (t * tv[t], 0)),
            pl.BlockSpec((None, None, d, de), lambda t, te, tv: (layer, te[t], 0, 0)),
            pl.BlockSpec((None, None, d, de), lambda t, te, tv: (layer, te[t], 0, 0)),
            pl.BlockSpec((None, None, de, d), lambda t, te, tv: (layer, te[t], 0, 0)),
        ],
        out_specs=pl.BlockSpec((tm * rpt, LANE), lambda t, te, tv: (t, 0)),
        scratch_shapes=[
            pltpu.VMEM((d, de), BF16),
            pltpu.VMEM((d, de), BF16),
            pltpu.VMEM((de, d), BF16),
        ],
    )
    return pl.pallas_call(
        _moe_kernel,
        grid_spec=grid_spec,
        out_shape=jax.ShapeDtypeStruct((n_tiles * tm * rpt, LANE), F32),
        compiler_params=_cparams(("arbitrary",)),
        name="moe_experts",
    )(tile_e, valid, xs, w_gate, w_up, w_down)


def _combine_kernel(pos0_ref, posn_ref, x_ref, g_ref, rt_ref, ys_hbm, o_ref, buf, sem, *, n):
    i = pl.program_id(0)
    tt, d = x_ref.shape
    rpt = d // LANE

    def gather(pos_ref, slot):
        def issue(r, carry):
            for k in range(TOP_K):
                src = ys_hbm.at[pl.ds(pl.multiple_of(pos_ref[k, r] * rpt, rpt), rpt)]
                dst = buf.at[slot, k, pl.ds(pl.multiple_of(r * rpt, rpt), rpt)]
                pltpu.make_async_copy(src, dst, sem.at[slot, k]).start(priority=k)
            return carry
        lax.fori_loop(0, tt, issue, 0, unroll=8)

    slot = lax.rem(i, 2)

    @pl.when(i == 0)
    def _():
        gather(pos0_ref, 0)

    @pl.when(i + 1 < n)
    def _():
        gather(posn_ref, 1 - slot)

    for k in range(TOP_K):
        pltpu.make_async_copy(ys_hbm.at[pl.ds(0, tt * rpt)], buf.at[slot, k], sem.at[slot, k]).wait()
    w1 = rt_ref[:, 2:3]
    w2 = rt_ref[:, 3:4]
    for j in range(rpt):
        r1 = buf[slot, 0, pl.ds(j, tt, stride=rpt), :]
        r2 = buf[slot, 1, pl.ds(j, tt, stride=rpt), :]
        sl = slice(j * LANE, (j + 1) * LANE)
        o_ref[:, sl] = x_ref[:, sl] + g_ref[:, sl] * (w1 * r1 + w2 * r2)


def _moe_combine(x, g2, rt, pos, tok_off, ys, tt):
    b, l, d = x.shape
    rpt = d // LANE
    n = b * l
    per_b = l // tt
    per_blk = PLAN_BLOCK // tt
    blk0 = tok_off // PLAN_BLOCK

    def pos_map(step):
        return lambda i: (blk0 + step(i) // per_blk, 0, step(i) % per_blk)

    return pl.pallas_call(
        functools.partial(_combine_kernel, n=n // tt),
        grid=(n // tt,),
        in_specs=[
            pl.BlockSpec((None, TOP_K, tt), pos_map(lambda i: i), memory_space=pltpu.SMEM),
            pl.BlockSpec((None, TOP_K, tt), pos_map(lambda i: jnp.minimum(i + 1, n // tt - 1)),
                         memory_space=pltpu.SMEM),
            pl.BlockSpec((tt, d), lambda i: (i, 0)),
            pl.BlockSpec((None, 1, d), lambda i: (i // per_b, 0, 0)),
            pl.BlockSpec((tt, LANE), lambda i: (i, 0)),
            pl.BlockSpec(memory_space=pl.ANY),
        ],
        out_specs=pl.BlockSpec((tt, d), lambda i: (i, 0)),
        out_shape=jax.ShapeDtypeStruct((n, d), F32),
        scratch_shapes=[
            pltpu.VMEM((2, TOP_K, tt * rpt, LANE), F32),
            pltpu.SemaphoreType.DMA((2, TOP_K)),
        ],
        compiler_params=_cparams(("arbitrary",)),
        name="moe_combine",
    )(pos, pos, x.reshape(n, d), g2, rt.reshape(n, LANE), ys).reshape(b, l, d)


def _hier_moe(streams, layer, w_gate, w_up, w_down):
    sizes = [s[0].shape[0] * s[0].shape[1] for s in streams]
    n = sum(sizes)
    tm = MOE_TILE
    n_tiles = (TOP_K * n) // tm + N_EXPERTS
    assert n_tiles <= MAX_TILES and all(sz % PLAN_BLOCK == 0 for sz in sizes)
    rt_all = jnp.concatenate([s[2].reshape(-1, LANE) for s in streams], axis=0)
    pos, tile_e, valid, pad = _moe_plan(rt_all, tm)
    tile_e, valid = tile_e.reshape(-1), valid.reshape(-1)
    xs = _dispatch(pad[:, 0], pad[:, 1], valid, pos, [s[1] for s in streams], n_tiles, tm)
    ys = _moe_experts(xs, tile_e, valid, n_tiles, layer, w_gate, w_up, w_down, tm)
    outs = []
    off = 0
    for (xl, _, rt, g2), sz in zip(streams, sizes):
        outs.append(_moe_combine(xl, g2, rt, pos, off, ys, COMBINE_TILE))
        off += sz
    return outs


def _inproj_cd_kernel(x_ref, sh_ref, sc_ref, w_ref, *out_refs):
    h = (_rms(x_ref[...]) * (1.0 + sc_ref[...]) + sh_ref[...]).astype(BF16)
    off = 0
    for ref in out_refs:
        n = ref.shape[-1]
        ref[...] = _dot(h, w_ref[:, off:off + n]).astype(ref.dtype)
        off += n


def _inproj_cd(x, sh, sc, w_bf16, widths, tm):
    b, l, d = x.shape
    n = w_bf16.shape[1]
    tok = lambda i, j: (i, j, 0)
    return pl.pallas_call(
        _inproj_cd_kernel,
        grid=(b, l // tm),
        in_specs=[
            pl.BlockSpec((None, tm, d), tok),
            pl.BlockSpec((None, 1, d), lambda i, j: (i, 0, 0)),
            pl.BlockSpec((None, 1, d), lambda i, j: (i, 0, 0)),
            pl.BlockSpec((d, n), lambda i, j: (0, 0)),
        ],
        out_specs=[pl.BlockSpec((None, tm, w), tok) for w in widths],
        out_shape=[jax.ShapeDtypeStruct((b, l, w), F32) for w in widths],
        compiler_params=_cparams(("arbitrary", "arbitrary")),
        name="inproj_cd",
    )(x, sh, sc, w_bf16)


def _s5_param_kernel(are_ref, aim_ref, ldt_ref, bre_ref, bim_ref, abre_ref, abim_ref, bbre_ref, bbim_ref):
    a_re = are_ref[...]
    a_im = aim_ref[...]
    dt = jnp.exp(ldt_ref[...])
    mag = jnp.exp(dt * a_re)
    ab_re = mag * jnp.cos(dt * a_im)
    ab_im = mag * jnp.sin(dt * a_im)
    den = a_re * a_re + a_im * a_im
    nr = ab_re - 1.0
    f_re = (nr * a_re + ab_im * a_im) / den
    f_im = (ab_im * a_re - nr * a_im) / den
    abre_ref[...] = ab_re
    abim_ref[...] = ab_im
    b_re = bre_ref[...]
    b_im = bim_ref[...]
    bbre_ref[...] = f_re[None] * b_re - f_im[None] * b_im
    bbim_ref[...] = f_re[None] * b_im + f_im[None] * b_re


def _s5_params(a_re, a_im, log_dt, b_re, b_im):
    nd, g, p = a_re.shape
    c = b_re.shape[-1]
    rows = nd * g * p // LANE
    flat = lambda t: t.reshape(rows, LANE)
    chan_major = lambda t: jnp.moveaxis(t, -1, 0).reshape(c, rows, LANE)
    ldt = jnp.broadcast_to(log_dt[:, :, None], (nd, g, p))
    shapes = [jax.ShapeDtypeStruct((rows, LANE), F32)] * 2 + [jax.ShapeDtypeStruct((c, rows, LANE), F32)] * 2
    ab_re, ab_im, bb_re, bb_im = pl.pallas_call(
        _s5_param_kernel, out_shape=shapes, name="s5_params",
    )(flat(a_re), flat(a_im), flat(ldt), chan_major(b_re), chan_major(b_im))
    unflat = lambda t: jnp.moveaxis(t.reshape(c, nd, g, p), 0, -1)
    return ab_re.reshape(nd, g, p), ab_im.reshape(nd, g, p), unflat(bb_re), unflat(bb_im)


def _s5_kernel(uf_ref, ub_ref, a_ref, bsb_ref, csb_ref, yf_ref, yb_ref, hf, hb, cf, cb):
    nb = cf.shape[0]
    steps = hf.shape[0] // nb
    n_super = bsb_ref.shape[1]
    cw = bsb_ref.shape[2]
    sw = bsb_ref.shape[3]
    half = sw // 2

    @pl.when(pl.program_id(0) == 0)
    def _():
        cf[...] = jnp.zeros_like(cf)
        cb[...] = jnp.zeros_like(cb)

    for s in range(n_super):
        hf[:, s * sw:(s + 1) * sw] = _dot(uf_ref[:, s * cw:(s + 1) * cw].astype(BF16), bsb_ref[0, s])
        hb[:, s * sw:(s + 1) * sw] = _dot(ub_ref[:, s * cw:(s + 1) * cw].astype(BF16), bsb_ref[1, s])

    for s in range(n_super):
        re = slice(s * sw, s * sw + half)
        im = slice(s * sw + half, (s + 1) * sw)
        st = slice(s * half, (s + 1) * half)
        arf, aif = a_ref[0, 0, :, st], a_ref[0, 1, :, st]
        arb, aib = a_ref[1, 0, :, st], a_ref[1, 1, :, st]

        def step(t, carry):
            hfr, hfi, hbr, hbi = carry
            rf = pl.multiple_of(t * nb, nb)
            nfr = arf * hfr - aif * hfi + hf[pl.ds(rf, nb), re]
            nfi = arf * hfi + aif * hfr + hf[pl.ds(rf, nb), im]
            hf[pl.ds(rf, nb), re] = nfr
            hf[pl.ds(rf, nb), im] = nfi
            rb = pl.multiple_of((steps - 1 - t) * nb, nb)
            nbr = arb * hbr - aib * hbi + hb[pl.ds(rb, nb), re]
            nbi = arb * hbi + aib * hbr + hb[pl.ds(rb, nb), im]
            hb[pl.ds(rb, nb), re] = nbr
            hb[pl.ds(rb, nb), im] = nbi
            return nfr, nfi, nbr, nbi

        out = lax.fori_loop(0, steps, step, (cf[:, re], cf[:, im], cb[:, re], cb[:, im]), unroll=4)
        cf[:, re], cf[:, im], cb[:, re], cb[:, im] = out

    for s in range(n_super):
        yf_ref[:, s * cw:(s + 1) * cw] = _dot(hf[:, s * sw:(s + 1) * sw].astype(BF16), csb_ref[0, s])
        yb_ref[:, s * cw:(s + 1) * cw] = _dot(hb[:, s * sw:(s + 1) * sw].astype(BF16), csb_ref[1, s])


def _s5_scan(u_tm, n_ctx_blocks, a_bc, bsb, csb, nb):
    rows, c = u_tm.shape
    blk = S5_STEPS * nb
    n_blocks = rows // blk
    n_state = bsb.shape[1] * bsb.shape[3]
    fwd = lambda i: (i, 0)
    bwd = lambda i: (jnp.where(i < n_ctx_blocks, n_ctx_blocks - 1 - i, n_blocks - 1 - (i - n_ctx_blocks)), 0)
    return pl.pallas_call(
        _s5_kernel,
        grid=(n_blocks,),
        in_specs=[
            pl.BlockSpec((blk, c), fwd),
            pl.BlockSpec((blk, c), bwd),
            pl.BlockSpec(a_bc.shape, lambda i: (0, 0, 0, 0)),
            pl.BlockSpec(bsb.shape, lambda i: (0, 0, 0, 0)),
            pl.BlockSpec(csb.shape, lambda i: (0, 0, 0, 0)),
        ],
        out_specs=[pl.BlockSpec((blk, c), fwd), pl.BlockSpec((blk, c), bwd)],
        out_shape=[jax.ShapeDtypeStruct((rows, c), F32)] * 2,
        scratch_shapes=[
            pltpu.VMEM((blk, n_state), F32),
            pltpu.VMEM((blk, n_state), F32),
            pltpu.VMEM((nb, n_state), F32),
            pltpu.VMEM((nb, n_state), F32),
        ],
        compiler_params=_cparams(("arbitrary",)),
        name="s5_scan",
    )(u_tm, u_tm, a_bc, bsb, csb)


def _s5_block_matrices(ab_re, ab_im, bb_re, bb_im, c_re, c_im, nb):
    nd, g, p = ab_re.shape
    c = bb_re.shape[-1]
    ns = g // S5_SUPER
    eye = jnp.eye(S5_SUPER, dtype=F32)

    def in_mat(bb):
        t = bb.reshape(nd, ns, S5_SUPER, p, c)
        return jnp.einsum('dsgpc,gh->dsgchp', t, eye).reshape(nd, ns, S5_SUPER * c, S5_SUPER * p)

    def out_mat(cc):
        t = cc.reshape(nd, ns, S5_SUPER, c, p)
        return jnp.einsum('dsgcp,gh->dsgphc', t, eye).reshape(nd, ns, S5_SUPER * p, S5_SUPER * c)

    bsb = jnp.concatenate([in_mat(bb_re), in_mat(bb_im)], axis=-1).astype(BF16)
    csb = jnp.concatenate([out_mat(c_re), out_mat(-c_im)], axis=-2).astype(BF16)
    a_bc = jnp.stack([ab_re.reshape(nd, g * p), ab_im.reshape(nd, g * p)], axis=1)
    a_bc = jnp.broadcast_to(a_bc[:, :, None, :], (nd, 2, nb, g * p))
    return a_bc, bsb, csb


def _head_scale(x):
    return lax.rsqrt(jnp.sum(x * x, axis=-1, keepdims=True) * (1.0 / QK_HEAD) + EPS)


def _mla_prep_lat_kernel(cq_ref, ckv_ref, krp_ref, krs_ref, qn_ref, kvn_ref, wq_ref, wqs_ref, wk_ref, wv_ref,
                         qc_ref, qs_ref, kc_ref, ks_ref, q_ref, k_ref, v_ref):
    ckvn = (_rms(ckv_ref[...]) * kvn_ref[...]).astype(BF16)
    v_ref[...] = _dot(ckvn, wv_ref[...]).astype(v_ref.dtype)
    kn = _dot(ckvn, wk_ref[...])
    krp = krp_ref[...]
    k_swap = krs_ref[...] * ks_ref[...]
    kc = kc_ref[...]
    for h in range(N_HEADS):
        sl = slice(h * HEAD_PAD, (h + 1) * HEAD_PAD)
        kh = kn[:, sl] + krp
        k_ref[:, sl] = (_head_scale(kh) * (kh * kc + k_swap)).astype(k_ref.dtype)
    cqn = (_rms(cq_ref[...]) * qn_ref[...]).astype(BF16)
    qn = _dot(cqn, wq_ref[...])
    q_swap = _dot(cqn, wqs_ref[...])
    qc = qc_ref[...]
    qs = qs_ref[...]
    for h in range(N_HEADS):
        sl = slice(h * HEAD_PAD, (h + 1) * HEAD_PAD)
        qh = qn[:, sl]
        scale = _head_scale(qh) * (QK_HEAD ** -0.5)
        q_ref[:, sl] = (scale * (qh * qc + q_swap[:, sl] * qs)).astype(q_ref.dtype)


def _mla_prep_ctx_kernel(ckv_ref, krp_ref, kvn_ref, wk_ref, wv_ref, kg_ref, k_ref, v_ref):
    ckvn = (_rms(ckv_ref[...]) * kvn_ref[...]).astype(BF16)
    v_ref[...] = _dot(ckvn, wv_ref[...]).astype(v_ref.dtype)
    kn = _dot(ckvn, wk_ref[...])
    krp = krp_ref[...]
    kg = kg_ref[...]
    for h in range(N_HEADS):
        sl = slice(h * HEAD_PAD, (h + 1) * HEAD_PAD)
        kh = kn[:, sl] + krp
        k_ref[:, sl] = (_head_scale(kh) * (kh * kg)).astype(k_ref.dtype)


def _mla_prep(kernel_fn, name, tok_args, const_args, table_args, n_q_out, tm):
    b, l, _ = tok_args[0].shape
    tok = lambda i, j: (i, j, 0)
    specs = [pl.BlockSpec((None, tm, a.shape[-1]), tok) for a in tok_args]
    specs += [pl.BlockSpec(a.shape, lambda i, j: (0, 0)) for a in const_args]
    specs += [pl.BlockSpec((tm, HEAD_PAD), lambda i, j: (j, 0)) for _ in table_args]
    hk = N_HEADS * HEAD_PAD
    hv = N_HEADS * V_HEAD
    widths = [hk] * n_q_out + [hk, hv]
    return pl.pallas_call(
        kernel_fn,
        grid=(b, l // tm),
        in_specs=specs,
        out_specs=[pl.BlockSpec((None, tm, w), tok) for w in widths],
        out_shape=[jax.ShapeDtypeStruct((b, l, w), BF16) for w in widths],
        compiler_params=_cparams(("arbitrary", "arbitrary")),
        name=name,
    )(*tok_args, *const_args, *table_args)


def _attn_kernel(q_ref, kl_ref, kc_ref, vl_ref, vc_ref, o_ref, *, heads):
    nt = (((1,), (1,)), ((), ()))
    for pr in range(heads // 2):
        vsl = slice(pr * 2 * V_HEAD, (pr + 1) * 2 * V_HEAD)
        outs = []
        for hh in range(2):
            h = 2 * pr + hh
            sl = slice(h * HEAD_PAD, (h + 1) * HEAD_PAD)
            q = q_ref[:, sl]
            s_l = lax.dot_general(q, kl_ref[:, sl], nt, preferred_element_type=F32)
            s_c = lax.dot_general(q, kc_ref[:, sl], nt, preferred_element_type=F32)
            m = jnp.maximum(jnp.max(s_l, axis=-1, keepdims=True), jnp.max(s_c, axis=-1, keepdims=True))
            p_l = jnp.exp(s_l - m)
            p_c = jnp.exp(s_c - m)
            den = jnp.sum(p_l, axis=-1, keepdims=True) + jnp.sum(p_c, axis=-1, keepdims=True)
            o = _dot(p_l.astype(BF16), vl_ref[:, vsl]) + _dot(p_c.astype(BF16), vc_ref[:, vsl])
            outs.append(o / den)
        lane = lax.broadcasted_iota(I32, outs[0].shape, 1)
        o_ref[:, vsl] = jnp.where(lane < V_HEAD, outs[0], outs[1]).astype(o_ref.dtype)


def _attention(q, k_l, k_c, v_l, v_c, tq, heads):
    b, l, _ = q.shape
    lc = k_c.shape[1]
    hp = N_HEADS // heads
    kw = heads * HEAD_PAD
    vw = heads * V_HEAD
    return pl.pallas_call(
        functools.partial(_attn_kernel, heads=heads),
        grid=(b, hp, l // tq),
        in_specs=[
            pl.BlockSpec((None, tq, kw), lambda i, h, j: (i, j, h)),
            pl.BlockSpec((None, l, kw), lambda i, h, j: (i, 0, h)),
            pl.BlockSpec((None, lc, kw), lambda i, h, j: (i, 0, h)),
            pl.BlockSpec((None, l, vw), lambda i, h, j: (i, 0, h)),
            pl.BlockSpec((None, lc, vw), lambda i, h, j: (i, 0, h)),
        ],
        out_specs=pl.BlockSpec((None, tq, vw), lambda i, h, j: (i, j, h)),
        out_shape=jax.ShapeDtypeStruct((b, l, N_HEADS * V_HEAD), BF16),
        compiler_params=_cparams(("arbitrary", "arbitrary", "arbitrary")),
        name="mla_attention",
    )(q, k_l, k_c, v_l, v_c)


def _mix_cd_kernel(yf_ref, yb_ref, u_ref, o_ref, x_ref, g1_ref, sh2_ref, sc2_ref,
                   dskip_ref, wglu_ref, wout_ref, wr_ref, br_ref, xl_ref, xm_ref, rt_ref):
    dc = u_ref.shape[-1]
    y = yf_ref[...] + yb_ref[...] + u_ref[...] * dskip_ref[...]
    g = jax.nn.gelu(y)
    s5 = (g * _sigmoid(_dot(g.astype(BF16), wglu_ref[...]))).astype(BF16)
    y_mix = _dot(s5, wout_ref[0:dc, :]) + _dot(o_ref[...], wout_ref[dc:, :])
    _residual_and_route(x_ref[...], y_mix, g1_ref[...], sh2_ref[...], sc2_ref[...], wr_ref, br_ref,
                        xl_ref, xm_ref, rt_ref)


def _mix_cd(yf, yb, u, o, x, g1, sh2, sc2, dskip, wglu, wout, wr, br, tm):
    b, l, d = x.shape
    dc = u.shape[-1]
    tok = lambda i, j: (i, j, 0)
    per_b = lambda i, j: (i, 0, 0)
    const2 = lambda i, j: (0, 0)
    return pl.pallas_call(
        _mix_cd_kernel,
        grid=(b, l // tm),
        in_specs=[
            pl.BlockSpec((None, tm, dc), tok),
            pl.BlockSpec((None, tm, dc), tok),
            pl.BlockSpec((None, tm, dc), tok),
            pl.BlockSpec((None, tm, o.shape[-1]), tok),
            pl.BlockSpec((None, tm, d), tok),
            pl.BlockSpec((None, 1, d), per_b),
            pl.BlockSpec((None, 1, d), per_b),
            pl.BlockSpec((None, 1, d), per_b),
            pl.BlockSpec((1, dc), const2),
            pl.BlockSpec(wglu.shape, const2),
            pl.BlockSpec(wout.shape, const2),
            pl.BlockSpec((d, LANE), const2),
            pl.BlockSpec((1, LANE), const2),
        ],
        out_specs=[
            pl.BlockSpec((None, tm, d), tok),
            pl.BlockSpec((tm * (d // LANE), LANE), lambda i, j: (i * (l // tm) + j, 0)),
            pl.BlockSpec((None, tm, LANE), tok),
        ],
        out_shape=[
            jax.ShapeDtypeStruct((b, l, d), F32),
            jax.ShapeDtypeStruct((b * l * (d // LANE), LANE), F32),
            jax.ShapeDtypeStruct((b, l, LANE), F32),
        ],
        compiler_params=_cparams(("arbitrary", "arbitrary")),
        name="mix_cd",
    )(yf, yb, u, o, x, g1, sh2, sc2, dskip, wglu, wout, wr, br)


def _router_weights(w_grp, b_grp, w_exp, b_exp):
    d = w_grp.shape[0]
    pad = LANE - N_GROUPS - N_EXPERTS
    wr = jnp.concatenate([w_grp, w_exp, jnp.zeros((d, pad), F32)], axis=1)
    br = jnp.concatenate([b_grp, b_exp, jnp.zeros((pad,), F32)]).reshape(1, LANE)
    return wr, br


def _pad_heads(w, width):
    k = w.shape[0]
    w = w.reshape(k, N_HEADS, width)
    return jnp.pad(w, ((0, 0), (0, 0), (0, HEAD_PAD - width))).reshape(k, N_HEADS * HEAD_PAD)


def _rope_tables(l):
    rows = l // GRID_W
    row = jnp.repeat(jnp.arange(rows, dtype=F32), GRID_W)
    col = jnp.tile(jnp.arange(GRID_W, dtype=F32), rows)
    inv_freq = ROPE_BASE ** (-jnp.arange(AXIS_PAIRS, dtype=F32) / AXIS_PAIRS)
    ar, ac = row[:, None] * inv_freq, col[:, None] * inv_freq
    one = jnp.ones((l, QK_NOPE), F32)
    zn = jnp.zeros((l, QK_NOPE), F32)
    zp = jnp.zeros((l, HEAD_PAD - QK_HEAD), F32)
    cos = jnp.concatenate([one, jnp.cos(ar), jnp.cos(ar), jnp.cos(ac), jnp.cos(ac), zp], axis=1)
    sin = jnp.concatenate([zn, jnp.sin(ar), jnp.sin(ar), jnp.sin(ac), jnp.sin(ac), zp], axis=1)
    return cos, sin


def _rot_swap(t, signed):
    a = AXIS_PAIRS
    s = -1.0 if signed else 1.0
    return jnp.concatenate([s * t[..., a:2 * a], t[..., 0:a], s * t[..., 3 * a:4 * a], t[..., 2 * a:3 * a]], axis=-1)


def _head_lanes(rot):
    pad = [(0, 0)] * (rot.ndim - 1) + [(QK_NOPE, HEAD_PAD - QK_HEAD)]
    return jnp.pad(rot, pad)


def _split_mod(mod_row, b, d):
    parts = jnp.split(mod_row, 6, axis=-1)
    return [jnp.broadcast_to(p.reshape(-1, 1, d), (b, 1, d)) for p in parts]


def kernel(x, c, ctx, c_ctx, ada_w, ada_b, ab_w_in, sgu_norm, sgu_w, sgu_b, conv_w, ab_w_out, cd_w_in, s5_a_re, s5_a_im, s5_log_dt, s5_b_re, s5_b_im, s5_c_re, s5_c_im, s5_d, s5_w_glu, mla_q_norm, mla_kv_norm, mla_w_uq, mla_w_uk, mla_w_uv, mla_qn_gain, mla_kn_gain, cd_w_out, moe_w_grp, moe_b_grp, moe_w_exp, moe_b_exp, moe_w_gate, moe_w_up, moe_w_down):
    b, l, d = x.shape
    lc = ctx.shape[1]
    depth = ada_w.shape[0]
    assert depth == 2, "layer 0 = gated-MLP/conv mixers, layer 1 = S5/attention mixers"
    tm = min(TOKEN_TILE, l)
    tmc = min(TOKEN_TILE, lc)

    mod_rows = 2 * SUBLANE
    cvec = jnp.zeros((mod_rows, d), F32).at[:b].set(c).at[b].set(c_ctx)
    mod = _ada(cvec, ada_w, ada_b)

    sh1, sc1, g1, sh2, sc2, g2 = _split_mod(mod[0, :b], b, d)
    csh1, csc1, cg1, csh2, csc2, cg2 = _split_mod(mod[0, b:b + 1], b, d)
    w_in = ab_w_in[0].astype(BF16)
    wr, br = _router_weights(moe_w_grp[0], moe_b_grp[0], moe_w_exp[0], moe_b_exp[0])
    sgub = jnp.repeat(sgu_b[0].T, d // G_A, axis=1)
    mix_args = (sgu_norm[0].reshape(1, d), sgu_w[0].astype(BF16), sgub, conv_w[0],
                ab_w_out[0].astype(BF16), wr, br)
    z, p = _inproj_ab(x, sh1, sc1, w_in, tm)
    xl, xm, rt = _mix_ab(z, p, x, g1, sh2, sc2, *mix_args, tm)
    zc, pc = _inproj_ab(ctx, csh1, csc1, w_in, tmc)
    xc, xmc, rtc = _mix_ab(zc, pc, ctx, cg1, csh2, csc2, *mix_args, tmc)
    xl, xc = _hier_moe([(xl, xm, rt, g2), (xc, xmc, rtc, cg2)], 0, moe_w_gate, moe_w_up, moe_w_down)

    sh1, sc1, g1, sh2, sc2, g2 = _split_mod(mod[1, :b], b, d)
    csh1, csc1, _, _, _, _ = _split_mod(mod[1, b:b + 1], b, d)
    d_c = s5_d.shape[-1]
    q_lora = mla_q_norm.shape[-1]
    kv_lora = mla_kv_norm.shape[-1]
    w_cd = cd_w_in[0]
    o_kr = d_c + q_lora + kv_lora
    w_kr = _head_lanes(w_cd[:, o_kr:])
    w_krs = _head_lanes(_rot_swap(w_cd[:, o_kr:], True))
    w_cd = jnp.concatenate([w_cd[:, :o_kr], w_kr, w_krs], axis=1).astype(BF16)
    widths = (d_c, q_lora, kv_lora, HEAD_PAD, HEAD_PAD)
    u_l, cq_l, ckv_l, krp_l, krs_l = _inproj_cd(xl, sh1, sc1, w_cd, widths, tm)
    u_c, _, ckv_c, krp_c, _ = _inproj_cd(xc, csh1, csc1, w_cd, widths, tmc)

    ab_re, ab_im, bb_re, bb_im = _s5_params(s5_a_re[0], s5_a_im[0], s5_log_dt[0], s5_b_re[0], s5_b_im[0])
    a_bc, bsb, csb = _s5_block_matrices(ab_re, ab_im, bb_re, bb_im, s5_c_re[0], s5_c_im[0], b)
    u_tm = jnp.concatenate([u_c.transpose(1, 0, 2), u_l.transpose(1, 0, 2)], axis=0).reshape((lc + l) * b, d_c)
    yf, yb = _s5_scan(u_tm, lc // S5_STEPS, a_bc, bsb, csb, b)
    to_bm = lambda y: y.reshape(lc + l, b, d_c)[lc:].transpose(1, 0, 2)
    yf, yb = to_bm(yf), to_bm(yb)

    wq = _pad_heads(mla_w_uq[0], QK_HEAD).astype(BF16)
    wk = _pad_heads(mla_w_uk[0], QK_NOPE).astype(BF16)
    wv = mla_w_uv[0].astype(BF16)
    w_uq = mla_w_uq[0].reshape(q_lora, N_HEADS, QK_HEAD)
    wqs = _head_lanes(_rot_swap(w_uq[..., QK_NOPE:], True)).reshape(q_lora, N_HEADS * HEAD_PAD).astype(BF16)
    qg, kg = mla_qn_gain[0], mla_kn_gain[0]
    full = lambda g: jnp.pad(g, (0, HEAD_PAD - QK_HEAD)).reshape(1, HEAD_PAD)
    partner = lambda g: _head_lanes(_rot_swap(g[QK_NOPE:], False)).reshape(1, HEAD_PAD)
    cos, sin = _rope_tables(l)
    tables = (cos * full(qg), sin * partner(qg), cos * full(kg), sin * partner(kg))
    qn = mla_q_norm[0].reshape(1, q_lora)
    kvn = mla_kv_norm[0].reshape(1, kv_lora)
    q_l, k_l, v_l = _mla_prep(_mla_prep_lat_kernel, "mla_prep_lat", (cq_l, ckv_l, krp_l, krs_l),
                              (qn, kvn, wq, wqs, wk, wv), tables, 1, tm)
    k_c, v_c = _mla_prep(_mla_prep_ctx_kernel, "mla_prep_ctx", (ckv_c, krp_c), (kvn, wk, wv, full(kg)), (), 0, tmc)
    o_l = _attention(q_l, k_l, k_c, v_l, v_c, ATTN_Q_TILE, ATTN_HEADS)

    wr, br = _router_weights(moe_w_grp[1], moe_b_grp[1], moe_w_exp[1], moe_b_exp[1])
    xl, xm, rt = _mix_cd(yf, yb, u_l, o_l, xl, g1, sh2, sc2, s5_d[0].reshape(1, d_c),
                         s5_w_glu[0].astype(BF16), cd_w_out[0].astype(BF16), wr, br, tm)
    (xl,) = _hier_moe([(xl, xm, rt, g2)], 1, moe_w_gate, moe_w_up, moe_w_down)
    return xl
```

```python
import functools
import math

import jax
import jax.numpy as jnp
from jax import lax
from jax.experimental import pallas as pl
from jax.experimental.pallas import tpu as pltpu

F32 = jnp.float32
BF16 = jnp.bfloat16
I32 = jnp.int32
HIGHEST = lax.Precision.HIGHEST

EPS = 1e-6
GRID_W = 64
CHUNK = 128
G_A = 8
CONV_W = 3
S5_GROUP = 16
S5_STATE = 64
N_HEADS = 8
QK_NOPE = 64
QK_ROPE = 32
QK_HEAD = QK_NOPE + QK_ROPE
V_HEAD = 64
AXIS_PAIRS = QK_ROPE // 4
ROPE_BASE = 10000.0
N_GROUPS = 4
EXPERTS_PER_GROUP = 8
N_EXPERTS = N_GROUPS * EXPERTS_PER_GROUP
TOP_K = 2

LANE = 128
SUBLANE = 8
HEAD_PAD = LANE
TOKEN_TILE = 512
MIX_CD_TILE = 256
MOE_TILE = 512
COMBINE_TILE = 256
PLAN_BLOCK = 1024
MAX_TILES = 256
TOKEN_ROWS = SUBLANE
ATTN_Q_TILE = 256
ATTN_HEADS = 8
S5_STEPS = 64
S5_SUPER = LANE // S5_GROUP
VMEM_LIMIT = 56 * 1024 * 1024


def _cparams(sem):
    return pltpu.CompilerParams(dimension_semantics=sem, vmem_limit_bytes=VMEM_LIMIT)


def _rms(x):
    return x * lax.rsqrt(jnp.mean(x * x, axis=-1, keepdims=True) + EPS)


def _sigmoid(x):
    return 1.0 / (1.0 + jnp.exp(-x))


def _dot(a, b):
    return jnp.dot(a, b, preferred_element_type=F32)


def _ada_kernel(c_ref, w_ref, b_ref, o_ref):
    c = c_ref[...]
    s = c * _sigmoid(c)
    o_ref[...] = jnp.dot(s, w_ref[...], precision=HIGHEST, preferred_element_type=F32) + b_ref[...]


def _ada(cvec, ada_w, ada_b):
    depth, d, n = ada_w.shape
    rows = cvec.shape[0]
    tn = 1024
    return pl.pallas_call(
        _ada_kernel,
        grid=(depth, n // tn),
        in_specs=[
            pl.BlockSpec((rows, d), lambda i, j: (0, 0)),
            pl.BlockSpec((None, d, tn), lambda i, j: (i, 0, j)),
            pl.BlockSpec((None, 1, tn), lambda i, j: (i, 0, j)),
        ],
        out_specs=pl.BlockSpec((None, rows, tn), lambda i, j: (i, 0, j)),
        out_shape=jax.ShapeDtypeStruct((depth, rows, n), F32),
        compiler_params=_cparams(("arbitrary", "arbitrary")),
        name="ada_mod",
    )(cvec, ada_w, ada_b.reshape(depth, 1, n))


def _route(xm, wr_ref, br_ref):
    logits = jnp.dot(xm, wr_ref[...], precision=HIGHEST, preferred_element_type=F32) + br_ref[...]
    lane = lax.broadcasted_iota(I32, logits.shape, 1).astype(F32)
    neg = jnp.float32(-1e30)
    big = jnp.float32(1e6)
    is_grp = lane < N_GROUPS
    gl = jnp.where(is_grp, logits, neg)
    gmax = jnp.max(gl, axis=-1, keepdims=True)
    gsum = jnp.sum(jnp.where(is_grp, jnp.exp(gl - gmax), 0.0), axis=-1, keepdims=True)
    p_top = 1.0 / gsum
    grp = jnp.min(jnp.where(gl == gmax, lane, big), axis=-1, keepdims=True)
    eidx = lane - N_GROUPS
    in_grp = (eidx >= grp * EXPERTS_PER_GROUP) & (eidx < (grp + 1.0) * EXPERTS_PER_GROUP)
    el = jnp.where(in_grp, logits, neg)
    m1 = jnp.max(el, axis=-1, keepdims=True)
    i1 = jnp.min(jnp.where(el == m1, lane, big), axis=-1, keepdims=True)
    el2 = jnp.where(lane == i1, neg, el)
    m2 = jnp.max(el2, axis=-1, keepdims=True)
    i2 = jnp.min(jnp.where(el2 == m2, lane, big), axis=-1, keepdims=True)
    t = jnp.exp(m2 - m1)
    w1 = p_top / (1.0 + t)
    w2 = p_top * t / (1.0 + t)
    e1 = i1 - N_GROUPS
    e2 = i2 - N_GROUPS
    return jnp.where(lane == 0, e1, jnp.where(lane == 1, e2, jnp.where(lane == 2, w1, jnp.where(lane == 3, w2, 0.0))))


def _residual_and_route(x, y, g1, sh2, sc2, wr_ref, br_ref, xl_ref, xm_ref, rt_ref):
    xl = x + g1 * y
    xl_ref[...] = xl
    xm = _rms(xl) * (1.0 + sc2) + sh2
    _store_token_tiles(xm_ref, xm)
    rt_ref[...] = _route(xm, wr_ref, br_ref)


def _inproj_ab_kernel(x_ref, sh_ref, sc_ref, w_ref, z_ref, p_ref):
    d = x_ref.shape[-1]
    h = (_rms(x_ref[...]) * (1.0 + sc_ref[...]) + sh_ref[...]).astype(BF16)
    for j in range(3):
        z_ref[:, j * d:(j + 1) * d] = _dot(h, w_ref[:, j * d:(j + 1) * d]).astype(BF16)
    gate_c = _dot(h, w_ref[:, 3 * d:4 * d])
    xb = _dot(h, w_ref[:, 4 * d:5 * d])
    p_ref[...] = gate_c * xb


def _inproj_ab(x, sh, sc, w_bf16, tm):
    b, l, d = x.shape
    n = w_bf16.shape[1]
    return pl.pallas_call(
        _inproj_ab_kernel,
        grid=(b, l // tm),
        in_specs=[
            pl.BlockSpec((None, tm, d), lambda i, j: (i, j, 0)),
            pl.BlockSpec((None, 1, d), lambda i, j: (i, 0, 0)),
            pl.BlockSpec((None, 1, d), lambda i, j: (i, 0, 0)),
            pl.BlockSpec((d, n), lambda i, j: (0, 0)),
        ],
        out_specs=[
            pl.BlockSpec((None, tm, 3 * d), lambda i, j: (i, j, 0)),
            pl.BlockSpec((None, tm, d), lambda i, j: (i, j, 0)),
        ],
        out_shape=[
            jax.ShapeDtypeStruct((b, l, 3 * d), BF16),
            jax.ShapeDtypeStruct((b, l, d), F32),
        ],
        compiler_params=_cparams(("arbitrary", "arbitrary")),
        name="inproj_ab",
    )(x, sh, sc, w_bf16)


def _mix_ab_kernel(z_ref, p_ref, pprev_ref, pnext_ref, x_ref, g1_ref, sh2_ref, sc2_ref,
                   sgun_ref, sguw_ref, sgub_ref, convw_ref, wout_ref, wr_ref, br_ref,
                   xl_ref, xm_ref, rt_ref, pbuf, ycat):
    i = pl.program_id(1)
    nt = pl.num_programs(1)
    tm, d = x_ref.shape
    cg = d // G_A
    v = jax.nn.gelu(z_ref[:, d:2 * d].astype(F32))
    vc = (_rms(v) * sgun_ref[...]).astype(BF16)
    for c in range(tm // CHUNK):
        r0 = c * CHUNK
        cols = [_dot(sguw_ref[g], vc[r0:r0 + CHUNK, g * cg:(g + 1) * cg]) for g in range(G_A)]
        s = jnp.concatenate(cols, axis=1) + sgub_ref[...]
        u = jax.nn.gelu(z_ref[r0:r0 + CHUNK, 0:d].astype(F32))
        ycat[r0:r0 + CHUNK, 0:d] = (u * s).astype(BF16)
    pbuf[SUBLANE:SUBLANE + tm, :] = p_ref[...]
    pbuf[0:SUBLANE, :] = jnp.where(i > 0, pprev_ref[...], 0.0)
    pbuf[SUBLANE + tm:2 * SUBLANE + tm, :] = jnp.where(i < nt - 1, pnext_ref[...], 0.0)
    conv = (convw_ref[0:1, :] * pbuf[SUBLANE - 1:SUBLANE - 1 + tm, :]
            + convw_ref[1:2, :] * pbuf[SUBLANE:SUBLANE + tm, :]
            + convw_ref[2:3, :] * pbuf[SUBLANE + 1:SUBLANE + 1 + tm, :])
    ycat[:, d:2 * d] = (z_ref[:, 2 * d:3 * d].astype(F32) * conv).astype(BF16)
    y = _dot(ycat[...], wout_ref[...])
    _residual_and_route(x_ref[...], y, g1_ref[...], sh2_ref[...], sc2_ref[...], wr_ref, br_ref,
                        xl_ref, xm_ref, rt_ref)


def _mix_ab(z, p, x, g1, sh2, sc2, sgun, sguw, sgub, convw, wout, wr, br, tm):
    b, l, d = x.shape
    hb = tm // SUBLANE
    nhb = l // SUBLANE
    tok = lambda i, j: (i, j, 0)
    per_b = lambda i, j: (i, 0, 0)
    const2 = lambda i, j: (0, 0)
    return pl.pallas_call(
        _mix_ab_kernel,
        grid=(b, l // tm),
        in_specs=[
            pl.BlockSpec((None, tm, 3 * d), tok),
            pl.BlockSpec((None, tm, d), tok),
            pl.BlockSpec((None, SUBLANE, d), lambda i, j: (i, jnp.maximum(j * hb - 1, 0), 0)),
            pl.BlockSpec((None, SUBLANE, d), lambda i, j: (i, jnp.minimum((j + 1) * hb, nhb - 1), 0)),
            pl.BlockSpec((None, tm, d), tok),
            pl.BlockSpec((None, 1, d), per_b),
            pl.BlockSpec((None, 1, d), per_b),
            pl.BlockSpec((None, 1, d), per_b),
            pl.BlockSpec((1, d), const2),
            pl.BlockSpec((G_A, CHUNK, CHUNK), lambda i, j: (0, 0, 0)),
            pl.BlockSpec((CHUNK, d), const2),
            pl.BlockSpec((CONV_W, d), const2),
            pl.BlockSpec((2 * d, d), const2),
            pl.BlockSpec((d, LANE), const2),
            pl.BlockSpec((1, LANE), const2),
        ],
        out_specs=[
            pl.BlockSpec((None, tm, d), tok),
            pl.BlockSpec((tm * TOKEN_ROWS, LANE), lambda i, j: (i * (l // tm) + j, 0)),
            pl.BlockSpec((None, tm, LANE), tok),
        ],
        out_shape=[
            jax.ShapeDtypeStruct((b, l, d), F32),
            jax.ShapeDtypeStruct((b * l * TOKEN_ROWS, LANE), F32),
            jax.ShapeDtypeStruct((b, l, LANE), F32),
        ],
        scratch_shapes=[
            pltpu.VMEM((tm + 2 * SUBLANE, d), F32),
            pltpu.VMEM((tm, 2 * d), BF16),
        ],
        compiler_params=_cparams(("arbitrary", "arbitrary")),
        name="mix_ab",
    )(z, p, p, p, x, g1, sh2, sc2, sgun, sguw, sgub, convw, wout, wr, br)


def _store_token_tiles(ref, val):
    tm, d = val.shape
    rpt = d // LANE
    for j in range(rpt):
        ref[pl.ds(j, tm, stride=rpt), :] = val[:, j * LANE:(j + 1) * LANE]


def _load_token_tile_col(ref, tm, rpt, j):
    return ref[pl.ds(j, tm, stride=rpt), :]


def _plan_kernel(rt_ref, tri_ref, pos_ref, te_ref, tv_ref, pad_ref, run, *, tm):
    ps = pl.program_id(0)
    i = pl.program_id(1)
    blk = rt_ref.shape[0]
    ne = N_EXPERTS

    @pl.when(jnp.logical_and(ps == 0, i == 0))
    def _():
        run[...] = jnp.zeros_like(run)

    @pl.when(jnp.logical_and(ps == 1, i == 0))
    def _():
        counts = run[...]
        padded = jnp.floor((counts + (tm - 1)) * (1.0 / tm)) * tm
        r = lax.broadcasted_iota(I32, (ne, ne), 0)
        c = lax.broadcasted_iota(I32, (ne, ne), 1)
        starts = jnp.dot((c < r).astype(F32), padded, precision=HIGHEST, preferred_element_type=F32)
        nt = te_ref.shape[1]
        ends = jnp.concatenate([(starts + padded) * (1.0 / tm)] * (nt // LANE), axis=1)
        t = lax.broadcasted_iota(I32, (ne, nt), 1).astype(F32)
        te = jnp.sum((t >= ends).astype(F32), axis=0, keepdims=True)
        total = jnp.max(ends, axis=0, keepdims=True)
        te_ref[...] = jnp.minimum(te, ne - 1.0).astype(I32)
        tv_ref[...] = (t[0:1, :] < total).astype(I32)
        lane = lax.broadcasted_iota(I32, (ne, LANE), 1)
        pad_ref[...] = jnp.where(lane == 0, starts + counts,
                                 jnp.where(lane == 1, padded - counts, 0.0)).astype(I32)
        run[...] = starts

    slab_t = rt_ref[...].T
    ex = lax.broadcasted_iota(I32, (ne, blk), 0).astype(F32)
    oh1 = ex == slab_t[0:1, :]
    oh2 = ex == slab_t[1:2, :]
    oh = jnp.where(jnp.logical_or(oh1, oh2), 1.0, 0.0)
    prefix = _dot(oh.astype(BF16), tri_ref[...]) + run[:, 0:1]
    pos1 = jnp.sum(jnp.where(oh1, prefix, 0.0), axis=0, keepdims=True)
    pos2 = jnp.sum(jnp.where(oh2, prefix, 0.0), axis=0, keepdims=True)
    pos_ref[...] = jnp.concatenate([pos1, pos2], axis=0).astype(I32)
    run[...] = run[...] + jnp.sum(oh, axis=1, keepdims=True)


def _moe_plan(rt_all, tm):
    n = rt_all.shape[0]
    blk = PLAN_BLOCK
    nb = n // blk
    tri = jnp.triu(jnp.ones((blk, blk), BF16), k=1)
    return pl.pallas_call(
        functools.partial(_plan_kernel, tm=tm),
        grid=(2, nb),
        in_specs=[
            pl.BlockSpec((blk, LANE), lambda p, i: (i, 0)),
            pl.BlockSpec((blk, blk), lambda p, i: (0, 0)),
        ],
        out_specs=[
            pl.BlockSpec((None, TOP_K, blk), lambda p, i: (jnp.where(p == 0, nb, i), 0, 0)),
            pl.BlockSpec((1, MAX_TILES), lambda p, i: (0, 0)),
            pl.BlockSpec((1, MAX_TILES), lambda p, i: (0, 0)),
            pl.BlockSpec((N_EXPERTS, LANE), lambda p, i: (0, 0)),
        ],
        out_shape=[
            jax.ShapeDtypeStruct((nb + 1, TOP_K, blk), I32),
            jax.ShapeDtypeStruct((1, MAX_TILES), I32),
            jax.ShapeDtypeStruct((1, MAX_TILES), I32),
            jax.ShapeDtypeStruct((N_EXPERTS, LANE), I32),
        ],
        scratch_shapes=[pltpu.VMEM((N_EXPERTS, LANE), F32)],
        compiler_params=_cparams(("arbitrary", "arbitrary")),
        name="moe_plan",
    )(rt_all, tri)


def _dispatch_kernel(ps_ref, pl_ref, tv_ref, pos_ref, *rest, blocks, n_tiles):
    xm_refs = rest[:len(blocks)]
    xs_hbm, zbuf, sem = rest[len(blocks):]
    i = pl.program_id(0)
    tt = pos_ref.shape[1]
    rpt = xm_refs[0].shape[0] // tt
    tile_rows = zbuf.shape[0]

    @pl.when(i == 0)
    def _():
        zbuf[...] = jnp.zeros_like(zbuf)

        def gap_copies(e, wait):
            base = ps_ref[e]
            n = pl_ref[e]
            bit = tile_rows // (2 * rpt)
            while bit >= 1:
                lower = n & (bit - 1)

                @pl.when((n & bit) != 0)
                def _(lower=lower, bit=bit):
                    dst = xs_hbm.at[pl.ds((base + lower) * rpt, bit * rpt)]
                    cp = pltpu.make_async_copy(zbuf.at[pl.ds(0, bit * rpt)], dst, sem.at[1])
                    cp.wait() if wait else cp.start()
                bit //= 2

        def tile_copy(t, wait):
            @pl.when(tv_ref[t] != 1)
            def _():
                dst = xs_hbm.at[pl.ds(pl.multiple_of(t * tile_rows, tile_rows), tile_rows)]
                cp = pltpu.make_async_copy(zbuf, dst, sem.at[1])
                cp.wait() if wait else cp.start()

        for wait in (False, True):
            lax.fori_loop(0, N_EXPERTS, lambda e, c, w=wait: (gap_copies(e, w), c)[1], 0)
            lax.fori_loop(0, n_tiles, lambda t, c, w=wait: (tile_copy(t, w), c)[1], 0)

    def scatter(xm_ref):
        def issue(r, carry):
            src = xm_ref.at[pl.ds(pl.multiple_of(r * rpt, rpt), rpt)]
            for k in range(TOP_K):
                dst = xs_hbm.at[pl.ds(pl.multiple_of(pos_ref[k, r] * rpt, rpt), rpt)]
                pltpu.make_async_copy(src, dst, sem.at[0]).start(priority=k)
            return carry
        lax.fori_loop(0, tt, issue, 0, unroll=8)
        for k in range(TOP_K):
            pltpu.make_async_copy(xm_ref, xs_hbm.at[pl.ds(0, tt * rpt)], sem.at[0]).wait()

    lo = 0
    for xm_ref, nblk in zip(xm_refs, blocks):
        pl.when(jnp.logical_and(i >= lo, i < lo + nblk))(functools.partial(scatter, xm_ref))
        lo += nblk


def _dispatch(pad_start, pad_len, valid, pos, xm_streams, n_tiles, tm):
    tt = pos.shape[-1]
    rpt = TOKEN_ROWS
    blocks = tuple(xm.shape[0] // (rpt * tt) for xm in xm_streams)
    in_specs = [pl.BlockSpec((None, TOP_K, tt), lambda i, a, b, c: (i, 0, 0), memory_space=pltpu.SMEM)]
    lo = 0
    for nblk in blocks:
        in_specs.append(pl.BlockSpec(
            (tt * rpt, LANE), lambda i, a, b, c, lo=lo, nblk=nblk: (jnp.clip(i - lo, 0, nblk - 1), 0)))
        lo += nblk
    return pl.pallas_call(
        functools.partial(_dispatch_kernel, blocks=blocks, n_tiles=n_tiles),
        grid_spec=pltpu.PrefetchScalarGridSpec(
            num_scalar_prefetch=3,
            grid=(sum(blocks),),
            in_specs=in_specs,
            out_specs=pl.BlockSpec(memory_space=pl.ANY),
            scratch_shapes=[pltpu.VMEM((tm * rpt, LANE), F32), pltpu.SemaphoreType.DMA((2,))],
        ),
        out_shape=jax.ShapeDtypeStruct((n_tiles * tm * rpt, LANE), F32),
        compiler_params=_cparams(("arbitrary",)),
        name="moe_dispatch",
    )(pad_start, pad_len, valid, pos, *xm_streams)


def _moe_kernel(te_ref, tv_ref, x_ref, wg_ref, wu_ref, wd_ref, y_ref, wg_s, wu_s, wd_s):
    t = pl.program_id(0)
    rpt = TOKEN_ROWS
    tm = x_ref.shape[0] // rpt

    @pl.when(tv_ref[t] == 1)
    def _():
        changed = jnp.logical_or(t == 0, te_ref[t] != te_ref[jnp.maximum(t - 1, 0)])

        @pl.when(changed)
        def _():
            wg_s[...] = wg_ref[...].astype(BF16)
            wu_s[...] = wu_ref[...].astype(BF16)
            wd_s[...] = wd_ref[...].astype(BF16)

        x = jnp.concatenate([_load_token_tile_col(x_ref, tm, rpt, j) for j in range(rpt)], axis=1).astype(BF16)
        g = _dot(x, wg_s[...])
        u = _dot(x, wu_s[...])
        h = (g * _sigmoid(g) * u).astype(BF16)
        _store_token_tiles(y_ref, _dot(h, wd_s[...]))

    @pl.when(tv_ref[t] != 1)
    def _():
        y_ref[...] = jnp.zeros_like(y_ref)


def _moe_experts(xs, tile_e, valid, n_tiles, layer, w_gate, w_up, w_down, tm):
    _, _, d, de = w_gate.shape
    rpt = TOKEN_ROWS
    grid_spec = pltpu.PrefetchScalarGridSpec(
        num_scalar_prefetch=2,
        grid=(n_tiles,),
        in_specs=[
            pl.BlockSpec((tm * rpt, LANE), lambda t, te, tv: (t * tv[t], 0)),
            pl.BlockSpec((None, None, d, de), lambda t, te, tv: (layer, te[t], 0, 0)),
            pl.BlockSpec((None, None, d, de), lambda t, te, tv: (layer, te[t], 0, 0)),
            pl.BlockSpec((None, None, de, d), lambda t, te, tv: (layer, te[t], 0, 0)),
        ],
        out_specs=pl.BlockSpec((tm * rpt, LANE), lambda t, te, tv: (t, 0)),
        scratch_shapes=[
            pltpu.VMEM((d, de), BF16),
            pltpu.VMEM((d, de), BF16),
            pltpu.VMEM((de, d), BF16),
        ],
    )
    return pl.pallas_call(
        _moe_kernel,
        grid_spec=grid_spec,
        out_shape=jax.ShapeDtypeStruct((n_tiles * tm * rpt, LANE), F32),
        compiler_params=_cparams(("arbitrary",)),
        name="moe_experts",
    )(tile_e, valid, xs, w_gate, w_up, w_down)


def _combine_kernel(pos0_ref, posn_ref, x_ref, g_ref, rt_ref, ys_hbm, o_ref, buf, sem, *, n):
    i = pl.program_id(0)
    tt, d = x_ref.shape
    rpt = TOKEN_ROWS

    def gather(pos_ref, slot):
        def issue(r, carry):
            for k in range(TOP_K):
                src = ys_hbm.at[pl.ds(pl.multiple_of(pos_ref[k, r] * rpt, rpt), rpt)]
                dst = buf.at[slot, k, pl.ds(pl.multiple_of(r * rpt, rpt), rpt)]
                pltpu.make_async_copy(src, dst, sem.at[slot, k]).start(priority=k)
            return carry
        lax.fori_loop(0, tt, issue, 0, unroll=8)

    slot = lax.rem(i, 2)

    @pl.when(i == 0)
    def _():
        gather(pos0_ref, 0)

    @pl.when(i + 1 < n)
    def _():
        gather(posn_ref, 1 - slot)

    for k in range(TOP_K):
        pltpu.make_async_copy(ys_hbm.at[pl.ds(0, tt * rpt)], buf.at[slot, k], sem.at[slot, k]).wait()
    w1 = rt_ref[:, 2:3]
    w2 = rt_ref[:, 3:4]
    for j in range(rpt):
        r1 = buf[slot, 0, pl.ds(j, tt, stride=rpt), :]
        r2 = buf[slot, 1, pl.ds(j, tt, stride=rpt), :]
        sl = slice(j * LANE, (j + 1) * LANE)
        o_ref[:, sl] = x_ref[:, sl] + g_ref[:, sl] * (w1 * r1 + w2 * r2)


def _moe_combine(x, g2, rt, pos, tok_off, ys, tt):
    b, l, d = x.shape
    rpt = TOKEN_ROWS
    n = b * l
    per_b = l // tt
    per_blk = PLAN_BLOCK // tt
    blk0 = tok_off // PLAN_BLOCK

    def pos_map(step):
        return lambda i: (blk0 + step(i) // per_blk, 0, step(i) % per_blk)

    return pl.pallas_call(
        functools.partial(_combine_kernel, n=n // tt),
        grid=(n // tt,),
        in_specs=[
            pl.BlockSpec((None, TOP_K, tt), pos_map(lambda i: i), memory_space=pltpu.SMEM),
            pl.BlockSpec((None, TOP_K, tt), pos_map(lambda i: jnp.minimum(i + 1, n // tt - 1)),
                         memory_space=pltpu.SMEM),
            pl.BlockSpec((tt, d), lambda i: (i, 0)),
            pl.BlockSpec((None, 1, d), lambda i: (i // per_b, 0, 0)),
            pl.BlockSpec((tt, LANE), lambda i: (i, 0)),
            pl.BlockSpec(memory_space=pl.ANY),
        ],
        out_specs=pl.BlockSpec((tt, d), lambda i: (i, 0)),
        out_shape=jax.ShapeDtypeStruct((n, d), F32),
        scratch_shapes=[
            pltpu.VMEM((2, TOP_K, tt * rpt, LANE), F32),
            pltpu.SemaphoreType.DMA((2, TOP_K)),
        ],
        compiler_params=_cparams(("arbitrary",)),
        name="moe_combine",
    )(pos, pos, x.reshape(n, d), g2, rt.reshape(n, LANE), ys).reshape(b, l, d)


def _hier_moe(streams, layer, w_gate, w_up, w_down):
    sizes = [s[0].shape[0] * s[0].shape[1] for s in streams]
    n = sum(sizes)
    tm = MOE_TILE
    n_tiles = (TOP_K * n) // tm + N_EXPERTS
    assert n_tiles <= MAX_TILES and all(sz % PLAN_BLOCK == 0 for sz in sizes)
    rt_all = jnp.concatenate([s[2].reshape(-1, LANE) for s in streams], axis=0)
    pos, tile_e, valid, pad = _moe_plan(rt_all, tm)
    tile_e, valid = tile_e.reshape(-1), valid.reshape(-1)
    xs = _dispatch(pad[:, 0], pad[:, 1], valid, pos, [s[1] for s in streams], n_tiles, tm)
    ys = _moe_experts(xs, tile_e, valid, n_tiles, layer, w_gate, w_up, w_down, tm)
    outs = []
    off = 0
    for (xl, _, rt, g2), sz in zip(streams, sizes):
        outs.append(_moe_combine(xl, g2, rt, pos, off, ys, COMBINE_TILE))
        off += sz
    return outs


def _inproj_cd_kernel(x_ref, sh_ref, sc_ref, w_ref, u_ref, *out_refs):
    h = (_rms(x_ref[...]) * (1.0 + sc_ref[...]) + sh_ref[...]).astype(BF16)
    tm = x_ref.shape[0]
    n_slab, rows, cw = u_ref.shape
    nb = rows // tm
    bi = pl.program_id(1)
    u = _dot(h, w_ref[:, 0:n_slab * cw])
    for s in range(n_slab):
        u_ref[s, pl.ds(bi, tm, stride=nb), :] = u[:, s * cw:(s + 1) * cw]
    off = n_slab * cw
    for ref in out_refs:
        n = ref.shape[-1]
        ref[...] = _dot(h, w_ref[:, off:off + n]).astype(ref.dtype)
        off += n


def _inproj_cd(x, sh, sc, w_bf16, widths, tm):
    b, l, d = x.shape
    n = w_bf16.shape[1]
    n_slab = widths[0] // LANE
    tok = lambda j, i: (i, j, 0)
    return pl.pallas_call(
        _inproj_cd_kernel,
        grid=(l // tm, b),
        in_specs=[
            pl.BlockSpec((None, tm, d), tok),
            pl.BlockSpec((None, 1, d), lambda j, i: (i, 0, 0)),
            pl.BlockSpec((None, 1, d), lambda j, i: (i, 0, 0)),
            pl.BlockSpec((d, n), lambda j, i: (0, 0)),
        ],
        out_specs=([pl.BlockSpec((n_slab, tm * b, LANE), lambda j, i: (0, j, 0))]
                   + [pl.BlockSpec((None, tm, w), tok) for w in widths[1:]]),
        out_shape=([jax.ShapeDtypeStruct((n_slab, l * b, LANE), F32)]
                   + [jax.ShapeDtypeStruct((b, l, w), F32) for w in widths[1:]]),
        compiler_params=_cparams(("arbitrary", "arbitrary")),
        name="inproj_cd",
    )(x, sh, sc, w_bf16)


def _s5_param_kernel(are_ref, aim_ref, ldt_ref, bre_ref, bim_ref, abre_ref, abim_ref, bbre_ref, bbim_ref):
    a_re = are_ref[...]
    a_im = aim_ref[...]
    dt = jnp.exp(ldt_ref[...])
    mag = jnp.exp(dt * a_re)
    ab_re = mag * jnp.cos(dt * a_im)
    ab_im = mag * jnp.sin(dt * a_im)
    den = a_re * a_re + a_im * a_im
    nr = ab_re - 1.0
    f_re = (nr * a_re + ab_im * a_im) / den
    f_im = (ab_im * a_re - nr * a_im) / den
    abre_ref[...] = ab_re
    abim_ref[...] = ab_im
    b_re = bre_ref[...]
    b_im = bim_ref[...]
    bbre_ref[...] = f_re[None] * b_re - f_im[None] * b_im
    bbim_ref[...] = f_re[None] * b_im + f_im[None] * b_re


def _s5_params(a_re, a_im, log_dt, b_re, b_im):
    nd, g, p = a_re.shape
    c = b_re.shape[-1]
    rows = nd * g * p // LANE
    flat = lambda t: t.reshape(rows, LANE)
    chan_major = lambda t: jnp.moveaxis(t, -1, 0).reshape(c, rows, LANE)
    ldt = jnp.broadcast_to(log_dt[:, :, None], (nd, g, p))
    shapes = [jax.ShapeDtypeStruct((rows, LANE), F32)] * 2 + [jax.ShapeDtypeStruct((c, rows, LANE), F32)] * 2
    ab_re, ab_im, bb_re, bb_im = pl.pallas_call(
        _s5_param_kernel, out_shape=shapes, name="s5_params",
    )(flat(a_re), flat(a_im), flat(ldt), chan_major(b_re), chan_major(b_im))
    unflat = lambda t: jnp.moveaxis(t.reshape(c, nd, g, p), 0, -1)
    return ab_re.reshape(nd, g, p), ab_im.reshape(nd, g, p), unflat(bb_re), unflat(bb_im)


def _s5_kernel(ufc_ref, ubc_ref, ufl_ref, ubl_ref, a_ref, bsb_ref, csb_ref, d_ref, yf_ref, yb_ref,
               hf, hb, cf, cb, *, n_ctx):
    i = pl.program_id(0)
    nb = cf.shape[0]
    steps = hf.shape[0] // nb
    n_super = bsb_ref.shape[1]
    cw = bsb_ref.shape[2]
    sw = bsb_ref.shape[3]
    half = sw // 2

    @pl.when(i == 0)
    def _():
        cf[...] = jnp.zeros_like(cf)
        cb[...] = jnp.zeros_like(cb)

    def drive(uf_ref, ub_ref):
        for s in range(n_super):
            hf[:, s * sw:(s + 1) * sw] = _dot(uf_ref[s].astype(BF16), bsb_ref[0, s])
            hb[:, s * sw:(s + 1) * sw] = _dot(ub_ref[s].astype(BF16), bsb_ref[1, s])

    pl.when(i < n_ctx)(functools.partial(drive, ufc_ref, ubc_ref))
    pl.when(i >= n_ctx)(functools.partial(drive, ufl_ref, ubl_ref))

    for s in range(n_super):
        re = slice(s * sw, s * sw + half)
        im = slice(s * sw + half, (s + 1) * sw)
        st = slice(s * half, (s + 1) * half)
        arf, aif = a_ref[0, 0, :, st], a_ref[0, 1, :, st]
        arb, aib = a_ref[1, 0, :, st], a_ref[1, 1, :, st]

        def step(t, carry):
            hfr, hfi, hbr, hbi = carry
            rf = pl.multiple_of(t * nb, nb)
            nfr = arf * hfr - aif * hfi + hf[pl.ds(rf, nb), re]
            nfi = arf * hfi + aif * hfr + hf[pl.ds(rf, nb), im]
            hf[pl.ds(rf, nb), re] = nfr
            hf[pl.ds(rf, nb), im] = nfi
            rb = pl.multiple_of((steps - 1 - t) * nb, nb)
            nbr = arb * hbr - aib * hbi + hb[pl.ds(rb, nb), re]
            nbi = arb * hbi + aib * hbr + hb[pl.ds(rb, nb), im]
            hb[pl.ds(rb, nb), re] = nbr
            hb[pl.ds(rb, nb), im] = nbi
            return nfr, nfi, nbr, nbi

        out = lax.fori_loop(0, steps, step, (cf[:, re], cf[:, im], cb[:, re], cb[:, im]), unroll=4)
        cf[:, re], cf[:, im], cb[:, re], cb[:, im] = out

    @pl.when(i >= n_ctx)
    def _():
        for s in range(n_super):
            skip = ufl_ref[s] * d_ref[:, s * cw:(s + 1) * cw]
            yf_ref[s] = _dot(hf[:, s * sw:(s + 1) * sw].astype(BF16), csb_ref[0, s]) + skip
            yb_ref[s] = _dot(hb[:, s * sw:(s + 1) * sw].astype(BF16), csb_ref[1, s])


def _s5_scan(u_c, u_l, a_bc, bsb, csb, d_skip, nb):
    n_slab, rows_l, _ = u_l.shape
    blk = S5_STEPS * nb
    nc = u_c.shape[1] // blk
    nl = rows_l // blk
    n_state = bsb.shape[1] * bsb.shape[3]
    clip = jnp.clip
    ctx_f = lambda i: (0, clip(i, 0, nc - 1), 0)
    ctx_b = lambda i: (0, clip(nc - 1 - i, 0, nc - 1), 0)
    lat_f = lambda i: (0, clip(i - nc, 0, nl - 1), 0)
    lat_b = lambda i: (0, clip(nl - 1 - (i - nc), 0, nl - 1), 0)
    whole = lambda a: pl.BlockSpec(a.shape, lambda i: (0,) * a.ndim)
    slab = lambda index_map: pl.BlockSpec((n_slab, blk, LANE), index_map)
    return pl.pallas_call(
        functools.partial(_s5_kernel, n_ctx=nc),
        grid=(nc + nl,),
        in_specs=[slab(ctx_f), slab(ctx_b), slab(lat_f), slab(lat_b), whole(a_bc), whole(bsb), whole(csb),
                  whole(d_skip)],
        out_specs=[slab(lat_f), slab(lat_b)],
        out_shape=[jax.ShapeDtypeStruct(u_l.shape, F32)] * 2,
        scratch_shapes=[
            pltpu.VMEM((blk, n_state), F32),
            pltpu.VMEM((blk, n_state), F32),
            pltpu.VMEM((nb, n_state), F32),
            pltpu.VMEM((nb, n_state), F32),
        ],
        compiler_params=_cparams(("arbitrary",)),
        name="s5_scan",
    )(u_c, u_c, u_l, u_l, a_bc, bsb, csb, d_skip)


def _s5_block_matrices(ab_re, ab_im, bb_re, bb_im, c_re, c_im, nb):
    nd, g, p = ab_re.shape
    c = bb_re.shape[-1]
    ns = g // S5_SUPER
    eye = jnp.eye(S5_SUPER, dtype=F32)

    def in_mat(bb):
        t = bb.reshape(nd, ns, S5_SUPER, p, c)
        return jnp.einsum('dsgpc,gh->dsgchp', t, eye).reshape(nd, ns, S5_SUPER * c, S5_SUPER * p)

    def out_mat(cc):
        t = cc.reshape(nd, ns, S5_SUPER, c, p)
        return jnp.einsum('dsgcp,gh->dsgphc', t, eye).reshape(nd, ns, S5_SUPER * p, S5_SUPER * c)

    bsb = jnp.concatenate([in_mat(bb_re), in_mat(bb_im)], axis=-1).astype(BF16)
    csb = jnp.concatenate([out_mat(c_re), out_mat(-c_im)], axis=-2).astype(BF16)
    a_bc = jnp.stack([ab_re.reshape(nd, g * p), ab_im.reshape(nd, g * p)], axis=1)
    a_bc = jnp.broadcast_to(a_bc[:, :, None, :], (nd, 2, nb, g * p))
    return a_bc, bsb, csb


def _head_scale(x):
    return lax.rsqrt(jnp.sum(x * x, axis=-1, keepdims=True) * (1.0 / QK_HEAD) + EPS)


def _mla_prep_lat_kernel(cq_ref, ckv_ref, krp_ref, krs_ref, qn_ref, kvn_ref, wq_ref, wqs_ref, wk_ref, wv_ref,
                         qc_ref, qs_ref, kc_ref, ks_ref, q_ref, k_ref, v_ref):
    ckvn = (_rms(ckv_ref[...]) * kvn_ref[...]).astype(BF16)
    v_ref[...] = _dot(ckvn, wv_ref[...]).astype(v_ref.dtype)
    kn = _dot(ckvn, wk_ref[...])
    krp = krp_ref[...]
    k_swap = krs_ref[...] * ks_ref[...]
    kc = kc_ref[...]
    for h in range(N_HEADS):
        sl = slice(h * HEAD_PAD, (h + 1) * HEAD_PAD)
        kh = kn[:, sl] + krp
        k_ref[:, sl] = (_head_scale(kh) * (kh * kc + k_swap)).astype(k_ref.dtype)
    cqn = (_rms(cq_ref[...]) * qn_ref[...]).astype(BF16)
    qn = _dot(cqn, wq_ref[...])
    q_swap = _dot(cqn, wqs_ref[...])
    qc = qc_ref[...]
    qs = qs_ref[...]
    for h in range(N_HEADS):
        sl = slice(h * HEAD_PAD, (h + 1) * HEAD_PAD)
        qh = qn[:, sl]
        scale = _head_scale(qh) * (QK_HEAD ** -0.5)
        q_ref[:, sl] = (scale * (qh * qc + q_swap[:, sl] * qs)).astype(q_ref.dtype)


def _mla_prep_ctx_kernel(ckv_ref, krp_ref, kvn_ref, wk_ref, wv_ref, kg_ref, k_ref, v_ref):
    ckvn = (_rms(ckv_ref[...]) * kvn_ref[...]).astype(BF16)
    v_ref[...] = _dot(ckvn, wv_ref[...]).astype(v_ref.dtype)
    kn = _dot(ckvn, wk_ref[...])
    krp = krp_ref[...]
    kg = kg_ref[...]
    for h in range(N_HEADS):
        sl = slice(h * HEAD_PAD, (h + 1) * HEAD_PAD)
        kh = kn[:, sl] + krp
        k_ref[:, sl] = (_head_scale(kh) * (kh * kg)).astype(k_ref.dtype)


def _mla_prep(kernel_fn, name, tok_args, const_args, table_args, n_q_out, tm):
    b, l, _ = tok_args[0].shape
    tok = lambda i, j: (i, j, 0)
    specs = [pl.BlockSpec((None, tm, a.shape[-1]), tok) for a in tok_args]
    specs += [pl.BlockSpec(a.shape, lambda i, j: (0, 0)) for a in const_args]
    specs += [pl.BlockSpec((tm, HEAD_PAD), lambda i, j: (j, 0)) for _ in table_args]
    hk = N_HEADS * HEAD_PAD
    hv = N_HEADS * V_HEAD
    widths = [hk] * n_q_out + [hk, hv]
    return pl.pallas_call(
        kernel_fn,
        grid=(b, l // tm),
        in_specs=specs,
        out_specs=[pl.BlockSpec((None, tm, w), tok) for w in widths],
        out_shape=[jax.ShapeDtypeStruct((b, l, w), BF16) for w in widths],
        compiler_params=_cparams(("arbitrary", "arbitrary")),
        name=name,
    )(*tok_args, *const_args, *table_args)


def _attn_kernel(q_ref, kl_ref, kc_ref, vl_ref, vc_ref, o_ref, *, heads):
    nt = (((1,), (1,)), ((), ()))
    for pr in range(heads // 2):
        vsl = slice(pr * 2 * V_HEAD, (pr + 1) * 2 * V_HEAD)
        outs = []
        for hh in range(2):
            h = 2 * pr + hh
            sl = slice(h * HEAD_PAD, (h + 1) * HEAD_PAD)
            q = q_ref[:, sl]
            s_l = lax.dot_general(q, kl_ref[:, sl], nt, preferred_element_type=F32)
            s_c = lax.dot_general(q, kc_ref[:, sl], nt, preferred_element_type=F32)
            m = jnp.maximum(jnp.max(s_l, axis=-1, keepdims=True), jnp.max(s_c, axis=-1, keepdims=True))
            p_l = jnp.exp(s_l - m)
            p_c = jnp.exp(s_c - m)
            den = jnp.sum(p_l, axis=-1, keepdims=True) + jnp.sum(p_c, axis=-1, keepdims=True)
            o = _dot(p_l.astype(BF16), vl_ref[:, vsl]) + _dot(p_c.astype(BF16), vc_ref[:, vsl])
            outs.append(o / den)
        lane = lax.broadcasted_iota(I32, outs[0].shape, 1)
        o_ref[:, vsl] = jnp.where(lane < V_HEAD, outs[0], outs[1]).astype(o_ref.dtype)


def _attention(q, k_l, k_c, v_l, v_c, tq, heads):
    b, l, _ = q.shape
    lc = k_c.shape[1]
    hp = N_HEADS // heads
    kw = heads * HEAD_PAD
    vw = heads * V_HEAD
    return pl.pallas_call(
        functools.partial(_attn_kernel, heads=heads),
        grid=(b, hp, l // tq),
        in_specs=[
            pl.BlockSpec((None, tq, kw), lambda i, h, j: (i, j, h)),
            pl.BlockSpec((None, l, kw), lambda i, h, j: (i, 0, h)),
            pl.BlockSpec((None, lc, kw), lambda i, h, j: (i, 0, h)),
            pl.BlockSpec((None, l, vw), lambda i, h, j: (i, 0, h)),
            pl.BlockSpec((None, lc, vw), lambda i, h, j: (i, 0, h)),
        ],
        out_specs=pl.BlockSpec((None, tq, vw), lambda i, h, j: (i, j, h)),
        out_shape=jax.ShapeDtypeStruct((b, l, N_HEADS * V_HEAD), BF16),
        compiler_params=_cparams(("arbitrary", "arbitrary", "arbitrary")),
        name="mla_attention",
    )(q, k_l, k_c, v_l, v_c)


def _mix_cd_kernel(yf_ref, yb_ref, o_ref, x_ref, g1_ref, sh2_ref, sc2_ref,
                   wglu_ref, wout_ref, wr_ref, br_ref, xl_ref, xm_ref, rt_ref):
    tm = x_ref.shape[0]
    n_slab, rows, _ = yf_ref.shape
    nb = rows // tm
    bi = pl.program_id(1)
    dc = n_slab * LANE
    pick = lambda ref, s: ref[s, pl.ds(bi, tm, stride=nb), :]
    y = jnp.concatenate([pick(yf_ref, s) + pick(yb_ref, s) for s in range(n_slab)], axis=1)
    g = jax.nn.gelu(y)
    s5 = (g * _sigmoid(_dot(g.astype(BF16), wglu_ref[...]))).astype(BF16)
    y_mix = _dot(s5, wout_ref[0:dc, :]) + _dot(o_ref[...], wout_ref[dc:, :])
    _residual_and_route(x_ref[...], y_mix, g1_ref[...], sh2_ref[...], sc2_ref[...], wr_ref, br_ref,
                        xl_ref, xm_ref, rt_ref)


def _mix_cd(yf, yb, o, x, g1, sh2, sc2, wglu, wout, wr, br, tm):
    b, l, d = x.shape
    n_slab = yf.shape[0]
    tok = lambda j, i: (i, j, 0)
    per_b = lambda j, i: (i, 0, 0)
    const2 = lambda j, i: (0, 0)
    tm_blk = pl.BlockSpec((n_slab, tm * b, LANE), lambda j, i: (0, j, 0))
    return pl.pallas_call(
        _mix_cd_kernel,
        grid=(l // tm, b),
        in_specs=[
            tm_blk,
            tm_blk,
            pl.BlockSpec((None, tm, o.shape[-1]), tok),
            pl.BlockSpec((None, tm, d), tok),
            pl.BlockSpec((None, 1, d), per_b),
            pl.BlockSpec((None, 1, d), per_b),
            pl.BlockSpec((None, 1, d), per_b),
            pl.BlockSpec(wglu.shape, const2),
            pl.BlockSpec(wout.shape, const2),
            pl.BlockSpec((d, LANE), const2),
            pl.BlockSpec((1, LANE), const2),
        ],
        out_specs=[
            pl.BlockSpec((None, tm, d), tok),
            pl.BlockSpec((tm * TOKEN_ROWS, LANE), lambda j, i: (i * (l // tm) + j, 0)),
            pl.BlockSpec((None, tm, LANE), tok),
        ],
        out_shape=[
            jax.ShapeDtypeStruct((b, l, d), F32),
            jax.ShapeDtypeStruct((b * l * TOKEN_ROWS, LANE), F32),
            jax.ShapeDtypeStruct((b, l, LANE), F32),
        ],
        compiler_params=_cparams(("arbitrary", "arbitrary")),
        name="mix_cd",
    )(yf, yb, o, x, g1, sh2, sc2, wglu, wout, wr, br)


def _router_weights(w_grp, b_grp, w_exp, b_exp):
    d = w_grp.shape[0]
    pad = LANE - N_GROUPS - N_EXPERTS
    wr = jnp.concatenate([w_grp, w_exp, jnp.zeros((d, pad), F32)], axis=1)
    br = jnp.concatenate([b_grp, b_exp, jnp.zeros((pad,), F32)]).reshape(1, LANE)
    return wr, br


def _pad_heads(w, width):
    k = w.shape[0]
    w = w.reshape(k, N_HEADS, width)
    return jnp.pad(w, ((0, 0), (0, 0), (0, HEAD_PAD - width))).reshape(k, N_HEADS * HEAD_PAD)


def _rope_tables(l):
    rows = l // GRID_W
    row = jnp.repeat(jnp.arange(rows, dtype=F32), GRID_W)
    col = jnp.tile(jnp.arange(GRID_W, dtype=F32), rows)
    inv_freq = ROPE_BASE ** (-jnp.arange(AXIS_PAIRS, dtype=F32) / AXIS_PAIRS)
    ar, ac = row[:, None] * inv_freq, col[:, None] * inv_freq
    one = jnp.ones((l, QK_NOPE), F32)
    zn = jnp.zeros((l, QK_NOPE), F32)
    zp = jnp.zeros((l, HEAD_PAD - QK_HEAD), F32)
    cos = jnp.concatenate([one, jnp.cos(ar), jnp.cos(ar), jnp.cos(ac), jnp.cos(ac), zp], axis=1)
    sin = jnp.concatenate([zn, jnp.sin(ar), jnp.sin(ar), jnp.sin(ac), jnp.sin(ac), zp], axis=1)
    return cos, sin


def _rot_swap(t, signed):
    a = AXIS_PAIRS
    s = -1.0 if signed else 1.0
    return jnp.concatenate([s * t[..., a:2 * a], t[..., 0:a], s * t[..., 3 * a:4 * a], t[..., 2 * a:3 * a]], axis=-1)


def _head_lanes(rot):
    pad = [(0, 0)] * (rot.ndim - 1) + [(QK_NOPE, HEAD_PAD - QK_HEAD)]
    return jnp.pad(rot, pad)


def _split_mod(mod_row, b, d):
    parts = jnp.split(mod_row, 6, axis=-1)
    return [jnp.broadcast_to(p.reshape(-1, 1, d), (b, 1, d)) for p in parts]


def kernel(x, c, ctx, c_ctx, ada_w, ada_b, ab_w_in, sgu_norm, sgu_w, sgu_b, conv_w, ab_w_out, cd_w_in, s5_a_re, s5_a_im, s5_log_dt, s5_b_re, s5_b_im, s5_c_re, s5_c_im, s5_d, s5_w_glu, mla_q_norm, mla_kv_norm, mla_w_uq, mla_w_uk, mla_w_uv, mla_qn_gain, mla_kn_gain, cd_w_out, moe_w_grp, moe_b_grp, moe_w_exp, moe_b_exp, moe_w_gate, moe_w_up, moe_w_down):
    b, l, d = x.shape
    lc = ctx.shape[1]
    depth = ada_w.shape[0]
    assert depth == 2, "layer 0 = gated-MLP/conv mixers, layer 1 = S5/attention mixers"
    assert d == TOKEN_ROWS * LANE, "token-tile layout: one vreg tile per token"
    tm = min(TOKEN_TILE, l)
    tmc = min(TOKEN_TILE, lc)

    mod_rows = 2 * SUBLANE
    cvec = jnp.zeros((mod_rows, d), F32).at[:b].set(c).at[b].set(c_ctx)
    mod = _ada(cvec, ada_w, ada_b)

    sh1, sc1, g1, sh2, sc2, g2 = _split_mod(mod[0, :b], b, d)
    csh1, csc1, cg1, csh2, csc2, cg2 = _split_mod(mod[0, b:b + 1], b, d)
    w_in = ab_w_in[0].astype(BF16)
    wr, br = _router_weights(moe_w_grp[0], moe_b_grp[0], moe_w_exp[0], moe_b_exp[0])
    sgub = jnp.repeat(sgu_b[0].T, d // G_A, axis=1)
    mix_args = (sgu_norm[0].reshape(1, d), sgu_w[0].astype(BF16), sgub, conv_w[0],
                ab_w_out[0].astype(BF16), wr, br)
    z, p = _inproj_ab(x, sh1, sc1, w_in, tm)
    xl, xm, rt = _mix_ab(z, p, x, g1, sh2, sc2, *mix_args, tm)
    zc, pc = _inproj_ab(ctx, csh1, csc1, w_in, tmc)
    xc, xmc, rtc = _mix_ab(zc, pc, ctx, cg1, csh2, csc2, *mix_args, tmc)
    xl, xc = _hier_moe([(xl, xm, rt, g2), (xc, xmc, rtc, cg2)], 0, moe_w_gate, moe_w_up, moe_w_down)

    sh1, sc1, g1, sh2, sc2, g2 = _split_mod(mod[1, :b], b, d)
    csh1, csc1, _, _, _, _ = _split_mod(mod[1, b:b + 1], b, d)
    d_c = s5_d.shape[-1]
    q_lora = mla_q_norm.shape[-1]
    kv_lora = mla_kv_norm.shape[-1]
    w_cd = cd_w_in[0]
    o_kr = d_c + q_lora + kv_lora
    w_kr = _head_lanes(w_cd[:, o_kr:])
    w_krs = _head_lanes(_rot_swap(w_cd[:, o_kr:], True))
    w_cd = jnp.concatenate([w_cd[:, :o_kr], w_kr, w_krs], axis=1).astype(BF16)
    widths = (d_c, q_lora, kv_lora, HEAD_PAD, HEAD_PAD)
    u_l, cq_l, ckv_l, krp_l, krs_l = _inproj_cd(xl, sh1, sc1, w_cd, widths, tm)
    u_c, _, ckv_c, krp_c, _ = _inproj_cd(xc, csh1, csc1, w_cd, widths, tmc)

    ab_re, ab_im, bb_re, bb_im = _s5_params(s5_a_re[0], s5_a_im[0], s5_log_dt[0], s5_b_re[0], s5_b_im[0])
    a_bc, bsb, csb = _s5_block_matrices(ab_re, ab_im, bb_re, bb_im, s5_c_re[0], s5_c_im[0], b)
    yf, yb = _s5_scan(u_c, u_l, a_bc, bsb, csb, s5_d[0].reshape(1, d_c), b)

    wq = _pad_heads(mla_w_uq[0], QK_HEAD).astype(BF16)
    wk = _pad_heads(mla_w_uk[0], QK_NOPE).astype(BF16)
    wv = mla_w_uv[0].astype(BF16)
    w_uq = mla_w_uq[0].reshape(q_lora, N_HEADS, QK_HEAD)
    wqs = _head_lanes(_rot_swap(w_uq[..., QK_NOPE:], True)).reshape(q_lora, N_HEADS * HEAD_PAD).astype(BF16)
    qg, kg = mla_qn_gain[0], mla_kn_gain[0]
    full = lambda g: jnp.pad(g, (0, HEAD_PAD - QK_HEAD)).reshape(1, HEAD_PAD)
    partner = lambda g: _head_lanes(_rot_swap(g[QK_NOPE:], False)).reshape(1, HEAD_PAD)
    cos, sin = _rope_tables(l)
    tables = (cos * full(qg), sin * partner(qg), cos * full(kg), sin * partner(kg))
    qn = mla_q_norm[0].reshape(1, q_lora)
    kvn = mla_kv_norm[0].reshape(1, kv_lora)
    q_l, k_l, v_l = _mla_prep(_mla_prep_lat_kernel, "mla_prep_lat", (cq_l, ckv_l, krp_l, krs_l),
                              (qn, kvn, wq, wqs, wk, wv), tables, 1, tm)
    k_c, v_c = _mla_prep(_mla_prep_ctx_kernel, "mla_prep_ctx", (ckv_c, krp_c), (kvn, wk, wv, full(kg)), (), 0, tmc)
    o_l = _attention(q_l, k_l, k_c, v_l, v_c, ATTN_Q_TILE, ATTN_HEADS)

    wr, br = _router_weights(moe_w_grp[1], moe_b_grp[1], moe_w_exp[1], moe_b_exp[1])
    xl, xm, rt = _mix_cd(yf, yb, o_l, xl, g1, sh2, sc2,
                         s5_w_glu[0].astype(BF16), cd_w_out[0].astype(BF16), wr, br, MIX_CD_TILE)
    (xl,) = _hier_moe([(xl, xm, rt, g2)], 1, moe_w_gate, moe_w_up, moe_w_down)
    return xl
```

```python
import functools
import math

import jax
import jax.numpy as jnp
from jax import lax
from jax.experimental import pallas as pl
from jax.experimental.pallas import tpu as pltpu

F32 = jnp.float32
BF16 = jnp.bfloat16
I32 = jnp.int32
HIGHEST = lax.Precision.HIGHEST

EPS = 1e-6
LOG2_E = math.log2(math.e)
GRID_W = 64
CHUNK = 128
G_A = 8
CONV_W = 3
S5_GROUP = 16
S5_STATE = 64
N_HEADS = 8
QK_NOPE = 64
QK_ROPE = 32
QK_HEAD = QK_NOPE + QK_ROPE
V_HEAD = 64
AXIS_PAIRS = QK_ROPE // 4
ROPE_BASE = 10000.0
N_GROUPS = 4
EXPERTS_PER_GROUP = 8
N_EXPERTS = N_GROUPS * EXPERTS_PER_GROUP
TOP_K = 2

LANE = 128
SUBLANE = 8
HEAD_PAD = LANE
TOKEN_TILE = 512
MIX_SUBTILES = 2
MIX_CD_TILE = 256
MOE_TILE = 512
COMBINE_TILE = 256
PLAN_BLOCK = 1024
MAX_TILES = 256
TOKEN_ROWS = SUBLANE
ATTN_Q_TILE = 256
ATTN_HEADS = 8
S5_STEPS = 64
S5_SUPER = LANE // S5_GROUP
VMEM_LIMIT = 56 * 1024 * 1024


def _cparams(sem):
    return pltpu.CompilerParams(dimension_semantics=sem, vmem_limit_bytes=VMEM_LIMIT)


def _rms(x):
    return x * lax.rsqrt(jnp.mean(x * x, axis=-1, keepdims=True) + EPS)


def _sigmoid(x):
    return 1.0 / (1.0 + jnp.exp(-x))


def _dot(a, b):
    return jnp.dot(a, b, preferred_element_type=F32)


def _ada_kernel(c_ref, w_ref, b_ref, o_ref):
    c = c_ref[...]
    s = c * _sigmoid(c)
    o_ref[...] = jnp.dot(s, w_ref[...], precision=HIGHEST, preferred_element_type=F32) + b_ref[...]


def _ada(cvec, ada_w, ada_b):
    depth, d, n = ada_w.shape
    rows = cvec.shape[0]
    tn = 1024
    return pl.pallas_call(
        _ada_kernel,
        grid=(depth, n // tn),
        in_specs=[
            pl.BlockSpec((rows, d), lambda i, j: (0, 0)),
            pl.BlockSpec((None, d, tn), lambda i, j: (i, 0, j)),
            pl.BlockSpec((None, 1, tn), lambda i, j: (i, 0, j)),
        ],
        out_specs=pl.BlockSpec((None, rows, tn), lambda i, j: (i, 0, j)),
        out_shape=jax.ShapeDtypeStruct((depth, rows, n), F32),
        compiler_params=_cparams(("arbitrary", "arbitrary")),
        name="ada_mod",
    )(cvec, ada_w, ada_b.reshape(depth, 1, n))


def _route(xm, wr_ref, br_ref):
    if wr_ref.dtype == BF16:
        hi = xm.astype(BF16)
        lo = (xm - hi.astype(F32)).astype(BF16)
        logits = _dot(jnp.concatenate([hi, lo, hi], axis=1), wr_ref[...]) + br_ref[...]
    else:
        logits = jnp.dot(xm, wr_ref[...], precision=HIGHEST, preferred_element_type=F32) + br_ref[...]
    lane = lax.broadcasted_iota(I32, logits.shape, 1).astype(F32)
    neg = jnp.float32(-1e30)
    big = jnp.float32(1e6)
    is_grp = lane < N_GROUPS
    gl = jnp.where(is_grp, logits, neg)
    gmax = jnp.max(gl, axis=-1, keepdims=True)
    gsum = jnp.sum(jnp.where(is_grp, jnp.exp(gl - gmax), 0.0), axis=-1, keepdims=True)
    p_top = 1.0 / gsum
    grp = jnp.min(jnp.where(gl == gmax, lane, big), axis=-1, keepdims=True)
    eidx = lane - N_GROUPS
    in_grp = (eidx >= grp * EXPERTS_PER_GROUP) & (eidx < (grp + 1.0) * EXPERTS_PER_GROUP)
    el = jnp.where(in_grp, logits, neg)
    m1 = jnp.max(el, axis=-1, keepdims=True)
    i1 = jnp.min(jnp.where(el == m1, lane, big), axis=-1, keepdims=True)
    el2 = jnp.where(lane == i1, neg, el)
    m2 = jnp.max(el2, axis=-1, keepdims=True)
    i2 = jnp.min(jnp.where(el2 == m2, lane, big), axis=-1, keepdims=True)
    t = jnp.exp(m2 - m1)
    w1 = p_top / (1.0 + t)
    w2 = p_top * t / (1.0 + t)
    e1 = i1 - N_GROUPS
    e2 = i2 - N_GROUPS
    return jnp.where(lane == 0, e1, jnp.where(lane == 1, e2, jnp.where(lane == 2, w1, jnp.where(lane == 3, w2, 0.0))))


def _residual_and_route(x, y, g1, sh2, sc2, wr_ref, br_ref, xl_ref, xm_ref, rt_ref):
    xl = x + g1 * y
    xl_ref[...] = xl
    xm = _rms(xl) * (1.0 + sc2) + sh2
    _store_token_tiles(xm_ref, xm)
    rt_ref[...] = _route(xm, wr_ref, br_ref)


def _inproj_ab_kernel(x_ref, sh_ref, sc_ref, w_ref, z_ref, p_ref):
    d = x_ref.shape[-1]
    h = (_rms(x_ref[...]) * (1.0 + sc_ref[...]) + sh_ref[...]).astype(BF16)
    for j in range(3):
        z_ref[:, j * d:(j + 1) * d] = _dot(h, w_ref[:, j * d:(j + 1) * d]).astype(BF16)
    gate_c = _dot(h, w_ref[:, 3 * d:4 * d])
    xb = _dot(h, w_ref[:, 4 * d:5 * d])
    p_ref[...] = gate_c * xb


def _inproj_ab(x, sh, sc, w_bf16, tm):
    b, l, d = x.shape
    n = w_bf16.shape[1]
    return pl.pallas_call(
        _inproj_ab_kernel,
        grid=(b, l // tm),
        in_specs=[
            pl.BlockSpec((None, tm, d), lambda i, j: (i, j, 0)),
            pl.BlockSpec((None, 1, d), lambda i, j: (i, 0, 0)),
            pl.BlockSpec((None, 1, d), lambda i, j: (i, 0, 0)),
            pl.BlockSpec((d, n), lambda i, j: (0, 0)),
        ],
        out_specs=[
            pl.BlockSpec((None, tm, 3 * d), lambda i, j: (i, j, 0)),
            pl.BlockSpec((None, tm, d), lambda i, j: (i, j, 0)),
        ],
        out_shape=[
            jax.ShapeDtypeStruct((b, l, 3 * d), BF16),
            jax.ShapeDtypeStruct((b, l, d), F32),
        ],
        compiler_params=_cparams(("arbitrary", "arbitrary")),
        name="inproj_ab",
    )(x, sh, sc, w_bf16)


def _mix_ab_kernel(z_ref, p_ref, pprev_ref, pnext_ref, x_ref, g1_ref, sh2_ref, sc2_ref,
                   sgun_ref, sguw_ref, sgub_ref, convw_ref, wout_ref, wr_ref, br_ref,
                   xl_ref, xm_ref, rt_ref, pbuf, ycat):
    i = pl.program_id(1)
    nt = pl.num_programs(1)
    tm, d = x_ref.shape
    cg = d // G_A
    pbuf[SUBLANE:SUBLANE + tm, :] = p_ref[...]
    pbuf[0:SUBLANE, :] = jnp.where(i > 0, pprev_ref[...], 0.0)
    pbuf[SUBLANE + tm:2 * SUBLANE + tm, :] = jnp.where(i < nt - 1, pnext_ref[...], 0.0)
    ts = tm // (MIX_SUBTILES if tm % (MIX_SUBTILES * CHUNK) == 0 else 1)
    for t0 in range(0, tm, ts):
        rows = slice(t0, t0 + ts)
        v = jax.nn.gelu(z_ref[rows, d:2 * d].astype(F32))
        vc = (_rms(v) * sgun_ref[...]).astype(BF16)
        for c in range(ts // CHUNK):
            lr = c * CHUNK
            r0 = t0 + lr
            cols = [_dot(sguw_ref[g], vc[lr:lr + CHUNK, g * cg:(g + 1) * cg]) for g in range(G_A)]
            s = jnp.concatenate(cols, axis=1) + sgub_ref[...]
            u = jax.nn.gelu(z_ref[r0:r0 + CHUNK, 0:d].astype(F32))
            ycat[r0:r0 + CHUNK, 0:d] = (u * s).astype(BF16)
        conv = (convw_ref[0:1, :] * pbuf[SUBLANE - 1 + t0:SUBLANE - 1 + t0 + ts, :]
                + convw_ref[1:2, :] * pbuf[SUBLANE + t0:SUBLANE + t0 + ts, :]
                + convw_ref[2:3, :] * pbuf[SUBLANE + 1 + t0:SUBLANE + 1 + t0 + ts, :])
        ycat[rows, d:2 * d] = (z_ref[rows, 2 * d:3 * d].astype(F32) * conv).astype(BF16)
        y = _dot(ycat[rows, :], wout_ref[...])
        _residual_and_route(x_ref[rows, :], y, g1_ref[...], sh2_ref[...], sc2_ref[...], wr_ref, br_ref,
                            xl_ref.at[rows, :], xm_ref.at[pl.ds(t0 * TOKEN_ROWS, ts * TOKEN_ROWS), :],
                            rt_ref.at[rows, :])


def _mix_ab(z, p, x, g1, sh2, sc2, sgun, sguw, sgub, convw, wout, wr, br, tm):
    b, l, d = x.shape
    hb = tm // SUBLANE
    nhb = l // SUBLANE
    tok = lambda i, j: (i, j, 0)
    per_b = lambda i, j: (i, 0, 0)
    const2 = lambda i, j: (0, 0)
    return pl.pallas_call(
        _mix_ab_kernel,
        grid=(b, l // tm),
        in_specs=[
            pl.BlockSpec((None, tm, 3 * d), tok),
            pl.BlockSpec((None, tm, d), tok),
            pl.BlockSpec((None, SUBLANE, d), lambda i, j: (i, jnp.maximum(j * hb - 1, 0), 0)),
            pl.BlockSpec((None, SUBLANE, d), lambda i, j: (i, jnp.minimum((j + 1) * hb, nhb - 1), 0)),
            pl.BlockSpec((None, tm, d), tok),
            pl.BlockSpec((None, 1, d), per_b),
            pl.BlockSpec((None, 1, d), per_b),
            pl.BlockSpec((None, 1, d), per_b),
            pl.BlockSpec((1, d), const2),
            pl.BlockSpec((G_A, CHUNK, CHUNK), lambda i, j: (0, 0, 0)),
            pl.BlockSpec((CHUNK, d), const2),
            pl.BlockSpec((CONV_W, d), const2),
            pl.BlockSpec((2 * d, d), const2),
            pl.BlockSpec(wr.shape, const2),
            pl.BlockSpec((1, LANE), const2),
        ],
        out_specs=[
            pl.BlockSpec((None, tm, d), tok),
            pl.BlockSpec((tm * TOKEN_ROWS, LANE), lambda i, j: (i * (l // tm) + j, 0)),
            pl.BlockSpec((None, tm, LANE), tok),
        ],
        out_shape=[
            jax.ShapeDtypeStruct((b, l, d), F32),
            jax.ShapeDtypeStruct((b * l * TOKEN_ROWS, LANE), F32),
            jax.ShapeDtypeStruct((b, l, LANE), F32),
        ],
        scratch_shapes=[
            pltpu.VMEM((tm + 2 * SUBLANE, d), F32),
            pltpu.VMEM((tm, 2 * d), BF16),
        ],
        compiler_params=_cparams(("arbitrary", "arbitrary")),
        name="mix_ab",
    )(z, p, p, p, x, g1, sh2, sc2, sgun, sguw, sgub, convw, wout, wr, br)


def _store_token_tiles(ref, val):
    tm, d = val.shape
    rpt = d // LANE
    for j in range(rpt):
        ref[pl.ds(j, tm, stride=rpt), :] = val[:, j * LANE:(j + 1) * LANE]


def _load_token_tile_col(ref, tm, rpt, j):
    return ref[pl.ds(j, tm, stride=rpt), :]


def _plan_kernel(rt_ref, tri_ref, pos_ref, te_ref, tv_ref, pad_ref, run, *, tm):
    ps = pl.program_id(0)
    i = pl.program_id(1)
    blk = rt_ref.shape[0]
    ne = N_EXPERTS

    @pl.when(jnp.logical_and(ps == 0, i == 0))
    def _():
        run[...] = jnp.zeros_like(run)

    @pl.when(jnp.logical_and(ps == 1, i == 0))
    def _():
        counts = run[...]
        padded = jnp.floor((counts + (tm - 1)) * (1.0 / tm)) * tm
        r = lax.broadcasted_iota(I32, (ne, ne), 0)
        c = lax.broadcasted_iota(I32, (ne, ne), 1)
        starts = jnp.dot((c < r).astype(F32), padded, precision=HIGHEST, preferred_element_type=F32)
        nt = te_ref.shape[1]
        ends = jnp.concatenate([(starts + padded) * (1.0 / tm)] * (nt // LANE), axis=1)
        t = lax.broadcasted_iota(I32, (ne, nt), 1).astype(F32)
        te = jnp.sum((t >= ends).astype(F32), axis=0, keepdims=True)
        total = jnp.max(ends, axis=0, keepdims=True)
        te_ref[...] = jnp.minimum(te, ne - 1.0).astype(I32)
        tv_ref[...] = (t[0:1, :] < total).astype(I32)
        lane = lax.broadcasted_iota(I32, (ne, LANE), 1)
        pad_ref[...] = jnp.where(lane == 0, starts + counts,
                                 jnp.where(lane == 1, padded - counts, 0.0)).astype(I32)
        run[...] = starts

    slab_t = rt_ref[...].T
    ex = lax.broadcasted_iota(I32, (ne, blk), 0).astype(F32)
    oh1 = ex == slab_t[0:1, :]
    oh2 = ex == slab_t[1:2, :]
    oh = jnp.where(jnp.logical_or(oh1, oh2), 1.0, 0.0)
    prefix = _dot(oh.astype(BF16), tri_ref[...]) + run[:, 0:1]
    pos1 = jnp.sum(jnp.where(oh1, prefix, 0.0), axis=0, keepdims=True)
    pos2 = jnp.sum(jnp.where(oh2, prefix, 0.0), axis=0, keepdims=True)
    pos_ref[...] = jnp.concatenate([pos1, pos2], axis=0).astype(I32)
    run[...] = run[...] + jnp.sum(oh, axis=1, keepdims=True)


def _moe_plan(rt_all, tm):
    n = rt_all.shape[0]
    blk = PLAN_BLOCK
    nb = n // blk
    tri = jnp.triu(jnp.ones((blk, blk), BF16), k=1)
    return pl.pallas_call(
        functools.partial(_plan_kernel, tm=tm),
        grid=(2, nb),
        in_specs=[
            pl.BlockSpec((blk, LANE), lambda p, i: (i, 0)),
            pl.BlockSpec((blk, blk), lambda p, i: (0, 0)),
        ],
        out_specs=[
            pl.BlockSpec((None, TOP_K, blk), lambda p, i: (jnp.where(p == 0, nb, i), 0, 0)),
            pl.BlockSpec((1, MAX_TILES), lambda p, i: (0, 0)),
            pl.BlockSpec((1, MAX_TILES), lambda p, i: (0, 0)),
            pl.BlockSpec((N_EXPERTS, LANE), lambda p, i: (0, 0)),
        ],
        out_shape=[
            jax.ShapeDtypeStruct((nb + 1, TOP_K, blk), I32),
            jax.ShapeDtypeStruct((1, MAX_TILES), I32),
            jax.ShapeDtypeStruct((1, MAX_TILES), I32),
            jax.ShapeDtypeStruct((N_EXPERTS, LANE), I32),
        ],
        scratch_shapes=[pltpu.VMEM((N_EXPERTS, LANE), F32)],
        compiler_params=_cparams(("arbitrary", "arbitrary")),
        name="moe_plan",
    )(rt_all, tri)


def _dispatch_kernel(ps_ref, pl_ref, tv_ref, pos_ref, *rest, blocks, n_tiles):
    xm_refs = rest[:len(blocks)]
    xs_hbm, zbuf, sem = rest[len(blocks):]
    i = pl.program_id(0)
    tt = pos_ref.shape[1]
    rpt = xm_refs[0].shape[0] // tt
    tile_rows = zbuf.shape[0]

    @pl.when(i == 0)
    def _():
        zbuf[...] = jnp.zeros_like(zbuf)

        def gap_copies(e, wait):
            base = ps_ref[e]
            n = pl_ref[e]
            bit = tile_rows // (2 * rpt)
            while bit >= 1:
                lower = n & (bit - 1)

                @pl.when((n & bit) != 0)
                def _(lower=lower, bit=bit):
                    dst = xs_hbm.at[pl.ds((base + lower) * rpt, bit * rpt)]
                    cp = pltpu.make_async_copy(zbuf.at[pl.ds(0, bit * rpt)], dst, sem.at[1])
                    cp.wait() if wait else cp.start()
                bit //= 2

        def tile_copy(t, wait):
            @pl.when(tv_ref[t] != 1)
            def _():
                dst = xs_hbm.at[pl.ds(pl.multiple_of(t * tile_rows, tile_rows), tile_rows)]
                cp = pltpu.make_async_copy(zbuf, dst, sem.at[1])
                cp.wait() if wait else cp.start()

        for wait in (False, True):
            lax.fori_loop(0, N_EXPERTS, lambda e, c, w=wait: (gap_copies(e, w), c)[1], 0)
            lax.fori_loop(0, n_tiles, lambda t, c, w=wait: (tile_copy(t, w), c)[1], 0)

    def scatter(xm_ref):
        def issue(r, carry):
            src = xm_ref.at[pl.ds(pl.multiple_of(r * rpt, rpt), rpt)]
            for k in range(TOP_K):
                dst = xs_hbm.at[pl.ds(pl.multiple_of(pos_ref[k, r] * rpt, rpt), rpt)]
                pltpu.make_async_copy(src, dst, sem.at[0]).start(priority=k)
            return carry
        lax.fori_loop(0, tt, issue, 0, unroll=8)
        for k in range(TOP_K):
            pltpu.make_async_copy(xm_ref, xs_hbm.at[pl.ds(0, tt * rpt)], sem.at[0]).wait()

    lo = 0
    for xm_ref, nblk in zip(xm_refs, blocks):
        pl.when(jnp.logical_and(i >= lo, i < lo + nblk))(functools.partial(scatter, xm_ref))
        lo += nblk


def _dispatch(pad_start, pad_len, valid, pos, xm_streams, n_tiles, tm):
    tt = pos.shape[-1]
    rpt = TOKEN_ROWS
    blocks = tuple(xm.shape[0] // (rpt * tt) for xm in xm_streams)
    in_specs = [pl.BlockSpec((None, TOP_K, tt), lambda i, a, b, c: (i, 0, 0), memory_space=pltpu.SMEM)]
    lo = 0
    for nblk in blocks:
        in_specs.append(pl.BlockSpec(
            (tt * rpt, LANE), lambda i, a, b, c, lo=lo, nblk=nblk: (jnp.clip(i - lo, 0, nblk - 1), 0)))
        lo += nblk
    return pl.pallas_call(
        functools.partial(_dispatch_kernel, blocks=blocks, n_tiles=n_tiles),
        grid_spec=pltpu.PrefetchScalarGridSpec(
            num_scalar_prefetch=3,
            grid=(sum(blocks),),
            in_specs=in_specs,
            out_specs=pl.BlockSpec(memory_space=pl.ANY),
            scratch_shapes=[pltpu.VMEM((tm * rpt, LANE), F32), pltpu.SemaphoreType.DMA((2,))],
        ),
        out_shape=jax.ShapeDtypeStruct((n_tiles * tm * rpt, LANE), F32),
        compiler_params=_cparams(("arbitrary",)),
        name="moe_dispatch",
    )(pad_start, pad_len, valid, pos, *xm_streams)


def _moe_kernel(te_ref, tv_ref, x_ref, wg_ref, wu_ref, wd_ref, y_ref, wg_s, wu_s, wd_s):
    t = pl.program_id(0)
    rpt = TOKEN_ROWS
    tm = x_ref.shape[0] // rpt

    @pl.when(tv_ref[t] == 1)
    def _():
        changed = jnp.logical_or(t == 0, te_ref[t] != te_ref[jnp.maximum(t - 1, 0)])

        @pl.when(changed)
        def _():
            wg_s[...] = wg_ref[...].astype(BF16)
            wu_s[...] = wu_ref[...].astype(BF16)
            wd_s[...] = wd_ref[...].astype(BF16)

        x = jnp.concatenate([_load_token_tile_col(x_ref, tm, rpt, j) for j in range(rpt)], axis=1).astype(BF16)
        g = _dot(x, wg_s[...])
        u = _dot(x, wu_s[...])
        h = (g * _sigmoid(g) * u).astype(BF16)
        _store_token_tiles(y_ref, _dot(h, wd_s[...]))

    @pl.when(tv_ref[t] != 1)
    def _():
        y_ref[...] = jnp.zeros_like(y_ref)


def _moe_experts(xs, tile_e, valid, n_tiles, layer, w_gate, w_up, w_down, tm):
    _, _, d, de = w_gate.shape
    rpt = TOKEN_ROWS
    grid_spec = pltpu.PrefetchScalarGridSpec(
        num_scalar_prefetch=2,
        grid=(n_tiles,),
        in_specs=[
            pl.BlockSpec((tm * rpt, LANE), lambda t, te, tv: (t * tv[t], 0)),
            pl.BlockSpec((None, None, d, de), lambda t, te, tv: (layer, te[t], 0, 0)),
            pl.BlockSpec((None, None, d, de), lambda t, te, tv: (layer, te[t], 0, 0)),
            pl.BlockSpec((None, None, de, d), lambda t, te, tv: (layer, te[t], 0, 0)),
        ],
        out_specs=pl.BlockSpec((tm * rpt, LANE), lambda t, te, tv: (t, 0)),
        scratch_shapes=[
            pltpu.VMEM((d, de), BF16),
            pltpu.VMEM((d, de), BF16),
            pltpu.VMEM((de, d), BF16),
        ],
    )
    return pl.pallas_call(
        _moe_kernel,
        grid_spec=grid_spec,
        out_shape=jax.ShapeDtypeStruct((n_tiles * tm * rpt, LANE), F32),
        compiler_params=_cparams(("arbitrary",)),
        name="moe_experts",
    )(tile_e, valid, xs, w_gate, w_up, w_down)


def _combine_kernel(pos0_ref, posn_ref, x_ref, g_ref, rt_ref, ys_hbm, o_ref, buf, sem, *, n):
    i = pl.program_id(0)
    tt, d = x_ref.shape
    rpt = TOKEN_ROWS

    def gather(pos_ref, slot):
        def issue(r, carry):
            for k in range(TOP_K):
                src = ys_hbm.at[pl.ds(pl.multiple_of(pos_ref[k, r] * rpt, rpt), rpt)]
                dst = buf.at[slot, k, pl.ds(pl.multiple_of(r * rpt, rpt), rpt)]
                pltpu.make_async_copy(src, dst, sem.at[slot, k]).start(priority=k)
            return carry
        lax.fori_loop(0, tt, issue, 0, unroll=8)

    slot = lax.rem(i, 2)

    @pl.when(i == 0)
    def _():
        gather(pos0_ref, 0)

    @pl.when(i + 1 < n)
    def _():
        gather(posn_ref, 1 - slot)

    for k in range(TOP_K):
        pltpu.make_async_copy(ys_hbm.at[pl.ds(0, tt * rpt)], buf.at[slot, k], sem.at[slot, k]).wait()
    w1 = rt_ref[:, 2:3]
    w2 = rt_ref[:, 3:4]
    for j in range(rpt):
        r1 = buf[slot, 0, pl.ds(j, tt, stride=rpt), :]
        r2 = buf[slot, 1, pl.ds(j, tt, stride=rpt), :]
        sl = slice(j * LANE, (j + 1) * LANE)
        o_ref[:, sl] = x_ref[:, sl] + g_ref[:, sl] * (w1 * r1 + w2 * r2)


def _moe_combine(x, g2, rt, pos, tok_off, ys, tt):
    b, l, d = x.shape
    rpt = TOKEN_ROWS
    n = b * l
    per_b = l // tt
    per_blk = PLAN_BLOCK // tt
    blk0 = tok_off // PLAN_BLOCK

    def pos_map(step):
        return lambda i: (blk0 + step(i) // per_blk, 0, step(i) % per_blk)

    return pl.pallas_call(
        functools.partial(_combine_kernel, n=n // tt),
        grid=(n // tt,),
        in_specs=[
            pl.BlockSpec((None, TOP_K, tt), pos_map(lambda i: i), memory_space=pltpu.SMEM),
            pl.BlockSpec((None, TOP_K, tt), pos_map(lambda i: jnp.minimum(i + 1, n // tt - 1)),
                         memory_space=pltpu.SMEM),
            pl.BlockSpec((tt, d), lambda i: (i, 0)),
            pl.BlockSpec((None, 1, d), lambda i: (i // per_b, 0, 0)),
            pl.BlockSpec((tt, LANE), lambda i: (i, 0)),
            pl.BlockSpec(memory_space=pl.ANY),
        ],
        out_specs=pl.BlockSpec((tt, d), lambda i: (i, 0)),
        out_shape=jax.ShapeDtypeStruct((n, d), F32),
        scratch_shapes=[
            pltpu.VMEM((2, TOP_K, tt * rpt, LANE), F32),
            pltpu.SemaphoreType.DMA((2, TOP_K)),
        ],
        compiler_params=_cparams(("arbitrary",)),
        name="moe_combine",
    )(pos, pos, x.reshape(n, d), g2, rt.reshape(n, LANE), ys).reshape(b, l, d)


def _hier_moe(streams, layer, w_gate, w_up, w_down):
    sizes = [s[0].shape[0] * s[0].shape[1] for s in streams]
    n = sum(sizes)
    tm = MOE_TILE
    n_tiles = (TOP_K * n) // tm + N_EXPERTS
    assert n_tiles <= MAX_TILES and all(sz % PLAN_BLOCK == 0 for sz in sizes)
    rt_all = jnp.concatenate([s[2].reshape(-1, LANE) for s in streams], axis=0)
    pos, tile_e, valid, pad = _moe_plan(rt_all, tm)
    tile_e, valid = tile_e.reshape(-1), valid.reshape(-1)
    xs = _dispatch(pad[:, 0], pad[:, 1], valid, pos, [s[1] for s in streams], n_tiles, tm)
    ys = _moe_experts(xs, tile_e, valid, n_tiles, layer, w_gate, w_up, w_down, tm)
    outs = []
    off = 0
    for (xl, _, rt, g2), sz in zip(streams, sizes):
        outs.append(_moe_combine(xl, g2, rt, pos, off, ys, COMBINE_TILE))
        off += sz
    return outs


def _inproj_cd_kernel(x_ref, sh_ref, sc_ref, w_ref, u_ref, *out_refs):
    h = (_rms(x_ref[...]) * (1.0 + sc_ref[...]) + sh_ref[...]).astype(BF16)
    tm = x_ref.shape[0]
    n_slab, rows, cw = u_ref.shape
    nb = rows // tm
    bi = pl.program_id(1)
    u = _dot(h, w_ref[:, 0:n_slab * cw])
    for s in range(n_slab):
        u_ref[s, pl.ds(bi, tm, stride=nb), :] = u[:, s * cw:(s + 1) * cw]
    off = n_slab * cw
    for ref in out_refs:
        n = ref.shape[-1]
        ref[...] = _dot(h, w_ref[:, off:off + n]).astype(ref.dtype)
        off += n


def _inproj_cd(x, sh, sc, w_bf16, widths, tm):
    b, l, d = x.shape
    n = w_bf16.shape[1]
    n_slab = widths[0] // LANE
    tok = lambda j, i: (i, j, 0)
    return pl.pallas_call(
        _inproj_cd_kernel,
        grid=(l // tm, b),
        in_specs=[
            pl.BlockSpec((None, tm, d), tok),
            pl.BlockSpec((None, 1, d), lambda j, i: (i, 0, 0)),
            pl.BlockSpec((None, 1, d), lambda j, i: (i, 0, 0)),
            pl.BlockSpec((d, n), lambda j, i: (0, 0)),
        ],
        out_specs=([pl.BlockSpec((n_slab, tm * b, LANE), lambda j, i: (0, j, 0))]
                   + [pl.BlockSpec((None, tm, w), tok) for w in widths[1:]]),
        out_shape=([jax.ShapeDtypeStruct((n_slab, l * b, LANE), F32)]
                   + [jax.ShapeDtypeStruct((b, l, w), F32) for w in widths[1:]]),
        compiler_params=_cparams(("arbitrary", "arbitrary")),
        name="inproj_cd",
    )(x, sh, sc, w_bf16)


def _s5_param_kernel(are_ref, aim_ref, ldt_ref, bre_ref, bim_ref, abre_ref, abim_ref, bbre_ref, bbim_ref):
    a_re = are_ref[...]
    a_im = aim_ref[...]
    dt = jnp.exp(ldt_ref[...])
    mag = jnp.exp(dt * a_re)
    ab_re = mag * jnp.cos(dt * a_im)
    ab_im = mag * jnp.sin(dt * a_im)
    den = a_re * a_re + a_im * a_im
    nr = ab_re - 1.0
    f_re = (nr * a_re + ab_im * a_im) / den
    f_im = (ab_im * a_re - nr * a_im) / den
    abre_ref[...] = ab_re
    abim_ref[...] = ab_im
    b_re = bre_ref[...]
    b_im = bim_ref[...]
    bbre_ref[...] = f_re[None] * b_re - f_im[None] * b_im
    bbim_ref[...] = f_re[None] * b_im + f_im[None] * b_re


def _s5_params(a_re, a_im, log_dt, b_re, b_im):
    nd, g, p = a_re.shape
    c = b_re.shape[-1]
    rows = nd * g * p // LANE
    flat = lambda t: t.reshape(rows, LANE)
    chan_major = lambda t: jnp.moveaxis(t, -1, 0).reshape(c, rows, LANE)
    ldt = jnp.broadcast_to(log_dt[:, :, None], (nd, g, p))
    shapes = [jax.ShapeDtypeStruct((rows, LANE), F32)] * 2 + [jax.ShapeDtypeStruct((c, rows, LANE), F32)] * 2
    ab_re, ab_im, bb_re, bb_im = pl.pallas_call(
        _s5_param_kernel, out_shape=shapes, name="s5_params",
    )(flat(a_re), flat(a_im), flat(ldt), chan_major(b_re), chan_major(b_im))
    unflat = lambda t: jnp.moveaxis(t.reshape(c, nd, g, p), 0, -1)
    return ab_re.reshape(nd, g, p), ab_im.reshape(nd, g, p), unflat(bb_re), unflat(bb_im)


def _s5_kernel(ufc_ref, ubc_ref, ufl_ref, ubl_ref, a_ref, bsb_ref, csb_ref, d_ref, yf_ref, yb_ref,
               hf, hb, cf, cb, *, n_ctx):
    i = pl.program_id(0)
    nb = cf.shape[0]
    steps = hf.shape[0] // nb
    n_super = bsb_ref.shape[1]
    cw = bsb_ref.shape[2]
    sw = bsb_ref.shape[3]
    half = sw // 2

    @pl.when(i == 0)
    def _():
        cf[...] = jnp.zeros_like(cf)
        cb[...] = jnp.zeros_like(cb)

    def drive(uf_ref, ub_ref):
        for s in range(n_super):
            hf[:, s * sw:(s + 1) * sw] = _dot(uf_ref[s].astype(BF16), bsb_ref[0, s])
            hb[:, s * sw:(s + 1) * sw] = _dot(ub_ref[s].astype(BF16), bsb_ref[1, s])

    pl.when(i < n_ctx)(functools.partial(drive, ufc_ref, ubc_ref))
    pl.when(i >= n_ctx)(functools.partial(drive, ufl_ref, ubl_ref))

    for s in range(n_super):
        re = slice(s * sw, s * sw + half)
        im = slice(s * sw + half, (s + 1) * sw)
        st = slice(s * half, (s + 1) * half)
        arf, aif = a_ref[0, 0, :, st], a_ref[0, 1, :, st]
        arb, aib = a_ref[1, 0, :, st], a_ref[1, 1, :, st]

        def step(t, carry):
            hfr, hfi, hbr, hbi = carry
            rf = pl.multiple_of(t * nb, nb)
            nfr = arf * hfr - aif * hfi + hf[pl.ds(rf, nb), re]
            nfi = arf * hfi + aif * hfr + hf[pl.ds(rf, nb), im]
            hf[pl.ds(rf, nb), re] = nfr
            hf[pl.ds(rf, nb), im] = nfi
            rb = pl.multiple_of((steps - 1 - t) * nb, nb)
            nbr = arb * hbr - aib * hbi + hb[pl.ds(rb, nb), re]
            nbi = arb * hbi + aib * hbr + hb[pl.ds(rb, nb), im]
            hb[pl.ds(rb, nb), re] = nbr
            hb[pl.ds(rb, nb), im] = nbi
            return nfr, nfi, nbr, nbi

        out = lax.fori_loop(0, steps, step, (cf[:, re], cf[:, im], cb[:, re], cb[:, im]), unroll=4)
        cf[:, re], cf[:, im], cb[:, re], cb[:, im] = out

    @pl.when(i >= n_ctx)
    def _():
        for s in range(n_super):
            skip = ufl_ref[s] * d_ref[:, s * cw:(s + 1) * cw]
            yf_ref[s] = _dot(hf[:, s * sw:(s + 1) * sw].astype(BF16), csb_ref[0, s]) + skip
            yb_ref[s] = _dot(hb[:, s * sw:(s + 1) * sw].astype(BF16), csb_ref[1, s])


def _s5_scan(u_c, u_l, a_bc, bsb, csb, d_skip, nb):
    n_slab, rows_l, _ = u_l.shape
    blk = S5_STEPS * nb
    nc = u_c.shape[1] // blk
    nl = rows_l // blk
    n_state = bsb.shape[1] * bsb.shape[3]
    clip = jnp.clip
    ctx_f = lambda i: (0, clip(i, 0, nc - 1), 0)
    ctx_b = lambda i: (0, clip(nc - 1 - i, 0, nc - 1), 0)
    lat_f = lambda i: (0, clip(i - nc, 0, nl - 1), 0)
    lat_b = lambda i: (0, clip(nl - 1 - (i - nc), 0, nl - 1), 0)
    whole = lambda a: pl.BlockSpec(a.shape, lambda i: (0,) * a.ndim)
    slab = lambda index_map: pl.BlockSpec((n_slab, blk, LANE), index_map)
    return pl.pallas_call(
        functools.partial(_s5_kernel, n_ctx=nc),
        grid=(nc + nl,),
        in_specs=[slab(ctx_f), slab(ctx_b), slab(lat_f), slab(lat_b), whole(a_bc), whole(bsb), whole(csb),
                  whole(d_skip)],
        out_specs=[slab(lat_f), slab(lat_b)],
        out_shape=[jax.ShapeDtypeStruct(u_l.shape, F32)] * 2,
        scratch_shapes=[
            pltpu.VMEM((blk, n_state), F32),
            pltpu.VMEM((blk, n_state), F32),
            pltpu.VMEM((nb, n_state), F32),
            pltpu.VMEM((nb, n_state), F32),
        ],
        compiler_params=_cparams(("arbitrary",)),
        name="s5_scan",
    )(u_c, u_c, u_l, u_l, a_bc, bsb, csb, d_skip)


def _s5_block_matrices(ab_re, ab_im, bb_re, bb_im, c_re, c_im, nb):
    nd, g, p = ab_re.shape
    c = bb_re.shape[-1]
    ns = g // S5_SUPER
    eye = jnp.eye(S5_SUPER, dtype=F32)

    def in_mat(bb):
        t = bb.reshape(nd, ns, S5_SUPER, p, c)
        return jnp.einsum('dsgpc,gh->dsgchp', t, eye).reshape(nd, ns, S5_SUPER * c, S5_SUPER * p)

    def out_mat(cc):
        t = cc.reshape(nd, ns, S5_SUPER, c, p)
        return jnp.einsum('dsgcp,gh->dsgphc', t, eye).reshape(nd, ns, S5_SUPER * p, S5_SUPER * c)

    bsb = jnp.concatenate([in_mat(bb_re), in_mat(bb_im)], axis=-1).astype(BF16)
    csb = jnp.concatenate([out_mat(c_re), out_mat(-c_im)], axis=-2).astype(BF16)
    a_bc = jnp.stack([ab_re.reshape(nd, g * p), ab_im.reshape(nd, g * p)], axis=1)
    a_bc = jnp.broadcast_to(a_bc[:, :, None, :], (nd, 2, nb, g * p))
    return a_bc, bsb, csb


def _head_scale(x):
    return lax.rsqrt(jnp.sum(x * x, axis=-1, keepdims=True) * (1.0 / QK_HEAD) + EPS)


def _mla_prep_lat_kernel(cq_ref, ckv_ref, krp_ref, krs_ref, qn_ref, kvn_ref, wq_ref, wqs_ref, wk_ref, wv_ref,
                         qc_ref, qs_ref, kc_ref, ks_ref, q_ref, k_ref, v_ref):
    ckvn = (_rms(ckv_ref[...]) * kvn_ref[...]).astype(BF16)
    v_ref[...] = _dot(ckvn, wv_ref[...]).astype(v_ref.dtype)
    kn = _dot(ckvn, wk_ref[...])
    krp = krp_ref[...]
    k_swap = krs_ref[...] * ks_ref[...]
    kc = kc_ref[...]
    for h in range(N_HEADS):
        sl = slice(h * HEAD_PAD, (h + 1) * HEAD_PAD)
        kh = kn[:, sl] + krp
        k_ref[:, sl] = (_head_scale(kh) * (kh * kc + k_swap)).astype(k_ref.dtype)
    cqn = (_rms(cq_ref[...]) * qn_ref[...]).astype(BF16)
    qn = _dot(cqn, wq_ref[...])
    q_swap = _dot(cqn, wqs_ref[...])
    qc = qc_ref[...]
    qs = qs_ref[...]
    for h in range(N_HEADS):
        sl = slice(h * HEAD_PAD, (h + 1) * HEAD_PAD)
        qh = qn[:, sl]
        scale = _head_scale(qh) * (QK_HEAD ** -0.5 * LOG2_E)
        q_ref[:, sl] = (scale * (qh * qc + q_swap[:, sl] * qs)).astype(q_ref.dtype)


def _mla_prep_ctx_kernel(ckv_ref, krp_ref, kvn_ref, wk_ref, wv_ref, kg_ref, k_ref, v_ref):
    ckvn = (_rms(ckv_ref[...]) * kvn_ref[...]).astype(BF16)
    v_ref[...] = _dot(ckvn, wv_ref[...]).astype(v_ref.dtype)
    kn = _dot(ckvn, wk_ref[...])
    krp = krp_ref[...]
    kg = kg_ref[...]
    for h in range(N_HEADS):
        sl = slice(h * HEAD_PAD, (h + 1) * HEAD_PAD)
        kh = kn[:, sl] + krp
        k_ref[:, sl] = (_head_scale(kh) * (kh * kg)).astype(k_ref.dtype)


def _mla_prep(kernel_fn, name, tok_args, const_args, table_args, n_q_out, tm):
    b, l, _ = tok_args[0].shape
    tok = lambda i, j: (i, j, 0)
    specs = [pl.BlockSpec((None, tm, a.shape[-1]), tok) for a in tok_args]
    specs += [pl.BlockSpec(a.shape, lambda i, j: (0, 0)) for a in const_args]
    specs += [pl.BlockSpec((tm, HEAD_PAD), lambda i, j: (j, 0)) for _ in table_args]
    hk = N_HEADS * HEAD_PAD
    hv = N_HEADS * V_HEAD
    widths = [hk] * n_q_out + [hk, hv]
    return pl.pallas_call(
        kernel_fn,
        grid=(b, l // tm),
        in_specs=specs,
        out_specs=[pl.BlockSpec((None, tm, w), tok) for w in widths],
        out_shape=[jax.ShapeDtypeStruct((b, l, w), BF16) for w in widths],
        compiler_params=_cparams(("arbitrary", "arbitrary")),
        name=name,
    )(*tok_args, *const_args, *table_args)


def _attn_kernel(q_ref, kl_ref, kc_ref, vl_ref, vc_ref, o_ref, *, heads):
    nt = (((1,), (1,)), ((), ()))
    for pr in range(heads // 2):
        vsl = slice(pr * 2 * V_HEAD, (pr + 1) * 2 * V_HEAD)
        outs = []
        for hh in range(2):
            h = 2 * pr + hh
            sl = slice(h * HEAD_PAD, (h + 1) * HEAD_PAD)
            q = q_ref[:, sl]
            s_l = lax.dot_general(q, kl_ref[:, sl], nt, preferred_element_type=F32)
            s_c = lax.dot_general(q, kc_ref[:, sl], nt, preferred_element_type=F32)
            m = jnp.maximum(jnp.max(s_l, axis=-1, keepdims=True), jnp.max(s_c, axis=-1, keepdims=True))
            p_l = jnp.exp2(s_l - m)
            p_c = jnp.exp2(s_c - m)
            den = jnp.sum(p_l, axis=-1, keepdims=True) + jnp.sum(p_c, axis=-1, keepdims=True)
            o = _dot(p_l.astype(BF16), vl_ref[:, vsl]) + _dot(p_c.astype(BF16), vc_ref[:, vsl])
            outs.append(o / den)
        lane = lax.broadcasted_iota(I32, outs[0].shape, 1)
        o_ref[:, vsl] = jnp.where(lane < V_HEAD, outs[0], outs[1]).astype(o_ref.dtype)


def _attention(q, k_l, k_c, v_l, v_c, tq, heads):
    b, l, _ = q.shape
    lc = k_c.shape[1]
    hp = N_HEADS // heads
    kw = heads * HEAD_PAD
    vw = heads * V_HEAD
    return pl.pallas_call(
        functools.partial(_attn_kernel, heads=heads),
        grid=(b, hp, l // tq),
        in_specs=[
            pl.BlockSpec((None, tq, kw), lambda i, h, j: (i, j, h)),
            pl.BlockSpec((None, l, kw), lambda i, h, j: (i, 0, h)),
            pl.BlockSpec((None, lc, kw), lambda i, h, j: (i, 0, h)),
            pl.BlockSpec((None, l, vw), lambda i, h, j: (i, 0, h)),
            pl.BlockSpec((None, lc, vw), lambda i, h, j: (i, 0, h)),
        ],
        out_specs=pl.BlockSpec((None, tq, vw), lambda i, h, j: (i, j, h)),
        out_shape=jax.ShapeDtypeStruct((b, l, N_HEADS * V_HEAD), BF16),
        compiler_params=_cparams(("arbitrary", "arbitrary", "arbitrary")),
        name="mla_attention",
    )(q, k_l, k_c, v_l, v_c)


def _mix_cd_kernel(yf_ref, yb_ref, o_ref, x_ref, g1_ref, sh2_ref, sc2_ref,
                   wglu_ref, wout_ref, wr_ref, br_ref, xl_ref, xm_ref, rt_ref):
    tm = x_ref.shape[0]
    n_slab, rows, _ = yf_ref.shape
    nb = rows // tm
    bi = pl.program_id(1)
    dc = n_slab * LANE
    pick = lambda ref, s: ref[s, pl.ds(bi, tm, stride=nb), :]
    y = jnp.concatenate([pick(yf_ref, s) + pick(yb_ref, s) for s in range(n_slab)], axis=1)
    g = jax.nn.gelu(y)
    s5 = (g * _sigmoid(_dot(g.astype(BF16), wglu_ref[...]))).astype(BF16)
    y_mix = _dot(s5, wout_ref[0:dc, :]) + _dot(o_ref[...], wout_ref[dc:, :])
    _residual_and_route(x_ref[...], y_mix, g1_ref[...], sh2_ref[...], sc2_ref[...], wr_ref, br_ref,
                        xl_ref, xm_ref, rt_ref)


def _mix_cd(yf, yb, o, x, g1, sh2, sc2, wglu, wout, wr, br, tm):
    b, l, d = x.shape
    n_slab = yf.shape[0]
    tok = lambda j, i: (i, j, 0)
    per_b = lambda j, i: (i, 0, 0)
    const2 = lambda j, i: (0, 0)
    tm_blk = pl.BlockSpec((n_slab, tm * b, LANE), lambda j, i: (0, j, 0))
    return pl.pallas_call(
        _mix_cd_kernel,
        grid=(l // tm, b),
        in_specs=[
            tm_blk,
            tm_blk,
            pl.BlockSpec((None, tm, o.shape[-1]), tok),
            pl.BlockSpec((None, tm, d), tok),
            pl.BlockSpec((None, 1, d), per_b),
            pl.BlockSpec((None, 1, d), per_b),
            pl.BlockSpec((None, 1, d), per_b),
            pl.BlockSpec(wglu.shape, const2),
            pl.BlockSpec(wout.shape, const2),
            pl.BlockSpec(wr.shape, const2),
            pl.BlockSpec((1, LANE), const2),
        ],
        out_specs=[
            pl.BlockSpec((None, tm, d), tok),
            pl.BlockSpec((tm * TOKEN_ROWS, LANE), lambda j, i: (i * (l // tm) + j, 0)),
            pl.BlockSpec((None, tm, LANE), tok),
        ],
        out_shape=[
            jax.ShapeDtypeStruct((b, l, d), F32),
            jax.ShapeDtypeStruct((b * l * TOKEN_ROWS, LANE), F32),
            jax.ShapeDtypeStruct((b, l, LANE), F32),
        ],
        compiler_params=_cparams(("arbitrary", "arbitrary")),
        name="mix_cd",
    )(yf, yb, o, x, g1, sh2, sc2, wglu, wout, wr, br)


def _router_weights(w_grp, b_grp, w_exp, b_exp, split):
    d = w_grp.shape[0]
    pad = LANE - N_GROUPS - N_EXPERTS
    wr = jnp.concatenate([w_grp, w_exp, jnp.zeros((d, pad), F32)], axis=1)
    br = jnp.concatenate([b_grp, b_exp, jnp.zeros((pad,), F32)]).reshape(1, LANE)
    if not split:
        return wr, br
    w_hi = lax.reduce_precision(wr, exponent_bits=8, mantissa_bits=7)
    w_lo = wr - w_hi
    return jnp.concatenate([w_hi, w_hi, w_lo], axis=0).astype(BF16), br


def _pad_heads(w, width):
    k = w.shape[0]
    w = w.reshape(k, N_HEADS, width)
    return jnp.pad(w, ((0, 0), (0, 0), (0, HEAD_PAD - width))).reshape(k, N_HEADS * HEAD_PAD)


def _rope_tables(l):
    rows = l // GRID_W
    row = jnp.repeat(jnp.arange(rows, dtype=F32), GRID_W)
    col = jnp.tile(jnp.arange(GRID_W, dtype=F32), rows)
    inv_freq = ROPE_BASE ** (-jnp.arange(AXIS_PAIRS, dtype=F32) / AXIS_PAIRS)
    ar, ac = row[:, None] * inv_freq, col[:, None] * inv_freq
    one = jnp.ones((l, QK_NOPE), F32)
    zn = jnp.zeros((l, QK_NOPE), F32)
    zp = jnp.zeros((l, HEAD_PAD - QK_HEAD), F32)
    cos = jnp.concatenate([one, jnp.cos(ar), jnp.cos(ar), jnp.cos(ac), jnp.cos(ac), zp], axis=1)
    sin = jnp.concatenate([zn, jnp.sin(ar), jnp.sin(ar), jnp.sin(ac), jnp.sin(ac), zp], axis=1)
    return cos, sin


def _rot_swap(t, signed):
    a = AXIS_PAIRS
    s = -1.0 if signed else 1.0
    return jnp.concatenate([s * t[..., a:2 * a], t[..., 0:a], s * t[..., 3 * a:4 * a], t[..., 2 * a:3 * a]], axis=-1)


def _head_lanes(rot):
    pad = [(0, 0)] * (rot.ndim - 1) + [(QK_NOPE, HEAD_PAD - QK_HEAD)]
    return jnp.pad(rot, pad)


def _split_mod(mod_row, b, d):
    parts = jnp.split(mod_row, 6, axis=-1)
    return [jnp.broadcast_to(p.reshape(-1, 1, d), (b, 1, d)) for p in parts]


def kernel(x, c, ctx, c_ctx, ada_w, ada_b, ab_w_in, sgu_norm, sgu_w, sgu_b, conv_w, ab_w_out, cd_w_in, s5_a_re, s5_a_im, s5_log_dt, s5_b_re, s5_b_im, s5_c_re, s5_c_im, s5_d, s5_w_glu, mla_q_norm, mla_kv_norm, mla_w_uq, mla_w_uk, mla_w_uv, mla_qn_gain, mla_kn_gain, cd_w_out, moe_w_grp, moe_b_grp, moe_w_exp, moe_b_exp, moe_w_gate, moe_w_up, moe_w_down):
    b, l, d = x.shape
    lc = ctx.shape[1]
    depth = ada_w.shape[0]
    assert depth == 2, "layer 0 = gated-MLP/conv mixers, layer 1 = S5/attention mixers"
    assert d == TOKEN_ROWS * LANE, "token-tile layout: one vreg tile per token"
    tm = min(TOKEN_TILE, l)
    tmc = min(TOKEN_TILE, lc)

    mod_rows = 2 * SUBLANE
    cvec = jnp.zeros((mod_rows, d), F32).at[:b].set(c).at[b].set(c_ctx)
    mod = _ada(cvec, ada_w, ada_b)

    sh1, sc1, g1, sh2, sc2, g2 = _split_mod(mod[0, :b], b, d)
    csh1, csc1, cg1, csh2, csc2, cg2 = _split_mod(mod[0, b:b + 1], b, d)
    w_in = ab_w_in[0].astype(BF16)
    wr, br = _router_weights(moe_w_grp[0], moe_b_grp[0], moe_w_exp[0], moe_b_exp[0], True)
    sgub = jnp.repeat(sgu_b[0].T, d // G_A, axis=1)
    mix_args = (sgu_norm[0].reshape(1, d), sgu_w[0].astype(BF16), sgub, conv_w[0],
                ab_w_out[0].astype(BF16), wr, br)
    z, p = _inproj_ab(x, sh1, sc1, w_in, tm)
    xl, xm, rt = _mix_ab(z, p, x, g1, sh2, sc2, *mix_args, tm)
    zc, pc = _inproj_ab(ctx, csh1, csc1, w_in, tmc)
    xc, xmc, rtc = _mix_ab(zc, pc, ctx, cg1, csh2, csc2, *mix_args, tmc)
    xl, xc = _hier_moe([(xl, xm, rt, g2), (xc, xmc, rtc, cg2)], 0, moe_w_gate, moe_w_up, moe_w_down)

    sh1, sc1, g1, sh2, sc2, g2 = _split_mod(mod[1, :b], b, d)
    csh1, csc1, _, _, _, _ = _split_mod(mod[1, b:b + 1], b, d)
    d_c = s5_d.shape[-1]
    q_lora = mla_q_norm.shape[-1]
    kv_lora = mla_kv_norm.shape[-1]
    w_cd = cd_w_in[0]
    o_kr = d_c + q_lora + kv_lora
    w_kr = _head_lanes(w_cd[:, o_kr:])
    w_krs = _head_lanes(_rot_swap(w_cd[:, o_kr:], True))
    w_cd = jnp.concatenate([w_cd[:, :o_kr], w_kr, w_krs], axis=1).astype(BF16)
    widths = (d_c, q_lora, kv_lora, HEAD_PAD, HEAD_PAD)
    u_l, cq_l, ckv_l, krp_l, krs_l = _inproj_cd(xl, sh1, sc1, w_cd, widths, tm)
    u_c, _, ckv_c, krp_c, _ = _inproj_cd(xc, csh1, csc1, w_cd, widths, tmc)

    ab_re, ab_im, bb_re, bb_im = _s5_params(s5_a_re[0], s5_a_im[0], s5_log_dt[0], s5_b_re[0], s5_b_im[0])
    a_bc, bsb, csb = _s5_block_matrices(ab_re, ab_im, bb_re, bb_im, s5_c_re[0], s5_c_im[0], b)
    yf, yb = _s5_scan(u_c, u_l, a_bc, bsb, csb, s5_d[0].reshape(1, d_c), b)

    wq = _pad_heads(mla_w_uq[0], QK_HEAD).astype(BF16)
    wk = _pad_heads(mla_w_uk[0], QK_NOPE).astype(BF16)
    wv = mla_w_uv[0].astype(BF16)
    w_uq = mla_w_uq[0].reshape(q_lora, N_HEADS, QK_HEAD)
    wqs = _head_lanes(_rot_swap(w_uq[..., QK_NOPE:], True)).reshape(q_lora, N_HEADS * HEAD_PAD).astype(BF16)
    qg, kg = mla_qn_gain[0], mla_kn_gain[0]
    full = lambda g: jnp.pad(g, (0, HEAD_PAD - QK_HEAD)).reshape(1, HEAD_PAD)
    partner = lambda g: _head_lanes(_rot_swap(g[QK_NOPE:], False)).reshape(1, HEAD_PAD)
    cos, sin = _rope_tables(l)
    tables = (cos * full(qg), sin * partner(qg), cos * full(kg), sin * partner(kg))
    qn = mla_q_norm[0].reshape(1, q_lora)
    kvn = mla_kv_norm[0].reshape(1, kv_lora)
    q_l, k_l, v_l = _mla_prep(_mla_prep_lat_kernel, "mla_prep_lat", (cq_l, ckv_l, krp_l, krs_l),
                              (qn, kvn, wq, wqs, wk, wv), tables, 1, tm)
    k_c, v_c = _mla_prep(_mla_prep_ctx_kernel, "mla_prep_ctx", (ckv_c, krp_c), (kvn, wk, wv, full(kg)), (), 0, tmc)
    o_l = _attention(q_l, k_l, k_c, v_l, v_c, ATTN_Q_TILE, ATTN_HEADS)

    wr, br = _router_weights(moe_w_grp[1], moe_b_grp[1], moe_w_exp[1], moe_b_exp[1], True)
    xl, xm, rt = _mix_cd(yf, yb, o_l, xl, g1, sh2, sc2,
                         s5_w_glu[0].astype(BF16), cd_w_out[0].astype(BF16), wr, br, MIX_CD_TILE)
    (xl,) = _hier_moe([(xl, xm, rt, g2)], 1, moe_w_gate, moe_w_up, moe_w_down)
    return xl
```

```python
import functools
import math

import jax
import jax.numpy as jnp
from jax import lax
from jax.experimental import pallas as pl
from jax.experimental.pallas import tpu as pltpu

F32 = jnp.float32
BF16 = jnp.bfloat16
I32 = jnp.int32
HIGHEST = lax.Precision.HIGHEST

EPS = 1e-6
LOG2_E = math.log2(math.e)
GRID_W = 64
CHUNK = 128
G_A = 8
CONV_W = 3
S5_GROUP = 16
S5_STATE = 64
N_HEADS = 8
QK_NOPE = 64
QK_ROPE = 32
QK_HEAD = QK_NOPE + QK_ROPE
V_HEAD = 64
AXIS_PAIRS = QK_ROPE // 4
ROPE_BASE = 10000.0
N_GROUPS = 4
EXPERTS_PER_GROUP = 8
N_EXPERTS = N_GROUPS * EXPERTS_PER_GROUP
TOP_K = 2

LANE = 128
SUBLANE = 8
HEAD_PAD = LANE
TOKEN_TILE = 512
MIX_SUBTILES = 2
MIX_CD_TILE = 256
MOE_TILE = 512
COMBINE_TILE = 256
PLAN_BLOCK = 1024
MAX_TILES = 256
TOKEN_ROWS = SUBLANE
ATTN_Q_TILE = 512
ATTN_HEADS = 8
S5_STEPS = 64
S5_SUPER = LANE // S5_GROUP
VMEM_LIMIT = 56 * 1024 * 1024


def _cparams(sem):
    return pltpu.CompilerParams(dimension_semantics=sem, vmem_limit_bytes=VMEM_LIMIT)


def _rms(x):
    return x * lax.rsqrt(jnp.mean(x * x, axis=-1, keepdims=True) + EPS)


def _sigmoid(x):
    return 1.0 / (1.0 + jnp.exp(-x))


def _dot(a, b):
    return jnp.dot(a, b, preferred_element_type=F32)


def _ada_kernel(c_ref, w_ref, b_ref, o_ref):
    c = c_ref[...]
    s = c * _sigmoid(c)
    o_ref[...] = jnp.dot(s, w_ref[...], precision=HIGHEST, preferred_element_type=F32) + b_ref[...]


def _ada(cvec, ada_w, ada_b):
    depth, d, n = ada_w.shape
    rows = cvec.shape[0]
    tn = 1024
    return pl.pallas_call(
        _ada_kernel,
        grid=(depth, n // tn),
        in_specs=[
            pl.BlockSpec((rows, d), lambda i, j: (0, 0)),
            pl.BlockSpec((None, d, tn), lambda i, j: (i, 0, j)),
            pl.BlockSpec((None, 1, tn), lambda i, j: (i, 0, j)),
        ],
        out_specs=pl.BlockSpec((None, rows, tn), lambda i, j: (i, 0, j)),
        out_shape=jax.ShapeDtypeStruct((depth, rows, n), F32),
        compiler_params=_cparams(("arbitrary", "arbitrary")),
        name="ada_mod",
    )(cvec, ada_w, ada_b.reshape(depth, 1, n))


def _route(xm, wr_ref, br_ref):
    hi = xm.astype(BF16)
    lo = (xm - hi.astype(F32)).astype(BF16)
    w = wr_ref[...]
    w_hi = w.astype(BF16)
    w_lo = (w - w_hi.astype(F32)).astype(BF16)
    logits = _dot(jnp.concatenate([hi, lo, hi], axis=1), jnp.concatenate([w_hi, w_hi, w_lo], axis=0)) + br_ref[...]
    lane = lax.broadcasted_iota(I32, logits.shape, 1).astype(F32)
    neg = jnp.float32(-1e30)
    big = jnp.float32(1e6)
    is_grp = lane < N_GROUPS
    gl = jnp.where(is_grp, logits, neg)
    gmax = jnp.max(gl, axis=-1, keepdims=True)
    gsum = jnp.sum(jnp.where(is_grp, jnp.exp(gl - gmax), 0.0), axis=-1, keepdims=True)
    p_top = 1.0 / gsum
    grp = jnp.min(jnp.where(gl == gmax, lane, big), axis=-1, keepdims=True)
    eidx = lane - N_GROUPS
    in_grp = (eidx >= grp * EXPERTS_PER_GROUP) & (eidx < (grp + 1.0) * EXPERTS_PER_GROUP)
    el = jnp.where(in_grp, logits, neg)
    m1 = jnp.max(el, axis=-1, keepdims=True)
    i1 = jnp.min(jnp.where(el == m1, lane, big), axis=-1, keepdims=True)
    el2 = jnp.where(lane == i1, neg, el)
    m2 = jnp.max(el2, axis=-1, keepdims=True)
    i2 = jnp.min(jnp.where(el2 == m2, lane, big), axis=-1, keepdims=True)
    t = jnp.exp(m2 - m1)
    w1 = p_top / (1.0 + t)
    w2 = p_top * t / (1.0 + t)
    e1 = i1 - N_GROUPS
    e2 = i2 - N_GROUPS
    return jnp.where(lane == 0, e1, jnp.where(lane == 1, e2, jnp.where(lane == 2, w1, jnp.where(lane == 3, w2, 0.0))))


def _residual_and_route(x, y, g1, sh2, sc2, wr_ref, br_ref, xl_ref, xm_ref, rt_ref):
    xl = x + g1 * y
    xl_ref[...] = xl
    xm = _rms(xl) * (1.0 + sc2) + sh2
    _store_token_tiles(xm_ref, xm)
    rt_ref[...] = _route(xm, wr_ref, br_ref)


def _inproj_ab_kernel(x_ref, sh_ref, sc_ref, w_ref, z_ref, p_ref):
    d = x_ref.shape[-1]
    h = (_rms(x_ref[...]) * (1.0 + sc_ref[...]) + sh_ref[...]).astype(BF16)
    for j in range(3):
        z_ref[:, j * d:(j + 1) * d] = _dot(h, w_ref[:, j * d:(j + 1) * d]).astype(BF16)
    gate_c = _dot(h, w_ref[:, 3 * d:4 * d])
    xb = _dot(h, w_ref[:, 4 * d:5 * d])
    p_ref[...] = gate_c * xb


def _inproj_ab(x, sh, sc, w_bf16, tm):
    b, l, d = x.shape
    n = w_bf16.shape[1]
    return pl.pallas_call(
        _inproj_ab_kernel,
        grid=(b, l // tm),
        in_specs=[
            pl.BlockSpec((None, tm, d), lambda i, j: (i, j, 0)),
            pl.BlockSpec((None, 1, d), lambda i, j: (i, 0, 0)),
            pl.BlockSpec((None, 1, d), lambda i, j: (i, 0, 0)),
            pl.BlockSpec((d, n), lambda i, j: (0, 0)),
        ],
        out_specs=[
            pl.BlockSpec((None, tm, 3 * d), lambda i, j: (i, j, 0)),
            pl.BlockSpec((None, tm, d), lambda i, j: (i, j, 0)),
        ],
        out_shape=[
            jax.ShapeDtypeStruct((b, l, 3 * d), BF16),
            jax.ShapeDtypeStruct((b, l, d), F32),
        ],
        compiler_params=_cparams(("arbitrary", "arbitrary")),
        name="inproj_ab",
    )(x, sh, sc, w_bf16)


def _mix_ab_kernel(z_ref, p_ref, pprev_ref, pnext_ref, x_ref, g1_ref, sh2_ref, sc2_ref,
                   sgun_ref, sguw_ref, sgub_ref, convw_ref, wout_ref, wr_ref, br_ref,
                   xl_ref, xm_ref, rt_ref, pbuf, ycat):
    i = pl.program_id(1)
    nt = pl.num_programs(1)
    tm, d = x_ref.shape
    cg = d // G_A
    pbuf[SUBLANE:SUBLANE + tm, :] = p_ref[...]
    pbuf[0:SUBLANE, :] = jnp.where(i > 0, pprev_ref[...], 0.0)
    pbuf[SUBLANE + tm:2 * SUBLANE + tm, :] = jnp.where(i < nt - 1, pnext_ref[...], 0.0)
    ts = tm // (MIX_SUBTILES if tm % (MIX_SUBTILES * CHUNK) == 0 else 1)
    for t0 in range(0, tm, ts):
        rows = slice(t0, t0 + ts)
        v = jax.nn.gelu(z_ref[rows, d:2 * d].astype(F32))
        vc = (_rms(v) * sgun_ref[...]).astype(BF16)
        for c in range(ts // CHUNK):
            lr = c * CHUNK
            r0 = t0 + lr
            cols = [_dot(sguw_ref[g], vc[lr:lr + CHUNK, g * cg:(g + 1) * cg]) for g in range(G_A)]
            s = jnp.concatenate(cols, axis=1) + sgub_ref[...]
            u = jax.nn.gelu(z_ref[r0:r0 + CHUNK, 0:d].astype(F32))
            ycat[r0:r0 + CHUNK, 0:d] = (u * s).astype(BF16)
        conv = (convw_ref[0:1, :] * pbuf[SUBLANE - 1 + t0:SUBLANE - 1 + t0 + ts, :]
                + convw_ref[1:2, :] * pbuf[SUBLANE + t0:SUBLANE + t0 + ts, :]
                + convw_ref[2:3, :] * pbuf[SUBLANE + 1 + t0:SUBLANE + 1 + t0 + ts, :])
        ycat[rows, d:2 * d] = (z_ref[rows, 2 * d:3 * d].astype(F32) * conv).astype(BF16)
        y = _dot(ycat[rows, :], wout_ref[...])
        _residual_and_route(x_ref[rows, :], y, g1_ref[...], sh2_ref[...], sc2_ref[...], wr_ref, br_ref,
                            xl_ref.at[rows, :], xm_ref.at[pl.ds(t0 * TOKEN_ROWS, ts * TOKEN_ROWS), :],
                            rt_ref.at[rows, :])


def _mix_ab(z, p, x, g1, sh2, sc2, sgun, sguw, sgub, convw, wout, wr, br, tm):
    b, l, d = x.shape
    hb = tm // SUBLANE
    nhb = l // SUBLANE
    tok = lambda i, j: (i, j, 0)
    per_b = lambda i, j: (i, 0, 0)
    const2 = lambda i, j: (0, 0)
    return pl.pallas_call(
        _mix_ab_kernel,
        grid=(b, l // tm),
        in_specs=[
            pl.BlockSpec((None, tm, 3 * d), tok),
            pl.BlockSpec((None, tm, d), tok),
            pl.BlockSpec((None, SUBLANE, d), lambda i, j: (i, jnp.maximum(j * hb - 1, 0), 0)),
            pl.BlockSpec((None, SUBLANE, d), lambda i, j: (i, jnp.minimum((j + 1) * hb, nhb - 1), 0)),
            pl.BlockSpec((None, tm, d), tok),
            pl.BlockSpec((None, 1, d), per_b),
            pl.BlockSpec((None, 1, d), per_b),
            pl.BlockSpec((None, 1, d), per_b),
            pl.BlockSpec((1, d), const2),
            pl.BlockSpec((G_A, CHUNK, CHUNK), lambda i, j: (0, 0, 0)),
            pl.BlockSpec((CHUNK, d), const2),
            pl.BlockSpec((CONV_W, d), const2),
            pl.BlockSpec((2 * d, d), const2),
            pl.BlockSpec(wr.shape, const2),
            pl.BlockSpec((1, LANE), const2),
        ],
        out_specs=[
            pl.BlockSpec((None, tm, d), tok),
            pl.BlockSpec((tm * TOKEN_ROWS, LANE), lambda i, j: (i * (l // tm) + j, 0)),
            pl.BlockSpec((None, tm, LANE), tok),
        ],
        out_shape=[
            jax.ShapeDtypeStruct((b, l, d), F32),
            jax.ShapeDtypeStruct((b * l * TOKEN_ROWS, LANE), F32),
            jax.ShapeDtypeStruct((b, l, LANE), F32),
        ],
        scratch_shapes=[
            pltpu.VMEM((tm + 2 * SUBLANE, d), F32),
            pltpu.VMEM((tm, 2 * d), BF16),
        ],
        compiler_params=_cparams(("arbitrary", "arbitrary")),
        name="mix_ab",
    )(z, p, p, p, x, g1, sh2, sc2, sgun, sguw, sgub, convw, wout, wr, br)


def _store_token_tiles(ref, val):
    tm, d = val.shape
    rpt = d // LANE
    for j in range(rpt):
        ref[pl.ds(j, tm, stride=rpt), :] = val[:, j * LANE:(j + 1) * LANE]


def _load_token_tile_col(ref, tm, rpt, j):
    return ref[pl.ds(j, tm, stride=rpt), :]


def _plan_kernel(rt_ref, tri_ref, pos_ref, te_ref, tv_ref, pad_ref, run, *, tm):
    ps = pl.program_id(0)
    i = pl.program_id(1)
    blk = rt_ref.shape[0]
    ne = N_EXPERTS

    @pl.when(jnp.logical_and(ps == 0, i == 0))
    def _():
        run[...] = jnp.zeros_like(run)

    @pl.when(jnp.logical_and(ps == 1, i == 0))
    def _():
        counts = run[...]
        padded = jnp.floor((counts + (tm - 1)) * (1.0 / tm)) * tm
        r = lax.broadcasted_iota(I32, (ne, ne), 0)
        c = lax.broadcasted_iota(I32, (ne, ne), 1)
        starts = jnp.dot((c < r).astype(F32), padded, precision=HIGHEST, preferred_element_type=F32)
        nt = te_ref.shape[1]
        ends = jnp.concatenate([(starts + padded) * (1.0 / tm)] * (nt // LANE), axis=1)
        t = lax.broadcasted_iota(I32, (ne, nt), 1).astype(F32)
        te = jnp.sum((t >= ends).astype(F32), axis=0, keepdims=True)
        total = jnp.max(ends, axis=0, keepdims=True)
        te_ref[...] = jnp.minimum(te, ne - 1.0).astype(I32)
        tv_ref[...] = (t[0:1, :] < total).astype(I32)
        lane = lax.broadcasted_iota(I32, (ne, LANE), 1)
        pad_ref[...] = jnp.where(lane == 0, starts + counts,
                                 jnp.where(lane == 1, padded - counts, 0.0)).astype(I32)
        run[...] = starts

    slab_t = rt_ref[...].T
    ex = lax.broadcasted_iota(I32, (ne, blk), 0).astype(F32)
    oh1 = ex == slab_t[0:1, :]
    oh2 = ex == slab_t[1:2, :]
    oh = jnp.where(jnp.logical_or(oh1, oh2), 1.0, 0.0)
    prefix = _dot(oh.astype(BF16), tri_ref[...]) + run[:, 0:1]
    pos1 = jnp.sum(jnp.where(oh1, prefix, 0.0), axis=0, keepdims=True)
    pos2 = jnp.sum(jnp.where(oh2, prefix, 0.0), axis=0, keepdims=True)
    pos_ref[...] = jnp.concatenate([pos1, pos2], axis=0).astype(I32)
    run[...] = run[...] + jnp.sum(oh, axis=1, keepdims=True)


def _moe_plan(rt_all, tm):
    n = rt_all.shape[0]
    blk = PLAN_BLOCK
    nb = n // blk
    tri = jnp.triu(jnp.ones((blk, blk), BF16), k=1)
    return pl.pallas_call(
        functools.partial(_plan_kernel, tm=tm),
        grid=(2, nb),
        in_specs=[
            pl.BlockSpec((blk, LANE), lambda p, i: (i, 0)),
            pl.BlockSpec((blk, blk), lambda p, i: (0, 0)),
        ],
        out_specs=[
            pl.BlockSpec((None, TOP_K, blk), lambda p, i: (jnp.where(p == 0, nb, i), 0, 0)),
            pl.BlockSpec((1, MAX_TILES), lambda p, i: (0, 0)),
            pl.BlockSpec((1, MAX_TILES), lambda p, i: (0, 0)),
            pl.BlockSpec((N_EXPERTS, LANE), lambda p, i: (0, 0)),
        ],
        out_shape=[
            jax.ShapeDtypeStruct((nb + 1, TOP_K, blk), I32),
            jax.ShapeDtypeStruct((1, MAX_TILES), I32),
            jax.ShapeDtypeStruct((1, MAX_TILES), I32),
            jax.ShapeDtypeStruct((N_EXPERTS, LANE), I32),
        ],
        scratch_shapes=[pltpu.VMEM((N_EXPERTS, LANE), F32)],
        compiler_params=_cparams(("arbitrary", "arbitrary")),
        name="moe_plan",
    )(rt_all, tri)


def _dispatch_kernel(ps_ref, pl_ref, tv_ref, pos_ref, *rest, blocks, n_tiles):
    xm_refs = rest[:len(blocks)]
    xs_hbm, zbuf, sem = rest[len(blocks):]
    i = pl.program_id(0)
    tt = pos_ref.shape[1]
    rpt = xm_refs[0].shape[0] // tt
    tile_rows = zbuf.shape[0]

    @pl.when(i == 0)
    def _():
        zbuf[...] = jnp.zeros_like(zbuf)

        def gap_copies(e, wait):
            base = ps_ref[e]
            n = pl_ref[e]
            bit = tile_rows // (2 * rpt)
            while bit >= 1:
                lower = n & (bit - 1)

                @pl.when((n & bit) != 0)
                def _(lower=lower, bit=bit):
                    dst = xs_hbm.at[pl.ds((base + lower) * rpt, bit * rpt)]
                    cp = pltpu.make_async_copy(zbuf.at[pl.ds(0, bit * rpt)], dst, sem.at[1])
                    cp.wait() if wait else cp.start()
                bit //= 2

        def tile_copy(t, wait):
            @pl.when(tv_ref[t] != 1)
            def _():
                dst = xs_hbm.at[pl.ds(pl.multiple_of(t * tile_rows, tile_rows), tile_rows)]
                cp = pltpu.make_async_copy(zbuf, dst, sem.at[1])
                cp.wait() if wait else cp.start()

        for wait in (False, True):
            lax.fori_loop(0, N_EXPERTS, lambda e, c, w=wait: (gap_copies(e, w), c)[1], 0)
            lax.fori_loop(0, n_tiles, lambda t, c, w=wait: (tile_copy(t, w), c)[1], 0)

    def scatter(xm_ref):
        def issue(r, carry):
            src = xm_ref.at[pl.ds(pl.multiple_of(r * rpt, rpt), rpt)]
            for k in range(TOP_K):
                dst = xs_hbm.at[pl.ds(pl.multiple_of(pos_ref[k, r] * rpt, rpt), rpt)]
                pltpu.make_async_copy(src, dst, sem.at[0]).start(priority=k)
            return carry
        lax.fori_loop(0, tt, issue, 0, unroll=8)
        for k in range(TOP_K):
            pltpu.make_async_copy(xm_ref, xs_hbm.at[pl.ds(0, tt * rpt)], sem.at[0]).wait()

    lo = 0
    for xm_ref, nblk in zip(xm_refs, blocks):
        pl.when(jnp.logical_and(i >= lo, i < lo + nblk))(functools.partial(scatter, xm_ref))
        lo += nblk


def _dispatch(pad_start, pad_len, valid, pos, xm_streams, n_tiles, tm):
    tt = pos.shape[-1]
    rpt = TOKEN_ROWS
    blocks = tuple(xm.shape[0] // (rpt * tt) for xm in xm_streams)
    in_specs = [pl.BlockSpec((None, TOP_K, tt), lambda i, a, b, c: (i, 0, 0), memory_space=pltpu.SMEM)]
    lo = 0
    for nblk in blocks:
        in_specs.append(pl.BlockSpec(
            (tt * rpt, LANE), lambda i, a, b, c, lo=lo, nblk=nblk: (jnp.clip(i - lo, 0, nblk - 1), 0)))
        lo += nblk
    return pl.pallas_call(
        functools.partial(_dispatch_kernel, blocks=blocks, n_tiles=n_tiles),
        grid_spec=pltpu.PrefetchScalarGridSpec(
            num_scalar_prefetch=3,
            grid=(sum(blocks),),
            in_specs=in_specs,
            out_specs=pl.BlockSpec(memory_space=pl.ANY),
            scratch_shapes=[pltpu.VMEM((tm * rpt, LANE), F32), pltpu.SemaphoreType.DMA((2,))],
        ),
        out_shape=jax.ShapeDtypeStruct((n_tiles * tm * rpt, LANE), F32),
        compiler_params=_cparams(("arbitrary",)),
        name="moe_dispatch",
    )(pad_start, pad_len, valid, pos, *xm_streams)


def _moe_kernel(te_ref, tv_ref, x_ref, wg_ref, wu_ref, wd_ref, y_ref, wg_s, wu_s, wd_s):
    t = pl.program_id(0)
    rpt = TOKEN_ROWS
    tm = x_ref.shape[0] // rpt

    @pl.when(tv_ref[t] == 1)
    def _():
        changed = jnp.logical_or(t == 0, te_ref[t] != te_ref[jnp.maximum(t - 1, 0)])

        @pl.when(changed)
        def _():
            wg_s[...] = wg_ref[...].astype(BF16)
            wu_s[...] = wu_ref[...].astype(BF16)
            wd_s[...] = wd_ref[...].astype(BF16)

        x = jnp.concatenate([_load_token_tile_col(x_ref, tm, rpt, j) for j in range(rpt)], axis=1).astype(BF16)
        g = _dot(x, wg_s[...])
        u = _dot(x, wu_s[...])
        h = (g * _sigmoid(g) * u).astype(BF16)
        _store_token_tiles(y_ref, _dot(h, wd_s[...]))

    @pl.when(tv_ref[t] != 1)
    def _():
        y_ref[...] = jnp.zeros_like(y_ref)


def _moe_experts(xs, tile_e, valid, n_tiles, layer, w_gate, w_up, w_down, tm):
    _, _, d, de = w_gate.shape
    rpt = TOKEN_ROWS
    grid_spec = pltpu.PrefetchScalarGridSpec(
        num_scalar_prefetch=2,
        grid=(n_tiles,),
        in_specs=[
            pl.BlockSpec((tm * rpt, LANE), lambda t, te, tv: (t * tv[t], 0)),
            pl.BlockSpec((None, None, d, de), lambda t, te, tv: (layer, te[t], 0, 0)),
            pl.BlockSpec((None, None, d, de), lambda t, te, tv: (layer, te[t], 0, 0)),
            pl.BlockSpec((None, None, de, d), lambda t, te, tv: (layer, te[t], 0, 0)),
        ],
        out_specs=pl.BlockSpec((tm * rpt, LANE), lambda t, te, tv: (t, 0)),
        scratch_shapes=[
            pltpu.VMEM((d, de), BF16),
            pltpu.VMEM((d, de), BF16),
            pltpu.VMEM((de, d), BF16),
        ],
    )
    return pl.pallas_call(
        _moe_kernel,
        grid_spec=grid_spec,
        out_shape=jax.ShapeDtypeStruct((n_tiles * tm * rpt, LANE), F32),
        compiler_params=_cparams(("arbitrary",)),
        name="moe_experts",
    )(tile_e, valid, xs, w_gate, w_up, w_down)


def _combine_kernel(pos0_ref, posn_ref, x_ref, g_ref, rt_ref, ys_hbm, o_ref, buf, sem, *, n):
    i = pl.program_id(0)
    tt, d = x_ref.shape
    rpt = TOKEN_ROWS

    def gather(pos_ref, slot):
        def issue(r, carry):
            for k in range(TOP_K):
                src = ys_hbm.at[pl.ds(pl.multiple_of(pos_ref[k, r] * rpt, rpt), rpt)]
                dst = buf.at[slot, k, pl.ds(pl.multiple_of(r * rpt, rpt), rpt)]
                pltpu.make_async_copy(src, dst, sem.at[slot, k]).start(priority=k)
            return carry
        lax.fori_loop(0, tt, issue, 0, unroll=8)

    slot = lax.rem(i, 2)

    @pl.when(i == 0)
    def _():
        gather(pos0_ref, 0)

    @pl.when(i + 1 < n)
    def _():
        gather(posn_ref, 1 - slot)

    for k in range(TOP_K):
        pltpu.make_async_copy(ys_hbm.at[pl.ds(0, tt * rpt)], buf.at[slot, k], sem.at[slot, k]).wait()
    w1 = rt_ref[:, 2:3]
    w2 = rt_ref[:, 3:4]
    for j in range(rpt):
        r1 = buf[slot, 0, pl.ds(j, tt, stride=rpt), :]
        r2 = buf[slot, 1, pl.ds(j, tt, stride=rpt), :]
        sl = slice(j * LANE, (j + 1) * LANE)
        o_ref[:, sl] = x_ref[:, sl] + g_ref[:, sl] * (w1 * r1 + w2 * r2)


def _moe_combine(x, g2, rt, pos, tok_off, ys, tt):
    b, l, d = x.shape
    rpt = TOKEN_ROWS
    n = b * l
    per_b = l // tt
    per_blk = PLAN_BLOCK // tt
    blk0 = tok_off // PLAN_BLOCK

    def pos_map(step):
        return lambda i: (blk0 + step(i) // per_blk, 0, step(i) % per_blk)

    return pl.pallas_call(
        functools.partial(_combine_kernel, n=n // tt),
        grid=(n // tt,),
        in_specs=[
            pl.BlockSpec((None, TOP_K, tt), pos_map(lambda i: i), memory_space=pltpu.SMEM),
            pl.BlockSpec((None, TOP_K, tt), pos_map(lambda i: jnp.minimum(i + 1, n // tt - 1)),
                         memory_space=pltpu.SMEM),
            pl.BlockSpec((tt, d), lambda i: (i, 0)),
            pl.BlockSpec((None, 1, d), lambda i: (i // per_b, 0, 0)),
            pl.BlockSpec((tt, LANE), lambda i: (i, 0)),
            pl.BlockSpec(memory_space=pl.ANY),
        ],
        out_specs=pl.BlockSpec((tt, d), lambda i: (i, 0)),
        out_shape=jax.ShapeDtypeStruct((n, d), F32),
        scratch_shapes=[
            pltpu.VMEM((2, TOP_K, tt * rpt, LANE), F32),
            pltpu.SemaphoreType.DMA((2, TOP_K)),
        ],
        compiler_params=_cparams(("arbitrary",)),
        name="moe_combine",
    )(pos, pos, x.reshape(n, d), g2, rt.reshape(n, LANE), ys).reshape(b, l, d)


def _hier_moe(streams, layer, w_gate, w_up, w_down):
    sizes = [s[0].shape[0] * s[0].shape[1] for s in streams]
    n = sum(sizes)
    tm = MOE_TILE
    n_tiles = (TOP_K * n) // tm + N_EXPERTS
    assert n_tiles <= MAX_TILES and all(sz % PLAN_BLOCK == 0 for sz in sizes)
    rt_all = jnp.concatenate([s[2].reshape(-1, LANE) for s in streams], axis=0)
    pos, tile_e, valid, pad = _moe_plan(rt_all, tm)
    tile_e, valid = tile_e.reshape(-1), valid.reshape(-1)
    xs = _dispatch(pad[:, 0], pad[:, 1], valid, pos, [s[1] for s in streams], n_tiles, tm)
    ys = _moe_experts(xs, tile_e, valid, n_tiles, layer, w_gate, w_up, w_down, tm)
    outs = []
    off = 0
    for (xl, _, rt, g2), sz in zip(streams, sizes):
        outs.append(_moe_combine(xl, g2, rt, pos, off, ys, COMBINE_TILE))
        off += sz
    return outs


def _inproj_cd_kernel(x_ref, sh_ref, sc_ref, w_ref, u_ref, *out_refs):
    h = (_rms(x_ref[...]) * (1.0 + sc_ref[...]) + sh_ref[...]).astype(BF16)
    tm = x_ref.shape[0]
    n_slab, rows, cw = u_ref.shape
    nb = rows // tm
    bi = pl.program_id(1)
    u = _dot(h, w_ref[:, 0:n_slab * cw])
    for s in range(n_slab):
        u_ref[s, pl.ds(bi, tm, stride=nb), :] = u[:, s * cw:(s + 1) * cw]
    off = n_slab * cw
    for ref in out_refs:
        n = ref.shape[-1]
        ref[...] = _dot(h, w_ref[:, off:off + n]).astype(ref.dtype)
        off += n


def _inproj_cd(x, sh, sc, w_bf16, widths, tm):
    b, l, d = x.shape
    n = w_bf16.shape[1]
    n_slab = widths[0] // LANE
    tok = lambda j, i: (i, j, 0)
    return pl.pallas_call(
        _inproj_cd_kernel,
        grid=(l // tm, b),
        in_specs=[
            pl.BlockSpec((None, tm, d), tok),
            pl.BlockSpec((None, 1, d), lambda j, i: (i, 0, 0)),
            pl.BlockSpec((None, 1, d), lambda j, i: (i, 0, 0)),
            pl.BlockSpec((d, n), lambda j, i: (0, 0)),
        ],
        out_specs=([pl.BlockSpec((n_slab, tm * b, LANE), lambda j, i: (0, j, 0))]
                   + [pl.BlockSpec((None, tm, w), tok) for w in widths[1:]]),
        out_shape=([jax.ShapeDtypeStruct((n_slab, l * b, LANE), F32)]
                   + [jax.ShapeDtypeStruct((b, l, w), F32) for w in widths[1:]]),
        compiler_params=_cparams(("arbitrary", "arbitrary")),
        name="inproj_cd",
    )(x, sh, sc, w_bf16)


def _s5_param_kernel(are_ref, aim_ref, ldt_ref, bre_ref, bim_ref, abre_ref, abim_ref, bbre_ref, bbim_ref):
    a_re = are_ref[...]
    a_im = aim_ref[...]
    dt = jnp.exp(ldt_ref[...])
    mag = jnp.exp(dt * a_re)
    ab_re = mag * jnp.cos(dt * a_im)
    ab_im = mag * jnp.sin(dt * a_im)
    den = a_re * a_re + a_im * a_im
    nr = ab_re - 1.0
    f_re = (nr * a_re + ab_im * a_im) / den
    f_im = (ab_im * a_re - nr * a_im) / den
    abre_ref[...] = ab_re
    abim_ref[...] = ab_im
    b_re = bre_ref[...]
    b_im = bim_ref[...]
    bbre_ref[...] = f_re[None] * b_re - f_im[None] * b_im
    bbim_ref[...] = f_re[None] * b_im + f_im[None] * b_re


def _s5_params(a_re, a_im, log_dt, b_re, b_im):
    nd, g, p = a_re.shape
    c = b_re.shape[-1]
    rows = nd * g * p // LANE
    flat = lambda t: t.reshape(rows, LANE)
    chan_major = lambda t: jnp.moveaxis(t, -1, 0).reshape(c, rows, LANE)
    ldt = jnp.broadcast_to(log_dt[:, :, None], (nd, g, p))
    shapes = [jax.ShapeDtypeStruct((rows, LANE), F32)] * 2 + [jax.ShapeDtypeStruct((c, rows, LANE), F32)] * 2
    ab_re, ab_im, bb_re, bb_im = pl.pallas_call(
        _s5_param_kernel, out_shape=shapes, name="s5_params",
    )(flat(a_re), flat(a_im), flat(ldt), chan_major(b_re), chan_major(b_im))
    unflat = lambda t: jnp.moveaxis(t.reshape(c, nd, g, p), 0, -1)
    return ab_re.reshape(nd, g, p), ab_im.reshape(nd, g, p), unflat(bb_re), unflat(bb_im)


def _s5_kernel(ufc_ref, ubc_ref, ufl_ref, ubl_ref, a_ref, bsb_ref, csb_ref, d_ref, yf_ref, yb_ref,
               hf, hb, cf, cb, *, n_ctx):
    i = pl.program_id(0)
    nb = cf.shape[0]
    steps = hf.shape[0] // nb
    n_super = bsb_ref.shape[1]
    cw = bsb_ref.shape[2]
    sw = bsb_ref.shape[3]
    half = sw // 2

    @pl.when(i == 0)
    def _():
        cf[...] = jnp.zeros_like(cf)
        cb[...] = jnp.zeros_like(cb)

    def drive(uf_ref, ub_ref):
        for s in range(n_super):
            hf[:, s * sw:(s + 1) * sw] = _dot(uf_ref[s].astype(BF16), bsb_ref[0, s])
            hb[:, s * sw:(s + 1) * sw] = _dot(ub_ref[s].astype(BF16), bsb_ref[1, s])

    pl.when(i < n_ctx)(functools.partial(drive, ufc_ref, ubc_ref))
    pl.when(i >= n_ctx)(functools.partial(drive, ufl_ref, ubl_ref))

    for s in range(n_super):
        re = slice(s * sw, s * sw + half)
        im = slice(s * sw + half, (s + 1) * sw)
        st = slice(s * half, (s + 1) * half)
        arf, aif = a_ref[0, 0, :, st], a_ref[0, 1, :, st]
        arb, aib = a_ref[1, 0, :, st], a_ref[1, 1, :, st]

        def step(t, carry):
            hfr, hfi, hbr, hbi = carry
            rf = pl.multiple_of(t * nb, nb)
            nfr = arf * hfr - aif * hfi + hf[pl.ds(rf, nb), re]
            nfi = arf * hfi + aif * hfr + hf[pl.ds(rf, nb), im]
            hf[pl.ds(rf, nb), re] = nfr
            hf[pl.ds(rf, nb), im] = nfi
            rb = pl.multiple_of((steps - 1 - t) * nb, nb)
            nbr = arb * hbr - aib * hbi + hb[pl.ds(rb, nb), re]
            nbi = arb * hbi + aib * hbr + hb[pl.ds(rb, nb), im]
            hb[pl.ds(rb, nb), re] = nbr
            hb[pl.ds(rb, nb), im] = nbi
            return nfr, nfi, nbr, nbi

        out = lax.fori_loop(0, steps, step, (cf[:, re], cf[:, im], cb[:, re], cb[:, im]), unroll=4)
        cf[:, re], cf[:, im], cb[:, re], cb[:, im] = out

    @pl.when(i >= n_ctx)
    def _():
        for s in range(n_super):
            skip = ufl_ref[s] * d_ref[:, s * cw:(s + 1) * cw]
            yf_ref[s] = _dot(hf[:, s * sw:(s + 1) * sw].astype(BF16), csb_ref[0, s]) + skip
            yb_ref[s] = _dot(hb[:, s * sw:(s + 1) * sw].astype(BF16), csb_ref[1, s])


def _s5_scan(u_c, u_l, a_bc, bsb, csb, d_skip, nb):
    n_slab, rows_l, _ = u_l.shape
    blk = S5_STEPS * nb
    nc = u_c.shape[1] // blk
    nl = rows_l // blk
    n_state = bsb.shape[1] * bsb.shape[3]
    clip = jnp.clip
    ctx_f = lambda i: (0, clip(i, 0, nc - 1), 0)
    ctx_b = lambda i: (0, clip(nc - 1 - i, 0, nc - 1), 0)
    lat_f = lambda i: (0, clip(i - nc, 0, nl - 1), 0)
    lat_b = lambda i: (0, clip(nl - 1 - (i - nc), 0, nl - 1), 0)
    whole = lambda a: pl.BlockSpec(a.shape, lambda i: (0,) * a.ndim)
    slab = lambda index_map: pl.BlockSpec((n_slab, blk, LANE), index_map)
    return pl.pallas_call(
        functools.partial(_s5_kernel, n_ctx=nc),
        grid=(nc + nl,),
        in_specs=[slab(ctx_f), slab(ctx_b), slab(lat_f), slab(lat_b), whole(a_bc), whole(bsb), whole(csb),
                  whole(d_skip)],
        out_specs=[slab(lat_f), slab(lat_b)],
        out_shape=[jax.ShapeDtypeStruct(u_l.shape, F32)] * 2,
        scratch_shapes=[
            pltpu.VMEM((blk, n_state), F32),
            pltpu.VMEM((blk, n_state), F32),
            pltpu.VMEM((nb, n_state), F32),
            pltpu.VMEM((nb, n_state), F32),
        ],
        compiler_params=_cparams(("arbitrary",)),
        name="s5_scan",
    )(u_c, u_c, u_l, u_l, a_bc, bsb, csb, d_skip)


def _s5_block_matrices(ab_re, ab_im, bb_re, bb_im, c_re, c_im, nb):
    nd, g, p = ab_re.shape
    c = bb_re.shape[-1]
    ns = g // S5_SUPER
    eye = jnp.eye(S5_SUPER, dtype=F32)

    def in_mat(bb):
        t = bb.reshape(nd, ns, S5_SUPER, p, c)
        return jnp.einsum('dsgpc,gh->dsgchp', t, eye).reshape(nd, ns, S5_SUPER * c, S5_SUPER * p)

    def out_mat(cc):
        t = cc.reshape(nd, ns, S5_SUPER, c, p)
        return jnp.einsum('dsgcp,gh->dsgphc', t, eye).reshape(nd, ns, S5_SUPER * p, S5_SUPER * c)

    bsb = jnp.concatenate([in_mat(bb_re), in_mat(bb_im)], axis=-1).astype(BF16)
    csb = jnp.concatenate([out_mat(c_re), out_mat(-c_im)], axis=-2).astype(BF16)
    a_bc = jnp.stack([ab_re.reshape(nd, g * p), ab_im.reshape(nd, g * p)], axis=1)
    a_bc = jnp.broadcast_to(a_bc[:, :, None, :], (nd, 2, nb, g * p))
    return a_bc, bsb, csb


def _head_scale(x):
    return lax.rsqrt(jnp.sum(x * x, axis=-1, keepdims=True) * (1.0 / QK_HEAD) + EPS)


def _mla_prep_lat_kernel(cq_ref, ckv_ref, krp_ref, krs_ref, qn_ref, kvn_ref, wq_ref, wqs_ref, wk_ref, wv_ref,
                         qc_ref, qs_ref, kc_ref, ks_ref, q_ref, k_ref, v_ref):
    ckvn = (_rms(ckv_ref[...]) * kvn_ref[...]).astype(BF16)
    v_ref[...] = _dot(ckvn, wv_ref[...]).astype(v_ref.dtype)
    kn = _dot(ckvn, wk_ref[...])
    krp = krp_ref[...]
    k_swap = krs_ref[...] * ks_ref[...]
    kc = kc_ref[...]
    for h in range(N_HEADS):
        sl = slice(h * HEAD_PAD, (h + 1) * HEAD_PAD)
        kh = kn[:, sl] + krp
        k_ref[:, sl] = (_head_scale(kh) * (kh * kc + k_swap)).astype(k_ref.dtype)
    cqn = (_rms(cq_ref[...]) * qn_ref[...]).astype(BF16)
    qn = _dot(cqn, wq_ref[...])
    q_swap = _dot(cqn, wqs_ref[...])
    qc = qc_ref[...]
    qs = qs_ref[...]
    for h in range(N_HEADS):
        sl = slice(h * HEAD_PAD, (h + 1) * HEAD_PAD)
        qh = qn[:, sl]
        scale = _head_scale(qh) * (QK_HEAD ** -0.5 * LOG2_E)
        q_ref[:, sl] = (scale * (qh * qc + q_swap[:, sl] * qs)).astype(q_ref.dtype)


def _mla_prep_ctx_kernel(ckv_ref, krp_ref, kvn_ref, wk_ref, wv_ref, kg_ref, k_ref, v_ref):
    ckvn = (_rms(ckv_ref[...]) * kvn_ref[...]).astype(BF16)
    v_ref[...] = _dot(ckvn, wv_ref[...]).astype(v_ref.dtype)
    kn = _dot(ckvn, wk_ref[...])
    krp = krp_ref[...]
    kg = kg_ref[...]
    for h in range(N_HEADS):
        sl = slice(h * HEAD_PAD, (h + 1) * HEAD_PAD)
        kh = kn[:, sl] + krp
        k_ref[:, sl] = (_head_scale(kh) * (kh * kg)).astype(k_ref.dtype)


def _mla_prep(kernel_fn, name, tok_args, const_args, table_args, n_q_out, tm):
    b, l, _ = tok_args[0].shape
    tok = lambda i, j: (i, j, 0)
    specs = [pl.BlockSpec((None, tm, a.shape[-1]), tok) for a in tok_args]
    specs += [pl.BlockSpec(a.shape, lambda i, j: (0, 0)) for a in const_args]
    specs += [pl.BlockSpec((tm, HEAD_PAD), lambda i, j: (j, 0)) for _ in table_args]
    hk = N_HEADS * HEAD_PAD
    hv = N_HEADS * V_HEAD
    widths = [hk] * n_q_out + [hk, hv]
    return pl.pallas_call(
        kernel_fn,
        grid=(b, l // tm),
        in_specs=specs,
        out_specs=[pl.BlockSpec((None, tm, w), tok) for w in widths],
        out_shape=[jax.ShapeDtypeStruct((b, l, w), BF16) for w in widths],
        compiler_params=_cparams(("arbitrary", "arbitrary")),
        name=name,
    )(*tok_args, *const_args, *table_args)


def _attn_kernel(q_ref, kl_ref, kc_ref, vl_ref, vc_ref, o_ref, *, heads):
    nt = (((1,), (1,)), ((), ()))
    for pr in range(heads // 2):
        vsl = slice(pr * 2 * V_HEAD, (pr + 1) * 2 * V_HEAD)
        outs = []
        for hh in range(2):
            h = 2 * pr + hh
            sl = slice(h * HEAD_PAD, (h + 1) * HEAD_PAD)
            q = q_ref[:, sl]
            s_l = lax.dot_general(q, kl_ref[:, sl], nt, preferred_element_type=F32)
            s_c = lax.dot_general(q, kc_ref[:, sl], nt, preferred_element_type=F32)
            m = jnp.maximum(jnp.max(s_l, axis=-1, keepdims=True), jnp.max(s_c, axis=-1, keepdims=True))
            p_l = jnp.exp2(s_l - m)
            p_c = jnp.exp2(s_c - m)
            den = jnp.sum(p_l, axis=-1, keepdims=True) + jnp.sum(p_c, axis=-1, keepdims=True)
            o = _dot(p_l.astype(BF16), vl_ref[:, vsl]) + _dot(p_c.astype(BF16), vc_ref[:, vsl])
            outs.append(o / den)
        lane = lax.broadcasted_iota(I32, outs[0].shape, 1)
        o_ref[:, vsl] = jnp.where(lane < V_HEAD, outs[0], outs[1]).astype(o_ref.dtype)


def _attention(q, k_l, k_c, v_l, v_c, tq, heads):
    b, l, _ = q.shape
    lc = k_c.shape[1]
    hp = N_HEADS // heads
    kw = heads * HEAD_PAD
    vw = heads * V_HEAD
    return pl.pallas_call(
        functools.partial(_attn_kernel, heads=heads),
        grid=(b, hp, l // tq),
        in_specs=[
            pl.BlockSpec((None, tq, kw), lambda i, h, j: (i, j, h)),
            pl.BlockSpec((None, l, kw), lambda i, h, j: (i, 0, h)),
            pl.BlockSpec((None, lc, kw), lambda i, h, j: (i, 0, h)),
            pl.BlockSpec((None, l, vw), lambda i, h, j: (i, 0, h)),
            pl.BlockSpec((None, lc, vw), lambda i, h, j: (i, 0, h)),
        ],
        out_specs=pl.BlockSpec((None, tq, vw), lambda i, h, j: (i, j, h)),
        out_shape=jax.ShapeDtypeStruct((b, l, N_HEADS * V_HEAD), BF16),
        compiler_params=_cparams(("arbitrary", "arbitrary", "arbitrary")),
        name="mla_attention",
    )(q, k_l, k_c, v_l, v_c)


def _mix_cd_kernel(yf_ref, yb_ref, o_ref, x_ref, g1_ref, sh2_ref, sc2_ref,
                   wglu_ref, wout_ref, wr_ref, br_ref, xl_ref, xm_ref, rt_ref):
    tm = x_ref.shape[0]
    n_slab, rows, _ = yf_ref.shape
    nb = rows // tm
    bi = pl.program_id(1)
    dc = n_slab * LANE
    pick = lambda ref, s: ref[s, pl.ds(bi, tm, stride=nb), :]
    y = jnp.concatenate([pick(yf_ref, s) + pick(yb_ref, s) for s in range(n_slab)], axis=1)
    g = jax.nn.gelu(y)
    s5 = (g * _sigmoid(_dot(g.astype(BF16), wglu_ref[...]))).astype(BF16)
    y_mix = _dot(s5, wout_ref[0:dc, :]) + _dot(o_ref[...], wout_ref[dc:, :])
    _residual_and_route(x_ref[...], y_mix, g1_ref[...], sh2_ref[...], sc2_ref[...], wr_ref, br_ref,
                        xl_ref, xm_ref, rt_ref)


def _mix_cd(yf, yb, o, x, g1, sh2, sc2, wglu, wout, wr, br, tm):
    b, l, d = x.shape
    n_slab = yf.shape[0]
    tok = lambda j, i: (i, j, 0)
    per_b = lambda j, i: (i, 0, 0)
    const2 = lambda j, i: (0, 0)
    tm_blk = pl.BlockSpec((n_slab, tm * b, LANE), lambda j, i: (0, j, 0))
    return pl.pallas_call(
        _mix_cd_kernel,
        grid=(l // tm, b),
        in_specs=[
            tm_blk,
            tm_blk,
            pl.BlockSpec((None, tm, o.shape[-1]), tok),
            pl.BlockSpec((None, tm, d), tok),
            pl.BlockSpec((None, 1, d), per_b),
            pl.BlockSpec((None, 1, d), per_b),
            pl.BlockSpec((None, 1, d), per_b),
            pl.BlockSpec(wglu.shape, const2),
            pl.BlockSpec(wout.shape, const2),
            pl.BlockSpec(wr.shape, const2),
            pl.BlockSpec((1, LANE), const2),
        ],
        out_specs=[
            pl.BlockSpec((None, tm, d), tok),
            pl.BlockSpec((tm * TOKEN_ROWS, LANE), lambda j, i: (i * (l // tm) + j, 0)),
            pl.BlockSpec((None, tm, LANE), tok),
        ],
        out_shape=[
            jax.ShapeDtypeStruct((b, l, d), F32),
            jax.ShapeDtypeStruct((b * l * TOKEN_ROWS, LANE), F32),
            jax.ShapeDtypeStruct((b, l, LANE), F32),
        ],
        compiler_params=_cparams(("arbitrary", "arbitrary")),
        name="mix_cd",
    )(yf, yb, o, x, g1, sh2, sc2, wglu, wout, wr, br)


def _router_weights(w_grp, b_grp, w_exp, b_exp):
    d = w_grp.shape[0]
    pad = LANE - N_GROUPS - N_EXPERTS
    wr = jnp.concatenate([w_grp, w_exp, jnp.zeros((d, pad), F32)], axis=1)
    br = jnp.concatenate([b_grp, b_exp, jnp.zeros((pad,), F32)]).reshape(1, LANE)
    return wr, br


def _pad_heads(w, width):
    k = w.shape[0]
    w = w.reshape(k, N_HEADS, width)
    return jnp.pad(w, ((0, 0), (0, 0), (0, HEAD_PAD - width))).reshape(k, N_HEADS * HEAD_PAD)


def _rope_tables(l):
    rows = l // GRID_W
    row = jnp.repeat(jnp.arange(rows, dtype=F32), GRID_W)
    col = jnp.tile(jnp.arange(GRID_W, dtype=F32), rows)
    inv_freq = ROPE_BASE ** (-jnp.arange(AXIS_PAIRS, dtype=F32) / AXIS_PAIRS)
    ar, ac = row[:, None] * inv_freq, col[:, None] * inv_freq
    one = jnp.ones((l, QK_NOPE), F32)
    zn = jnp.zeros((l, QK_NOPE), F32)
    zp = jnp.zeros((l, HEAD_PAD - QK_HEAD), F32)
    cos = jnp.concatenate([one, jnp.cos(ar), jnp.cos(ar), jnp.cos(ac), jnp.cos(ac), zp], axis=1)
    sin = jnp.concatenate([zn, jnp.sin(ar), jnp.sin(ar), jnp.sin(ac), jnp.sin(ac), zp], axis=1)
    return cos, sin


def _rot_swap(t, signed):
    a = AXIS_PAIRS
    s = -1.0 if signed else 1.0
    return jnp.concatenate([s * t[..., a:2 * a], t[..., 0:a], s * t[..., 3 * a:4 * a], t[..., 2 * a:3 * a]], axis=-1)


def _head_lanes(rot):
    pad = [(0, 0)] * (rot.ndim - 1) + [(QK_NOPE, HEAD_PAD - QK_HEAD)]
    return jnp.pad(rot, pad)


def _split_mod(mod_row, b, d):
    parts = jnp.split(mod_row, 6, axis=-1)
    return [jnp.broadcast_to(p.reshape(-1, 1, d), (b, 1, d)) for p in parts]


def kernel(x, c, ctx, c_ctx, ada_w, ada_b, ab_w_in, sgu_norm, sgu_w, sgu_b, conv_w, ab_w_out, cd_w_in, s5_a_re, s5_a_im, s5_log_dt, s5_b_re, s5_b_im, s5_c_re, s5_c_im, s5_d, s5_w_glu, mla_q_norm, mla_kv_norm, mla_w_uq, mla_w_uk, mla_w_uv, mla_qn_gain, mla_kn_gain, cd_w_out, moe_w_grp, moe_b_grp, moe_w_exp, moe_b_exp, moe_w_gate, moe_w_up, moe_w_down):
    b, l, d = x.shape
    lc = ctx.shape[1]
    depth = ada_w.shape[0]
    assert depth == 2, "layer 0 = gated-MLP/conv mixers, layer 1 = S5/attention mixers"
    assert d == TOKEN_ROWS * LANE, "token-tile layout: one vreg tile per token"
    tm = min(TOKEN_TILE, l)
    tmc = min(TOKEN_TILE, lc)

    mod_rows = 2 * SUBLANE
    cvec = jnp.zeros((mod_rows, d), F32).at[:b].set(c).at[b].set(c_ctx)
    mod = _ada(cvec, ada_w, ada_b)

    sh1, sc1, g1, sh2, sc2, g2 = _split_mod(mod[0, :b], b, d)
    csh1, csc1, cg1, csh2, csc2, cg2 = _split_mod(mod[0, b:b + 1], b, d)
    w_in = ab_w_in[0].astype(BF16)
    wr, br = _router_weights(moe_w_grp[0], moe_b_grp[0], moe_w_exp[0], moe_b_exp[0])
    sgub = jnp.repeat(sgu_b[0].T, d // G_A, axis=1)
    mix_args = (sgu_norm[0].reshape(1, d), sgu_w[0].astype(BF16), sgub, conv_w[0],
                ab_w_out[0].astype(BF16), wr, br)
    z, p = _inproj_ab(x, sh1, sc1, w_in, tm)
    xl, xm, rt = _mix_ab(z, p, x, g1, sh2, sc2, *mix_args, tm)
    zc, pc = _inproj_ab(ctx, csh1, csc1, w_in, tmc)
    xc, xmc, rtc = _mix_ab(zc, pc, ctx, cg1, csh2, csc2, *mix_args, tmc)
    xl, xc = _hier_moe([(xl, xm, rt, g2), (xc, xmc, rtc, cg2)], 0, moe_w_gate, moe_w_up, moe_w_down)

    sh1, sc1, g1, sh2, sc2, g2 = _split_mod(mod[1, :b], b, d)
    csh1, csc1, _, _, _, _ = _split_mod(mod[1, b:b + 1], b, d)
    d_c = s5_d.shape[-1]
    q_lora = mla_q_norm.shape[-1]
    kv_lora = mla_kv_norm.shape[-1]
    w_cd = cd_w_in[0]
    o_kr = d_c + q_lora + kv_lora
    w_kr = _head_lanes(w_cd[:, o_kr:])
    w_krs = _head_lanes(_rot_swap(w_cd[:, o_kr:], True))
    w_cd = jnp.concatenate([w_cd[:, :o_kr], w_kr, w_krs], axis=1).astype(BF16)
    widths = (d_c, q_lora, kv_lora, HEAD_PAD, HEAD_PAD)
    u_l, cq_l, ckv_l, krp_l, krs_l = _inproj_cd(xl, sh1, sc1, w_cd, widths, tm)
    u_c, _, ckv_c, krp_c, _ = _inproj_cd(xc, csh1, csc1, w_cd, widths, tmc)

    ab_re, ab_im, bb_re, bb_im = _s5_params(s5_a_re[0], s5_a_im[0], s5_log_dt[0], s5_b_re[0], s5_b_im[0])
    a_bc, bsb, csb = _s5_block_matrices(ab_re, ab_im, bb_re, bb_im, s5_c_re[0], s5_c_im[0], b)
    yf, yb = _s5_scan(u_c, u_l, a_bc, bsb, csb, s5_d[0].reshape(1, d_c), b)

    wq = _pad_heads(mla_w_uq[0], QK_HEAD).astype(BF16)
    wk = _pad_heads(mla_w_uk[0], QK_NOPE).astype(BF16)
    wv = mla_w_uv[0].astype(BF16)
    w_uq = mla_w_uq[0].reshape(q_lora, N_HEADS, QK_HEAD)
    wqs = _head_lanes(_rot_swap(w_uq[..., QK_NOPE:], True)).reshape(q_lora, N_HEADS * HEAD_PAD).astype(BF16)
    qg, kg = mla_qn_gain[0], mla_kn_gain[0]
    full = lambda g: jnp.pad(g, (0, HEAD_PAD - QK_HEAD)).reshape(1, HEAD_PAD)
    partner = lambda g: _head_lanes(_rot_swap(g[QK_NOPE:], False)).reshape(1, HEAD_PAD)
    cos, sin = _rope_tables(l)
    tables = (cos * full(qg), sin * partner(qg), cos * full(kg), sin * partner(kg))
    qn = mla_q_norm[0].reshape(1, q_lora)
    kvn = mla_kv_norm[0].reshape(1, kv_lora)
    q_l, k_l, v_l = _mla_prep(_mla_prep_lat_kernel, "mla_prep_lat", (cq_l, ckv_l, krp_l, krs_l),
                              (qn, kvn, wq, wqs, wk, wv), tables, 1, tm)
    k_c, v_c = _mla_prep(_mla_prep_ctx_kernel, "mla_prep_ctx", (ckv_c, krp_c), (kvn, wk, wv, full(kg)), (), 0, tmc)
    o_l = _attention(q_l, k_l, k_c, v_l, v_c, ATTN_Q_TILE, ATTN_HEADS)

    wr, br = _router_weights(moe_w_grp[1], moe_b_grp[1], moe_w_exp[1], moe_b_exp[1])
    xl, xm, rt = _mix_cd(yf, yb, o_l, xl, g1, sh2, sc2,
                         s5_w_glu[0].astype(BF16), cd_w_out[0].astype(BF16), wr, br, MIX_CD_TILE)
    (xl,) = _hier_moe([(xl, xm, rt, g2)], 1, moe_w_gate, moe_w_up, moe_w_down)
    return xl
```

```python
import functools
import math

import jax
import jax.numpy as jnp
from jax import lax
from jax.experimental import pallas as pl
from jax.experimental.pallas import tpu as pltpu

F32 = jnp.float32
BF16 = jnp.bfloat16
I32 = jnp.int32
HIGHEST = lax.Precision.HIGHEST

EPS = 1e-6
LOG2_E = math.log2(math.e)
GRID_W = 64
CHUNK = 128
G_A = 8
CONV_W = 3
S5_GROUP = 16
S5_STATE = 64
N_HEADS = 8
QK_NOPE = 64
QK_ROPE = 32
QK_HEAD = QK_NOPE + QK_ROPE
V_HEAD = 64
AXIS_PAIRS = QK_ROPE // 4
ROPE_BASE = 10000.0
N_GROUPS = 4
EXPERTS_PER_GROUP = 8
N_EXPERTS = N_GROUPS * EXPERTS_PER_GROUP
TOP_K = 2

LANE = 128
SUBLANE = 8
HEAD_PAD = LANE
TOKEN_TILE = 512
MIX_SUBTILES = 2
MIX_CD_TILE = 256
MOE_TILE = 512
COMBINE_TILE = 256
PLAN_BLOCK = 1024
MAX_TILES = 256
TOKEN_ROWS = SUBLANE
ATTN_Q_TILE = 512
ATTN_HEADS = 8
S5_STEPS = 64
S5_SUPER = LANE // S5_GROUP
VMEM_LIMIT = 56 * 1024 * 1024


def _cparams(sem):
    return pltpu.CompilerParams(dimension_semantics=sem, vmem_limit_bytes=VMEM_LIMIT)


def _rms(x):
    return x * lax.rsqrt(jnp.mean(x * x, axis=-1, keepdims=True) + EPS)


def _sigmoid(x):
    return 1.0 / (1.0 + jnp.exp(-x))


def _dot(a, b):
    return jnp.dot(a, b, preferred_element_type=F32)


def _ada_kernel(c_ref, w_ref, b_ref, o_ref):
    c = c_ref[...]
    s = c * _sigmoid(c)
    o_ref[...] = jnp.dot(s, w_ref[...], precision=HIGHEST, preferred_element_type=F32) + b_ref[...]


def _ada(cvec, ada_w, ada_b):
    depth, d, n = ada_w.shape
    rows = cvec.shape[0]
    tn = 1024
    return pl.pallas_call(
        _ada_kernel,
        grid=(depth, n // tn),
        in_specs=[
            pl.BlockSpec((rows, d), lambda i, j: (0, 0)),
            pl.BlockSpec((None, d, tn), lambda i, j: (i, 0, j)),
            pl.BlockSpec((None, 1, tn), lambda i, j: (i, 0, j)),
        ],
        out_specs=pl.BlockSpec((None, rows, tn), lambda i, j: (i, 0, j)),
        out_shape=jax.ShapeDtypeStruct((depth, rows, n), F32),
        compiler_params=_cparams(("arbitrary", "arbitrary")),
        name="ada_mod",
    )(cvec, ada_w, ada_b.reshape(depth, 1, n))


def _route(xm, wr_ref, br_ref):
    hi = xm.astype(BF16)
    lo = (xm - hi.astype(F32)).astype(BF16)
    w = wr_ref[...]
    w_hi = w.astype(BF16)
    w_lo = (w - w_hi.astype(F32)).astype(BF16)
    logits = _dot(jnp.concatenate([hi, lo, hi], axis=1), jnp.concatenate([w_hi, w_hi, w_lo], axis=0)) + br_ref[...]
    lane = lax.broadcasted_iota(I32, logits.shape, 1).astype(F32)
    neg = jnp.float32(-1e30)
    big = jnp.float32(1e6)
    is_grp = lane < N_GROUPS
    gl = jnp.where(is_grp, logits, neg)
    gmax = jnp.max(gl, axis=-1, keepdims=True)
    gsum = jnp.sum(jnp.where(is_grp, jnp.exp(gl - gmax), 0.0), axis=-1, keepdims=True)
    p_top = 1.0 / gsum
    grp = jnp.min(jnp.where(gl == gmax, lane, big), axis=-1, keepdims=True)
    eidx = lane - N_GROUPS
    in_grp = (eidx >= grp * EXPERTS_PER_GROUP) & (eidx < (grp + 1.0) * EXPERTS_PER_GROUP)
    el = jnp.where(in_grp, logits, neg)
    m1 = jnp.max(el, axis=-1, keepdims=True)
    i1 = jnp.min(jnp.where(el == m1, lane, big), axis=-1, keepdims=True)
    el2 = jnp.where(lane == i1, neg, el)
    m2 = jnp.max(el2, axis=-1, keepdims=True)
    i2 = jnp.min(jnp.where(el2 == m2, lane, big), axis=-1, keepdims=True)
    t = jnp.exp(m2 - m1)
    w1 = p_top / (1.0 + t)
    w2 = p_top * t / (1.0 + t)
    e1 = i1 - N_GROUPS
    e2 = i2 - N_GROUPS
    return jnp.where(lane == 0, e1, jnp.where(lane == 1, e2, jnp.where(lane == 2, w1, jnp.where(lane == 3, w2, 0.0))))


def _residual_and_route(x, y, g1, sh2, sc2, wr_ref, br_ref, xl_ref, xm_ref, rt_ref):
    xl = x + g1 * y
    xl_ref[...] = xl
    xm = _rms(xl) * (1.0 + sc2) + sh2
    _store_token_tiles(xm_ref, xm)
    rt_ref[...] = _route(xm, wr_ref, br_ref)


def _inproj_ab_kernel(x_ref, sh_ref, sc_ref, w_ref, z_ref, p_ref):
    d = x_ref.shape[-1]
    h = (_rms(x_ref[...]) * (1.0 + sc_ref[...]) + sh_ref[...]).astype(BF16)
    for j in range(3):
        z_ref[:, j * d:(j + 1) * d] = _dot(h, w_ref[:, j * d:(j + 1) * d]).astype(BF16)
    gate_c = _dot(h, w_ref[:, 3 * d:4 * d])
    xb = _dot(h, w_ref[:, 4 * d:5 * d])
    p_ref[...] = gate_c * xb


def _inproj_ab(x, sh, sc, w_bf16, tm):
    b, l, d = x.shape
    n = w_bf16.shape[1]
    return pl.pallas_call(
        _inproj_ab_kernel,
        grid=(b, l // tm),
        in_specs=[
            pl.BlockSpec((None, tm, d), lambda i, j: (i, j, 0)),
            pl.BlockSpec((None, 1, d), lambda i, j: (i, 0, 0)),
            pl.BlockSpec((None, 1, d), lambda i, j: (i, 0, 0)),
            pl.BlockSpec((d, n), lambda i, j: (0, 0)),
        ],
        out_specs=[
            pl.BlockSpec((None, tm, 3 * d), lambda i, j: (i, j, 0)),
            pl.BlockSpec((None, tm, d), lambda i, j: (i, j, 0)),
        ],
        out_shape=[
            jax.ShapeDtypeStruct((b, l, 3 * d), BF16),
            jax.ShapeDtypeStruct((b, l, d), F32),
        ],
        compiler_params=_cparams(("arbitrary", "arbitrary")),
        name="inproj_ab",
    )(x, sh, sc, w_bf16)


def _mix_ab_kernel(z_ref, p_ref, pprev_ref, pnext_ref, x_ref, g1_ref, sh2_ref, sc2_ref,
                   sgun_ref, sguw_ref, sgub_ref, convw_ref, wout_ref, wr_ref, br_ref,
                   xl_ref, xm_ref, rt_ref, pbuf, ycat):
    i = pl.program_id(1)
    nt = pl.num_programs(1)
    tm, d = x_ref.shape
    cg = d // G_A
    pbuf[SUBLANE:SUBLANE + tm, :] = p_ref[...]
    pbuf[0:SUBLANE, :] = jnp.where(i > 0, pprev_ref[...], 0.0)
    pbuf[SUBLANE + tm:2 * SUBLANE + tm, :] = jnp.where(i < nt - 1, pnext_ref[...], 0.0)
    ts = tm // (MIX_SUBTILES if tm % (MIX_SUBTILES * CHUNK) == 0 else 1)
    for t0 in range(0, tm, ts):
        rows = slice(t0, t0 + ts)
        v = jax.nn.gelu(z_ref[rows, d:2 * d].astype(F32))
        vc = (_rms(v) * sgun_ref[...]).astype(BF16)
        for c in range(ts // CHUNK):
            lr = c * CHUNK
            r0 = t0 + lr
            cols = [_dot(sguw_ref[g], vc[lr:lr + CHUNK, g * cg:(g + 1) * cg]) for g in range(G_A)]
            s = jnp.concatenate(cols, axis=1) + sgub_ref[...]
            u = jax.nn.gelu(z_ref[r0:r0 + CHUNK, 0:d].astype(F32))
            ycat[r0:r0 + CHUNK, 0:d] = (u * s).astype(BF16)
        conv = (convw_ref[0:1, :] * pbuf[SUBLANE - 1 + t0:SUBLANE - 1 + t0 + ts, :]
                + convw_ref[1:2, :] * pbuf[SUBLANE + t0:SUBLANE + t0 + ts, :]
                + convw_ref[2:3, :] * pbuf[SUBLANE + 1 + t0:SUBLANE + 1 + t0 + ts, :])
        ycat[rows, d:2 * d] = (z_ref[rows, 2 * d:3 * d].astype(F32) * conv).astype(BF16)
        y = _dot(ycat[rows, :], wout_ref[...])
        _residual_and_route(x_ref[rows, :], y, g1_ref[...], sh2_ref[...], sc2_ref[...], wr_ref, br_ref,
                            xl_ref.at[rows, :], xm_ref.at[pl.ds(t0 * TOKEN_ROWS, ts * TOKEN_ROWS), :],
                            rt_ref.at[rows, :])


def _mix_ab(z, p, x, g1, sh2, sc2, sgun, sguw, sgub, convw, wout, wr, br, tm):
    b, l, d = x.shape
    hb = tm // SUBLANE
    nhb = l // SUBLANE
    tok = lambda i, j: (i, j, 0)
    per_b = lambda i, j: (i, 0, 0)
    const2 = lambda i, j: (0, 0)
    return pl.pallas_call(
        _mix_ab_kernel,
        grid=(b, l // tm),
        in_specs=[
            pl.BlockSpec((None, tm, 3 * d), tok),
            pl.BlockSpec((None, tm, d), tok),
            pl.BlockSpec((None, SUBLANE, d), lambda i, j: (i, jnp.maximum(j * hb - 1, 0), 0)),
            pl.BlockSpec((None, SUBLANE, d), lambda i, j: (i, jnp.minimum((j + 1) * hb, nhb - 1), 0)),
            pl.BlockSpec((None, tm, d), tok),
            pl.BlockSpec((None, 1, d), per_b),
            pl.BlockSpec((None, 1, d), per_b),
            pl.BlockSpec((None, 1, d), per_b),
            pl.BlockSpec((1, d), const2),
            pl.BlockSpec((G_A, CHUNK, CHUNK), lambda i, j: (0, 0, 0)),
            pl.BlockSpec((CHUNK, d), const2),
            pl.BlockSpec((CONV_W, d), const2),
            pl.BlockSpec((2 * d, d), const2),
            pl.BlockSpec(wr.shape, const2),
            pl.BlockSpec((1, LANE), const2),
        ],
        out_specs=[
            pl.BlockSpec((None, tm, d), tok),
            pl.BlockSpec((tm * TOKEN_ROWS, LANE), lambda i, j: (i * (l // tm) + j, 0)),
            pl.BlockSpec((None, tm, LANE), tok),
        ],
        out_shape=[
            jax.ShapeDtypeStruct((b, l, d), F32),
            jax.ShapeDtypeStruct((b * l * TOKEN_ROWS, LANE), F32),
            jax.ShapeDtypeStruct((b, l, LANE), F32),
        ],
        scratch_shapes=[
            pltpu.VMEM((tm + 2 * SUBLANE, d), F32),
            pltpu.VMEM((tm, 2 * d), BF16),
        ],
        compiler_params=_cparams(("arbitrary", "arbitrary")),
        name="mix_ab",
    )(z, p, p, p, x, g1, sh2, sc2, sgun, sguw, sgub, convw, wout, wr, br)


def _store_token_tiles(ref, val):
    tm, d = val.shape
    rpt = d // LANE
    for j in range(rpt):
        ref[pl.ds(j, tm, stride=rpt), :] = val[:, j * LANE:(j + 1) * LANE]


def _load_token_tile_col(ref, tm, rpt, j):
    return ref[pl.ds(j, tm, stride=rpt), :]


def _plan_kernel(rt_ref, tri_ref, pos_ref, te_ref, tv_ref, pad_ref, run, *, tm):
    ps = pl.program_id(0)
    i = pl.program_id(1)
    blk = rt_ref.shape[0]
    ne = N_EXPERTS

    @pl.when(jnp.logical_and(ps == 0, i == 0))
    def _():
        run[...] = jnp.zeros_like(run)

    @pl.when(jnp.logical_and(ps == 1, i == 0))
    def _():
        counts = run[...]
        padded = jnp.floor((counts + (tm - 1)) * (1.0 / tm)) * tm
        r = lax.broadcasted_iota(I32, (ne, ne), 0)
        c = lax.broadcasted_iota(I32, (ne, ne), 1)
        starts = jnp.dot((c < r).astype(F32), padded, precision=HIGHEST, preferred_element_type=F32)
        nt = te_ref.shape[1]
        ends = jnp.concatenate([(starts + padded) * (1.0 / tm)] * (nt // LANE), axis=1)
        t = lax.broadcasted_iota(I32, (ne, nt), 1).astype(F32)
        te = jnp.sum((t >= ends).astype(F32), axis=0, keepdims=True)
        total = jnp.max(ends, axis=0, keepdims=True)
        te_ref[...] = jnp.minimum(te, ne - 1.0).astype(I32)
        tv_ref[...] = (t[0:1, :] < total).astype(I32)
        lane = lax.broadcasted_iota(I32, (ne, LANE), 1)
        pad_ref[...] = jnp.where(lane == 0, starts + counts,
                                 jnp.where(lane == 1, padded - counts, 0.0)).astype(I32)
        run[...] = starts

    slab_t = rt_ref[...].T
    ex = lax.broadcasted_iota(I32, (ne, blk), 0).astype(F32)
    oh1 = ex == slab_t[0:1, :]
    oh2 = ex == slab_t[1:2, :]
    oh = jnp.where(jnp.logical_or(oh1, oh2), 1.0, 0.0)
    prefix = _dot(oh.astype(BF16), tri_ref[...]) + run[:, 0:1]
    pos1 = jnp.sum(jnp.where(oh1, prefix, 0.0), axis=0, keepdims=True)
    pos2 = jnp.sum(jnp.where(oh2, prefix, 0.0), axis=0, keepdims=True)
    pos_ref[...] = jnp.concatenate([pos1, pos2], axis=0).astype(I32)
    run[...] = run[...] + jnp.sum(oh, axis=1, keepdims=True)


def _moe_plan(rt_all, tm):
    n = rt_all.shape[0]
    blk = PLAN_BLOCK
    nb = n // blk
    tri = jnp.triu(jnp.ones((blk, blk), BF16), k=1)
    return pl.pallas_call(
        functools.partial(_plan_kernel, tm=tm),
        grid=(2, nb),
        in_specs=[
            pl.BlockSpec((blk, LANE), lambda p, i: (i, 0)),
            pl.BlockSpec((blk, blk), lambda p, i: (0, 0)),
        ],
        out_specs=[
            pl.BlockSpec((None, TOP_K, blk), lambda p, i: (jnp.where(p == 0, nb, i), 0, 0)),
            pl.BlockSpec((1, MAX_TILES), lambda p, i: (0, 0)),
            pl.BlockSpec((1, MAX_TILES), lambda p, i: (0, 0)),
            pl.BlockSpec((N_EXPERTS, LANE), lambda p, i: (0, 0)),
        ],
        out_shape=[
            jax.ShapeDtypeStruct((nb + 1, TOP_K, blk), I32),
            jax.ShapeDtypeStruct((1, MAX_TILES), I32),
            jax.ShapeDtypeStruct((1, MAX_TILES), I32),
            jax.ShapeDtypeStruct((N_EXPERTS, LANE), I32),
        ],
        scratch_shapes=[pltpu.VMEM((N_EXPERTS, LANE), F32)],
        compiler_params=_cparams(("arbitrary", "arbitrary")),
        name="moe_plan",
    )(rt_all, tri)


def _dispatch_kernel(ps_ref, pl_ref, tv_ref, pos_ref, *rest, blocks, n_tiles):
    xm_refs = rest[:len(blocks)]
    xs_hbm, zbuf, sem = rest[len(blocks):]
    i = pl.program_id(0)
    tt = pos_ref.shape[1]
    rpt = xm_refs[0].shape[0] // tt
    tile_rows = zbuf.shape[0]

    @pl.when(i == 0)
    def _():
        zbuf[...] = jnp.zeros_like(zbuf)

        def gap_copies(e, wait):
            base = ps_ref[e]
            n = pl_ref[e]
            bit = tile_rows // (2 * rpt)
            while bit >= 1:
                lower = n & (bit - 1)

                @pl.when((n & bit) != 0)
                def _(lower=lower, bit=bit):
                    dst = xs_hbm.at[pl.ds((base + lower) * rpt, bit * rpt)]
                    cp = pltpu.make_async_copy(zbuf.at[pl.ds(0, bit * rpt)], dst, sem.at[1])
                    cp.wait() if wait else cp.start()
                bit //= 2

        def tile_copy(t, wait):
            @pl.when(tv_ref[t] != 1)
            def _():
                dst = xs_hbm.at[pl.ds(pl.multiple_of(t * tile_rows, tile_rows), tile_rows)]
                cp = pltpu.make_async_copy(zbuf, dst, sem.at[1])
                cp.wait() if wait else cp.start()

        for wait in (False, True):
            lax.fori_loop(0, N_EXPERTS, lambda e, c, w=wait: (gap_copies(e, w), c)[1], 0)
            lax.fori_loop(0, n_tiles, lambda t, c, w=wait: (tile_copy(t, w), c)[1], 0)

    def scatter(xm_ref):
        def issue(r, carry):
            src = xm_ref.at[pl.ds(pl.multiple_of(r * rpt, rpt), rpt)]
            for k in range(TOP_K):
                dst = xs_hbm.at[pl.ds(pl.multiple_of(pos_ref[k, r] * rpt, rpt), rpt)]
                pltpu.make_async_copy(src, dst, sem.at[0]).start(priority=k)
            return carry
        lax.fori_loop(0, tt, issue, 0, unroll=8)
        for k in range(TOP_K):
            pltpu.make_async_copy(xm_ref, xs_hbm.at[pl.ds(0, tt * rpt)], sem.at[0]).wait()

    lo = 0
    for xm_ref, nblk in zip(xm_refs, blocks):
        pl.when(jnp.logical_and(i >= lo, i < lo + nblk))(functools.partial(scatter, xm_ref))
        lo += nblk


def _dispatch(pad_start, pad_len, valid, pos, xm_streams, n_tiles, tm):
    tt = pos.shape[-1]
    rpt = TOKEN_ROWS
    blocks = tuple(xm.shape[0] // (rpt * tt) for xm in xm_streams)
    in_specs = [pl.BlockSpec((None, TOP_K, tt), lambda i, a, b, c: (i, 0, 0), memory_space=pltpu.SMEM)]
    lo = 0
    for nblk in blocks:
        in_specs.append(pl.BlockSpec(
            (tt * rpt, LANE), lambda i, a, b, c, lo=lo, nblk=nblk: (jnp.clip(i - lo, 0, nblk - 1), 0)))
        lo += nblk
    return pl.pallas_call(
        functools.partial(_dispatch_kernel, blocks=blocks, n_tiles=n_tiles),
        grid_spec=pltpu.PrefetchScalarGridSpec(
            num_scalar_prefetch=3,
            grid=(sum(blocks),),
            in_specs=in_specs,
            out_specs=pl.BlockSpec(memory_space=pl.ANY),
            scratch_shapes=[pltpu.VMEM((tm * rpt, LANE), F32), pltpu.SemaphoreType.DMA((2,))],
        ),
        out_shape=jax.ShapeDtypeStruct((n_tiles * tm * rpt, LANE), F32),
        compiler_params=_cparams(("arbitrary",)),
        name="moe_dispatch",
    )(pad_start, pad_len, valid, pos, *xm_streams)


def _moe_kernel(te_ref, tv_ref, x_ref, wg_hbm, wu_hbm, wd_hbm, y_ref,
                wg_f, wu_f, wd_f, wg_s, wu_s, wd_s, slot_ref, sem, *, layer, n_tiles):
    t = pl.program_id(0)
    rpt = TOKEN_ROWS
    tm = x_ref.shape[0] // rpt

    def weight_copies(e, slot):
        return [pltpu.make_async_copy(src.at[layer, e], dst.at[slot], sem.at[slot, k])
                for k, (src, dst) in enumerate(((wg_hbm, wg_f), (wu_hbm, wu_f), (wd_hbm, wd_f)))]

    @pl.when(tv_ref[t] == 1)
    def _():
        e = te_ref[t]
        first = jnp.logical_or(t == 0, te_ref[jnp.maximum(t - 1, 0)] != e)

        @pl.when(first)
        def _():
            @pl.when(t == 0)
            def _():
                slot_ref[0] = 0
                for cp in weight_copies(e, 0):
                    cp.start()

            slot = slot_ref[0]
            for cp in weight_copies(e, slot):
                cp.wait()
            at = lambda ref, i: ref[jnp.minimum(i, n_tiles - 1)]
            same = lambda i: (i < n_tiles) & (at(tv_ref, i) == 1) & (at(te_ref, i) == e)
            t2 = lax.while_loop(same, lambda i: i + 1, t + 1)

            @pl.when((t2 < n_tiles) & (at(tv_ref, t2) == 1))
            def _():
                for cp in weight_copies(at(te_ref, t2), 1 - slot):
                    cp.start()

            wg_s[...] = wg_f[slot].astype(BF16)
            wu_s[...] = wu_f[slot].astype(BF16)
            wd_s[...] = wd_f[slot].astype(BF16)
            slot_ref[0] = 1 - slot

        x = jnp.concatenate([_load_token_tile_col(x_ref, tm, rpt, j) for j in range(rpt)], axis=1).astype(BF16)
        g = _dot(x, wg_s[...])
        u = _dot(x, wu_s[...])
        h = (g * _sigmoid(g) * u).astype(BF16)
        _store_token_tiles(y_ref, _dot(h, wd_s[...]))

    @pl.when(tv_ref[t] != 1)
    def _():
        y_ref[...] = jnp.zeros_like(y_ref)


def _moe_experts(xs, tile_e, valid, n_tiles, layer, w_gate, w_up, w_down, tm):
    _, _, d, de = w_gate.shape
    rpt = TOKEN_ROWS
    grid_spec = pltpu.PrefetchScalarGridSpec(
        num_scalar_prefetch=2,
        grid=(n_tiles,),
        in_specs=[
            pl.BlockSpec((tm * rpt, LANE), lambda t, te, tv: (t * tv[t], 0)),
            pl.BlockSpec(memory_space=pl.ANY),
            pl.BlockSpec(memory_space=pl.ANY),
            pl.BlockSpec(memory_space=pl.ANY),
        ],
        out_specs=pl.BlockSpec((tm * rpt, LANE), lambda t, te, tv: (t, 0)),
        scratch_shapes=[
            pltpu.VMEM((2, d, de), F32),
            pltpu.VMEM((2, d, de), F32),
            pltpu.VMEM((2, de, d), F32),
            pltpu.VMEM((d, de), BF16),
            pltpu.VMEM((d, de), BF16),
            pltpu.VMEM((de, d), BF16),
            pltpu.SMEM((1,), I32),
            pltpu.SemaphoreType.DMA((2, 3)),
        ],
    )
    return pl.pallas_call(
        functools.partial(_moe_kernel, layer=layer, n_tiles=n_tiles),
        grid_spec=grid_spec,
        out_shape=jax.ShapeDtypeStruct((n_tiles * tm * rpt, LANE), F32),
        compiler_params=_cparams(("arbitrary",)),
        name="moe_experts",
    )(tile_e, valid, xs, w_gate, w_up, w_down)


def _combine_kernel(pos0_ref, posn_ref, x_ref, g_ref, rt_ref, ys_hbm, o_ref, buf, sem, *, n):
    i = pl.program_id(0)
    tt, d = x_ref.shape
    rpt = TOKEN_ROWS

    def gather(pos_ref, slot):
        def issue(r, carry):
            for k in range(TOP_K):
                src = ys_hbm.at[pl.ds(pl.multiple_of(pos_ref[k, r] * rpt, rpt), rpt)]
                dst = buf.at[slot, k, pl.ds(pl.multiple_of(r * rpt, rpt), rpt)]
                pltpu.make_async_copy(src, dst, sem.at[slot, k]).start(priority=k)
            return carry
        lax.fori_loop(0, tt, issue, 0, unroll=8)

    slot = lax.rem(i, 2)

    @pl.when(i == 0)
    def _():
        gather(pos0_ref, 0)

    @pl.when(i + 1 < n)
    def _():
        gather(posn_ref, 1 - slot)

    for k in range(TOP_K):
        pltpu.make_async_copy(ys_hbm.at[pl.ds(0, tt * rpt)], buf.at[slot, k], sem.at[slot, k]).wait()
    w1 = rt_ref[:, 2:3]
    w2 = rt_ref[:, 3:4]
    for j in range(rpt):
        r1 = buf[slot, 0, pl.ds(j, tt, stride=rpt), :]
        r2 = buf[slot, 1, pl.ds(j, tt, stride=rpt), :]
        sl = slice(j * LANE, (j + 1) * LANE)
        o_ref[:, sl] = x_ref[:, sl] + g_ref[:, sl] * (w1 * r1 + w2 * r2)


def _moe_combine(x, g2, rt, pos, tok_off, ys, tt):
    b, l, d = x.shape
    rpt = TOKEN_ROWS
    n = b * l
    per_b = l // tt
    per_blk = PLAN_BLOCK // tt
    blk0 = tok_off // PLAN_BLOCK

    def pos_map(step):
        return lambda i: (blk0 + step(i) // per_blk, 0, step(i) % per_blk)

    return pl.pallas_call(
        functools.partial(_combine_kernel, n=n // tt),
        grid=(n // tt,),
        in_specs=[
            pl.BlockSpec((None, TOP_K, tt), pos_map(lambda i: i), memory_space=pltpu.SMEM),
            pl.BlockSpec((None, TOP_K, tt), pos_map(lambda i: jnp.minimum(i + 1, n // tt - 1)),
                         memory_space=pltpu.SMEM),
            pl.BlockSpec((tt, d), lambda i: (i, 0)),
            pl.BlockSpec((None, 1, d), lambda i: (i // per_b, 0, 0)),
            pl.BlockSpec((tt, LANE), lambda i: (i, 0)),
            pl.BlockSpec(memory_space=pl.ANY),
        ],
        out_specs=pl.BlockSpec((tt, d), lambda i: (i, 0)),
        out_shape=jax.ShapeDtypeStruct((n, d), F32),
        scratch_shapes=[
            pltpu.VMEM((2, TOP_K, tt * rpt, LANE), F32),
            pltpu.SemaphoreType.DMA((2, TOP_K)),
        ],
        compiler_params=_cparams(("arbitrary",)),
        name="moe_combine",
    )(pos, pos, x.reshape(n, d), g2, rt.reshape(n, LANE), ys).reshape(b, l, d)


def _hier_moe(streams, layer, w_gate, w_up, w_down):
    sizes = [s[0].shape[0] * s[0].shape[1] for s in streams]
    n = sum(sizes)
    tm = MOE_TILE
    n_tiles = (TOP_K * n) // tm + N_EXPERTS
    assert n_tiles <= MAX_TILES and all(sz % PLAN_BLOCK == 0 for sz in sizes)
    rt_all = jnp.concatenate([s[2].reshape(-1, LANE) for s in streams], axis=0)
    pos, tile_e, valid, pad = _moe_plan(rt_all, tm)
    tile_e, valid = tile_e.reshape(-1), valid.reshape(-1)
    xs = _dispatch(pad[:, 0], pad[:, 1], valid, pos, [s[1] for s in streams], n_tiles, tm)
    ys = _moe_experts(xs, tile_e, valid, n_tiles, layer, w_gate, w_up, w_down, tm)
    outs = []
    off = 0
    for (xl, _, rt, g2), sz in zip(streams, sizes):
        outs.append(_moe_combine(xl, g2, rt, pos, off, ys, COMBINE_TILE))
        off += sz
    return outs


def _inproj_cd_kernel(x_ref, sh_ref, sc_ref, w_ref, u_ref, *out_refs):
    h = (_rms(x_ref[...]) * (1.0 + sc_ref[...]) + sh_ref[...]).astype(BF16)
    tm = x_ref.shape[0]
    n_slab, rows, cw = u_ref.shape
    nb = rows // tm
    bi = pl.program_id(1)
    u = _dot(h, w_ref[:, 0:n_slab * cw])
    for s in range(n_slab):
        u_ref[s, pl.ds(bi, tm, stride=nb), :] = u[:, s * cw:(s + 1) * cw]
    off = n_slab * cw
    for ref in out_refs:
        n = ref.shape[-1]
        ref[...] = _dot(h, w_ref[:, off:off + n]).astype(ref.dtype)
        off += n


def _inproj_cd(x, sh, sc, w_bf16, widths, tm):
    b, l, d = x.shape
    n = w_bf16.shape[1]
    n_slab = widths[0] // LANE
    tok = lambda j, i: (i, j, 0)
    return pl.pallas_call(
        _inproj_cd_kernel,
        grid=(l // tm, b),
        in_specs=[
            pl.BlockSpec((None, tm, d), tok),
            pl.BlockSpec((None, 1, d), lambda j, i: (i, 0, 0)),
            pl.BlockSpec((None, 1, d), lambda j, i: (i, 0, 0)),
            pl.BlockSpec((d, n), lambda j, i: (0, 0)),
        ],
        out_specs=([pl.BlockSpec((n_slab, tm * b, LANE), lambda j, i: (0, j, 0))]
                   + [pl.BlockSpec((None, tm, w), tok) for w in widths[1:]]),
        out_shape=([jax.ShapeDtypeStruct((n_slab, l * b, LANE), F32)]
                   + [jax.ShapeDtypeStruct((b, l, w), F32) for w in widths[1:]]),
        compiler_params=_cparams(("arbitrary", "arbitrary")),
        name="inproj_cd",
    )(x, sh, sc, w_bf16)


def _s5_param_kernel(are_ref, aim_ref, ldt_ref, bre_ref, bim_ref, abre_ref, abim_ref, bbre_ref, bbim_ref):
    a_re = are_ref[...]
    a_im = aim_ref[...]
    dt = jnp.exp(ldt_ref[...])
    mag = jnp.exp(dt * a_re)
    ab_re = mag * jnp.cos(dt * a_im)
    ab_im = mag * jnp.sin(dt * a_im)
    den = a_re * a_re + a_im * a_im
    nr = ab_re - 1.0
    f_re = (nr * a_re + ab_im * a_im) / den
    f_im = (ab_im * a_re - nr * a_im) / den
    abre_ref[...] = ab_re
    abim_ref[...] = ab_im
    b_re = bre_ref[...]
    b_im = bim_ref[...]
    bbre_ref[...] = f_re[None] * b_re - f_im[None] * b_im
    bbim_ref[...] = f_re[None] * b_im + f_im[None] * b_re


def _s5_params(a_re, a_im, log_dt, b_re, b_im):
    nd, g, p = a_re.shape
    c = b_re.shape[-1]
    rows = nd * g * p // LANE
    flat = lambda t: t.reshape(rows, LANE)
    chan_major = lambda t: jnp.moveaxis(t, -1, 0).reshape(c, rows, LANE)
    ldt = jnp.broadcast_to(log_dt[:, :, None], (nd, g, p))
    shapes = [jax.ShapeDtypeStruct((rows, LANE), F32)] * 2 + [jax.ShapeDtypeStruct((c, rows, LANE), F32)] * 2
    ab_re, ab_im, bb_re, bb_im = pl.pallas_call(
        _s5_param_kernel, out_shape=shapes, name="s5_params",
    )(flat(a_re), flat(a_im), flat(ldt), chan_major(b_re), chan_major(b_im))
    unflat = lambda t: jnp.moveaxis(t.reshape(c, nd, g, p), 0, -1)
    return ab_re.reshape(nd, g, p), ab_im.reshape(nd, g, p), unflat(bb_re), unflat(bb_im)


def _s5_kernel(ufc_ref, ubc_ref, ufl_ref, ubl_ref, a_ref, bsb_ref, csb_ref, d_ref, yf_ref, yb_ref,
               hf, hb, cf, cb, *, n_ctx):
    i = pl.program_id(0)
    nb = cf.shape[0]
    steps = hf.shape[0] // nb
    n_super = bsb_ref.shape[1]
    cw = bsb_ref.shape[2]
    sw = bsb_ref.shape[3]
    half = sw // 2

    @pl.when(i == 0)
    def _():
        cf[...] = jnp.zeros_like(cf)
        cb[...] = jnp.zeros_like(cb)

    def drive(uf_ref, ub_ref):
        for s in range(n_super):
            hf[:, s * sw:(s + 1) * sw] = _dot(uf_ref[s].astype(BF16), bsb_ref[0, s])
            hb[:, s * sw:(s + 1) * sw] = _dot(ub_ref[s].astype(BF16), bsb_ref[1, s])

    pl.when(i < n_ctx)(functools.partial(drive, ufc_ref, ubc_ref))
    pl.when(i >= n_ctx)(functools.partial(drive, ufl_ref, ubl_ref))

    for s in range(n_super):
        re = slice(s * sw, s * sw + half)
        im = slice(s * sw + half, (s + 1) * sw)
        st = slice(s * half, (s + 1) * half)
        arf, aif = a_ref[0, 0, :, st], a_ref[0, 1, :, st]
        arb, aib = a_ref[1, 0, :, st], a_ref[1, 1, :, st]

        def step(t, carry):
            hfr, hfi, hbr, hbi = carry
            rf = pl.multiple_of(t * nb, nb)
            nfr = arf * hfr - aif * hfi + hf[pl.ds(rf, nb), re]
            nfi = arf * hfi + aif * hfr + hf[pl.ds(rf, nb), im]
            hf[pl.ds(rf, nb), re] = nfr
            hf[pl.ds(rf, nb), im] = nfi
            rb = pl.multiple_of((steps - 1 - t) * nb, nb)
            nbr = arb * hbr - aib * hbi + hb[pl.ds(rb, nb), re]
            nbi = arb * hbi + aib * hbr + hb[pl.ds(rb, nb), im]
            hb[pl.ds(rb, nb), re] = nbr
            hb[pl.ds(rb, nb), im] = nbi
            return nfr, nfi, nbr, nbi

        out = lax.fori_loop(0, steps, step, (cf[:, re], cf[:, im], cb[:, re], cb[:, im]), unroll=4)
        cf[:, re], cf[:, im], cb[:, re], cb[:, im] = out

    @pl.when(i >= n_ctx)
    def _():
        for s in range(n_super):
            skip = ufl_ref[s] * d_ref[:, s * cw:(s + 1) * cw]
            yf_ref[s] = _dot(hf[:, s * sw:(s + 1) * sw].astype(BF16), csb_ref[0, s]) + skip
            yb_ref[s] = _dot(hb[:, s * sw:(s + 1) * sw].astype(BF16), csb_ref[1, s])


def _s5_scan(u_c, u_l, a_bc, bsb, csb, d_skip, nb):
    n_slab, rows_l, _ = u_l.shape
    blk = S5_STEPS * nb
    nc = u_c.shape[1] // blk
    nl = rows_l // blk
    n_state = bsb.shape[1] * bsb.shape[3]
    clip = jnp.clip
    ctx_f = lambda i: (0, clip(i, 0, nc - 1), 0)
    ctx_b = lambda i: (0, clip(nc - 1 - i, 0, nc - 1), 0)
    lat_f = lambda i: (0, clip(i - nc, 0, nl - 1), 0)
    lat_b = lambda i: (0, clip(nl - 1 - (i - nc), 0, nl - 1), 0)
    whole = lambda a: pl.BlockSpec(a.shape, lambda i: (0,) * a.ndim)
    slab = lambda index_map: pl.BlockSpec((n_slab, blk, LANE), index_map)
    return pl.pallas_call(
        functools.partial(_s5_kernel, n_ctx=nc),
        grid=(nc + nl,),
        in_specs=[slab(ctx_f), slab(ctx_b), slab(lat_f), slab(lat_b), whole(a_bc), whole(bsb), whole(csb),
                  whole(d_skip)],
        out_specs=[slab(lat_f), slab(lat_b)],
        out_shape=[jax.ShapeDtypeStruct(u_l.shape, F32)] * 2,
        scratch_shapes=[
            pltpu.VMEM((blk, n_state), F32),
            pltpu.VMEM((blk, n_state), F32),
            pltpu.VMEM((nb, n_state), F32),
            pltpu.VMEM((nb, n_state), F32),
        ],
        compiler_params=_cparams(("arbitrary",)),
        name="s5_scan",
    )(u_c, u_c, u_l, u_l, a_bc, bsb, csb, d_skip)


def _s5_block_matrices(ab_re, ab_im, bb_re, bb_im, c_re, c_im, nb):
    nd, g, p = ab_re.shape
    c = bb_re.shape[-1]
    ns = g // S5_SUPER
    eye = jnp.eye(S5_SUPER, dtype=F32)

    def in_mat(bb):
        t = bb.reshape(nd, ns, S5_SUPER, p, c)
        return jnp.einsum('dsgpc,gh->dsgchp', t, eye).reshape(nd, ns, S5_SUPER * c, S5_SUPER * p)

    def out_mat(cc):
        t = cc.reshape(nd, ns, S5_SUPER, c, p)
        return jnp.einsum('dsgcp,gh->dsgphc', t, eye).reshape(nd, ns, S5_SUPER * p, S5_SUPER * c)

    bsb = jnp.concatenate([in_mat(bb_re), in_mat(bb_im)], axis=-1).astype(BF16)
    csb = jnp.concatenate([out_mat(c_re), out_mat(-c_im)], axis=-2).astype(BF16)
    a_bc = jnp.stack([ab_re.reshape(nd, g * p), ab_im.reshape(nd, g * p)], axis=1)
    a_bc = jnp.broadcast_to(a_bc[:, :, None, :], (nd, 2, nb, g * p))
    return a_bc, bsb, csb


def _head_scale(x):
    return lax.rsqrt(jnp.sum(x * x, axis=-1, keepdims=True) * (1.0 / QK_HEAD) + EPS)


def _mla_prep_lat_kernel(cq_ref, ckv_ref, krp_ref, krs_ref, qn_ref, kvn_ref, wq_ref, wqs_ref, wk_ref, wv_ref,
                         qc_ref, qs_ref, kc_ref, ks_ref, q_ref, k_ref, v_ref):
    ckvn = (_rms(ckv_ref[...]) * kvn_ref[...]).astype(BF16)
    v_ref[...] = _dot(ckvn, wv_ref[...]).astype(v_ref.dtype)
    kn = _dot(ckvn, wk_ref[...])
    krp = krp_ref[...]
    k_swap = krs_ref[...] * ks_ref[...]
    kc = kc_ref[...]
    for h in range(N_HEADS):
        sl = slice(h * HEAD_PAD, (h + 1) * HEAD_PAD)
        kh = kn[:, sl] + krp
        k_ref[:, sl] = (_head_scale(kh) * (kh * kc + k_swap)).astype(k_ref.dtype)
    cqn = (_rms(cq_ref[...]) * qn_ref[...]).astype(BF16)
    qn = _dot(cqn, wq_ref[...])
    q_swap = _dot(cqn, wqs_ref[...])
    qc = qc_ref[...]
    qs = qs_ref[...]
    for h in range(N_HEADS):
        sl = slice(h * HEAD_PAD, (h + 1) * HEAD_PAD)
        qh = qn[:, sl]
        scale = _head_scale(qh) * (QK_HEAD ** -0.5 * LOG2_E)
        q_ref[:, sl] = (scale * (qh * qc + q_swap[:, sl] * qs)).astype(q_ref.dtype)


def _mla_prep_ctx_kernel(ckv_ref, krp_ref, kvn_ref, wk_ref, wv_ref, kg_ref, k_ref, v_ref):
    ckvn = (_rms(ckv_ref[...]) * kvn_ref[...]).astype(BF16)
    v_ref[...] = _dot(ckvn, wv_ref[...]).astype(v_ref.dtype)
    kn = _dot(ckvn, wk_ref[...])
    krp = krp_ref[...]
    kg = kg_ref[...]
    for h in range(N_HEADS):
        sl = slice(h * HEAD_PAD, (h + 1) * HEAD_PAD)
        kh = kn[:, sl] + krp
        k_ref[:, sl] = (_head_scale(kh) * (kh * kg)).astype(k_ref.dtype)


def _mla_prep(kernel_fn, name, tok_args, const_args, table_args, n_q_out, tm):
    b, l, _ = tok_args[0].shape
    tok = lambda i, j: (i, j, 0)
    specs = [pl.BlockSpec((None, tm, a.shape[-1]), tok) for a in tok_args]
    specs += [pl.BlockSpec(a.shape, lambda i, j: (0, 0)) for a in const_args]
    specs += [pl.BlockSpec((tm, HEAD_PAD), lambda i, j: (j, 0)) for _ in table_args]
    hk = N_HEADS * HEAD_PAD
    hv = N_HEADS * V_HEAD
    widths = [hk] * n_q_out + [hk, hv]
    return pl.pallas_call(
        kernel_fn,
        grid=(b, l // tm),
        in_specs=specs,
        out_specs=[pl.BlockSpec((None, tm, w), tok) for w in widths],
        out_shape=[jax.ShapeDtypeStruct((b, l, w), BF16) for w in widths],
        compiler_params=_cparams(("arbitrary", "arbitrary")),
        name=name,
    )(*tok_args, *const_args, *table_args)


def _attn_kernel(q_ref, kl_ref, kc_ref, vl_ref, vc_ref, o_ref, *, heads):
    nt = (((1,), (1,)), ((), ()))
    for pr in range(heads // 2):
        vsl = slice(pr * 2 * V_HEAD, (pr + 1) * 2 * V_HEAD)
        outs = []
        for hh in range(2):
            h = 2 * pr + hh
            sl = slice(h * HEAD_PAD, (h + 1) * HEAD_PAD)
            q = q_ref[:, sl]
            s_l = lax.dot_general(q, kl_ref[:, sl], nt, preferred_element_type=F32)
            s_c = lax.dot_general(q, kc_ref[:, sl], nt, preferred_element_type=F32)
            m = jnp.maximum(jnp.max(s_l, axis=-1, keepdims=True), jnp.max(s_c, axis=-1, keepdims=True))
            p_l = jnp.exp2(s_l - m)
            p_c = jnp.exp2(s_c - m)
            den = jnp.sum(p_l, axis=-1, keepdims=True) + jnp.sum(p_c, axis=-1, keepdims=True)
            o = _dot(p_l.astype(BF16), vl_ref[:, vsl]) + _dot(p_c.astype(BF16), vc_ref[:, vsl])
            outs.append(o / den)
        lane = lax.broadcasted_iota(I32, outs[0].shape, 1)
        o_ref[:, vsl] = jnp.where(lane < V_HEAD, outs[0], outs[1]).astype(o_ref.dtype)


def _attention(q, k_l, k_c, v_l, v_c, tq, heads):
    b, l, _ = q.shape
    lc = k_c.shape[1]
    hp = N_HEADS // heads
    kw = heads * HEAD_PAD
    vw = heads * V_HEAD
    return pl.pallas_call(
        functools.partial(_attn_kernel, heads=heads),
        grid=(b, hp, l // tq),
        in_specs=[
            pl.BlockSpec((None, tq, kw), lambda i, h, j: (i, j, h)),
            pl.BlockSpec((None, l, kw), lambda i, h, j: (i, 0, h)),
            pl.BlockSpec((None, lc, kw), lambda i, h, j: (i, 0, h)),
            pl.BlockSpec((None, l, vw), lambda i, h, j: (i, 0, h)),
            pl.BlockSpec((None, lc, vw), lambda i, h, j: (i, 0, h)),
        ],
        out_specs=pl.BlockSpec((None, tq, vw), lambda i, h, j: (i, j, h)),
        out_shape=jax.ShapeDtypeStruct((b, l, N_HEADS * V_HEAD), BF16),
        compiler_params=_cparams(("arbitrary", "arbitrary", "arbitrary")),
        name="mla_attention",
    )(q, k_l, k_c, v_l, v_c)


def _mix_cd_kernel(yf_ref, yb_ref, o_ref, x_ref, g1_ref, sh2_ref, sc2_ref,
                   wglu_ref, wout_ref, wr_ref, br_ref, xl_ref, xm_ref, rt_ref):
    tm = x_ref.shape[0]
    n_slab, rows, _ = yf_ref.shape
    nb = rows // tm
    bi = pl.program_id(1)
    dc = n_slab * LANE
    pick = lambda ref, s: ref[s, pl.ds(bi, tm, stride=nb), :]
    y = jnp.concatenate([pick(yf_ref, s) + pick(yb_ref, s) for s in range(n_slab)], axis=1)
    g = jax.nn.gelu(y)
    s5 = (g * _sigmoid(_dot(g.astype(BF16), wglu_ref[...]))).astype(BF16)
    y_mix = _dot(s5, wout_ref[0:dc, :]) + _dot(o_ref[...], wout_ref[dc:, :])
    _residual_and_route(x_ref[...], y_mix, g1_ref[...], sh2_ref[...], sc2_ref[...], wr_ref, br_ref,
                        xl_ref, xm_ref, rt_ref)


def _mix_cd(yf, yb, o, x, g1, sh2, sc2, wglu, wout, wr, br, tm):
    b, l, d = x.shape
    n_slab = yf.shape[0]
    tok = lambda j, i: (i, j, 0)
    per_b = lambda j, i: (i, 0, 0)
    const2 = lambda j, i: (0, 0)
    tm_blk = pl.BlockSpec((n_slab, tm * b, LANE), lambda j, i: (0, j, 0))
    return pl.pallas_call(
        _mix_cd_kernel,
        grid=(l // tm, b),
        in_specs=[
            tm_blk,
            tm_blk,
            pl.BlockSpec((None, tm, o.shape[-1]), tok),
            pl.BlockSpec((None, tm, d), tok),
            pl.BlockSpec((None, 1, d), per_b),
            pl.BlockSpec((None, 1, d), per_b),
            pl.BlockSpec((None, 1, d), per_b),
            pl.BlockSpec(wglu.shape, const2),
            pl.BlockSpec(wout.shape, const2),
            pl.BlockSpec(wr.shape, const2),
            pl.BlockSpec((1, LANE), const2),
        ],
        out_specs=[
            pl.BlockSpec((None, tm, d), tok),
            pl.BlockSpec((tm * TOKEN_ROWS, LANE), lambda j, i: (i * (l // tm) + j, 0)),
            pl.BlockSpec((None, tm, LANE), tok),
        ],
        out_shape=[
            jax.ShapeDtypeStruct((b, l, d), F32),
            jax.ShapeDtypeStruct((b * l * TOKEN_ROWS, LANE), F32),
            jax.ShapeDtypeStruct((b, l, LANE), F32),
        ],
        compiler_params=_cparams(("arbitrary", "arbitrary")),
        name="mix_cd",
    )(yf, yb, o, x, g1, sh2, sc2, wglu, wout, wr, br)


def _router_weights(w_grp, b_grp, w_exp, b_exp):
    d = w_grp.shape[0]
    pad = LANE - N_GROUPS - N_EXPERTS
    wr = jnp.concatenate([w_grp, w_exp, jnp.zeros((d, pad), F32)], axis=1)
    br = jnp.concatenate([b_grp, b_exp, jnp.zeros((pad,), F32)]).reshape(1, LANE)
    return wr, br


def _pad_heads(w, width):
    k = w.shape[0]
    w = w.reshape(k, N_HEADS, width)
    return jnp.pad(w, ((0, 0), (0, 0), (0, HEAD_PAD - width))).reshape(k, N_HEADS * HEAD_PAD)


def _rope_tables(l):
    rows = l // GRID_W
    row = jnp.repeat(jnp.arange(rows, dtype=F32), GRID_W)
    col = jnp.tile(jnp.arange(GRID_W, dtype=F32), rows)
    inv_freq = ROPE_BASE ** (-jnp.arange(AXIS_PAIRS, dtype=F32) / AXIS_PAIRS)
    ar, ac = row[:, None] * inv_freq, col[:, None] * inv_freq
    one = jnp.ones((l, QK_NOPE), F32)
    zn = jnp.zeros((l, QK_NOPE), F32)
    zp = jnp.zeros((l, HEAD_PAD - QK_HEAD), F32)
    cos = jnp.concatenate([one, jnp.cos(ar), jnp.cos(ar), jnp.cos(ac), jnp.cos(ac), zp], axis=1)
    sin = jnp.concatenate([zn, jnp.sin(ar), jnp.sin(ar), jnp.sin(ac), jnp.sin(ac), zp], axis=1)
    return cos, sin


def _rot_swap(t, signed):
    a = AXIS_PAIRS
    s = -1.0 if signed else 1.0
    return jnp.concatenate([s * t[..., a:2 * a], t[..., 0:a], s * t[..., 3 * a:4 * a], t[..., 2 * a:3 * a]], axis=-1)


def _head_lanes(rot):
    pad = [(0, 0)] * (rot.ndim - 1) + [(QK_NOPE, HEAD_PAD - QK_HEAD)]
    return jnp.pad(rot, pad)


def _split_mod(mod_row, b, d):
    parts = jnp.split(mod_row, 6, axis=-1)
    return [jnp.broadcast_to(p.reshape(-1, 1, d), (b, 1, d)) for p in parts]


def kernel(x, c, ctx, c_ctx, ada_w, ada_b, ab_w_in, sgu_norm, sgu_w, sgu_b, conv_w, ab_w_out, cd_w_in, s5_a_re, s5_a_im, s5_log_dt, s5_b_re, s5_b_im, s5_c_re, s5_c_im, s5_d, s5_w_glu, mla_q_norm, mla_kv_norm, mla_w_uq, mla_w_uk, mla_w_uv, mla_qn_gain, mla_kn_gain, cd_w_out, moe_w_grp, moe_b_grp, moe_w_exp, moe_b_exp, moe_w_gate, moe_w_up, moe_w_down):
    b, l, d = x.shape
    lc = ctx.shape[1]
    depth = ada_w.shape[0]
    assert depth == 2, "layer 0 = gated-MLP/conv mixers, layer 1 = S5/attention mixers"
    assert d == TOKEN_ROWS * LANE, "token-tile layout: one vreg tile per token"
    tm = min(TOKEN_TILE, l)
    tmc = min(TOKEN_TILE, lc)

    mod_rows = 2 * SUBLANE
    cvec = jnp.zeros((mod_rows, d), F32).at[:b].set(c).at[b].set(c_ctx)
    mod = _ada(cvec, ada_w, ada_b)

    sh1, sc1, g1, sh2, sc2, g2 = _split_mod(mod[0, :b], b, d)
    csh1, csc1, cg1, csh2, csc2, cg2 = _split_mod(mod[0, b:b + 1], b, d)
    w_in = ab_w_in[0].astype(BF16)
    wr, br = _router_weights(moe_w_grp[0], moe_b_grp[0], moe_w_exp[0], moe_b_exp[0])
    sgub = jnp.repeat(sgu_b[0].T, d // G_A, axis=1)
    mix_args = (sgu_norm[0].reshape(1, d), sgu_w[0].astype(BF16), sgub, conv_w[0],
                ab_w_out[0].astype(BF16), wr, br)
    z, p = _inproj_ab(x, sh1, sc1, w_in, tm)
    xl, xm, rt = _mix_ab(z, p, x, g1, sh2, sc2, *mix_args, tm)
    zc, pc = _inproj_ab(ctx, csh1, csc1, w_in, tmc)
    xc, xmc, rtc = _mix_ab(zc, pc, ctx, cg1, csh2, csc2, *mix_args, tmc)
    xl, xc = _hier_moe([(xl, xm, rt, g2), (xc, xmc, rtc, cg2)], 0, moe_w_gate, moe_w_up, moe_w_down)

    sh1, sc1, g1, sh2, sc2, g2 = _split_mod(mod[1, :b], b, d)
    csh1, csc1, _, _, _, _ = _split_mod(mod[1, b:b + 1], b, d)
    d_c = s5_d.shape[-1]
    q_lora = mla_q_norm.shape[-1]
    kv_lora = mla_kv_norm.shape[-1]
    w_cd = cd_w_in[0]
    o_kr = d_c + q_lora + kv_lora
    w_kr = _head_lanes(w_cd[:, o_kr:])
    w_krs = _head_lanes(_rot_swap(w_cd[:, o_kr:], True))
    w_cd = jnp.concatenate([w_cd[:, :o_kr], w_kr, w_krs], axis=1).astype(BF16)
    widths = (d_c, q_lora, kv_lora, HEAD_PAD, HEAD_PAD)
    u_l, cq_l, ckv_l, krp_l, krs_l = _inproj_cd(xl, sh1, sc1, w_cd, widths, tm)
    u_c, _, ckv_c, krp_c, _ = _inproj_cd(xc, csh1, csc1, w_cd, widths, tmc)

    ab_re, ab_im, bb_re, bb_im = _s5_params(s5_a_re[0], s5_a_im[0], s5_log_dt[0], s5_b_re[0], s5_b_im[0])
    a_bc, bsb, csb = _s5_block_matrices(ab_re, ab_im, bb_re, bb_im, s5_c_re[0], s5_c_im[0], b)
    yf, yb = _s5_scan(u_c, u_l, a_bc, bsb, csb, s5_d[0].reshape(1, d_c), b)

    wq = _pad_heads(mla_w_uq[0], QK_HEAD).astype(BF16)
    wk = _pad_heads(mla_w_uk[0], QK_NOPE).astype(BF16)
    wv = mla_w_uv[0].astype(BF16)
    w_uq = mla_w_uq[0].reshape(q_lora, N_HEADS, QK_HEAD)
    wqs = _head_lanes(_rot_swap(w_uq[..., QK_NOPE:], True)).reshape(q_lora, N_HEADS * HEAD_PAD).astype(BF16)
    qg, kg = mla_qn_gain[0], mla_kn_gain[0]
    full = lambda g: jnp.pad(g, (0, HEAD_PAD - QK_HEAD)).reshape(1, HEAD_PAD)
    partner = lambda g: _head_lanes(_rot_swap(g[QK_NOPE:], False)).reshape(1, HEAD_PAD)
    cos, sin = _rope_tables(l)
    tables = (cos * full(qg), sin * partner(qg), cos * full(kg), sin * partner(kg))
    qn = mla_q_norm[0].reshape(1, q_lora)
    kvn = mla_kv_norm[0].reshape(1, kv_lora)
    q_l, k_l, v_l = _mla_prep(_mla_prep_lat_kernel, "mla_prep_lat", (cq_l, ckv_l, krp_l, krs_l),
                              (qn, kvn, wq, wqs, wk, wv), tables, 1, tm)
    k_c, v_c = _mla_prep(_mla_prep_ctx_kernel, "mla_prep_ctx", (ckv_c, krp_c), (kvn, wk, wv, full(kg)), (), 0, tmc)
    o_l = _attention(q_l, k_l, k_c, v_l, v_c, ATTN_Q_TILE, ATTN_HEADS)

    wr, br = _router_weights(moe_w_grp[1], moe_b_grp[1], moe_w_exp[1], moe_b_exp[1])
    xl, xm, rt = _mix_cd(yf, yb, o_l, xl, g1, sh2, sc2,
                         s5_w_glu[0].astype(BF16), cd_w_out[0].astype(BF16), wr, br, MIX_CD_TILE)
    (xl,) = _hier_moe([(xl, xm, rt, g2)], 1, moe_w_gate, moe_w_up, moe_w_down)
    return xl
```

```python
import functools
import math

import jax
import jax.numpy as jnp
from jax import lax
from jax.experimental import pallas as pl
from jax.experimental.pallas import tpu as pltpu

F32 = jnp.float32
BF16 = jnp.bfloat16
I32 = jnp.int32
HIGHEST = lax.Precision.HIGHEST

EPS = 1e-6
LOG2_E = math.log2(math.e)
GRID_W = 64
CHUNK = 128
G_A = 8
CONV_W = 3
S5_GROUP = 16
S5_STATE = 64
N_HEADS = 8
QK_NOPE = 64
QK_ROPE = 32
QK_HEAD = QK_NOPE + QK_ROPE
V_HEAD = 64
AXIS_PAIRS = QK_ROPE // 4
ROPE_BASE = 10000.0
N_GROUPS = 4
EXPERTS_PER_GROUP = 8
N_EXPERTS = N_GROUPS * EXPERTS_PER_GROUP
TOP_K = 2

LANE = 128
SUBLANE = 8
HEAD_PAD = LANE
TOKEN_TILE = 512
MIX_SUBTILES = 2
MIX_CD_TILE = 256
MOE_TILE = 512
COMBINE_TILE = 256
PLAN_BLOCK = 2048
MAX_TILES = 256
TOKEN_ROWS = SUBLANE
ATTN_Q_TILE = 512
ATTN_HEADS = 8
S5_STEPS = 64
S5_SUPER = LANE // S5_GROUP
VMEM_LIMIT = 56 * 1024 * 1024


def _cparams(sem):
    return pltpu.CompilerParams(dimension_semantics=sem, vmem_limit_bytes=VMEM_LIMIT)


def _rms(x):
    return x * lax.rsqrt(jnp.mean(x * x, axis=-1, keepdims=True) + EPS)


def _sigmoid(x):
    return 1.0 / (1.0 + jnp.exp(-x))


def _dot(a, b):
    return jnp.dot(a, b, preferred_element_type=F32)


def _ada_kernel(c_ref, w_ref, b_ref, o_ref):
    c = c_ref[...]
    s = c * _sigmoid(c)
    o_ref[...] = jnp.dot(s, w_ref[...], precision=HIGHEST, preferred_element_type=F32) + b_ref[...]


def _ada(cvec, ada_w, ada_b):
    depth, d, n = ada_w.shape
    rows = cvec.shape[0]
    tn = 1024
    return pl.pallas_call(
        _ada_kernel,
        grid=(depth, n // tn),
        in_specs=[
            pl.BlockSpec((rows, d), lambda i, j: (0, 0)),
            pl.BlockSpec((None, d, tn), lambda i, j: (i, 0, j)),
            pl.BlockSpec((None, 1, tn), lambda i, j: (i, 0, j)),
        ],
        out_specs=pl.BlockSpec((None, rows, tn), lambda i, j: (i, 0, j)),
        out_shape=jax.ShapeDtypeStruct((depth, rows, n), F32),
        compiler_params=_cparams(("arbitrary", "arbitrary")),
        name="ada_mod",
    )(cvec, ada_w, ada_b.reshape(depth, 1, n))


def _route(xm, wr_ref, br_ref):
    hi = xm.astype(BF16)
    lo = (xm - hi.astype(F32)).astype(BF16)
    w = wr_ref[...]
    w_hi = w.astype(BF16)
    w_lo = (w - w_hi.astype(F32)).astype(BF16)
    logits = _dot(jnp.concatenate([hi, lo, hi], axis=1), jnp.concatenate([w_hi, w_hi, w_lo], axis=0)) + br_ref[...]
    lane = lax.broadcasted_iota(I32, logits.shape, 1).astype(F32)
    neg = jnp.float32(-1e30)
    big = jnp.float32(1e6)
    is_grp = lane < N_GROUPS
    gl = jnp.where(is_grp, logits, neg)
    gmax = jnp.max(gl, axis=-1, keepdims=True)
    gsum = jnp.sum(jnp.where(is_grp, jnp.exp(gl - gmax), 0.0), axis=-1, keepdims=True)
    p_top = 1.0 / gsum
    grp = jnp.min(jnp.where(gl == gmax, lane, big), axis=-1, keepdims=True)
    eidx = lane - N_GROUPS
    in_grp = (eidx >= grp * EXPERTS_PER_GROUP) & (eidx < (grp + 1.0) * EXPERTS_PER_GROUP)
    el = jnp.where(in_grp, logits, neg)
    m1 = jnp.max(el, axis=-1, keepdims=True)
    i1 = jnp.min(jnp.where(el == m1, lane, big), axis=-1, keepdims=True)
    el2 = jnp.where(lane == i1, neg, el)
    m2 = jnp.max(el2, axis=-1, keepdims=True)
    i2 = jnp.min(jnp.where(el2 == m2, lane, big), axis=-1, keepdims=True)
    t = jnp.exp(m2 - m1)
    w1 = p_top / (1.0 + t)
    w2 = p_top * t / (1.0 + t)
    e1 = i1 - N_GROUPS
    e2 = i2 - N_GROUPS
    return jnp.where(lane == 0, e1, jnp.where(lane == 1, e2, jnp.where(lane == 2, w1, jnp.where(lane == 3, w2, 0.0))))


def _residual_and_route(x, y, g1, sh2, sc2, wr_ref, br_ref, xl_ref, xm_ref, rt_ref):
    xl = x + g1 * y
    xl_ref[...] = xl
    xm = _rms(xl) * (1.0 + sc2) + sh2
    _store_token_tiles(xm_ref, xm)
    rt_ref[...] = _route(xm, wr_ref, br_ref)


def _inproj_ab_kernel(x_ref, sh_ref, sc_ref, w_ref, z_ref, p_ref):
    d = x_ref.shape[-1]
    h = (_rms(x_ref[...]) * (1.0 + sc_ref[...]) + sh_ref[...]).astype(BF16)
    for j in range(3):
        z_ref[:, j * d:(j + 1) * d] = _dot(h, w_ref[:, j * d:(j + 1) * d]).astype(BF16)
    gate_c = _dot(h, w_ref[:, 3 * d:4 * d])
    xb = _dot(h, w_ref[:, 4 * d:5 * d])
    p_ref[...] = gate_c * xb


def _inproj_ab(x, sh, sc, w_bf16, tm):
    b, l, d = x.shape
    n = w_bf16.shape[1]
    return pl.pallas_call(
        _inproj_ab_kernel,
        grid=(b, l // tm),
        in_specs=[
            pl.BlockSpec((None, tm, d), lambda i, j: (i, j, 0)),
            pl.BlockSpec((None, 1, d), lambda i, j: (i, 0, 0)),
            pl.BlockSpec((None, 1, d), lambda i, j: (i, 0, 0)),
            pl.BlockSpec((d, n), lambda i, j: (0, 0)),
        ],
        out_specs=[
            pl.BlockSpec((None, tm, 3 * d), lambda i, j: (i, j, 0)),
            pl.BlockSpec((None, tm, d), lambda i, j: (i, j, 0)),
        ],
        out_shape=[
            jax.ShapeDtypeStruct((b, l, 3 * d), BF16),
            jax.ShapeDtypeStruct((b, l, d), F32),
        ],
        compiler_params=_cparams(("arbitrary", "arbitrary")),
        name="inproj_ab",
    )(x, sh, sc, w_bf16)


def _mix_ab_kernel(z_ref, p_ref, pprev_ref, pnext_ref, x_ref, g1_ref, sh2_ref, sc2_ref,
                   sgun_ref, sguw_ref, sgub_ref, convw_ref, wout_ref, wr_ref, br_ref,
                   xl_ref, xm_ref, rt_ref, pbuf, ycat):
    i = pl.program_id(1)
    nt = pl.num_programs(1)
    tm, d = x_ref.shape
    cg = d // G_A
    pbuf[SUBLANE:SUBLANE + tm, :] = p_ref[...]
    pbuf[0:SUBLANE, :] = jnp.where(i > 0, pprev_ref[...], 0.0)
    pbuf[SUBLANE + tm:2 * SUBLANE + tm, :] = jnp.where(i < nt - 1, pnext_ref[...], 0.0)
    ts = tm // (MIX_SUBTILES if tm % (MIX_SUBTILES * CHUNK) == 0 else 1)
    for t0 in range(0, tm, ts):
        rows = slice(t0, t0 + ts)
        v = jax.nn.gelu(z_ref[rows, d:2 * d].astype(F32))
        vc = (_rms(v) * sgun_ref[...]).astype(BF16)
        for c in range(ts // CHUNK):
            lr = c * CHUNK
            r0 = t0 + lr
            cols = [_dot(sguw_ref[g], vc[lr:lr + CHUNK, g * cg:(g + 1) * cg]) for g in range(G_A)]
            s = jnp.concatenate(cols, axis=1) + sgub_ref[...]
            u = jax.nn.gelu(z_ref[r0:r0 + CHUNK, 0:d].astype(F32))
            ycat[r0:r0 + CHUNK, 0:d] = (u * s).astype(BF16)
        conv = (convw_ref[0:1, :] * pbuf[SUBLANE - 1 + t0:SUBLANE - 1 + t0 + ts, :]
                + convw_ref[1:2, :] * pbuf[SUBLANE + t0:SUBLANE + t0 + ts, :]
                + convw_ref[2:3, :] * pbuf[SUBLANE + 1 + t0:SUBLANE + 1 + t0 + ts, :])
        ycat[rows, d:2 * d] = (z_ref[rows, 2 * d:3 * d].astype(F32) * conv).astype(BF16)
        y = _dot(ycat[rows, :], wout_ref[...])
        _residual_and_route(x_ref[rows, :], y, g1_ref[...], sh2_ref[...], sc2_ref[...], wr_ref, br_ref,
                            xl_ref.at[rows, :], xm_ref.at[pl.ds(t0 * TOKEN_ROWS, ts * TOKEN_ROWS), :],
                            rt_ref.at[rows, :])


def _mix_ab(z, p, x, g1, sh2, sc2, sgun, sguw, sgub, convw, wout, wr, br, tm):
    b, l, d = x.shape
    hb = tm // SUBLANE
    nhb = l // SUBLANE
    tok = lambda i, j: (i, j, 0)
    per_b = lambda i, j: (i, 0, 0)
    const2 = lambda i, j: (0, 0)
    return pl.pallas_call(
        _mix_ab_kernel,
        grid=(b, l // tm),
        in_specs=[
            pl.BlockSpec((None, tm, 3 * d), tok),
            pl.BlockSpec((None, tm, d), tok),
            pl.BlockSpec((None, SUBLANE, d), lambda i, j: (i, jnp.maximum(j * hb - 1, 0), 0)),
            pl.BlockSpec((None, SUBLANE, d), lambda i, j: (i, jnp.minimum((j + 1) * hb, nhb - 1), 0)),
            pl.BlockSpec((None, tm, d), tok),
            pl.BlockSpec((None, 1, d), per_b),
            pl.BlockSpec((None, 1, d), per_b),
            pl.BlockSpec((None, 1, d), per_b),
            pl.BlockSpec((1, d), const2),
            pl.BlockSpec((G_A, CHUNK, CHUNK), lambda i, j: (0, 0, 0)),
            pl.BlockSpec((CHUNK, d), const2),
            pl.BlockSpec((CONV_W, d), const2),
            pl.BlockSpec((2 * d, d), const2),
            pl.BlockSpec(wr.shape, const2),
            pl.BlockSpec((1, LANE), const2),
        ],
        out_specs=[
            pl.BlockSpec((None, tm, d), tok),
            pl.BlockSpec((tm * TOKEN_ROWS, LANE), lambda i, j: (i * (l // tm) + j, 0)),
            pl.BlockSpec((None, tm, LANE), tok),
        ],
        out_shape=[
            jax.ShapeDtypeStruct((b, l, d), F32),
            jax.ShapeDtypeStruct((b * l * TOKEN_ROWS, LANE), F32),
            jax.ShapeDtypeStruct((b, l, LANE), F32),
        ],
        scratch_shapes=[
            pltpu.VMEM((tm + 2 * SUBLANE, d), F32),
            pltpu.VMEM((tm, 2 * d), BF16),
        ],
        compiler_params=_cparams(("arbitrary", "arbitrary")),
        name="mix_ab",
    )(z, p, p, p, x, g1, sh2, sc2, sgun, sguw, sgub, convw, wout, wr, br)


def _store_token_tiles(ref, val):
    tm, d = val.shape
    rpt = d // LANE
    for j in range(rpt):
        ref[pl.ds(j, tm, stride=rpt), :] = val[:, j * LANE:(j + 1) * LANE]


def _load_token_tile_col(ref, tm, rpt, j):
    return ref[pl.ds(j, tm, stride=rpt), :]


def _plan_kernel(rt_ref, tri_ref, pos_ref, te_ref, tv_ref, pad_ref, run, *, tm):
    ps = pl.program_id(0)
    i = pl.program_id(1)
    blk = rt_ref.shape[0]
    ne = N_EXPERTS

    @pl.when(jnp.logical_and(ps == 0, i == 0))
    def _():
        run[...] = jnp.zeros_like(run)

    @pl.when(jnp.logical_and(ps == 1, i == 0))
    def _():
        counts = run[...]
        padded = jnp.floor((counts + (tm - 1)) * (1.0 / tm)) * tm
        r = lax.broadcasted_iota(I32, (ne, ne), 0)
        c = lax.broadcasted_iota(I32, (ne, ne), 1)
        starts = jnp.dot((c < r).astype(F32), padded, precision=HIGHEST, preferred_element_type=F32)
        nt = te_ref.shape[1]
        ends = jnp.concatenate([(starts + padded) * (1.0 / tm)] * (nt // LANE), axis=1)
        t = lax.broadcasted_iota(I32, (ne, nt), 1).astype(F32)
        te = jnp.sum((t >= ends).astype(F32), axis=0, keepdims=True)
        total = jnp.max(ends, axis=0, keepdims=True)
        te_ref[...] = jnp.minimum(te, ne - 1.0).astype(I32)
        tv_ref[...] = (t[0:1, :] < total).astype(I32)
        lane = lax.broadcasted_iota(I32, (ne, LANE), 1)
        pad_ref[...] = jnp.where(lane == 0, starts + counts,
                                 jnp.where(lane == 1, padded - counts, 0.0)).astype(I32)
        run[...] = starts

    slab_t = rt_ref[...].T
    ex = lax.broadcasted_iota(I32, (ne, blk), 0).astype(F32)
    oh1 = ex == slab_t[0:1, :]
    oh2 = ex == slab_t[1:2, :]
    oh = jnp.where(jnp.logical_or(oh1, oh2), 1.0, 0.0)
    prefix = _dot(oh.astype(BF16), tri_ref[...]) + run[:, 0:1]
    pos1 = jnp.sum(jnp.where(oh1, prefix, 0.0), axis=0, keepdims=True)
    pos2 = jnp.sum(jnp.where(oh2, prefix, 0.0), axis=0, keepdims=True)
    pos_ref[...] = jnp.concatenate([pos1, pos2], axis=0).astype(I32)
    run[...] = run[...] + jnp.sum(oh, axis=1, keepdims=True)


def _moe_plan(rt_all, tm):
    n = rt_all.shape[0]
    blk = PLAN_BLOCK
    nb = n // blk
    tri = jnp.triu(jnp.ones((blk, blk), BF16), k=1)
    return pl.pallas_call(
        functools.partial(_plan_kernel, tm=tm),
        grid=(2, nb),
        in_specs=[
            pl.BlockSpec((blk, LANE), lambda p, i: (i, 0)),
            pl.BlockSpec((blk, blk), lambda p, i: (0, 0)),
        ],
        out_specs=[
            pl.BlockSpec((None, TOP_K, blk), lambda p, i: (jnp.where(p == 0, nb, i), 0, 0)),
            pl.BlockSpec((1, MAX_TILES), lambda p, i: (0, 0)),
            pl.BlockSpec((1, MAX_TILES), lambda p, i: (0, 0)),
            pl.BlockSpec((N_EXPERTS, LANE), lambda p, i: (0, 0)),
        ],
        out_shape=[
            jax.ShapeDtypeStruct((nb + 1, TOP_K, blk), I32),
            jax.ShapeDtypeStruct((1, MAX_TILES), I32),
            jax.ShapeDtypeStruct((1, MAX_TILES), I32),
            jax.ShapeDtypeStruct((N_EXPERTS, LANE), I32),
        ],
        scratch_shapes=[pltpu.VMEM((N_EXPERTS, LANE), F32)],
        compiler_params=_cparams(("arbitrary", "arbitrary")),
        name="moe_plan",
    )(rt_all, tri)


def _dispatch_kernel(ps_ref, pl_ref, tv_ref, pos_ref, *rest, blocks, n_tiles):
    xm_refs = rest[:len(blocks)]
    xs_hbm, zbuf, sem = rest[len(blocks):]
    i = pl.program_id(0)
    tt = pos_ref.shape[1]
    rpt = xm_refs[0].shape[0] // tt
    tile_rows = zbuf.shape[0]

    @pl.when(i == 0)
    def _():
        zbuf[...] = jnp.zeros_like(zbuf)

        def gap_copies(e, wait):
            base = ps_ref[e]
            n = pl_ref[e]
            bit = tile_rows // (2 * rpt)
            while bit >= 1:
                lower = n & (bit - 1)

                @pl.when((n & bit) != 0)
                def _(lower=lower, bit=bit):
                    dst = xs_hbm.at[pl.ds((base + lower) * rpt, bit * rpt)]
                    cp = pltpu.make_async_copy(zbuf.at[pl.ds(0, bit * rpt)], dst, sem.at[1])
                    cp.wait() if wait else cp.start()
                bit //= 2

        def tile_copy(t, wait):
            @pl.when(tv_ref[t] != 1)
            def _():
                dst = xs_hbm.at[pl.ds(pl.multiple_of(t * tile_rows, tile_rows), tile_rows)]
                cp = pltpu.make_async_copy(zbuf, dst, sem.at[1])
                cp.wait() if wait else cp.start()

        for wait in (False, True):
            lax.fori_loop(0, N_EXPERTS, lambda e, c, w=wait: (gap_copies(e, w), c)[1], 0)
            lax.fori_loop(0, n_tiles, lambda t, c, w=wait: (tile_copy(t, w), c)[1], 0)

    def scatter(xm_ref):
        def issue(r, carry):
            src = xm_ref.at[pl.ds(pl.multiple_of(r * rpt, rpt), rpt)]
            for k in range(TOP_K):
                dst = xs_hbm.at[pl.ds(pl.multiple_of(pos_ref[k, r] * rpt, rpt), rpt)]
                pltpu.make_async_copy(src, dst, sem.at[0]).start(priority=k)
            return carry
        lax.fori_loop(0, tt, issue, 0, unroll=8)
        for k in range(TOP_K):
            pltpu.make_async_copy(xm_ref, xs_hbm.at[pl.ds(0, tt * rpt)], sem.at[0]).wait()

    lo = 0
    for xm_ref, nblk in zip(xm_refs, blocks):
        pl.when(jnp.logical_and(i >= lo, i < lo + nblk))(functools.partial(scatter, xm_ref))
        lo += nblk


def _dispatch(pad_start, pad_len, valid, pos, xm_streams, n_tiles, tm):
    tt = pos.shape[-1]
    rpt = TOKEN_ROWS
    blocks = tuple(xm.shape[0] // (rpt * tt) for xm in xm_streams)
    in_specs = [pl.BlockSpec((None, TOP_K, tt), lambda i, a, b, c: (i, 0, 0), memory_space=pltpu.SMEM)]
    lo = 0
    for nblk in blocks:
        in_specs.append(pl.BlockSpec(
            (tt * rpt, LANE), lambda i, a, b, c, lo=lo, nblk=nblk: (jnp.clip(i - lo, 0, nblk - 1), 0)))
        lo += nblk
    return pl.pallas_call(
        functools.partial(_dispatch_kernel, blocks=blocks, n_tiles=n_tiles),
        grid_spec=pltpu.PrefetchScalarGridSpec(
            num_scalar_prefetch=3,
            grid=(sum(blocks),),
            in_specs=in_specs,
            out_specs=pl.BlockSpec(memory_space=pl.ANY),
            scratch_shapes=[pltpu.VMEM((tm * rpt, LANE), F32), pltpu.SemaphoreType.DMA((2,))],
        ),
        out_shape=jax.ShapeDtypeStruct((n_tiles * tm * rpt, LANE), F32),
        compiler_params=_cparams(("arbitrary",)),
        name="moe_dispatch",
    )(pad_start, pad_len, valid, pos, *xm_streams)


def _moe_kernel(te_ref, tv_ref, x_ref, wg_hbm, wu_hbm, wd_hbm, y_ref,
                wg_f, wu_f, wd_f, wg_s, wu_s, wd_s, slot_ref, sem, *, layer, n_tiles):
    t = pl.program_id(0)
    rpt = TOKEN_ROWS
    tm = x_ref.shape[0] // rpt

    def weight_copies(e, slot):
        return [pltpu.make_async_copy(src.at[layer, e], dst.at[slot], sem.at[slot, k])
                for k, (src, dst) in enumerate(((wg_hbm, wg_f), (wu_hbm, wu_f), (wd_hbm, wd_f)))]

    @pl.when(tv_ref[t] == 1)
    def _():
        e = te_ref[t]
        first = jnp.logical_or(t == 0, te_ref[jnp.maximum(t - 1, 0)] != e)

        @pl.when(first)
        def _():
            @pl.when(t == 0)
            def _():
                slot_ref[0] = 0
                for cp in weight_copies(e, 0):
                    cp.start()

            slot = slot_ref[0]
            for cp in weight_copies(e, slot):
                cp.wait()
            at = lambda ref, i: ref[jnp.minimum(i, n_tiles - 1)]
            same = lambda i: (i < n_tiles) & (at(tv_ref, i) == 1) & (at(te_ref, i) == e)
            t2 = lax.while_loop(same, lambda i: i + 1, t + 1)

            @pl.when((t2 < n_tiles) & (at(tv_ref, t2) == 1))
            def _():
                for cp in weight_copies(at(te_ref, t2), 1 - slot):
                    cp.start()

            wg_s[...] = wg_f[slot].astype(BF16)
            wu_s[...] = wu_f[slot].astype(BF16)
            wd_s[...] = wd_f[slot].astype(BF16)
            slot_ref[0] = 1 - slot

        x = jnp.concatenate([_load_token_tile_col(x_ref, tm, rpt, j) for j in range(rpt)], axis=1).astype(BF16)
        g = _dot(x, wg_s[...])
        u = _dot(x, wu_s[...])
        h = (g * _sigmoid(g) * u).astype(BF16)
        _store_token_tiles(y_ref, _dot(h, wd_s[...]))

    @pl.when(tv_ref[t] != 1)
    def _():
        y_ref[...] = jnp.zeros_like(y_ref)


def _moe_experts(xs, tile_e, valid, n_tiles, layer, w_gate, w_up, w_down, tm):
    _, _, d, de = w_gate.shape
    rpt = TOKEN_ROWS
    grid_spec = pltpu.PrefetchScalarGridSpec(
        num_scalar_prefetch=2,
        grid=(n_tiles,),
        in_specs=[
            pl.BlockSpec((tm * rpt, LANE), lambda t, te, tv: (t * tv[t], 0)),
            pl.BlockSpec(memory_space=pl.ANY),
            pl.BlockSpec(memory_space=pl.ANY),
            pl.BlockSpec(memory_space=pl.ANY),
        ],
        out_specs=pl.BlockSpec((tm * rpt, LANE), lambda t, te, tv: (t, 0)),
        scratch_shapes=[
            pltpu.VMEM((2, d, de), F32),
            pltpu.VMEM((2, d, de), F32),
            pltpu.VMEM((2, de, d), F32),
            pltpu.VMEM((d, de), BF16),
            pltpu.VMEM((d, de), BF16),
            pltpu.VMEM((de, d), BF16),
            pltpu.SMEM((1,), I32),
            pltpu.SemaphoreType.DMA((2, 3)),
        ],
    )
    return pl.pallas_call(
        functools.partial(_moe_kernel, layer=layer, n_tiles=n_tiles),
        grid_spec=grid_spec,
        out_shape=jax.ShapeDtypeStruct((n_tiles * tm * rpt, LANE), F32),
        compiler_params=_cparams(("arbitrary",)),
        name="moe_experts",
    )(tile_e, valid, xs, w_gate, w_up, w_down)


def _combine_kernel(pos0_ref, posn_ref, x_ref, g_ref, rt_ref, ys_hbm, o_ref, buf, sem, *, n):
    i = pl.program_id(0)
    tt, d = x_ref.shape
    rpt = TOKEN_ROWS

    def gather(pos_ref, slot):
        def issue(r, carry):
            for k in range(TOP_K):
                src = ys_hbm.at[pl.ds(pl.multiple_of(pos_ref[k, r] * rpt, rpt), rpt)]
                dst = buf.at[slot, k, pl.ds(pl.multiple_of(r * rpt, rpt), rpt)]
                pltpu.make_async_copy(src, dst, sem.at[slot, k]).start(priority=k)
            return carry
        lax.fori_loop(0, tt, issue, 0, unroll=8)

    slot = lax.rem(i, 2)

    @pl.when(i == 0)
    def _():
        gather(pos0_ref, 0)

    @pl.when(i + 1 < n)
    def _():
        gather(posn_ref, 1 - slot)

    for k in range(TOP_K):
        pltpu.make_async_copy(ys_hbm.at[pl.ds(0, tt * rpt)], buf.at[slot, k], sem.at[slot, k]).wait()
    w1 = rt_ref[:, 2:3]
    w2 = rt_ref[:, 3:4]
    for j in range(rpt):
        r1 = buf[slot, 0, pl.ds(j, tt, stride=rpt), :]
        r2 = buf[slot, 1, pl.ds(j, tt, stride=rpt), :]
        sl = slice(j * LANE, (j + 1) * LANE)
        o_ref[:, sl] = x_ref[:, sl] + g_ref[:, sl] * (w1 * r1 + w2 * r2)


def _moe_combine(x, g2, rt, pos, tok_off, ys, tt):
    b, l, d = x.shape
    rpt = TOKEN_ROWS
    n = b * l
    per_b = l // tt
    per_blk = PLAN_BLOCK // tt
    blk0 = tok_off // PLAN_BLOCK

    def pos_map(step):
        return lambda i: (blk0 + step(i) // per_blk, 0, step(i) % per_blk)

    return pl.pallas_call(
        functools.partial(_combine_kernel, n=n // tt),
        grid=(n // tt,),
        in_specs=[
            pl.BlockSpec((None, TOP_K, tt), pos_map(lambda i: i), memory_space=pltpu.SMEM),
            pl.BlockSpec((None, TOP_K, tt), pos_map(lambda i: jnp.minimum(i + 1, n // tt - 1)),
                         memory_space=pltpu.SMEM),
            pl.BlockSpec((tt, d), lambda i: (i, 0)),
            pl.BlockSpec((None, 1, d), lambda i: (i // per_b, 0, 0)),
            pl.BlockSpec((tt, LANE), lambda i: (i, 0)),
            pl.BlockSpec(memory_space=pl.ANY),
        ],
        out_specs=pl.BlockSpec((tt, d), lambda i: (i, 0)),
        out_shape=jax.ShapeDtypeStruct((n, d), F32),
        scratch_shapes=[
            pltpu.VMEM((2, TOP_K, tt * rpt, LANE), F32),
            pltpu.SemaphoreType.DMA((2, TOP_K)),
        ],
        compiler_params=_cparams(("arbitrary",)),
        name="moe_combine",
    )(pos, pos, x.reshape(n, d), g2, rt.reshape(n, LANE), ys).reshape(b, l, d)


def _hier_moe(streams, layer, w_gate, w_up, w_down):
    sizes = [s[0].shape[0] * s[0].shape[1] for s in streams]
    n = sum(sizes)
    tm = MOE_TILE
    n_tiles = (TOP_K * n) // tm + N_EXPERTS
    assert n_tiles <= MAX_TILES and all(sz % PLAN_BLOCK == 0 for sz in sizes)
    rt_all = jnp.concatenate([s[2].reshape(-1, LANE) for s in streams], axis=0)
    pos, tile_e, valid, pad = _moe_plan(rt_all, tm)
    tile_e, valid = tile_e.reshape(-1), valid.reshape(-1)
    xs = _dispatch(pad[:, 0], pad[:, 1], valid, pos, [s[1] for s in streams], n_tiles, tm)
    ys = _moe_experts(xs, tile_e, valid, n_tiles, layer, w_gate, w_up, w_down, tm)
    outs = []
    off = 0
    for (xl, _, rt, g2), sz in zip(streams, sizes):
        outs.append(_moe_combine(xl, g2, rt, pos, off, ys, COMBINE_TILE))
        off += sz
    return outs


def _inproj_cd_kernel(x_ref, sh_ref, sc_ref, w_ref, u_ref, *out_refs):
    h = (_rms(x_ref[...]) * (1.0 + sc_ref[...]) + sh_ref[...]).astype(BF16)
    tm = x_ref.shape[0]
    n_slab, rows, cw = u_ref.shape
    nb = rows // tm
    bi = pl.program_id(1)
    u = _dot(h, w_ref[:, 0:n_slab * cw])
    for s in range(n_slab):
        u_ref[s, pl.ds(bi, tm, stride=nb), :] = u[:, s * cw:(s + 1) * cw]
    off = n_slab * cw
    for ref in out_refs:
        n = ref.shape[-1]
        ref[...] = _dot(h, w_ref[:, off:off + n]).astype(ref.dtype)
        off += n


def _inproj_cd(x, sh, sc, w_bf16, widths, tm):
    b, l, d = x.shape
    n = w_bf16.shape[1]
    n_slab = widths[0] // LANE
    tok = lambda j, i: (i, j, 0)
    return pl.pallas_call(
        _inproj_cd_kernel,
        grid=(l // tm, b),
        in_specs=[
            pl.BlockSpec((None, tm, d), tok),
            pl.BlockSpec((None, 1, d), lambda j, i: (i, 0, 0)),
            pl.BlockSpec((None, 1, d), lambda j, i: (i, 0, 0)),
            pl.BlockSpec((d, n), lambda j, i: (0, 0)),
        ],
        out_specs=([pl.BlockSpec((n_slab, tm * b, LANE), lambda j, i: (0, j, 0))]
                   + [pl.BlockSpec((None, tm, w), tok) for w in widths[1:]]),
        out_shape=([jax.ShapeDtypeStruct((n_slab, l * b, LANE), F32)]
                   + [jax.ShapeDtypeStruct((b, l, w), F32) for w in widths[1:]]),
        compiler_params=_cparams(("arbitrary", "arbitrary")),
        name="inproj_cd",
    )(x, sh, sc, w_bf16)


def _s5_param_kernel(are_ref, aim_ref, ldt_ref, bre_ref, bim_ref, abre_ref, abim_ref, bbre_ref, bbim_ref):
    a_re = are_ref[...]
    a_im = aim_ref[...]
    dt = jnp.exp(ldt_ref[...])
    mag = jnp.exp(dt * a_re)
    ab_re = mag * jnp.cos(dt * a_im)
    ab_im = mag * jnp.sin(dt * a_im)
    den = a_re * a_re + a_im * a_im
    nr = ab_re - 1.0
    f_re = (nr * a_re + ab_im * a_im) / den
    f_im = (ab_im * a_re - nr * a_im) / den
    abre_ref[...] = ab_re
    abim_ref[...] = ab_im
    b_re = bre_ref[...]
    b_im = bim_ref[...]
    bbre_ref[...] = f_re[None] * b_re - f_im[None] * b_im
    bbim_ref[...] = f_re[None] * b_im + f_im[None] * b_re


def _s5_params(a_re, a_im, log_dt, b_re, b_im):
    nd, g, p = a_re.shape
    c = b_re.shape[-1]
    rows = nd * g * p // LANE
    flat = lambda t: t.reshape(rows, LANE)
    chan_major = lambda t: jnp.moveaxis(t, -1, 0).reshape(c, rows, LANE)
    ldt = jnp.broadcast_to(log_dt[:, :, None], (nd, g, p))
    shapes = [jax.ShapeDtypeStruct((rows, LANE), F32)] * 2 + [jax.ShapeDtypeStruct((c, rows, LANE), F32)] * 2
    ab_re, ab_im, bb_re, bb_im = pl.pallas_call(
        _s5_param_kernel, out_shape=shapes, name="s5_params",
    )(flat(a_re), flat(a_im), flat(ldt), chan_major(b_re), chan_major(b_im))
    unflat = lambda t: jnp.moveaxis(t.reshape(c, nd, g, p), 0, -1)
    return ab_re.reshape(nd, g, p), ab_im.reshape(nd, g, p), unflat(bb_re), unflat(bb_im)


def _s5_kernel(ufc_ref, ubc_ref, ufl_ref, ubl_ref, a_ref, bsb_ref, csb_ref, d_ref, yf_ref, yb_ref,
               hf, hb, cf, cb, *, n_ctx):
    i = pl.program_id(0)
    nb = cf.shape[0]
    steps = hf.shape[0] // nb
    n_super = bsb_ref.shape[1]
    cw = bsb_ref.shape[2]
    sw = bsb_ref.shape[3]
    half = sw // 2

    @pl.when(i == 0)
    def _():
        cf[...] = jnp.zeros_like(cf)
        cb[...] = jnp.zeros_like(cb)

    def drive(uf_ref, ub_ref):
        for s in range(n_super):
            hf[:, s * sw:(s + 1) * sw] = _dot(uf_ref[s].astype(BF16), bsb_ref[0, s])
            hb[:, s * sw:(s + 1) * sw] = _dot(ub_ref[s].astype(BF16), bsb_ref[1, s])

    pl.when(i < n_ctx)(functools.partial(drive, ufc_ref, ubc_ref))
    pl.when(i >= n_ctx)(functools.partial(drive, ufl_ref, ubl_ref))

    for s in range(n_super):
        re = slice(s * sw, s * sw + half)
        im = slice(s * sw + half, (s + 1) * sw)
        st = slice(s * half, (s + 1) * half)
        arf, aif = a_ref[0, 0, :, st], a_ref[0, 1, :, st]
        arb, aib = a_ref[1, 0, :, st], a_ref[1, 1, :, st]

        def step(t, carry):
            hfr, hfi, hbr, hbi = carry
            rf = pl.multiple_of(t * nb, nb)
            nfr = arf * hfr - aif * hfi + hf[pl.ds(rf, nb), re]
            nfi = arf * hfi + aif * hfr + hf[pl.ds(rf, nb), im]
            hf[pl.ds(rf, nb), re] = nfr
            hf[pl.ds(rf, nb), im] = nfi
            rb = pl.multiple_of((steps - 1 - t) * nb, nb)
            nbr = arb * hbr - aib * hbi + hb[pl.ds(rb, nb), re]
            nbi = arb * hbi + aib * hbr + hb[pl.ds(rb, nb), im]
            hb[pl.ds(rb, nb), re] = nbr
            hb[pl.ds(rb, nb), im] = nbi
            return nfr, nfi, nbr, nbi

        out = lax.fori_loop(0, steps, step, (cf[:, re], cf[:, im], cb[:, re], cb[:, im]), unroll=4)
        cf[:, re], cf[:, im], cb[:, re], cb[:, im] = out

    @pl.when(i >= n_ctx)
    def _():
        for s in range(n_super):
            skip = ufl_ref[s] * d_ref[:, s * cw:(s + 1) * cw]
            yf_ref[s] = _dot(hf[:, s * sw:(s + 1) * sw].astype(BF16), csb_ref[0, s]) + skip
            yb_ref[s] = _dot(hb[:, s * sw:(s + 1) * sw].astype(BF16), csb_ref[1, s])


def _s5_scan(u_c, u_l, a_bc, bsb, csb, d_skip, nb):
    n_slab, rows_l, _ = u_l.shape
    blk = S5_STEPS * nb
    nc = u_c.shape[1] // blk
    nl = rows_l // blk
    n_state = bsb.shape[1] * bsb.shape[3]
    clip = jnp.clip
    ctx_f = lambda i: (0, clip(i, 0, nc - 1), 0)
    ctx_b = lambda i: (0, clip(nc - 1 - i, 0, nc - 1), 0)
    lat_f = lambda i: (0, clip(i - nc, 0, nl - 1), 0)
    lat_b = lambda i: (0, clip(nl - 1 - (i - nc), 0, nl - 1), 0)
    whole = lambda a: pl.BlockSpec(a.shape, lambda i: (0,) * a.ndim)
    slab = lambda index_map: pl.BlockSpec((n_slab, blk, LANE), index_map)
    return pl.pallas_call(
        functools.partial(_s5_kernel, n_ctx=nc),
        grid=(nc + nl,),
        in_specs=[slab(ctx_f), slab(ctx_b), slab(lat_f), slab(lat_b), whole(a_bc), whole(bsb), whole(csb),
                  whole(d_skip)],
        out_specs=[slab(lat_f), slab(lat_b)],
        out_shape=[jax.ShapeDtypeStruct(u_l.shape, F32)] * 2,
        scratch_shapes=[
            pltpu.VMEM((blk, n_state), F32),
            pltpu.VMEM((blk, n_state), F32),
            pltpu.VMEM((nb, n_state), F32),
            pltpu.VMEM((nb, n_state), F32),
        ],
        compiler_params=_cparams(("arbitrary",)),
        name="s5_scan",
    )(u_c, u_c, u_l, u_l, a_bc, bsb, csb, d_skip)


def _s5_block_matrices(ab_re, ab_im, bb_re, bb_im, c_re, c_im, nb):
    nd, g, p = ab_re.shape
    c = bb_re.shape[-1]
    ns = g // S5_SUPER
    eye = jnp.eye(S5_SUPER, dtype=F32)

    def in_mat(bb):
        t = bb.reshape(nd, ns, S5_SUPER, p, c)
        return jnp.einsum('dsgpc,gh->dsgchp', t, eye).reshape(nd, ns, S5_SUPER * c, S5_SUPER * p)

    def out_mat(cc):
        t = cc.reshape(nd, ns, S5_SUPER, c, p)
        return jnp.einsum('dsgcp,gh->dsgphc', t, eye).reshape(nd, ns, S5_SUPER * p, S5_SUPER * c)

    bsb = jnp.concatenate([in_mat(bb_re), in_mat(bb_im)], axis=-1).astype(BF16)
    csb = jnp.concatenate([out_mat(c_re), out_mat(-c_im)], axis=-2).astype(BF16)
    a_bc = jnp.stack([ab_re.reshape(nd, g * p), ab_im.reshape(nd, g * p)], axis=1)
    a_bc = jnp.broadcast_to(a_bc[:, :, None, :], (nd, 2, nb, g * p))
    return a_bc, bsb, csb


def _head_scale(x):
    return lax.rsqrt(jnp.sum(x * x, axis=-1, keepdims=True) * (1.0 / QK_HEAD) + EPS)


def _mla_prep_lat_kernel(cq_ref, ckv_ref, krp_ref, krs_ref, qn_ref, kvn_ref, wq_ref, wqs_ref, wk_ref, wv_ref,
                         qc_ref, qs_ref, kc_ref, ks_ref, q_ref, k_ref, v_ref):
    ckvn = (_rms(ckv_ref[...]) * kvn_ref[...]).astype(BF16)
    v_ref[...] = _dot(ckvn, wv_ref[...]).astype(v_ref.dtype)
    kn = _dot(ckvn, wk_ref[...])
    krp = krp_ref[...]
    k_swap = krs_ref[...] * ks_ref[...]
    kc = kc_ref[...]
    for h in range(N_HEADS):
        sl = slice(h * HEAD_PAD, (h + 1) * HEAD_PAD)
        kh = kn[:, sl] + krp
        k_ref[:, sl] = (_head_scale(kh) * (kh * kc + k_swap)).astype(k_ref.dtype)
    cqn = (_rms(cq_ref[...]) * qn_ref[...]).astype(BF16)
    qn = _dot(cqn, wq_ref[...])
    q_swap = _dot(cqn, wqs_ref[...])
    qc = qc_ref[...]
    qs = qs_ref[...]
    for h in range(N_HEADS):
        sl = slice(h * HEAD_PAD, (h + 1) * HEAD_PAD)
        qh = qn[:, sl]
        scale = _head_scale(qh) * (QK_HEAD ** -0.5 * LOG2_E)
        q_ref[:, sl] = (scale * (qh * qc + q_swap[:, sl] * qs)).astype(q_ref.dtype)


def _mla_prep_ctx_kernel(ckv_ref, krp_ref, kvn_ref, wk_ref, wv_ref, kg_ref, k_ref, v_ref):
    ckvn = (_rms(ckv_ref[...]) * kvn_ref[...]).astype(BF16)
    v_ref[...] = _dot(ckvn, wv_ref[...]).astype(v_ref.dtype)
    kn = _dot(ckvn, wk_ref[...])
    krp = krp_ref[...]
    kg = kg_ref[...]
    for h in range(N_HEADS):
        sl = slice(h * HEAD_PAD, (h + 1) * HEAD_PAD)
        kh = kn[:, sl] + krp
        k_ref[:, sl] = (_head_scale(kh) * (kh * kg)).astype(k_ref.dtype)


def _mla_prep(kernel_fn, name, tok_args, const_args, table_args, n_q_out, tm):
    b, l, _ = tok_args[0].shape
    tok = lambda i, j: (i, j, 0)
    specs = [pl.BlockSpec((None, tm, a.shape[-1]), tok) for a in tok_args]
    specs += [pl.BlockSpec(a.shape, lambda i, j: (0, 0)) for a in const_args]
    specs += [pl.BlockSpec((tm, HEAD_PAD), lambda i, j: (j, 0)) for _ in table_args]
    hk = N_HEADS * HEAD_PAD
    hv = N_HEADS * V_HEAD
    widths = [hk] * n_q_out + [hk, hv]
    return pl.pallas_call(
        kernel_fn,
        grid=(b, l // tm),
        in_specs=specs,
        out_specs=[pl.BlockSpec((None, tm, w), tok) for w in widths],
        out_shape=[jax.ShapeDtypeStruct((b, l, w), BF16) for w in widths],
        compiler_params=_cparams(("arbitrary", "arbitrary")),
        name=name,
    )(*tok_args, *const_args, *table_args)


def _attn_kernel(q_ref, kl_ref, kc_ref, vl_ref, vc_ref, o_ref, *, heads):
    nt = (((1,), (1,)), ((), ()))
    for pr in range(heads // 2):
        vsl = slice(pr * 2 * V_HEAD, (pr + 1) * 2 * V_HEAD)
        outs = []
        for hh in range(2):
            h = 2 * pr + hh
            sl = slice(h * HEAD_PAD, (h + 1) * HEAD_PAD)
            q = q_ref[:, sl]
            s_l = lax.dot_general(q, kl_ref[:, sl], nt, preferred_element_type=F32)
            s_c = lax.dot_general(q, kc_ref[:, sl], nt, preferred_element_type=F32)
            m = jnp.maximum(jnp.max(s_l, axis=-1, keepdims=True), jnp.max(s_c, axis=-1, keepdims=True))
            p_l = jnp.exp2(s_l - m)
            p_c = jnp.exp2(s_c - m)
            den = jnp.sum(p_l, axis=-1, keepdims=True) + jnp.sum(p_c, axis=-1, keepdims=True)
            o = _dot(p_l.astype(BF16), vl_ref[:, vsl]) + _dot(p_c.astype(BF16), vc_ref[:, vsl])
            outs.append(o / den)
        lane = lax.broadcasted_iota(I32, outs[0].shape, 1)
        o_ref[:, vsl] = jnp.where(lane < V_HEAD, outs[0], outs[1]).astype(o_ref.dtype)


def _attention(q, k_l, k_c, v_l, v_c, tq, heads):
    b, l, _ = q.shape
    lc = k_c.shape[1]
    hp = N_HEADS // heads
    kw = heads * HEAD_PAD
    vw = heads * V_HEAD
    return pl.pallas_call(
        functools.partial(_attn_kernel, heads=heads),
        grid=(b, hp, l // tq),
        in_specs=[
            pl.BlockSpec((None, tq, kw), lambda i, h, j: (i, j, h)),
            pl.BlockSpec((None, l, kw), lambda i, h, j: (i, 0, h)),
            pl.BlockSpec((None, lc, kw), lambda i, h, j: (i, 0, h)),
            pl.BlockSpec((None, l, vw), lambda i, h, j: (i, 0, h)),
            pl.BlockSpec((None, lc, vw), lambda i, h, j: (i, 0, h)),
        ],
        out_specs=pl.BlockSpec((None, tq, vw), lambda i, h, j: (i, j, h)),
        out_shape=jax.ShapeDtypeStruct((b, l, N_HEADS * V_HEAD), BF16),
        compiler_params=_cparams(("arbitrary", "arbitrary", "arbitrary")),
        name="mla_attention",
    )(q, k_l, k_c, v_l, v_c)


def _mix_cd_kernel(yf_ref, yb_ref, o_ref, x_ref, g1_ref, sh2_ref, sc2_ref,
                   wglu_ref, wout_ref, wr_ref, br_ref, xl_ref, xm_ref, rt_ref):
    tm = x_ref.shape[0]
    n_slab, rows, _ = yf_ref.shape
    nb = rows // tm
    bi = pl.program_id(1)
    dc = n_slab * LANE
    pick = lambda ref, s: ref[s, pl.ds(bi, tm, stride=nb), :]
    y = jnp.concatenate([pick(yf_ref, s) + pick(yb_ref, s) for s in range(n_slab)], axis=1)
    g = jax.nn.gelu(y)
    s5 = (g * _sigmoid(_dot(g.astype(BF16), wglu_ref[...]))).astype(BF16)
    y_mix = _dot(s5, wout_ref[0:dc, :]) + _dot(o_ref[...], wout_ref[dc:, :])
    _residual_and_route(x_ref[...], y_mix, g1_ref[...], sh2_ref[...], sc2_ref[...], wr_ref, br_ref,
                        xl_ref, xm_ref, rt_ref)


def _mix_cd(yf, yb, o, x, g1, sh2, sc2, wglu, wout, wr, br, tm):
    b, l, d = x.shape
    n_slab = yf.shape[0]
    tok = lambda j, i: (i, j, 0)
    per_b = lambda j, i: (i, 0, 0)
    const2 = lambda j, i: (0, 0)
    tm_blk = pl.BlockSpec((n_slab, tm * b, LANE), lambda j, i: (0, j, 0))
    return pl.pallas_call(
        _mix_cd_kernel,
        grid=(l // tm, b),
        in_specs=[
            tm_blk,
            tm_blk,
            pl.BlockSpec((None, tm, o.shape[-1]), tok),
            pl.BlockSpec((None, tm, d), tok),
            pl.BlockSpec((None, 1, d), per_b),
            pl.BlockSpec((None, 1, d), per_b),
            pl.BlockSpec((None, 1, d), per_b),
            pl.BlockSpec(wglu.shape, const2),
            pl.BlockSpec(wout.shape, const2),
            pl.BlockSpec(wr.shape, const2),
            pl.BlockSpec((1, LANE), const2),
        ],
        out_specs=[
            pl.BlockSpec((None, tm, d), tok),
            pl.BlockSpec((tm * TOKEN_ROWS, LANE), lambda j, i: (i * (l // tm) + j, 0)),
            pl.BlockSpec((None, tm, LANE), tok),
        ],
        out_shape=[
            jax.ShapeDtypeStruct((b, l, d), F32),
            jax.ShapeDtypeStruct((b * l * TOKEN_ROWS, LANE), F32),
            jax.ShapeDtypeStruct((b, l, LANE), F32),
        ],
        compiler_params=_cparams(("arbitrary", "arbitrary")),
        name="mix_cd",
    )(yf, yb, o, x, g1, sh2, sc2, wglu, wout, wr, br)


def _router_weights(w_grp, b_grp, w_exp, b_exp):
    d = w_grp.shape[0]
    pad = LANE - N_GROUPS - N_EXPERTS
    wr = jnp.concatenate([w_grp, w_exp, jnp.zeros((d, pad), F32)], axis=1)
    br = jnp.concatenate([b_grp, b_exp, jnp.zeros((pad,), F32)]).reshape(1, LANE)
    return wr, br


def _pad_heads(w, width):
    k = w.shape[0]
    w = w.reshape(k, N_HEADS, width)
    return jnp.pad(w, ((0, 0), (0, 0), (0, HEAD_PAD - width))).reshape(k, N_HEADS * HEAD_PAD)


def _rope_tables(l):
    rows = l // GRID_W
    row = jnp.repeat(jnp.arange(rows, dtype=F32), GRID_W)
    col = jnp.tile(jnp.arange(GRID_W, dtype=F32), rows)
    inv_freq = ROPE_BASE ** (-jnp.arange(AXIS_PAIRS, dtype=F32) / AXIS_PAIRS)
    ar, ac = row[:, None] * inv_freq, col[:, None] * inv_freq
    one = jnp.ones((l, QK_NOPE), F32)
    zn = jnp.zeros((l, QK_NOPE), F32)
    zp = jnp.zeros((l, HEAD_PAD - QK_HEAD), F32)
    cos = jnp.concatenate([one, jnp.cos(ar), jnp.cos(ar), jnp.cos(ac), jnp.cos(ac), zp], axis=1)
    sin = jnp.concatenate([zn, jnp.sin(ar), jnp.sin(ar), jnp.sin(ac), jnp.sin(ac), zp], axis=1)
    return cos, sin


def _rot_swap(t, signed):
    a = AXIS_PAIRS
    s = -1.0 if signed else 1.0
    return jnp.concatenate([s * t[..., a:2 * a], t[..., 0:a], s * t[..., 3 * a:4 * a], t[..., 2 * a:3 * a]], axis=-1)


def _head_lanes(rot):
    pad = [(0, 0)] * (rot.ndim - 1) + [(QK_NOPE, HEAD_PAD - QK_HEAD)]
    return jnp.pad(rot, pad)


def _split_mod(mod_row, b, d):
    parts = jnp.split(mod_row, 6, axis=-1)
    return [jnp.broadcast_to(p.reshape(-1, 1, d), (b, 1, d)) for p in parts]


def kernel(x, c, ctx, c_ctx, ada_w, ada_b, ab_w_in, sgu_norm, sgu_w, sgu_b, conv_w, ab_w_out, cd_w_in, s5_a_re, s5_a_im, s5_log_dt, s5_b_re, s5_b_im, s5_c_re, s5_c_im, s5_d, s5_w_glu, mla_q_norm, mla_kv_norm, mla_w_uq, mla_w_uk, mla_w_uv, mla_qn_gain, mla_kn_gain, cd_w_out, moe_w_grp, moe_b_grp, moe_w_exp, moe_b_exp, moe_w_gate, moe_w_up, moe_w_down):
    b, l, d = x.shape
    lc = ctx.shape[1]
    depth = ada_w.shape[0]
    assert depth == 2, "layer 0 = gated-MLP/conv mixers, layer 1 = S5/attention mixers"
    assert d == TOKEN_ROWS * LANE, "token-tile layout: one vreg tile per token"
    tm = min(TOKEN_TILE, l)
    tmc = min(TOKEN_TILE, lc)

    mod_rows = 2 * SUBLANE
    cvec = jnp.zeros((mod_rows, d), F32).at[:b].set(c).at[b].set(c_ctx)
    mod = _ada(cvec, ada_w, ada_b)

    sh1, sc1, g1, sh2, sc2, g2 = _split_mod(mod[0, :b], b, d)
    csh1, csc1, cg1, csh2, csc2, cg2 = _split_mod(mod[0, b:b + 1], b, d)
    w_in = ab_w_in[0].astype(BF16)
    wr, br = _router_weights(moe_w_grp[0], moe_b_grp[0], moe_w_exp[0], moe_b_exp[0])
    sgub = jnp.repeat(sgu_b[0].T, d // G_A, axis=1)
    mix_args = (sgu_norm[0].reshape(1, d), sgu_w[0].astype(BF16), sgub, conv_w[0],
                ab_w_out[0].astype(BF16), wr, br)
    z, p = _inproj_ab(x, sh1, sc1, w_in, tm)
    xl, xm, rt = _mix_ab(z, p, x, g1, sh2, sc2, *mix_args, tm)
    zc, pc = _inproj_ab(ctx, csh1, csc1, w_in, tmc)
    xc, xmc, rtc = _mix_ab(zc, pc, ctx, cg1, csh2, csc2, *mix_args, tmc)
    xl, xc = _hier_moe([(xl, xm, rt, g2), (xc, xmc, rtc, cg2)], 0, moe_w_gate, moe_w_up, moe_w_down)

    sh1, sc1, g1, sh2, sc2, g2 = _split_mod(mod[1, :b], b, d)
    csh1, csc1, _, _, _, _ = _split_mod(mod[1, b:b + 1], b, d)
    d_c = s5_d.shape[-1]
    q_lora = mla_q_norm.shape[-1]
    kv_lora = mla_kv_norm.shape[-1]
    w_cd = cd_w_in[0]
    o_kr = d_c + q_lora + kv_lora
    w_kr = _head_lanes(w_cd[:, o_kr:])
    w_krs = _head_lanes(_rot_swap(w_cd[:, o_kr:], True))
    w_cd = jnp.concatenate([w_cd[:, :o_kr], w_kr, w_krs], axis=1).astype(BF16)
    widths = (d_c, q_lora, kv_lora, HEAD_PAD, HEAD_PAD)
    u_l, cq_l, ckv_l, krp_l, krs_l = _inproj_cd(xl, sh1, sc1, w_cd, widths, tm)
    u_c, _, ckv_c, krp_c, _ = _inproj_cd(xc, csh1, csc1, w_cd, widths, tmc)

    ab_re, ab_im, bb_re, bb_im = _s5_params(s5_a_re[0], s5_a_im[0], s5_log_dt[0], s5_b_re[0], s5_b_im[0])
    a_bc, bsb, csb = _s5_block_matrices(ab_re, ab_im, bb_re, bb_im, s5_c_re[0], s5_c_im[0], b)
    yf, yb = _s5_scan(u_c, u_l, a_bc, bsb, csb, s5_d[0].reshape(1, d_c), b)

    wq = _pad_heads(mla_w_uq[0], QK_HEAD).astype(BF16)
    wk = _pad_heads(mla_w_uk[0], QK_NOPE).astype(BF16)
    wv = mla_w_uv[0].astype(BF16)
    w_uq = mla_w_uq[0].reshape(q_lora, N_HEADS, QK_HEAD)
    wqs = _head_lanes(_rot_swap(w_uq[..., QK_NOPE:], True)).reshape(q_lora, N_HEADS * HEAD_PAD).astype(BF16)
    qg, kg = mla_qn_gain[0], mla_kn_gain[0]
    full = lambda g: jnp.pad(g, (0, HEAD_PAD - QK_HEAD)).reshape(1, HEAD_PAD)
    partner = lambda g: _head_lanes(_rot_swap(g[QK_NOPE:], False)).reshape(1, HEAD_PAD)
    cos, sin = _rope_tables(l)
    tables = (cos * full(qg), sin * partner(qg), cos * full(kg), sin * partner(kg))
    qn = mla_q_norm[0].reshape(1, q_lora)
    kvn = mla_kv_norm[0].reshape(1, kv_lora)
    q_l, k_l, v_l = _mla_prep(_mla_prep_lat_kernel, "mla_prep_lat", (cq_l, ckv_l, krp_l, krs_l),
                              (qn, kvn, wq, wqs, wk, wv), tables, 1, tm)
    k_c, v_c = _mla_prep(_mla_prep_ctx_kernel, "mla_prep_ctx", (ckv_c, krp_c), (kvn, wk, wv, full(kg)), (), 0, tmc)
    o_l = _attention(q_l, k_l, k_c, v_l, v_c, ATTN_Q_TILE, ATTN_HEADS)

    wr, br = _router_weights(moe_w_grp[1], moe_b_grp[1], moe_w_exp[1], moe_b_exp[1])
    xl, xm, rt = _mix_cd(yf, yb, o_l, xl, g1, sh2, sc2,
                         s5_w_glu[0].astype(BF16), cd_w_out[0].astype(BF16), wr, br, MIX_CD_TILE)
    (xl,) = _hier_moe([(xl, xm, rt, g2)], 1, moe_w_gate, moe_w_up, moe_w_down)
    return xl
```

```python
import functools
import math

import jax
import jax.numpy as jnp
from jax import lax
from jax.experimental import pallas as pl
from jax.experimental.pallas import tpu as pltpu

F32 = jnp.float32
BF16 = jnp.bfloat16
I32 = jnp.int32
HIGHEST = lax.Precision.HIGHEST

EPS = 1e-6
LOG2_E = math.log2(math.e)
GRID_W = 64
CHUNK = 128
G_A = 8
CONV_W = 3
S5_GROUP = 16
S5_STATE = 64
N_HEADS = 8
QK_NOPE = 64
QK_ROPE = 32
QK_HEAD = QK_NOPE + QK_ROPE
V_HEAD = 64
AXIS_PAIRS = QK_ROPE // 4
ROPE_BASE = 10000.0
N_GROUPS = 4
EXPERTS_PER_GROUP = 8
N_EXPERTS = N_GROUPS * EXPERTS_PER_GROUP
TOP_K = 2

LANE = 128
SUBLANE = 8
HEAD_PAD = LANE
TOKEN_TILE = 512
MIX_SUBTILES = 2
MIX_CD_TILE = 256
MOE_TILE = 512
COMBINE_TILE = 512
PLAN_BLOCK = 1024
MAX_TILES = 256
TOKEN_ROWS = SUBLANE
ATTN_Q_TILE = 512
ATTN_HEADS = 8
S5_STEPS = 64
S5_SUPER = LANE // S5_GROUP
VMEM_LIMIT = 56 * 1024 * 1024


def _cparams(sem):
    return pltpu.CompilerParams(dimension_semantics=sem, vmem_limit_bytes=VMEM_LIMIT)


def _rms(x):
    return x * lax.rsqrt(jnp.mean(x * x, axis=-1, keepdims=True) + EPS)


def _sigmoid(x):
    return 1.0 / (1.0 + jnp.exp(-x))


def _dot(a, b):
    return jnp.dot(a, b, preferred_element_type=F32)


def _ada_kernel(c_ref, w_ref, b_ref, o_ref):
    c = c_ref[...]
    s = c * _sigmoid(c)
    o_ref[...] = jnp.dot(s, w_ref[...], precision=HIGHEST, preferred_element_type=F32) + b_ref[...]


def _ada(cvec, ada_w, ada_b):
    depth, d, n = ada_w.shape
    rows = cvec.shape[0]
    tn = 1024
    return pl.pallas_call(
        _ada_kernel,
        grid=(depth, n // tn),
        in_specs=[
            pl.BlockSpec((rows, d), lambda i, j: (0, 0)),
            pl.BlockSpec((None, d, tn), lambda i, j: (i, 0, j)),
            pl.BlockSpec((None, 1, tn), lambda i, j: (i, 0, j)),
        ],
        out_specs=pl.BlockSpec((None, rows, tn), lambda i, j: (i, 0, j)),
        out_shape=jax.ShapeDtypeStruct((depth, rows, n), F32),
        compiler_params=_cparams(("arbitrary", "arbitrary")),
        name="ada_mod",
    )(cvec, ada_w, ada_b.reshape(depth, 1, n))


def _route(xm, wr_ref, br_ref):
    hi = xm.astype(BF16)
    lo = (xm - hi.astype(F32)).astype(BF16)
    w = wr_ref[...]
    w_hi = w.astype(BF16)
    w_lo = (w - w_hi.astype(F32)).astype(BF16)
    logits = _dot(jnp.concatenate([hi, lo, hi], axis=1), jnp.concatenate([w_hi, w_hi, w_lo], axis=0)) + br_ref[...]
    lane = lax.broadcasted_iota(I32, logits.shape, 1).astype(F32)
    neg = jnp.float32(-1e30)
    big = jnp.float32(1e6)
    is_grp = lane < N_GROUPS
    gl = jnp.where(is_grp, logits, neg)
    gmax = jnp.max(gl, axis=-1, keepdims=True)
    gsum = jnp.sum(jnp.where(is_grp, jnp.exp(gl - gmax), 0.0), axis=-1, keepdims=True)
    p_top = 1.0 / gsum
    grp = jnp.min(jnp.where(gl == gmax, lane, big), axis=-1, keepdims=True)
    eidx = lane - N_GROUPS
    in_grp = (eidx >= grp * EXPERTS_PER_GROUP) & (eidx < (grp + 1.0) * EXPERTS_PER_GROUP)
    el = jnp.where(in_grp, logits, neg)
    m1 = jnp.max(el, axis=-1, keepdims=True)
    i1 = jnp.min(jnp.where(el == m1, lane, big), axis=-1, keepdims=True)
    el2 = jnp.where(lane == i1, neg, el)
    m2 = jnp.max(el2, axis=-1, keepdims=True)
    i2 = jnp.min(jnp.where(el2 == m2, lane, big), axis=-1, keepdims=True)
    t = jnp.exp(m2 - m1)
    w1 = p_top / (1.0 + t)
    w2 = p_top * t / (1.0 + t)
    e1 = i1 - N_GROUPS
    e2 = i2 - N_GROUPS
    return jnp.where(lane == 0, e1, jnp.where(lane == 1, e2, jnp.where(lane == 2, w1, jnp.where(lane == 3, w2, 0.0))))


def _residual_and_route(x, y, g1, sh2, sc2, wr_ref, br_ref, xl_ref, xm_ref, rt_ref):
    xl = x + g1 * y
    xl_ref[...] = xl
    xm = _rms(xl) * (1.0 + sc2) + sh2
    _store_token_tiles(xm_ref, xm)
    rt_ref[...] = _route(xm, wr_ref, br_ref)


def _inproj_ab_kernel(x_ref, sh_ref, sc_ref, w_ref, z_ref, p_ref):
    d = x_ref.shape[-1]
    h = (_rms(x_ref[...]) * (1.0 + sc_ref[...]) + sh_ref[...]).astype(BF16)
    for j in range(3):
        z_ref[:, j * d:(j + 1) * d] = _dot(h, w_ref[:, j * d:(j + 1) * d]).astype(BF16)
    gate_c = _dot(h, w_ref[:, 3 * d:4 * d])
    xb = _dot(h, w_ref[:, 4 * d:5 * d])
    p_ref[...] = gate_c * xb


def _inproj_ab(x, sh, sc, w_bf16, tm):
    b, l, d = x.shape
    n = w_bf16.shape[1]
    return pl.pallas_call(
        _inproj_ab_kernel,
        grid=(b, l // tm),
        in_specs=[
            pl.BlockSpec((None, tm, d), lambda i, j: (i, j, 0)),
            pl.BlockSpec((None, 1, d), lambda i, j: (i, 0, 0)),
            pl.BlockSpec((None, 1, d), lambda i, j: (i, 0, 0)),
            pl.BlockSpec((d, n), lambda i, j: (0, 0)),
        ],
        out_specs=[
            pl.BlockSpec((None, tm, 3 * d), lambda i, j: (i, j, 0)),
            pl.BlockSpec((None, tm, d), lambda i, j: (i, j, 0)),
        ],
        out_shape=[
            jax.ShapeDtypeStruct((b, l, 3 * d), BF16),
            jax.ShapeDtypeStruct((b, l, d), F32),
        ],
        compiler_params=_cparams(("arbitrary", "arbitrary")),
        name="inproj_ab",
    )(x, sh, sc, w_bf16)


def _mix_ab_kernel(z_ref, p_ref, pprev_ref, pnext_ref, x_ref, g1_ref, sh2_ref, sc2_ref,
                   sgun_ref, sguw_ref, sgub_ref, convw_ref, wout_ref, wr_ref, br_ref,
                   xl_ref, xm_ref, rt_ref, pbuf, ycat):
    i = pl.program_id(1)
    nt = pl.num_programs(1)
    tm, d = x_ref.shape
    cg = d // G_A
    pbuf[SUBLANE:SUBLANE + tm, :] = p_ref[...]
    pbuf[0:SUBLANE, :] = jnp.where(i > 0, pprev_ref[...], 0.0)
    pbuf[SUBLANE + tm:2 * SUBLANE + tm, :] = jnp.where(i < nt - 1, pnext_ref[...], 0.0)
    ts = tm // (MIX_SUBTILES if tm % (MIX_SUBTILES * CHUNK) == 0 else 1)
    for t0 in range(0, tm, ts):
        rows = slice(t0, t0 + ts)
        v = jax.nn.gelu(z_ref[rows, d:2 * d].astype(F32))
        vc = (_rms(v) * sgun_ref[...]).astype(BF16)
        for c in range(ts // CHUNK):
            lr = c * CHUNK
            r0 = t0 + lr
            cols = [_dot(sguw_ref[g], vc[lr:lr + CHUNK, g * cg:(g + 1) * cg]) for g in range(G_A)]
            s = jnp.concatenate(cols, axis=1) + sgub_ref[...]
            u = jax.nn.gelu(z_ref[r0:r0 + CHUNK, 0:d].astype(F32))
            ycat[r0:r0 + CHUNK, 0:d] = (u * s).astype(BF16)
        conv = (convw_ref[0:1, :] * pbuf[SUBLANE - 1 + t0:SUBLANE - 1 + t0 + ts, :]
                + convw_ref[1:2, :] * pbuf[SUBLANE + t0:SUBLANE + t0 + ts, :]
                + convw_ref[2:3, :] * pbuf[SUBLANE + 1 + t0:SUBLANE + 1 + t0 + ts, :])
        ycat[rows, d:2 * d] = (z_ref[rows, 2 * d:3 * d].astype(F32) * conv).astype(BF16)
        y = _dot(ycat[rows, :], wout_ref[...])
        _residual_and_route(x_ref[rows, :], y, g1_ref[...], sh2_ref[...], sc2_ref[...], wr_ref, br_ref,
                            xl_ref.at[rows, :], xm_ref.at[pl.ds(t0 * TOKEN_ROWS, ts * TOKEN_ROWS), :],
                            rt_ref.at[rows, :])


def _mix_ab(z, p, x, g1, sh2, sc2, sgun, sguw, sgub, convw, wout, wr, br, tm):
    b, l, d = x.shape
    hb = tm // SUBLANE
    nhb = l // SUBLANE
    tok = lambda i, j: (i, j, 0)
    per_b = lambda i, j: (i, 0, 0)
    const2 = lambda i, j: (0, 0)
    return pl.pallas_call(
        _mix_ab_kernel,
        grid=(b, l // tm),
        in_specs=[
            pl.BlockSpec((None, tm, 3 * d), tok),
            pl.BlockSpec((None, tm, d), tok),
            pl.BlockSpec((None, SUBLANE, d), lambda i, j: (i, jnp.maximum(j * hb - 1, 0), 0)),
            pl.BlockSpec((None, SUBLANE, d), lambda i, j: (i, jnp.minimum((j + 1) * hb, nhb - 1), 0)),
            pl.BlockSpec((None, tm, d), tok),
            pl.BlockSpec((None, 1, d), per_b),
            pl.BlockSpec((None, 1, d), per_b),
            pl.BlockSpec((None, 1, d), per_b),
            pl.BlockSpec((1, d), const2),
            pl.BlockSpec((G_A, CHUNK, CHUNK), lambda i, j: (0, 0, 0)),
            pl.BlockSpec((CHUNK, d), const2),
            pl.BlockSpec((CONV_W, d), const2),
            pl.BlockSpec((2 * d, d), const2),
            pl.BlockSpec(wr.shape, const2),
            pl.BlockSpec((1, LANE), const2),
        ],
        out_specs=[
            pl.BlockSpec((None, tm, d), tok),
            pl.BlockSpec((tm * TOKEN_ROWS, LANE), lambda i, j: (i * (l // tm) + j, 0)),
            pl.BlockSpec((None, tm, LANE), tok),
        ],
        out_shape=[
            jax.ShapeDtypeStruct((b, l, d), F32),
            jax.ShapeDtypeStruct((b * l * TOKEN_ROWS, LANE), F32),
            jax.ShapeDtypeStruct((b, l, LANE), F32),
        ],
        scratch_shapes=[
            pltpu.VMEM((tm + 2 * SUBLANE, d), F32),
            pltpu.VMEM((tm, 2 * d), BF16),
        ],
        compiler_params=_cparams(("arbitrary", "arbitrary")),
        name="mix_ab",
    )(z, p, p, p, x, g1, sh2, sc2, sgun, sguw, sgub, convw, wout, wr, br)


def _store_token_tiles(ref, val):
    tm, d = val.shape
    rpt = d // LANE
    for j in range(rpt):
        ref[pl.ds(j, tm, stride=rpt), :] = val[:, j * LANE:(j + 1) * LANE]


def _load_token_tile_col(ref, tm, rpt, j):
    return ref[pl.ds(j, tm, stride=rpt), :]


def _plan_kernel(rt_ref, tri_ref, pos_ref, te_ref, tv_ref, pad_ref, run, *, tm):
    ps = pl.program_id(0)
    i = pl.program_id(1)
    blk = rt_ref.shape[0]
    ne = N_EXPERTS

    @pl.when(jnp.logical_and(ps == 0, i == 0))
    def _():
        run[...] = jnp.zeros_like(run)

    @pl.when(jnp.logical_and(ps == 1, i == 0))
    def _():
        counts = run[...]
        padded = jnp.floor((counts + (tm - 1)) * (1.0 / tm)) * tm
        r = lax.broadcasted_iota(I32, (ne, ne), 0)
        c = lax.broadcasted_iota(I32, (ne, ne), 1)
        starts = jnp.dot((c < r).astype(F32), padded, precision=HIGHEST, preferred_element_type=F32)
        nt = te_ref.shape[1]
        ends = jnp.concatenate([(starts + padded) * (1.0 / tm)] * (nt // LANE), axis=1)
        t = lax.broadcasted_iota(I32, (ne, nt), 1).astype(F32)
        te = jnp.sum((t >= ends).astype(F32), axis=0, keepdims=True)
        total = jnp.max(ends, axis=0, keepdims=True)
        te_ref[...] = jnp.minimum(te, ne - 1.0).astype(I32)
        tv_ref[...] = (t[0:1, :] < total).astype(I32)
        lane = lax.broadcasted_iota(I32, (ne, LANE), 1)
        pad_ref[...] = jnp.where(lane == 0, starts + counts,
                                 jnp.where(lane == 1, padded - counts, 0.0)).astype(I32)
        run[...] = starts

    slab_t = rt_ref[...].T
    ex = lax.broadcasted_iota(I32, (ne, blk), 0).astype(F32)
    oh1 = ex == slab_t[0:1, :]
    oh2 = ex == slab_t[1:2, :]
    oh = jnp.where(jnp.logical_or(oh1, oh2), 1.0, 0.0)
    @pl.when(ps == 0)
    def _():
        pos_ref[...] = jnp.zeros_like(pos_ref)

    @pl.when(ps == 1)
    def _():
        prefix = _dot(oh.astype(BF16), tri_ref[...]) + run[:, 0:1]
        pos1 = jnp.sum(jnp.where(oh1, prefix, 0.0), axis=0, keepdims=True)
        pos2 = jnp.sum(jnp.where(oh2, prefix, 0.0), axis=0, keepdims=True)
        pos_ref[...] = jnp.concatenate([pos1, pos2], axis=0).astype(I32)

    run[...] = run[...] + jnp.sum(oh, axis=1, keepdims=True)


def _moe_plan(rt_all, tm):
    n = rt_all.shape[0]
    blk = PLAN_BLOCK
    nb = n // blk
    tri = jnp.triu(jnp.ones((blk, blk), BF16), k=1)
    return pl.pallas_call(
        functools.partial(_plan_kernel, tm=tm),
        grid=(2, nb),
        in_specs=[
            pl.BlockSpec((blk, LANE), lambda p, i: (i, 0)),
            pl.BlockSpec((blk, blk), lambda p, i: (0, 0)),
        ],
        out_specs=[
            pl.BlockSpec((None, TOP_K, blk), lambda p, i: (jnp.where(p == 0, nb, i), 0, 0)),
            pl.BlockSpec((1, MAX_TILES), lambda p, i: (0, 0)),
            pl.BlockSpec((1, MAX_TILES), lambda p, i: (0, 0)),
            pl.BlockSpec((N_EXPERTS, LANE), lambda p, i: (0, 0)),
        ],
        out_shape=[
            jax.ShapeDtypeStruct((nb + 1, TOP_K, blk), I32),
            jax.ShapeDtypeStruct((1, MAX_TILES), I32),
            jax.ShapeDtypeStruct((1, MAX_TILES), I32),
            jax.ShapeDtypeStruct((N_EXPERTS, LANE), I32),
        ],
        scratch_shapes=[pltpu.VMEM((N_EXPERTS, LANE), F32)],
        compiler_params=_cparams(("arbitrary", "arbitrary")),
        name="moe_plan",
    )(rt_all, tri)


def _dispatch_kernel(ps_ref, pl_ref, tv_ref, pos_ref, *rest, blocks, n_tiles):
    xm_refs = rest[:len(blocks)]
    xs_hbm, zbuf, sem = rest[len(blocks):]
    i = pl.program_id(0)
    tt = pos_ref.shape[1]
    rpt = xm_refs[0].shape[0] // tt
    tile_rows = zbuf.shape[0]

    @pl.when(i == 0)
    def _():
        zbuf[...] = jnp.zeros_like(zbuf)

        def gap_copies(e, wait):
            base = ps_ref[e]
            n = pl_ref[e]
            bit = tile_rows // (2 * rpt)
            while bit >= 1:
                lower = n & (bit - 1)

                @pl.when((n & bit) != 0)
                def _(lower=lower, bit=bit):
                    dst = xs_hbm.at[pl.ds((base + lower) * rpt, bit * rpt)]
                    cp = pltpu.make_async_copy(zbuf.at[pl.ds(0, bit * rpt)], dst, sem.at[1])
                    cp.wait() if wait else cp.start()
                bit //= 2

        def tile_copy(t, wait):
            @pl.when(tv_ref[t] != 1)
            def _():
                dst = xs_hbm.at[pl.ds(pl.multiple_of(t * tile_rows, tile_rows), tile_rows)]
                cp = pltpu.make_async_copy(zbuf, dst, sem.at[1])
                cp.wait() if wait else cp.start()

        for wait in (False, True):
            lax.fori_loop(0, N_EXPERTS, lambda e, c, w=wait: (gap_copies(e, w), c)[1], 0)
            lax.fori_loop(0, n_tiles, lambda t, c, w=wait: (tile_copy(t, w), c)[1], 0)

    def scatter(xm_ref):
        def issue(r, carry):
            src = xm_ref.at[pl.ds(pl.multiple_of(r * rpt, rpt), rpt)]
            for k in range(TOP_K):
                dst = xs_hbm.at[pl.ds(pl.multiple_of(pos_ref[k, r] * rpt, rpt), rpt)]
                pltpu.make_async_copy(src, dst, sem.at[0]).start(priority=k)
            return carry
        lax.fori_loop(0, tt, issue, 0, unroll=8)
        for k in range(TOP_K):
            pltpu.make_async_copy(xm_ref, xs_hbm.at[pl.ds(0, tt * rpt)], sem.at[0]).wait()

    lo = 0
    for xm_ref, nblk in zip(xm_refs, blocks):
        pl.when(jnp.logical_and(i >= lo, i < lo + nblk))(functools.partial(scatter, xm_ref))
        lo += nblk


def _dispatch(pad_start, pad_len, valid, pos, xm_streams, n_tiles, tm):
    tt = pos.shape[-1]
    rpt = TOKEN_ROWS
    blocks = tuple(xm.shape[0] // (rpt * tt) for xm in xm_streams)
    in_specs = [pl.BlockSpec((None, TOP_K, tt), lambda i, a, b, c: (i, 0, 0), memory_space=pltpu.SMEM)]
    lo = 0
    for nblk in blocks:
        in_specs.append(pl.BlockSpec(
            (tt * rpt, LANE), lambda i, a, b, c, lo=lo, nblk=nblk: (jnp.clip(i - lo, 0, nblk - 1), 0)))
        lo += nblk
    return pl.pallas_call(
        functools.partial(_dispatch_kernel, blocks=blocks, n_tiles=n_tiles),
        grid_spec=pltpu.PrefetchScalarGridSpec(
            num_scalar_prefetch=3,
            grid=(sum(blocks),),
            in_specs=in_specs,
            out_specs=pl.BlockSpec(memory_space=pl.ANY),
            scratch_shapes=[pltpu.VMEM((tm * rpt, LANE), F32), pltpu.SemaphoreType.DMA((2,))],
        ),
        out_shape=jax.ShapeDtypeStruct((n_tiles * tm * rpt, LANE), F32),
        compiler_params=_cparams(("arbitrary",)),
        name="moe_dispatch",
    )(pad_start, pad_len, valid, pos, *xm_streams)


def _moe_kernel(te_ref, tv_ref, x_ref, wg_hbm, wu_hbm, wd_hbm, y_ref,
                wg_f, wu_f, wd_f, wg_s, wu_s, wd_s, slot_ref, sem, *, layer, n_tiles):
    t = pl.program_id(0)
    rpt = TOKEN_ROWS
    tm = x_ref.shape[0] // rpt

    def weight_copies(e, slot):
        return [pltpu.make_async_copy(src.at[layer, e], dst.at[slot], sem.at[slot, k])
                for k, (src, dst) in enumerate(((wg_hbm, wg_f), (wu_hbm, wu_f), (wd_hbm, wd_f)))]

    @pl.when(tv_ref[t] == 1)
    def _():
        e = te_ref[t]
        first = jnp.logical_or(t == 0, te_ref[jnp.maximum(t - 1, 0)] != e)

        @pl.when(first)
        def _():
            @pl.when(t == 0)
            def _():
                slot_ref[0] = 0
                for cp in weight_copies(e, 0):
                    cp.start()

            slot = slot_ref[0]
            for cp in weight_copies(e, slot):
                cp.wait()
            at = lambda ref, i: ref[jnp.minimum(i, n_tiles - 1)]
            same = lambda i: (i < n_tiles) & (at(tv_ref, i) == 1) & (at(te_ref, i) == e)
            t2 = lax.while_loop(same, lambda i: i + 1, t + 1)

            @pl.when((t2 < n_tiles) & (at(tv_ref, t2) == 1))
            def _():
                for cp in weight_copies(at(te_ref, t2), 1 - slot):
                    cp.start()

            wg_s[...] = wg_f[slot].astype(BF16)
            wu_s[...] = wu_f[slot].astype(BF16)
            wd_s[...] = wd_f[slot].astype(BF16)
            slot_ref[0] = 1 - slot

        x = jnp.concatenate([_load_token_tile_col(x_ref, tm, rpt, j) for j in range(rpt)], axis=1).astype(BF16)
        g = _dot(x, wg_s[...])
        u = _dot(x, wu_s[...])
        h = (g * _sigmoid(g) * u).astype(BF16)
        _store_token_tiles(y_ref, _dot(h, wd_s[...]))

    @pl.when(tv_ref[t] != 1)
    def _():
        y_ref[...] = jnp.zeros_like(y_ref)


def _moe_experts(xs, tile_e, valid, n_tiles, layer, w_gate, w_up, w_down, tm):
    _, _, d, de = w_gate.shape
    rpt = TOKEN_ROWS
    grid_spec = pltpu.PrefetchScalarGridSpec(
        num_scalar_prefetch=2,
        grid=(n_tiles,),
        in_specs=[
            pl.BlockSpec((tm * rpt, LANE), lambda t, te, tv: (t * tv[t], 0)),
            pl.BlockSpec(memory_space=pl.ANY),
            pl.BlockSpec(memory_space=pl.ANY),
            pl.BlockSpec(memory_space=pl.ANY),
        ],
        out_specs=pl.BlockSpec((tm * rpt, LANE), lambda t, te, tv: (t, 0)),
        scratch_shapes=[
            pltpu.VMEM((2, d, de), F32),
            pltpu.VMEM((2, d, de), F32),
            pltpu.VMEM((2, de, d), F32),
            pltpu.VMEM((d, de), BF16),
            pltpu.VMEM((d, de), BF16),
            pltpu.VMEM((de, d), BF16),
            pltpu.SMEM((1,), I32),
            pltpu.SemaphoreType.DMA((2, 3)),
        ],
    )
    return pl.pallas_call(
        functools.partial(_moe_kernel, layer=layer, n_tiles=n_tiles),
        grid_spec=grid_spec,
        out_shape=jax.ShapeDtypeStruct((n_tiles * tm * rpt, LANE), F32),
        compiler_params=_cparams(("arbitrary",)),
        name="moe_experts",
    )(tile_e, valid, xs, w_gate, w_up, w_down)


def _combine_kernel(pos0_ref, posn_ref, x_ref, g_ref, rt_ref, ys_hbm, o_ref, buf, sem, *, n):
    i = pl.program_id(0)
    tt, d = x_ref.shape
    rpt = TOKEN_ROWS

    def gather(pos_ref, slot):
        def issue(r, carry):
            for k in range(TOP_K):
                src = ys_hbm.at[pl.ds(pl.multiple_of(pos_ref[k, r] * rpt, rpt), rpt)]
                dst = buf.at[slot, k, pl.ds(pl.multiple_of(r * rpt, rpt), rpt)]
                pltpu.make_async_copy(src, dst, sem.at[slot, k]).start(priority=k)
            return carry
        lax.fori_loop(0, tt, issue, 0, unroll=8)

    slot = lax.rem(i, 2)

    @pl.when(i == 0)
    def _():
        gather(pos0_ref, 0)

    @pl.when(i + 1 < n)
    def _():
        gather(posn_ref, 1 - slot)

    for k in range(TOP_K):
        pltpu.make_async_copy(ys_hbm.at[pl.ds(0, tt * rpt)], buf.at[slot, k], sem.at[slot, k]).wait()
    w1 = rt_ref[:, 2:3]
    w2 = rt_ref[:, 3:4]
    for j in range(rpt):
        r1 = buf[slot, 0, pl.ds(j, tt, stride=rpt), :]
        r2 = buf[slot, 1, pl.ds(j, tt, stride=rpt), :]
        sl = slice(j * LANE, (j + 1) * LANE)
        o_ref[:, sl] = x_ref[:, sl] + g_ref[:, sl] * (w1 * r1 + w2 * r2)


def _moe_combine(x, g2, rt, pos, tok_off, ys, tt):
    b, l, d = x.shape
    rpt = TOKEN_ROWS
    n = b * l
    per_b = l // tt
    per_blk = PLAN_BLOCK // tt
    blk0 = tok_off // PLAN_BLOCK

    def pos_map(step):
        return lambda i: (blk0 + step(i) // per_blk, 0, step(i) % per_blk)

    return pl.pallas_call(
        functools.partial(_combine_kernel, n=n // tt),
        grid=(n // tt,),
        in_specs=[
            pl.BlockSpec((None, TOP_K, tt), pos_map(lambda i: i), memory_space=pltpu.SMEM),
            pl.BlockSpec((None, TOP_K, tt), pos_map(lambda i: jnp.minimum(i + 1, n // tt - 1)),
                         memory_space=pltpu.SMEM),
            pl.BlockSpec((tt, d), lambda i: (i, 0)),
            pl.BlockSpec((None, 1, d), lambda i: (i // per_b, 0, 0)),
            pl.BlockSpec((tt, LANE), lambda i: (i, 0)),
            pl.BlockSpec(memory_space=pl.ANY),
        ],
        out_specs=pl.BlockSpec((tt, d), lambda i: (i, 0)),
        out_shape=jax.ShapeDtypeStruct((n, d), F32),
        scratch_shapes=[
            pltpu.VMEM((2, TOP_K, tt * rpt, LANE), F32),
            pltpu.SemaphoreType.DMA((2, TOP_K)),
        ],
        compiler_params=_cparams(("arbitrary",)),
        name="moe_combine",
    )(pos, pos, x.reshape(n, d), g2, rt.reshape(n, LANE), ys).reshape(b, l, d)


def _hier_moe(streams, layer, w_gate, w_up, w_down):
    sizes = [s[0].shape[0] * s[0].shape[1] for s in streams]
    n = sum(sizes)
    tm = MOE_TILE
    n_tiles = (TOP_K * n) // tm + N_EXPERTS
    assert n_tiles <= MAX_TILES and all(sz % PLAN_BLOCK == 0 for sz in sizes)
    rt_all = jnp.concatenate([s[2].reshape(-1, LANE) for s in streams], axis=0)
    pos, tile_e, valid, pad = _moe_plan(rt_all, tm)
    tile_e, valid = tile_e.reshape(-1), valid.reshape(-1)
    xs = _dispatch(pad[:, 0], pad[:, 1], valid, pos, [s[1] for s in streams], n_tiles, tm)
    ys = _moe_experts(xs, tile_e, valid, n_tiles, layer, w_gate, w_up, w_down, tm)
    outs = []
    off = 0
    for (xl, _, rt, g2), sz in zip(streams, sizes):
        outs.append(_moe_combine(xl, g2, rt, pos, off, ys, min(COMBINE_TILE, xl.shape[1])))
        off += sz
    return outs


def _inproj_cd_kernel(x_ref, sh_ref, sc_ref, w_ref, u_ref, *out_refs):
    h = (_rms(x_ref[...]) * (1.0 + sc_ref[...]) + sh_ref[...]).astype(BF16)
    tm = x_ref.shape[0]
    n_slab, rows, cw = u_ref.shape
    nb = rows // tm
    bi = pl.program_id(1)
    u = _dot(h, w_ref[:, 0:n_slab * cw])
    for s in range(n_slab):
        u_ref[s, pl.ds(bi, tm, stride=nb), :] = u[:, s * cw:(s + 1) * cw]
    off = n_slab * cw
    for ref in out_refs:
        n = ref.shape[-1]
        ref[...] = _dot(h, w_ref[:, off:off + n]).astype(ref.dtype)
        off += n


def _inproj_cd(x, sh, sc, w_bf16, widths, tm):
    b, l, d = x.shape
    n = w_bf16.shape[1]
    n_slab = widths[0] // LANE
    tok = lambda j, i: (i, j, 0)
    return pl.pallas_call(
        _inproj_cd_kernel,
        grid=(l // tm, b),
        in_specs=[
            pl.BlockSpec((None, tm, d), tok),
            pl.BlockSpec((None, 1, d), lambda j, i: (i, 0, 0)),
            pl.BlockSpec((None, 1, d), lambda j, i: (i, 0, 0)),
            pl.BlockSpec((d, n), lambda j, i: (0, 0)),
        ],
        out_specs=([pl.BlockSpec((n_slab, tm * b, LANE), lambda j, i: (0, j, 0))]
                   + [pl.BlockSpec((None, tm, w), tok) for w in widths[1:]]),
        out_shape=([jax.ShapeDtypeStruct((n_slab, l * b, LANE), F32)]
                   + [jax.ShapeDtypeStruct((b, l, w), F32) for w in widths[1:]]),
        compiler_params=_cparams(("arbitrary", "arbitrary")),
        name="inproj_cd",
    )(x, sh, sc, w_bf16)


def _s5_param_kernel(are_ref, aim_ref, ldt_ref, bre_ref, bim_ref, abre_ref, abim_ref, bbre_ref, bbim_ref):
    a_re = are_ref[...]
    a_im = aim_ref[...]
    dt = jnp.exp(ldt_ref[...])
    mag = jnp.exp(dt * a_re)
    ab_re = mag * jnp.cos(dt * a_im)
    ab_im = mag * jnp.sin(dt * a_im)
    den = a_re * a_re + a_im * a_im
    nr = ab_re - 1.0
    f_re = (nr * a_re + ab_im * a_im) / den
    f_im = (ab_im * a_re - nr * a_im) / den
    abre_ref[...] = ab_re
    abim_ref[...] = ab_im
    b_re = bre_ref[...]
    b_im = bim_ref[...]
    bbre_ref[...] = f_re[None] * b_re - f_im[None] * b_im
    bbim_ref[...] = f_re[None] * b_im + f_im[None] * b_re


def _s5_params(a_re, a_im, log_dt, b_re, b_im):
    nd, g, p = a_re.shape
    c = b_re.shape[-1]
    rows = nd * g * p // LANE
    flat = lambda t: t.reshape(rows, LANE)
    chan_major = lambda t: jnp.moveaxis(t, -1, 0).reshape(c, rows, LANE)
    ldt = jnp.broadcast_to(log_dt[:, :, None], (nd, g, p))
    shapes = [jax.ShapeDtypeStruct((rows, LANE), F32)] * 2 + [jax.ShapeDtypeStruct((c, rows, LANE), F32)] * 2
    ab_re, ab_im, bb_re, bb_im = pl.pallas_call(
        _s5_param_kernel, out_shape=shapes, name="s5_params",
    )(flat(a_re), flat(a_im), flat(ldt), chan_major(b_re), chan_major(b_im))
    unflat = lambda t: jnp.moveaxis(t.reshape(c, nd, g, p), 0, -1)
    return ab_re.reshape(nd, g, p), ab_im.reshape(nd, g, p), unflat(bb_re), unflat(bb_im)


def _s5_kernel(ufc_ref, ubc_ref, ufl_ref, ubl_ref, a_ref, bsb_ref, csb_ref, d_ref, yf_ref, yb_ref,
               hf, hb, cf, cb, *, n_ctx):
    i = pl.program_id(0)
    nb = cf.shape[0]
    steps = hf.shape[0] // nb
    n_super = bsb_ref.shape[1]
    cw = bsb_ref.shape[2]
    sw = bsb_ref.shape[3]
    half = sw // 2

    @pl.when(i == 0)
    def _():
        cf[...] = jnp.zeros_like(cf)
        cb[...] = jnp.zeros_like(cb)

    def drive(uf_ref, ub_ref):
        for s in range(n_super):
            hf[:, s * sw:(s + 1) * sw] = _dot(uf_ref[s].astype(BF16), bsb_ref[0, s])
            hb[:, s * sw:(s + 1) * sw] = _dot(ub_ref[s].astype(BF16), bsb_ref[1, s])

    pl.when(i < n_ctx)(functools.partial(drive, ufc_ref, ubc_ref))
    pl.when(i >= n_ctx)(functools.partial(drive, ufl_ref, ubl_ref))

    for s in range(n_super):
        re = slice(s * sw, s * sw + half)
        im = slice(s * sw + half, (s + 1) * sw)
        st = slice(s * half, (s + 1) * half)
        arf, aif = a_ref[0, 0, :, st], a_ref[0, 1, :, st]
        arb, aib = a_ref[1, 0, :, st], a_ref[1, 1, :, st]

        def step(t, carry):
            hfr, hfi, hbr, hbi = carry
            rf = pl.multiple_of(t * nb, nb)
            nfr = arf * hfr - aif * hfi + hf[pl.ds(rf, nb), re]
            nfi = arf * hfi + aif * hfr + hf[pl.ds(rf, nb), im]
            hf[pl.ds(rf, nb), re] = nfr
            hf[pl.ds(rf, nb), im] = nfi
            rb = pl.multiple_of((steps - 1 - t) * nb, nb)
            nbr = arb * hbr - aib * hbi + hb[pl.ds(rb, nb), re]
            nbi = arb * hbi + aib * hbr + hb[pl.ds(rb, nb), im]
            hb[pl.ds(rb, nb), re] = nbr
            hb[pl.ds(rb, nb), im] = nbi
            return nfr, nfi, nbr, nbi

        out = lax.fori_loop(0, steps, step, (cf[:, re], cf[:, im], cb[:, re], cb[:, im]), unroll=4)
        cf[:, re], cf[:, im], cb[:, re], cb[:, im] = out

    @pl.when(i >= n_ctx)
    def _():
        for s in range(n_super):
            skip = ufl_ref[s] * d_ref[:, s * cw:(s + 1) * cw]
            yf_ref[s] = _dot(hf[:, s * sw:(s + 1) * sw].astype(BF16), csb_ref[0, s]) + skip
            yb_ref[s] = _dot(hb[:, s * sw:(s + 1) * sw].astype(BF16), csb_ref[1, s])


def _s5_scan(u_c, u_l, a_bc, bsb, csb, d_skip, nb):
    n_slab, rows_l, _ = u_l.shape
    blk = S5_STEPS * nb
    nc = u_c.shape[1] // blk
    nl = rows_l // blk
    n_state = bsb.shape[1] * bsb.shape[3]
    clip = jnp.clip
    ctx_f = lambda i: (0, clip(i, 0, nc - 1), 0)
    ctx_b = lambda i: (0, clip(nc - 1 - i, 0, nc - 1), 0)
    lat_f = lambda i: (0, clip(i - nc, 0, nl - 1), 0)
    lat_b = lambda i: (0, clip(nl - 1 - (i - nc), 0, nl - 1), 0)
    whole = lambda a: pl.BlockSpec(a.shape, lambda i: (0,) * a.ndim)
    slab = lambda index_map: pl.BlockSpec((n_slab, blk, LANE), index_map)
    return pl.pallas_call(
        functools.partial(_s5_kernel, n_ctx=nc),
        grid=(nc + nl,),
        in_specs=[slab(ctx_f), slab(ctx_b), slab(lat_f), slab(lat_b), whole(a_bc), whole(bsb), whole(csb),
                  whole(d_skip)],
        out_specs=[slab(lat_f), slab(lat_b)],
        out_shape=[jax.ShapeDtypeStruct(u_l.shape, F32)] * 2,
        scratch_shapes=[
            pltpu.VMEM((blk, n_state), F32),
            pltpu.VMEM((blk, n_state), F32),
            pltpu.VMEM((nb, n_state), F32),
            pltpu.VMEM((nb, n_state), F32),
        ],
        compiler_params=_cparams(("arbitrary",)),
        name="s5_scan",
    )(u_c, u_c, u_l, u_l, a_bc, bsb, csb, d_skip)


def _s5_block_matrices(ab_re, ab_im, bb_re, bb_im, c_re, c_im, nb):
    nd, g, p = ab_re.shape
    c = bb_re.shape[-1]
    ns = g // S5_SUPER
    eye = jnp.eye(S5_SUPER, dtype=F32)

    def in_mat(bb):
        t = bb.reshape(nd, ns, S5_SUPER, p, c)
        return jnp.einsum('dsgpc,gh->dsgchp', t, eye).reshape(nd, ns, S5_SUPER * c, S5_SUPER * p)

    def out_mat(cc):
        t = cc.reshape(nd, ns, S5_SUPER, c, p)
        return jnp.einsum('dsgcp,gh->dsgphc', t, eye).reshape(nd, ns, S5_SUPER * p, S5_SUPER * c)

    bsb = jnp.concatenate([in_mat(bb_re), in_mat(bb_im)], axis=-1).astype(BF16)
    csb = jnp.concatenate([out_mat(c_re), out_mat(-c_im)], axis=-2).astype(BF16)
    a_bc = jnp.stack([ab_re.reshape(nd, g * p), ab_im.reshape(nd, g * p)], axis=1)
    a_bc = jnp.broadcast_to(a_bc[:, :, None, :], (nd, 2, nb, g * p))
    return a_bc, bsb, csb


def _head_scale(x):
    return lax.rsqrt(jnp.sum(x * x, axis=-1, keepdims=True) * (1.0 / QK_HEAD) + EPS)


def _mla_prep_lat_kernel(cq_ref, ckv_ref, krp_ref, krs_ref, qn_ref, kvn_ref, wq_ref, wqs_ref, wk_ref, wv_ref,
                         qc_ref, qs_ref, kc_ref, ks_ref, q_ref, k_ref, v_ref):
    ckvn = (_rms(ckv_ref[...]) * kvn_ref[...]).astype(BF16)
    v_ref[...] = _dot(ckvn, wv_ref[...]).astype(v_ref.dtype)
    kn = _dot(ckvn, wk_ref[...])
    krp = krp_ref[...]
    k_swap = krs_ref[...] * ks_ref[...]
    kc = kc_ref[...]
    for h in range(N_HEADS):
        sl = slice(h * HEAD_PAD, (h + 1) * HEAD_PAD)
        kh = kn[:, sl] + krp
        k_ref[:, sl] = (_head_scale(kh) * (kh * kc + k_swap)).astype(k_ref.dtype)
    cqn = (_rms(cq_ref[...]) * qn_ref[...]).astype(BF16)
    qn = _dot(cqn, wq_ref[...])
    q_swap = _dot(cqn, wqs_ref[...])
    qc = qc_ref[...]
    qs = qs_ref[...]
    for h in range(N_HEADS):
        sl = slice(h * HEAD_PAD, (h + 1) * HEAD_PAD)
        qh = qn[:, sl]
        scale = _head_scale(qh) * (QK_HEAD ** -0.5 * LOG2_E)
        q_ref[:, sl] = (scale * (qh * qc + q_swap[:, sl] * qs)).astype(q_ref.dtype)


def _mla_prep_ctx_kernel(ckv_ref, krp_ref, kvn_ref, wk_ref, wv_ref, kg_ref, k_ref, v_ref):
    ckvn = (_rms(ckv_ref[...]) * kvn_ref[...]).astype(BF16)
    v_ref[...] = _dot(ckvn, wv_ref[...]).astype(v_ref.dtype)
    kn = _dot(ckvn, wk_ref[...])
    krp = krp_ref[...]
    kg = kg_ref[...]
    for h in range(N_HEADS):
        sl = slice(h * HEAD_PAD, (h + 1) * HEAD_PAD)
        kh = kn[:, sl] + krp
        k_ref[:, sl] = (_head_scale(kh) * (kh * kg)).astype(k_ref.dtype)


def _mla_prep(kernel_fn, name, tok_args, const_args, table_args, n_q_out, tm):
    b, l, _ = tok_args[0].shape
    tok = lambda i, j: (i, j, 0)
    specs = [pl.BlockSpec((None, tm, a.shape[-1]), tok) for a in tok_args]
    specs += [pl.BlockSpec(a.shape, lambda i, j: (0, 0)) for a in const_args]
    specs += [pl.BlockSpec((tm, HEAD_PAD), lambda i, j: (j, 0)) for _ in table_args]
    hk = N_HEADS * HEAD_PAD
    hv = N_HEADS * V_HEAD
    widths = [hk] * n_q_out + [hk, hv]
    return pl.pallas_call(
        kernel_fn,
        grid=(b, l // tm),
        in_specs=specs,
        out_specs=[pl.BlockSpec((None, tm, w), tok) for w in widths],
        out_shape=[jax.ShapeDtypeStruct((b, l, w), BF16) for w in widths],
        compiler_params=_cparams(("arbitrary", "arbitrary")),
        name=name,
    )(*tok_args, *const_args, *table_args)


def _attn_kernel(q_ref, kl_ref, kc_ref, vl_ref, vc_ref, o_ref, *, heads):
    nt = (((1,), (1,)), ((), ()))
    for pr in range(heads // 2):
        vsl = slice(pr * 2 * V_HEAD, (pr + 1) * 2 * V_HEAD)
        outs = []
        for hh in range(2):
            h = 2 * pr + hh
            sl = slice(h * HEAD_PAD, (h + 1) * HEAD_PAD)
            q = q_ref[:, sl]
            s_l = lax.dot_general(q, kl_ref[:, sl], nt, preferred_element_type=F32)
            s_c = lax.dot_general(q, kc_ref[:, sl], nt, preferred_element_type=F32)
            m = jnp.maximum(jnp.max(s_l, axis=-1, keepdims=True), jnp.max(s_c, axis=-1, keepdims=True))
            p_l = jnp.exp2(s_l - m)
            p_c = jnp.exp2(s_c - m)
            den = jnp.sum(p_l, axis=-1, keepdims=True) + jnp.sum(p_c, axis=-1, keepdims=True)
            o = _dot(p_l.astype(BF16), vl_ref[:, vsl]) + _dot(p_c.astype(BF16), vc_ref[:, vsl])
            outs.append(o / den)
        lane = lax.broadcasted_iota(I32, outs[0].shape, 1)
        o_ref[:, vsl] = jnp.where(lane < V_HEAD, outs[0], outs[1]).astype(o_ref.dtype)


def _attention(q, k_l, k_c, v_l, v_c, tq, heads):
    b, l, _ = q.shape
    lc = k_c.shape[1]
    hp = N_HEADS // heads
    kw = heads * HEAD_PAD
    vw = heads * V_HEAD
    return pl.pallas_call(
        functools.partial(_attn_kernel, heads=heads),
        grid=(b, hp, l // tq),
        in_specs=[
            pl.BlockSpec((None, tq, kw), lambda i, h, j: (i, j, h)),
            pl.BlockSpec((None, l, kw), lambda i, h, j: (i, 0, h)),
            pl.BlockSpec((None, lc, kw), lambda i, h, j: (i, 0, h)),
            pl.BlockSpec((None, l, vw), lambda i, h, j: (i, 0, h)),
            pl.BlockSpec((None, lc, vw), lambda i, h, j: (i, 0, h)),
        ],
        out_specs=pl.BlockSpec((None, tq, vw), lambda i, h, j: (i, j, h)),
        out_shape=jax.ShapeDtypeStruct((b, l, N_HEADS * V_HEAD), BF16),
        compiler_params=_cparams(("arbitrary", "arbitrary", "arbitrary")),
        name="mla_attention",
    )(q, k_l, k_c, v_l, v_c)


def _mix_cd_kernel(yf_ref, yb_ref, o_ref, x_ref, g1_ref, sh2_ref, sc2_ref,
                   wglu_ref, wout_ref, wr_ref, br_ref, xl_ref, xm_ref, rt_ref):
    tm = x_ref.shape[0]
    n_slab, rows, _ = yf_ref.shape
    nb = rows // tm
    bi = pl.program_id(1)
    dc = n_slab * LANE
    pick = lambda ref, s: ref[s, pl.ds(bi, tm, stride=nb), :]
    y = jnp.concatenate([pick(yf_ref, s) + pick(yb_ref, s) for s in range(n_slab)], axis=1)
    g = jax.nn.gelu(y)
    s5 = (g * _sigmoid(_dot(g.astype(BF16), wglu_ref[...]))).astype(BF16)
    y_mix = _dot(s5, wout_ref[0:dc, :]) + _dot(o_ref[...], wout_ref[dc:, :])
    _residual_and_route(x_ref[...], y_mix, g1_ref[...], sh2_ref[...], sc2_ref[...], wr_ref, br_ref,
                        xl_ref, xm_ref, rt_ref)


def _mix_cd(yf, yb, o, x, g1, sh2, sc2, wglu, wout, wr, br, tm):
    b, l, d = x.shape
    n_slab = yf.shape[0]
    tok = lambda j, i: (i, j, 0)
    per_b = lambda j, i: (i, 0, 0)
    const2 = lambda j, i: (0, 0)
    tm_blk = pl.BlockSpec((n_slab, tm * b, LANE), lambda j, i: (0, j, 0))
    return pl.pallas_call(
        _mix_cd_kernel,
        grid=(l // tm, b),
        in_specs=[
            tm_blk,
            tm_blk,
            pl.BlockSpec((None, tm, o.shape[-1]), tok),
            pl.BlockSpec((None, tm, d), tok),
            pl.BlockSpec((None, 1, d), per_b),
            pl.BlockSpec((None, 1, d), per_b),
            pl.BlockSpec((None, 1, d), per_b),
            pl.BlockSpec(wglu.shape, const2),
            pl.BlockSpec(wout.shape, const2),
            pl.BlockSpec(wr.shape, const2),
            pl.BlockSpec((1, LANE), const2),
        ],
        out_specs=[
            pl.BlockSpec((None, tm, d), tok),
            pl.BlockSpec((tm * TOKEN_ROWS, LANE), lambda j, i: (i * (l // tm) + j, 0)),
            pl.BlockSpec((None, tm, LANE), tok),
        ],
        out_shape=[
            jax.ShapeDtypeStruct((b, l, d), F32),
            jax.ShapeDtypeStruct((b * l * TOKEN_ROWS, LANE), F32),
            jax.ShapeDtypeStruct((b, l, LANE), F32),
        ],
        compiler_params=_cparams(("arbitrary", "arbitrary")),
        name="mix_cd",
    )(yf, yb, o, x, g1, sh2, sc2, wglu, wout, wr, br)


def _router_weights(w_grp, b_grp, w_exp, b_exp):
    d = w_grp.shape[0]
    pad = LANE - N_GROUPS - N_EXPERTS
    wr = jnp.concatenate([w_grp, w_exp, jnp.zeros((d, pad), F32)], axis=1)
    br = jnp.concatenate([b_grp, b_exp, jnp.zeros((pad,), F32)]).reshape(1, LANE)
    return wr, br


def _pad_heads(w, width):
    k = w.shape[0]
    w = w.reshape(k, N_HEADS, width)
    return jnp.pad(w, ((0, 0), (0, 0), (0, HEAD_PAD - width))).reshape(k, N_HEADS * HEAD_PAD)


def _rope_tables(l):
    rows = l // GRID_W
    row = jnp.repeat(jnp.arange(rows, dtype=F32), GRID_W)
    col = jnp.tile(jnp.arange(GRID_W, dtype=F32), rows)
    inv_freq = ROPE_BASE ** (-jnp.arange(AXIS_PAIRS, dtype=F32) / AXIS_PAIRS)
    ar, ac = row[:, None] * inv_freq, col[:, None] * inv_freq
    one = jnp.ones((l, QK_NOPE), F32)
    zn = jnp.zeros((l, QK_NOPE), F32)
    zp = jnp.zeros((l, HEAD_PAD - QK_HEAD), F32)
    cos = jnp.concatenate([one, jnp.cos(ar), jnp.cos(ar), jnp.cos(ac), jnp.cos(ac), zp], axis=1)
    sin = jnp.concatenate([zn, jnp.sin(ar), jnp.sin(ar), jnp.sin(ac), jnp.sin(ac), zp], axis=1)
    return cos, sin


def _rot_swap(t, signed):
    a = AXIS_PAIRS
    s = -1.0 if signed else 1.0
    return jnp.concatenate([s * t[..., a:2 * a], t[..., 0:a], s * t[..., 3 * a:4 * a], t[..., 2 * a:3 * a]], axis=-1)


def _head_lanes(rot):
    pad = [(0, 0)] * (rot.ndim - 1) + [(QK_NOPE, HEAD_PAD - QK_HEAD)]
    return jnp.pad(rot, pad)


def _split_mod(mod_row, b, d):
    parts = jnp.split(mod_row, 6, axis=-1)
    return [jnp.broadcast_to(p.reshape(-1, 1, d), (b, 1, d)) for p in parts]


def kernel(x, c, ctx, c_ctx, ada_w, ada_b, ab_w_in, sgu_norm, sgu_w, sgu_b, conv_w, ab_w_out, cd_w_in, s5_a_re, s5_a_im, s5_log_dt, s5_b_re, s5_b_im, s5_c_re, s5_c_im, s5_d, s5_w_glu, mla_q_norm, mla_kv_norm, mla_w_uq, mla_w_uk, mla_w_uv, mla_qn_gain, mla_kn_gain, cd_w_out, moe_w_grp, moe_b_grp, moe_w_exp, moe_b_exp, moe_w_gate, moe_w_up, moe_w_down):
    b, l, d = x.shape
    lc = ctx.shape[1]
    depth = ada_w.shape[0]
    assert depth == 2, "layer 0 = gated-MLP/conv mixers, layer 1 = S5/attention mixers"
    assert d == TOKEN_ROWS * LANE, "token-tile layout: one vreg tile per token"
    tm = min(TOKEN_TILE, l)
    tmc = min(TOKEN_TILE, lc)

    mod_rows = 2 * SUBLANE
    cvec = jnp.zeros((mod_rows, d), F32).at[:b].set(c).at[b].set(c_ctx)
    mod = _ada(cvec, ada_w, ada_b)

    sh1, sc1, g1, sh2, sc2, g2 = _split_mod(mod[0, :b], b, d)
    csh1, csc1, cg1, csh2, csc2, cg2 = _split_mod(mod[0, b:b + 1], b, d)
    w_in = ab_w_in[0].astype(BF16)
    wr, br = _router_weights(moe_w_grp[0], moe_b_grp[0], moe_w_exp[0], moe_b_exp[0])
    sgub = jnp.repeat(sgu_b[0].T, d // G_A, axis=1)
    mix_args = (sgu_norm[0].reshape(1, d), sgu_w[0].astype(BF16), sgub, conv_w[0],
                ab_w_out[0].astype(BF16), wr, br)
    z, p = _inproj_ab(x, sh1, sc1, w_in, tm)
    xl, xm, rt = _mix_ab(z, p, x, g1, sh2, sc2, *mix_args, tm)
    zc, pc = _inproj_ab(ctx, csh1, csc1, w_in, tmc)
    xc, xmc, rtc = _mix_ab(zc, pc, ctx, cg1, csh2, csc2, *mix_args, tmc)
    xl, xc = _hier_moe([(xl, xm, rt, g2), (xc, xmc, rtc, cg2)], 0, moe_w_gate, moe_w_up, moe_w_down)

    sh1, sc1, g1, sh2, sc2, g2 = _split_mod(mod[1, :b], b, d)
    csh1, csc1, _, _, _, _ = _split_mod(mod[1, b:b + 1], b, d)
    d_c = s5_d.shape[-1]
    q_lora = mla_q_norm.shape[-1]
    kv_lora = mla_kv_norm.shape[-1]
    w_cd = cd_w_in[0]
    o_kr = d_c + q_lora + kv_lora
    w_kr = _head_lanes(w_cd[:, o_kr:])
    w_krs = _head_lanes(_rot_swap(w_cd[:, o_kr:], True))
    w_cd = jnp.concatenate([w_cd[:, :o_kr], w_kr, w_krs], axis=1).astype(BF16)
    widths = (d_c, q_lora, kv_lora, HEAD_PAD, HEAD_PAD)
    u_l, cq_l, ckv_l, krp_l, krs_l = _inproj_cd(xl, sh1, sc1, w_cd, widths, tm)
    u_c, _, ckv_c, krp_c, _ = _inproj_cd(xc, csh1, csc1, w_cd, widths, tmc)

    ab_re, ab_im, bb_re, bb_im = _s5_params(s5_a_re[0], s5_a_im[0], s5_log_dt[0], s5_b_re[0], s5_b_im[0])
    a_bc, bsb, csb = _s5_block_matrices(ab_re, ab_im, bb_re, bb_im, s5_c_re[0], s5_c_im[0], b)
    yf, yb = _s5_scan(u_c, u_l, a_bc, bsb, csb, s5_d[0].reshape(1, d_c), b)

    wq = _pad_heads(mla_w_uq[0], QK_HEAD).astype(BF16)
    wk = _pad_heads(mla_w_uk[0], QK_NOPE).astype(BF16)
    wv = mla_w_uv[0].astype(BF16)
    w_uq = mla_w_uq[0].reshape(q_lora, N_HEADS, QK_HEAD)
    wqs = _head_lanes(_rot_swap(w_uq[..., QK_NOPE:], True)).reshape(q_lora, N_HEADS * HEAD_PAD).astype(BF16)
    qg, kg = mla_qn_gain[0], mla_kn_gain[0]
    full = lambda g: jnp.pad(g, (0, HEAD_PAD - QK_HEAD)).reshape(1, HEAD_PAD)
    partner = lambda g: _head_lanes(_rot_swap(g[QK_NOPE:], False)).reshape(1, HEAD_PAD)
    cos, sin = _rope_tables(l)
    tables = (cos * full(qg), sin * partner(qg), cos * full(kg), sin * partner(kg))
    qn = mla_q_norm[0].reshape(1, q_lora)
    kvn = mla_kv_norm[0].reshape(1, kv_lora)
    q_l, k_l, v_l = _mla_prep(_mla_prep_lat_kernel, "mla_prep_lat", (cq_l, ckv_l, krp_l, krs_l),
                              (qn, kvn, wq, wqs, wk, wv), tables, 1, tm)
    k_c, v_c = _mla_prep(_mla_prep_ctx_kernel, "mla_prep_ctx", (ckv_c, krp_c), (kvn, wk, wv, full(kg)), (), 0, tmc)
    o_l = _attention(q_l, k_l, k_c, v_l, v_c, ATTN_Q_TILE, ATTN_HEADS)

    wr, br = _router_weights(moe_w_grp[1], moe_b_grp[1], moe_w_exp[1], moe_b_exp[1])
    xl, xm, rt = _mix_cd(yf, yb, o_l, xl, g1, sh2, sc2,
                         s5_w_glu[0].astype(BF16), cd_w_out[0].astype(BF16), wr, br, MIX_CD_TILE)
    (xl,) = _hier_moe([(xl, xm, rt, g2)], 1, moe_w_gate, moe_w_up, moe_w_down)
    return xl
```

```python
import functools
import math

import jax
import jax.numpy as jnp
from jax import lax
from jax.experimental import pallas as pl
from jax.experimental.pallas import tpu as pltpu

F32 = jnp.float32
BF16 = jnp.bfloat16
I32 = jnp.int32
HIGHEST = lax.Precision.HIGHEST

EPS = 1e-6
LOG2_E = math.log2(math.e)
GRID_W = 64
CHUNK = 128
G_A = 8
CONV_W = 3
S5_GROUP = 16
S5_STATE = 64
N_HEADS = 8
QK_NOPE = 64
QK_ROPE = 32
QK_HEAD = QK_NOPE + QK_ROPE
V_HEAD = 64
AXIS_PAIRS = QK_ROPE // 4
ROPE_BASE = 10000.0
N_GROUPS = 4
EXPERTS_PER_GROUP = 8
N_EXPERTS = N_GROUPS * EXPERTS_PER_GROUP
TOP_K = 2

LANE = 128
SUBLANE = 8
HEAD_PAD = LANE
TOKEN_TILE = 512
MIX_SUBTILES = 2
MIX_CD_TILE = 256
MOE_TILE = 512
COMBINE_TILE = 512
PLAN_BLOCK = 1024
MAX_TILES = 256
TOKEN_ROWS = SUBLANE
ATTN_Q_TILE = 512
ATTN_HEADS = 8
S5_STEPS = 64
S5_SUPER = LANE // S5_GROUP
VMEM_LIMIT = 56 * 1024 * 1024


def _cparams(sem):
    return pltpu.CompilerParams(dimension_semantics=sem, vmem_limit_bytes=VMEM_LIMIT)


def _rms(x):
    return x * lax.rsqrt(jnp.mean(x * x, axis=-1, keepdims=True) + EPS)


def _sigmoid(x):
    return 1.0 / (1.0 + jnp.exp(-x))


def _dot(a, b):
    return jnp.dot(a, b, preferred_element_type=F32)


def _ada_kernel(c_ref, w_ref, b_ref, o_ref):
    c = c_ref[...]
    s = c * _sigmoid(c)
    o_ref[...] = jnp.dot(s, w_ref[...], precision=HIGHEST, preferred_element_type=F32) + b_ref[...]


def _ada(cvec, ada_w, ada_b):
    depth, d, n = ada_w.shape
    rows = cvec.shape[0]
    tn = 1024
    return pl.pallas_call(
        _ada_kernel,
        grid=(depth, n // tn),
        in_specs=[
            pl.BlockSpec((rows, d), lambda i, j: (0, 0)),
            pl.BlockSpec((None, d, tn), lambda i, j: (i, 0, j)),
            pl.BlockSpec((None, 1, tn), lambda i, j: (i, 0, j)),
        ],
        out_specs=pl.BlockSpec((None, rows, tn), lambda i, j: (i, 0, j)),
        out_shape=jax.ShapeDtypeStruct((depth, rows, n), F32),
        compiler_params=_cparams(("arbitrary", "arbitrary")),
        name="ada_mod",
    )(cvec, ada_w, ada_b.reshape(depth, 1, n))


def _route(xm, wr_ref, br_ref):
    hi = xm.astype(BF16)
    lo = (xm - hi.astype(F32)).astype(BF16)
    w = wr_ref[...]
    w_hi = w.astype(BF16)
    w_lo = (w - w_hi.astype(F32)).astype(BF16)
    logits = _dot(jnp.concatenate([hi, lo, hi], axis=1), jnp.concatenate([w_hi, w_hi, w_lo], axis=0)) + br_ref[...]
    lane = lax.broadcasted_iota(I32, logits.shape, 1).astype(F32)
    neg = jnp.float32(-1e30)
    big = jnp.float32(1e6)
    is_grp = lane < N_GROUPS
    gl = jnp.where(is_grp, logits, neg)
    gmax = jnp.max(gl, axis=-1, keepdims=True)
    gsum = jnp.sum(jnp.where(is_grp, jnp.exp(gl - gmax), 0.0), axis=-1, keepdims=True)
    p_top = 1.0 / gsum
    grp = jnp.min(jnp.where(gl == gmax, lane, big), axis=-1, keepdims=True)
    eidx = lane - N_GROUPS
    in_grp = (eidx >= grp * EXPERTS_PER_GROUP) & (eidx < (grp + 1.0) * EXPERTS_PER_GROUP)
    el = jnp.where(in_grp, logits, neg)
    m1 = jnp.max(el, axis=-1, keepdims=True)
    i1 = jnp.min(jnp.where(el == m1, lane, big), axis=-1, keepdims=True)
    el2 = jnp.where(lane == i1, neg, el)
    m2 = jnp.max(el2, axis=-1, keepdims=True)
    i2 = jnp.min(jnp.where(el2 == m2, lane, big), axis=-1, keepdims=True)
    t = jnp.exp(m2 - m1)
    w1 = p_top / (1.0 + t)
    w2 = p_top * t / (1.0 + t)
    e1 = i1 - N_GROUPS
    e2 = i2 - N_GROUPS
    return jnp.where(lane == 0, e1, jnp.where(lane == 1, e2, jnp.where(lane == 2, w1, jnp.where(lane == 3, w2, 0.0))))


def _residual_and_route(x, y, g1, sh2, sc2, wr_ref, br_ref, xl_ref, xm_ref, rt_ref):
    xl = x + g1 * y
    xl_ref[...] = xl
    xm = _rms(xl) * (1.0 + sc2) + sh2
    _store_token_tiles(xm_ref, xm)
    rt_ref[...] = _route(xm, wr_ref, br_ref)


def _inproj_ab_kernel(x_ref, sh_ref, sc_ref, w_ref, z_ref, p_ref):
    d = x_ref.shape[-1]
    h = (_rms(x_ref[...]) * (1.0 + sc_ref[...]) + sh_ref[...]).astype(BF16)
    for j in range(3):
        z_ref[:, j * d:(j + 1) * d] = _dot(h, w_ref[:, j * d:(j + 1) * d]).astype(BF16)
    gate_c = _dot(h, w_ref[:, 3 * d:4 * d])
    xb = _dot(h, w_ref[:, 4 * d:5 * d])
    p_ref[...] = gate_c * xb


def _inproj_ab(x, sh, sc, w_bf16, tm):
    b, l, d = x.shape
    n = w_bf16.shape[1]
    return pl.pallas_call(
        _inproj_ab_kernel,
        grid=(b, l // tm),
        in_specs=[
            pl.BlockSpec((None, tm, d), lambda i, j: (i, j, 0)),
            pl.BlockSpec((None, 1, d), lambda i, j: (i, 0, 0)),
            pl.BlockSpec((None, 1, d), lambda i, j: (i, 0, 0)),
            pl.BlockSpec((d, n), lambda i, j: (0, 0)),
        ],
        out_specs=[
            pl.BlockSpec((None, tm, 3 * d), lambda i, j: (i, j, 0)),
            pl.BlockSpec((None, tm, d), lambda i, j: (i, j, 0)),
        ],
        out_shape=[
            jax.ShapeDtypeStruct((b, l, 3 * d), BF16),
            jax.ShapeDtypeStruct((b, l, d), F32),
        ],
        compiler_params=_cparams(("arbitrary", "arbitrary")),
        name="inproj_ab",
    )(x, sh, sc, w_bf16)


def _mix_ab_kernel(z_ref, p_ref, pprev_ref, pnext_ref, x_ref, g1_ref, sh2_ref, sc2_ref,
                   sgun_ref, sguw_ref, sgub_ref, convw_ref, wout_ref, wr_ref, br_ref,
                   xl_ref, xm_ref, rt_ref, pbuf, ycat):
    i = pl.program_id(1)
    nt = pl.num_programs(1)
    tm, d = x_ref.shape
    cg = d // G_A
    pbuf[SUBLANE:SUBLANE + tm, :] = p_ref[...]
    pbuf[0:SUBLANE, :] = jnp.where(i > 0, pprev_ref[...], 0.0)
    pbuf[SUBLANE + tm:2 * SUBLANE + tm, :] = jnp.where(i < nt - 1, pnext_ref[...], 0.0)
    ts = tm // (MIX_SUBTILES if tm % (MIX_SUBTILES * CHUNK) == 0 else 1)
    for t0 in range(0, tm, ts):
        rows = slice(t0, t0 + ts)
        v = jax.nn.gelu(z_ref[rows, d:2 * d].astype(F32))
        vc = (_rms(v) * sgun_ref[...]).astype(BF16)
        for c in range(ts // CHUNK):
            lr = c * CHUNK
            r0 = t0 + lr
            cols = [_dot(sguw_ref[g], vc[lr:lr + CHUNK, g * cg:(g + 1) * cg]) for g in range(G_A)]
            s = jnp.concatenate(cols, axis=1) + sgub_ref[...]
            u = jax.nn.gelu(z_ref[r0:r0 + CHUNK, 0:d].astype(F32))
            ycat[r0:r0 + CHUNK, 0:d] = (u * s).astype(BF16)
        conv = (convw_ref[0:1, :] * pbuf[SUBLANE - 1 + t0:SUBLANE - 1 + t0 + ts, :]
                + convw_ref[1:2, :] * pbuf[SUBLANE + t0:SUBLANE + t0 + ts, :]
                + convw_ref[2:3, :] * pbuf[SUBLANE + 1 + t0:SUBLANE + 1 + t0 + ts, :])
        ycat[rows, d:2 * d] = (z_ref[rows, 2 * d:3 * d].astype(F32) * conv).astype(BF16)
        y = _dot(ycat[rows, :], wout_ref[...])
        _residual_and_route(x_ref[rows, :], y, g1_ref[...], sh2_ref[...], sc2_ref[...], wr_ref, br_ref,
                            xl_ref.at[rows, :], xm_ref.at[pl.ds(t0 * TOKEN_ROWS, ts * TOKEN_ROWS), :],
                            rt_ref.at[rows, :])


def _mix_ab(z, p, x, g1, sh2, sc2, sgun, sguw, sgub, convw, wout, wr, br, tm):
    b, l, d = x.shape
    hb = tm // SUBLANE
    nhb = l // SUBLANE
    tok = lambda i, j: (i, j, 0)
    per_b = lambda i, j: (i, 0, 0)
    const2 = lambda i, j: (0, 0)
    return pl.pallas_call(
        _mix_ab_kernel,
        grid=(b, l // tm),
        in_specs=[
            pl.BlockSpec((None, tm, 3 * d), tok),
            pl.BlockSpec((None, tm, d), tok),
            pl.BlockSpec((None, SUBLANE, d), lambda i, j: (i, jnp.maximum(j * hb - 1, 0), 0)),
            pl.BlockSpec((None, SUBLANE, d), lambda i, j: (i, jnp.minimum((j + 1) * hb, nhb - 1), 0)),
            pl.BlockSpec((None, tm, d), tok),
            pl.BlockSpec((None, 1, d), per_b),
            pl.BlockSpec((None, 1, d), per_b),
            pl.BlockSpec((None, 1, d), per_b),
            pl.BlockSpec((1, d), const2),
            pl.BlockSpec((G_A, CHUNK, CHUNK), lambda i, j: (0, 0, 0)),
            pl.BlockSpec((CHUNK, d), const2),
            pl.BlockSpec((CONV_W, d), const2),
            pl.BlockSpec((2 * d, d), const2),
            pl.BlockSpec(wr.shape, const2),
            pl.BlockSpec((1, LANE), const2),
        ],
        out_specs=[
            pl.BlockSpec((None, tm, d), tok),
            pl.BlockSpec((tm * TOKEN_ROWS, LANE), lambda i, j: (i * (l // tm) + j, 0)),
            pl.BlockSpec((None, tm, LANE), tok),
        ],
        out_shape=[
            jax.ShapeDtypeStruct((b, l, d), F32),
            jax.ShapeDtypeStruct((b * l * TOKEN_ROWS, LANE), F32),
            jax.ShapeDtypeStruct((b, l, LANE), F32),
        ],
        scratch_shapes=[
            pltpu.VMEM((tm + 2 * SUBLANE, d), F32),
            pltpu.VMEM((tm, 2 * d), BF16),
        ],
        compiler_params=_cparams(("arbitrary", "arbitrary")),
        name="mix_ab",
    )(z, p, p, p, x, g1, sh2, sc2, sgun, sguw, sgub, convw, wout, wr, br)


def _store_token_tiles(ref, val):
    tm, d = val.shape
    rpt = d // LANE
    for j in range(rpt):
        ref[pl.ds(j, tm, stride=rpt), :] = val[:, j * LANE:(j + 1) * LANE]


def _load_token_tile_col(ref, tm, rpt, j):
    return ref[pl.ds(j, tm, stride=rpt), :]


def _plan_kernel(rt_ref, tri_ref, pos_ref, te_ref, tv_ref, pad_ref, run, *, tm):
    ps = pl.program_id(0)
    i = pl.program_id(1)
    blk = rt_ref.shape[0]
    ne = N_EXPERTS

    @pl.when(jnp.logical_and(ps == 0, i == 0))
    def _():
        run[...] = jnp.zeros_like(run)

    @pl.when(jnp.logical_and(ps == 1, i == 0))
    def _():
        counts = run[...]
        padded = jnp.floor((counts + (tm - 1)) * (1.0 / tm)) * tm
        r = lax.broadcasted_iota(I32, (ne, ne), 0)
        c = lax.broadcasted_iota(I32, (ne, ne), 1)
        starts = jnp.dot((c < r).astype(F32), padded, precision=HIGHEST, preferred_element_type=F32)
        nt = te_ref.shape[1]
        ends = jnp.concatenate([(starts + padded) * (1.0 / tm)] * (nt // LANE), axis=1)
        t = lax.broadcasted_iota(I32, (ne, nt), 1).astype(F32)
        te = jnp.sum((t >= ends).astype(F32), axis=0, keepdims=True)
        total = jnp.max(ends, axis=0, keepdims=True)
        te_ref[...] = jnp.minimum(te, ne - 1.0).astype(I32)
        tv_ref[...] = (t[0:1, :] < total).astype(I32)
        lane = lax.broadcasted_iota(I32, (ne, LANE), 1)
        pad_ref[...] = jnp.where(lane == 0, starts + counts,
                                 jnp.where(lane == 1, padded - counts, 0.0)).astype(I32)
        run[...] = starts

    slab_t = rt_ref[...].T
    ex = lax.broadcasted_iota(I32, (ne, blk), 0).astype(F32)
    oh1 = ex == slab_t[0:1, :]
    oh2 = ex == slab_t[1:2, :]
    oh = jnp.where(jnp.logical_or(oh1, oh2), 1.0, 0.0)
    @pl.when(ps == 0)
    def _():
        pos_ref[...] = jnp.zeros_like(pos_ref)

    @pl.when(ps == 1)
    def _():
        prefix = _dot(oh.astype(BF16), tri_ref[...]) + run[:, 0:1]
        pos1 = jnp.sum(jnp.where(oh1, prefix, 0.0), axis=0, keepdims=True)
        pos2 = jnp.sum(jnp.where(oh2, prefix, 0.0), axis=0, keepdims=True)
        pos_ref[...] = jnp.concatenate([pos1, pos2], axis=0).astype(I32)

    run[...] = run[...] + jnp.sum(oh, axis=1, keepdims=True)


def _moe_plan(rt_all, tm):
    n = rt_all.shape[0]
    blk = PLAN_BLOCK
    nb = n // blk
    tri = jnp.triu(jnp.ones((blk, blk), BF16), k=1)
    return pl.pallas_call(
        functools.partial(_plan_kernel, tm=tm),
        grid=(2, nb),
        in_specs=[
            pl.BlockSpec((blk, LANE), lambda p, i: (i, 0)),
            pl.BlockSpec((blk, blk), lambda p, i: (0, 0)),
        ],
        out_specs=[
            pl.BlockSpec((None, TOP_K, blk), lambda p, i: (jnp.where(p == 0, nb, i), 0, 0)),
            pl.BlockSpec((1, MAX_TILES), lambda p, i: (0, 0)),
            pl.BlockSpec((1, MAX_TILES), lambda p, i: (0, 0)),
            pl.BlockSpec((N_EXPERTS, LANE), lambda p, i: (0, 0)),
        ],
        out_shape=[
            jax.ShapeDtypeStruct((nb + 1, TOP_K, blk), I32),
            jax.ShapeDtypeStruct((1, MAX_TILES), I32),
            jax.ShapeDtypeStruct((1, MAX_TILES), I32),
            jax.ShapeDtypeStruct((N_EXPERTS, LANE), I32),
        ],
        scratch_shapes=[pltpu.VMEM((N_EXPERTS, LANE), F32)],
        compiler_params=_cparams(("arbitrary", "arbitrary")),
        name="moe_plan",
    )(rt_all, tri)


def _dispatch_kernel(ps_ref, pl_ref, tv_ref, pos_ref, *rest, blocks, n_tiles):
    xm_refs = rest[:len(blocks)]
    xs_hbm, zbuf, sem = rest[len(blocks):]
    i = pl.program_id(0)
    tt = pos_ref.shape[1]
    rpt = TOKEN_ROWS
    tile_rows = zbuf.shape[0]

    @pl.when(i == 0)
    def _():
        zbuf[...] = jnp.zeros_like(zbuf)

        def gap_copies(e, wait):
            base = ps_ref[e]
            n = pl_ref[e]
            bit = tile_rows // (2 * rpt)
            while bit >= 1:
                lower = n & (bit - 1)

                @pl.when((n & bit) != 0)
                def _(lower=lower, bit=bit):
                    dst = xs_hbm.at[pl.ds((base + lower) * rpt, bit * rpt)]
                    cp = pltpu.make_async_copy(zbuf.at[pl.ds(0, bit * rpt)], dst, sem.at[1])
                    cp.wait() if wait else cp.start()
                bit //= 2

        def tile_copy(t, wait):
            @pl.when(tv_ref[t] != 1)
            def _():
                dst = xs_hbm.at[pl.ds(pl.multiple_of(t * tile_rows, tile_rows), tile_rows)]
                cp = pltpu.make_async_copy(zbuf, dst, sem.at[1])
                cp.wait() if wait else cp.start()

        for wait in (False, True):
            lax.fori_loop(0, N_EXPERTS, lambda e, c, w=wait: (gap_copies(e, w), c)[1], 0)
            lax.fori_loop(0, n_tiles, lambda t, c, w=wait: (tile_copy(t, w), c)[1], 0)

    def scatter(xm_hbm, lo):
        base = (i - lo) * tt

        def issue(r, carry):
            src = xm_hbm.at[pl.ds(pl.multiple_of((base + r) * rpt, rpt), rpt)]
            for k in range(TOP_K):
                dst = xs_hbm.at[pl.ds(pl.multiple_of(pos_ref[k, r] * rpt, rpt), rpt)]
                pltpu.make_async_copy(src, dst, sem.at[0]).start(priority=k)
            return carry
        lax.fori_loop(0, tt, issue, 0, unroll=8)

    def drain():
        for k in range(TOP_K):
            pltpu.make_async_copy(xm_refs[0].at[pl.ds(0, tt * rpt)], xs_hbm.at[pl.ds(0, tt * rpt)], sem.at[0]).wait()

    lo = 0
    for xm_ref, nblk in zip(xm_refs, blocks):
        pl.when(jnp.logical_and(i >= lo, i < lo + nblk))(functools.partial(scatter, xm_ref, lo))
        lo += nblk
    pl.when(i > 0)(drain)
    pl.when(i == sum(blocks) - 1)(drain)


def _dispatch(pad_start, pad_len, valid, pos, xm_streams, n_tiles, tm):
    tt = pos.shape[-1]
    rpt = TOKEN_ROWS
    blocks = tuple(xm.shape[0] // (rpt * tt) for xm in xm_streams)
    in_specs = [pl.BlockSpec((None, TOP_K, tt), lambda i, a, b, c: (i, 0, 0), memory_space=pltpu.SMEM)]
    in_specs += [pl.BlockSpec(memory_space=pl.ANY) for _ in blocks]
    return pl.pallas_call(
        functools.partial(_dispatch_kernel, blocks=blocks, n_tiles=n_tiles),
        grid_spec=pltpu.PrefetchScalarGridSpec(
            num_scalar_prefetch=3,
            grid=(sum(blocks),),
            in_specs=in_specs,
            out_specs=pl.BlockSpec(memory_space=pl.ANY),
            scratch_shapes=[pltpu.VMEM((tm * rpt, LANE), F32), pltpu.SemaphoreType.DMA((2,))],
        ),
        out_shape=jax.ShapeDtypeStruct((n_tiles * tm * rpt, LANE), F32),
        compiler_params=_cparams(("arbitrary",)),
        name="moe_dispatch",
    )(pad_start, pad_len, valid, pos, *xm_streams)


def _moe_kernel(te_ref, tv_ref, x_ref, wg_hbm, wu_hbm, wd_hbm, y_ref,
                wg_f, wu_f, wd_f, wg_s, wu_s, wd_s, slot_ref, sem, *, layer, n_tiles):
    t = pl.program_id(0)
    rpt = TOKEN_ROWS
    tm = x_ref.shape[0] // rpt

    def weight_copies(e, slot):
        return [pltpu.make_async_copy(src.at[layer, e], dst.at[slot], sem.at[slot, k])
                for k, (src, dst) in enumerate(((wg_hbm, wg_f), (wu_hbm, wu_f), (wd_hbm, wd_f)))]

    @pl.when(tv_ref[t] == 1)
    def _():
        e = te_ref[t]
        first = jnp.logical_or(t == 0, te_ref[jnp.maximum(t - 1, 0)] != e)

        @pl.when(first)
        def _():
            @pl.when(t == 0)
            def _():
                slot_ref[0] = 0
                for cp in weight_copies(e, 0):
                    cp.start()

            slot = slot_ref[0]
            for cp in weight_copies(e, slot):
                cp.wait()
            at = lambda ref, i: ref[jnp.minimum(i, n_tiles - 1)]
            same = lambda i: (i < n_tiles) & (at(tv_ref, i) == 1) & (at(te_ref, i) == e)
            t2 = lax.while_loop(same, lambda i: i + 1, t + 1)

            @pl.when((t2 < n_tiles) & (at(tv_ref, t2) == 1))
            def _():
                for cp in weight_copies(at(te_ref, t2), 1 - slot):
                    cp.start()

            wg_s[...] = wg_f[slot].astype(BF16)
            wu_s[...] = wu_f[slot].astype(BF16)
            wd_s[...] = wd_f[slot].astype(BF16)
            slot_ref[0] = 1 - slot

        x = jnp.concatenate([_load_token_tile_col(x_ref, tm, rpt, j) for j in range(rpt)], axis=1).astype(BF16)
        g = _dot(x, wg_s[...])
        u = _dot(x, wu_s[...])
        h = (g * _sigmoid(g) * u).astype(BF16)
        _store_token_tiles(y_ref, _dot(h, wd_s[...]))

    @pl.when(tv_ref[t] != 1)
    def _():
        y_ref[...] = jnp.zeros_like(y_ref)


def _moe_experts(xs, tile_e, valid, n_tiles, layer, w_gate, w_up, w_down, tm):
    _, _, d, de = w_gate.shape
    rpt = TOKEN_ROWS
    grid_spec = pltpu.PrefetchScalarGridSpec(
        num_scalar_prefetch=2,
        grid=(n_tiles,),
        in_specs=[
            pl.BlockSpec((tm * rpt, LANE), lambda t, te, tv: (t * tv[t], 0)),
            pl.BlockSpec(memory_space=pl.ANY),
            pl.BlockSpec(memory_space=pl.ANY),
            pl.BlockSpec(memory_space=pl.ANY),
        ],
        out_specs=pl.BlockSpec((tm * rpt, LANE), lambda t, te, tv: (t, 0)),
        scratch_shapes=[
            pltpu.VMEM((2, d, de), F32),
            pltpu.VMEM((2, d, de), F32),
            pltpu.VMEM((2, de, d), F32),
            pltpu.VMEM((d, de), BF16),
            pltpu.VMEM((d, de), BF16),
            pltpu.VMEM((de, d), BF16),
            pltpu.SMEM((1,), I32),
            pltpu.SemaphoreType.DMA((2, 3)),
        ],
    )
    return pl.pallas_call(
        functools.partial(_moe_kernel, layer=layer, n_tiles=n_tiles),
        grid_spec=grid_spec,
        out_shape=jax.ShapeDtypeStruct((n_tiles * tm * rpt, LANE), F32),
        compiler_params=_cparams(("arbitrary",)),
        name="moe_experts",
    )(tile_e, valid, xs, w_gate, w_up, w_down)


def _combine_kernel(pos0_ref, posn_ref, x_ref, g_ref, rt_ref, ys_hbm, o_ref, buf, sem, *, n):
    i = pl.program_id(0)
    tt, d = x_ref.shape
    rpt = TOKEN_ROWS

    def gather(pos_ref, slot):
        def issue(r, carry):
            for k in range(TOP_K):
                src = ys_hbm.at[pl.ds(pl.multiple_of(pos_ref[k, r] * rpt, rpt), rpt)]
                dst = buf.at[slot, k, pl.ds(pl.multiple_of(r * rpt, rpt), rpt)]
                pltpu.make_async_copy(src, dst, sem.at[slot, k]).start(priority=k)
            return carry
        lax.fori_loop(0, tt, issue, 0, unroll=8)

    slot = lax.rem(i, 2)

    @pl.when(i == 0)
    def _():
        gather(pos0_ref, 0)

    @pl.when(i + 1 < n)
    def _():
        gather(posn_ref, 1 - slot)

    for k in range(TOP_K):
        pltpu.make_async_copy(ys_hbm.at[pl.ds(0, tt * rpt)], buf.at[slot, k], sem.at[slot, k]).wait()
    w1 = rt_ref[:, 2:3]
    w2 = rt_ref[:, 3:4]
    for j in range(rpt):
        r1 = buf[slot, 0, pl.ds(j, tt, stride=rpt), :]
        r2 = buf[slot, 1, pl.ds(j, tt, stride=rpt), :]
        sl = slice(j * LANE, (j + 1) * LANE)
        o_ref[:, sl] = x_ref[:, sl] + g_ref[:, sl] * (w1 * r1 + w2 * r2)


def _moe_combine(x, g2, rt, pos, tok_off, ys, tt):
    b, l, d = x.shape
    rpt = TOKEN_ROWS
    n = b * l
    per_b = l // tt
    per_blk = PLAN_BLOCK // tt
    blk0 = tok_off // PLAN_BLOCK

    def pos_map(step):
        return lambda i: (blk0 + step(i) // per_blk, 0, step(i) % per_blk)

    return pl.pallas_call(
        functools.partial(_combine_kernel, n=n // tt),
        grid=(n // tt,),
        in_specs=[
            pl.BlockSpec((None, TOP_K, tt), pos_map(lambda i: i), memory_space=pltpu.SMEM),
            pl.BlockSpec((None, TOP_K, tt), pos_map(lambda i: jnp.minimum(i + 1, n // tt - 1)),
                         memory_space=pltpu.SMEM),
            pl.BlockSpec((tt, d), lambda i: (i, 0)),
            pl.BlockSpec((None, 1, d), lambda i: (i // per_b, 0, 0)),
            pl.BlockSpec((tt, LANE), lambda i: (i, 0)),
            pl.BlockSpec(memory_space=pl.ANY),
        ],
        out_specs=pl.BlockSpec((tt, d), lambda i: (i, 0)),
        out_shape=jax.ShapeDtypeStruct((n, d), F32),
        scratch_shapes=[
            pltpu.VMEM((2, TOP_K, tt * rpt, LANE), F32),
            pltpu.SemaphoreType.DMA((2, TOP_K)),
        ],
        compiler_params=_cparams(("arbitrary",)),
        name="moe_combine",
    )(pos, pos, x.reshape(n, d), g2, rt.reshape(n, LANE), ys).reshape(b, l, d)


def _hier_moe(streams, layer, w_gate, w_up, w_down):
    sizes = [s[0].shape[0] * s[0].shape[1] for s in streams]
    n = sum(sizes)
    tm = MOE_TILE
    n_tiles = (TOP_K * n) // tm + N_EXPERTS
    assert n_tiles <= MAX_TILES and all(sz % PLAN_BLOCK == 0 for sz in sizes)
    rt_all = jnp.concatenate([s[2].reshape(-1, LANE) for s in streams], axis=0)
    pos, tile_e, valid, pad = _moe_plan(rt_all, tm)
    tile_e, valid = tile_e.reshape(-1), valid.reshape(-1)
    xs = _dispatch(pad[:, 0], pad[:, 1], valid, pos, [s[1] for s in streams], n_tiles, tm)
    ys = _moe_experts(xs, tile_e, valid, n_tiles, layer, w_gate, w_up, w_down, tm)
    outs = []
    off = 0
    for (xl, _, rt, g2), sz in zip(streams, sizes):
        outs.append(_moe_combine(xl, g2, rt, pos, off, ys, min(COMBINE_TILE, xl.shape[1])))
        off += sz
    return outs


def _inproj_cd_kernel(x_ref, sh_ref, sc_ref, w_ref, u_ref, *out_refs):
    h = (_rms(x_ref[...]) * (1.0 + sc_ref[...]) + sh_ref[...]).astype(BF16)
    tm = x_ref.shape[0]
    n_slab, rows, cw = u_ref.shape
    nb = rows // tm
    bi = pl.program_id(1)
    u = _dot(h, w_ref[:, 0:n_slab * cw])
    for s in range(n_slab):
        u_ref[s, pl.ds(bi, tm, stride=nb), :] = u[:, s * cw:(s + 1) * cw]
    off = n_slab * cw
    for ref in out_refs:
        n = ref.shape[-1]
        ref[...] = _dot(h, w_ref[:, off:off + n]).astype(ref.dtype)
        off += n


def _inproj_cd(x, sh, sc, w_bf16, widths, tm):
    b, l, d = x.shape
    n = w_bf16.shape[1]
    n_slab = widths[0] // LANE
    tok = lambda j, i: (i, j, 0)
    return pl.pallas_call(
        _inproj_cd_kernel,
        grid=(l // tm, b),
        in_specs=[
            pl.BlockSpec((None, tm, d), tok),
            pl.BlockSpec((None, 1, d), lambda j, i: (i, 0, 0)),
            pl.BlockSpec((None, 1, d), lambda j, i: (i, 0, 0)),
            pl.BlockSpec((d, n), lambda j, i: (0, 0)),
        ],
        out_specs=([pl.BlockSpec((n_slab, tm * b, LANE), lambda j, i: (0, j, 0))]
                   + [pl.BlockSpec((None, tm, w), tok) for w in widths[1:]]),
        out_shape=([jax.ShapeDtypeStruct((n_slab, l * b, LANE), F32)]
                   + [jax.ShapeDtypeStruct((b, l, w), F32) for w in widths[1:]]),
        compiler_params=_cparams(("arbitrary", "arbitrary")),
        name="inproj_cd",
    )(x, sh, sc, w_bf16)


def _s5_param_kernel(are_ref, aim_ref, ldt_ref, bre_ref, bim_ref, abre_ref, abim_ref, bbre_ref, bbim_ref):
    a_re = are_ref[...]
    a_im = aim_ref[...]
    dt = jnp.exp(ldt_ref[...])
    mag = jnp.exp(dt * a_re)
    ab_re = mag * jnp.cos(dt * a_im)
    ab_im = mag * jnp.sin(dt * a_im)
    den = a_re * a_re + a_im * a_im
    nr = ab_re - 1.0
    f_re = (nr * a_re + ab_im * a_im) / den
    f_im = (ab_im * a_re - nr * a_im) / den
    abre_ref[...] = ab_re
    abim_ref[...] = ab_im
    b_re = bre_ref[...]
    b_im = bim_ref[...]
    bbre_ref[...] = f_re[None] * b_re - f_im[None] * b_im
    bbim_ref[...] = f_re[None] * b_im + f_im[None] * b_re


def _s5_params(a_re, a_im, log_dt, b_re, b_im):
    nd, g, p = a_re.shape
    c = b_re.shape[-1]
    rows = nd * g * p // LANE
    flat = lambda t: t.reshape(rows, LANE)
    chan_major = lambda t: jnp.moveaxis(t, -1, 0).reshape(c, rows, LANE)
    ldt = jnp.broadcast_to(log_dt[:, :, None], (nd, g, p))
    shapes = [jax.ShapeDtypeStruct((rows, LANE), F32)] * 2 + [jax.ShapeDtypeStruct((c, rows, LANE), F32)] * 2
    ab_re, ab_im, bb_re, bb_im = pl.pallas_call(
        _s5_param_kernel, out_shape=shapes, name="s5_params",
    )(flat(a_re), flat(a_im), flat(ldt), chan_major(b_re), chan_major(b_im))
    unflat = lambda t: jnp.moveaxis(t.reshape(c, nd, g, p), 0, -1)
    return ab_re.reshape(nd, g, p), ab_im.reshape(nd, g, p), unflat(bb_re), unflat(bb_im)


def _s5_kernel(ufc_ref, ubc_ref, ufl_ref, ubl_ref, a_ref, bsb_ref, csb_ref, d_ref, yf_ref, yb_ref,
               hf, hb, cf, cb, *, n_ctx):
    i = pl.program_id(0)
    nb = cf.shape[0]
    steps = hf.shape[0] // nb
    n_super = bsb_ref.shape[1]
    cw = bsb_ref.shape[2]
    sw = bsb_ref.shape[3]
    half = sw // 2

    @pl.when(i == 0)
    def _():
        cf[...] = jnp.zeros_like(cf)
        cb[...] = jnp.zeros_like(cb)

    def drive(uf_ref, ub_ref):
        for s in range(n_super):
            hf[:, s * sw:(s + 1) * sw] = _dot(uf_ref[s].astype(BF16), bsb_ref[0, s])
            hb[:, s * sw:(s + 1) * sw] = _dot(ub_ref[s].astype(BF16), bsb_ref[1, s])

    pl.when(i < n_ctx)(functools.partial(drive, ufc_ref, ubc_ref))
    pl.when(i >= n_ctx)(functools.partial(drive, ufl_ref, ubl_ref))

    for s in range(n_super):
        re = slice(s * sw, s * sw + half)
        im = slice(s * sw + half, (s + 1) * sw)
        st = slice(s * half, (s + 1) * half)
        arf, aif = a_ref[0, 0, :, st], a_ref[0, 1, :, st]
        arb, aib = a_ref[1, 0, :, st], a_ref[1, 1, :, st]

        def step(t, carry):
            hfr, hfi, hbr, hbi = carry
            rf = pl.multiple_of(t * nb, nb)
            nfr = arf * hfr - aif * hfi + hf[pl.ds(rf, nb), re]
            nfi = arf * hfi + aif * hfr + hf[pl.ds(rf, nb), im]
            hf[pl.ds(rf, nb), re] = nfr
            hf[pl.ds(rf, nb), im] = nfi
            rb = pl.multiple_of((steps - 1 - t) * nb, nb)
            nbr = arb * hbr - aib * hbi + hb[pl.ds(rb, nb), re]
            nbi = arb * hbi + aib * hbr + hb[pl.ds(rb, nb), im]
            hb[pl.ds(rb, nb), re] = nbr
            hb[pl.ds(rb, nb), im] = nbi
            return nfr, nfi, nbr, nbi

        out = lax.fori_loop(0, steps, step, (cf[:, re], cf[:, im], cb[:, re], cb[:, im]), unroll=4)
        cf[:, re], cf[:, im], cb[:, re], cb[:, im] = out

    @pl.when(i >= n_ctx)
    def _():
        for s in range(n_super):
            skip = ufl_ref[s] * d_ref[:, s * cw:(s + 1) * cw]
            yf_ref[s] = _dot(hf[:, s * sw:(s + 1) * sw].astype(BF16), csb_ref[0, s]) + skip
            yb_ref[s] = _dot(hb[:, s * sw:(s + 1) * sw].astype(BF16), csb_ref[1, s])


def _s5_scan(u_c, u_l, a_bc, bsb, csb, d_skip, nb):
    n_slab, rows_l, _ = u_l.shape
    blk = S5_STEPS * nb
    nc = u_c.shape[1] // blk
    nl = rows_l // blk
    n_state = bsb.shape[1] * bsb.shape[3]
    clip = jnp.clip
    ctx_f = lambda i: (0, clip(i, 0, nc - 1), 0)
    ctx_b = lambda i: (0, clip(nc - 1 - i, 0, nc - 1), 0)
    lat_f = lambda i: (0, clip(i - nc, 0, nl - 1), 0)
    lat_b = lambda i: (0, clip(nl - 1 - (i - nc), 0, nl - 1), 0)
    whole = lambda a: pl.BlockSpec(a.shape, lambda i: (0,) * a.ndim)
    slab = lambda index_map: pl.BlockSpec((n_slab, blk, LANE), index_map)
    return pl.pallas_call(
        functools.partial(_s5_kernel, n_ctx=nc),
        grid=(nc + nl,),
        in_specs=[slab(ctx_f), slab(ctx_b), slab(lat_f), slab(lat_b), whole(a_bc), whole(bsb), whole(csb),
                  whole(d_skip)],
        out_specs=[slab(lat_f), slab(lat_b)],
        out_shape=[jax.ShapeDtypeStruct(u_l.shape, F32)] * 2,
        scratch_shapes=[
            pltpu.VMEM((blk, n_state), F32),
            pltpu.VMEM((blk, n_state), F32),
            pltpu.VMEM((nb, n_state), F32),
            pltpu.VMEM((nb, n_state), F32),
        ],
        compiler_params=_cparams(("arbitrary",)),
        name="s5_scan",
    )(u_c, u_c, u_l, u_l, a_bc, bsb, csb, d_skip)


def _s5_block_matrices(ab_re, ab_im, bb_re, bb_im, c_re, c_im, nb):
    nd, g, p = ab_re.shape
    c = bb_re.shape[-1]
    ns = g // S5_SUPER
    eye = jnp.eye(S5_SUPER, dtype=F32)

    def in_mat(bb):
        t = bb.reshape(nd, ns, S5_SUPER, p, c)
        return jnp.einsum('dsgpc,gh->dsgchp', t, eye).reshape(nd, ns, S5_SUPER * c, S5_SUPER * p)

    def out_mat(cc):
        t = cc.reshape(nd, ns, S5_SUPER, c, p)
        return jnp.einsum('dsgcp,gh->dsgphc', t, eye).reshape(nd, ns, S5_SUPER * p, S5_SUPER * c)

    bsb = jnp.concatenate([in_mat(bb_re), in_mat(bb_im)], axis=-1).astype(BF16)
    csb = jnp.concatenate([out_mat(c_re), out_mat(-c_im)], axis=-2).astype(BF16)
    a_bc = jnp.stack([ab_re.reshape(nd, g * p), ab_im.reshape(nd, g * p)], axis=1)
    a_bc = jnp.broadcast_to(a_bc[:, :, None, :], (nd, 2, nb, g * p))
    return a_bc, bsb, csb


def _head_scale(x):
    return lax.rsqrt(jnp.sum(x * x, axis=-1, keepdims=True) * (1.0 / QK_HEAD) + EPS)


def _mla_prep_lat_kernel(cq_ref, ckv_ref, krp_ref, krs_ref, qn_ref, kvn_ref, wq_ref, wqs_ref, wk_ref, wv_ref,
                         qc_ref, qs_ref, kc_ref, ks_ref, q_ref, k_ref, v_ref):
    ckvn = (_rms(ckv_ref[...]) * kvn_ref[...]).astype(BF16)
    v_ref[...] = _dot(ckvn, wv_ref[...]).astype(v_ref.dtype)
    kn = _dot(ckvn, wk_ref[...])
    krp = krp_ref[...]
    k_swap = krs_ref[...] * ks_ref[...]
    kc = kc_ref[...]
    for h in range(N_HEADS):
        sl = slice(h * HEAD_PAD, (h + 1) * HEAD_PAD)
        kh = kn[:, sl] + krp
        k_ref[:, sl] = (_head_scale(kh) * (kh * kc + k_swap)).astype(k_ref.dtype)
    cqn = (_rms(cq_ref[...]) * qn_ref[...]).astype(BF16)
    qn = _dot(cqn, wq_ref[...])
    q_swap = _dot(cqn, wqs_ref[...])
    qc = qc_ref[...]
    qs = qs_ref[...]
    for h in range(N_HEADS):
        sl = slice(h * HEAD_PAD, (h + 1) * HEAD_PAD)
        qh = qn[:, sl]
        scale = _head_scale(qh) * (QK_HEAD ** -0.5 * LOG2_E)
        q_ref[:, sl] = (scale * (qh * qc + q_swap[:, sl] * qs)).astype(q_ref.dtype)


def _mla_prep_ctx_kernel(ckv_ref, krp_ref, kvn_ref, wk_ref, wv_ref, kg_ref, k_ref, v_ref):
    ckvn = (_rms(ckv_ref[...]) * kvn_ref[...]).astype(BF16)
    v_ref[...] = _dot(ckvn, wv_ref[...]).astype(v_ref.dtype)
    kn = _dot(ckvn, wk_ref[...])
    krp = krp_ref[...]
    kg = kg_ref[...]
    for h in range(N_HEADS):
        sl = slice(h * HEAD_PAD, (h + 1) * HEAD_PAD)
        kh = kn[:, sl] + krp
        k_ref[:, sl] = (_head_scale(kh) * (kh * kg)).astype(k_ref.dtype)


def _mla_prep(kernel_fn, name, tok_args, const_args, table_args, n_q_out, tm):
    b, l, _ = tok_args[0].shape
    tok = lambda i, j: (i, j, 0)
    specs = [pl.BlockSpec((None, tm, a.shape[-1]), tok) for a in tok_args]
    specs += [pl.BlockSpec(a.shape, lambda i, j: (0, 0)) for a in const_args]
    specs += [pl.BlockSpec((tm, HEAD_PAD), lambda i, j: (j, 0)) for _ in table_args]
    hk = N_HEADS * HEAD_PAD
    hv = N_HEADS * V_HEAD
    widths = [hk] * n_q_out + [hk, hv]
    return pl.pallas_call(
        kernel_fn,
        grid=(b, l // tm),
        in_specs=specs,
        out_specs=[pl.BlockSpec((None, tm, w), tok) for w in widths],
        out_shape=[jax.ShapeDtypeStruct((b, l, w), BF16) for w in widths],
        compiler_params=_cparams(("arbitrary", "arbitrary")),
        name=name,
    )(*tok_args, *const_args, *table_args)


def _attn_kernel(q_ref, kl_ref, kc_ref, vl_ref, vc_ref, o_ref, *, heads):
    nt = (((1,), (1,)), ((), ()))
    for pr in range(heads // 2):
        vsl = slice(pr * 2 * V_HEAD, (pr + 1) * 2 * V_HEAD)
        outs = []
        for hh in range(2):
            h = 2 * pr + hh
            sl = slice(h * HEAD_PAD, (h + 1) * HEAD_PAD)
            q = q_ref[:, sl]
            s_l = lax.dot_general(q, kl_ref[:, sl], nt, preferred_element_type=F32)
            s_c = lax.dot_general(q, kc_ref[:, sl], nt, preferred_element_type=F32)
            m = jnp.maximum(jnp.max(s_l, axis=-1, keepdims=True), jnp.max(s_c, axis=-1, keepdims=True))
            p_l = jnp.exp2(s_l - m)
            p_c = jnp.exp2(s_c - m)
            den = jnp.sum(p_l, axis=-1, keepdims=True) + jnp.sum(p_c, axis=-1, keepdims=True)
            o = _dot(p_l.astype(BF16), vl_ref[:, vsl]) + _dot(p_c.astype(BF16), vc_ref[:, vsl])
            outs.append(o / den)
        lane = lax.broadcasted_iota(I32, outs[0].shape, 1)
        o_ref[:, vsl] = jnp.where(lane < V_HEAD, outs[0], outs[1]).astype(o_ref.dtype)


def _attention(q, k_l, k_c, v_l, v_c, tq, heads):
    b, l, _ = q.shape
    lc = k_c.shape[1]
    hp = N_HEADS // heads
    kw = heads * HEAD_PAD
    vw = heads * V_HEAD
    return pl.pallas_call(
        functools.partial(_attn_kernel, heads=heads),
        grid=(b, hp, l // tq),
        in_specs=[
            pl.BlockSpec((None, tq, kw), lambda i, h, j: (i, j, h)),
            pl.BlockSpec((None, l, kw), lambda i, h, j: (i, 0, h)),
            pl.BlockSpec((None, lc, kw), lambda i, h, j: (i, 0, h)),
            pl.BlockSpec((None, l, vw), lambda i, h, j: (i, 0, h)),
            pl.BlockSpec((None, lc, vw), lambda i, h, j: (i, 0, h)),
        ],
        out_specs=pl.BlockSpec((None, tq, vw), lambda i, h, j: (i, j, h)),
        out_shape=jax.ShapeDtypeStruct((b, l, N_HEADS * V_HEAD), BF16),
        compiler_params=_cparams(("arbitrary", "arbitrary", "arbitrary")),
        name="mla_attention",
    )(q, k_l, k_c, v_l, v_c)


def _mix_cd_kernel(yf_ref, yb_ref, o_ref, x_ref, g1_ref, sh2_ref, sc2_ref,
                   wglu_ref, wout_ref, wr_ref, br_ref, xl_ref, xm_ref, rt_ref):
    tm = x_ref.shape[0]
    n_slab, rows, _ = yf_ref.shape
    nb = rows // tm
    bi = pl.program_id(1)
    dc = n_slab * LANE
    pick = lambda ref, s: ref[s, pl.ds(bi, tm, stride=nb), :]
    y = jnp.concatenate([pick(yf_ref, s) + pick(yb_ref, s) for s in range(n_slab)], axis=1)
    g = jax.nn.gelu(y)
    s5 = (g * _sigmoid(_dot(g.astype(BF16), wglu_ref[...]))).astype(BF16)
    y_mix = _dot(s5, wout_ref[0:dc, :]) + _dot(o_ref[...], wout_ref[dc:, :])
    _residual_and_route(x_ref[...], y_mix, g1_ref[...], sh2_ref[...], sc2_ref[...], wr_ref, br_ref,
                        xl_ref, xm_ref, rt_ref)


def _mix_cd(yf, yb, o, x, g1, sh2, sc2, wglu, wout, wr, br, tm):
    b, l, d = x.shape
    n_slab = yf.shape[0]
    tok = lambda j, i: (i, j, 0)
    per_b = lambda j, i: (i, 0, 0)
    const2 = lambda j, i: (0, 0)
    tm_blk = pl.BlockSpec((n_slab, tm * b, LANE), lambda j, i: (0, j, 0))
    return pl.pallas_call(
        _mix_cd_kernel,
        grid=(l // tm, b),
        in_specs=[
            tm_blk,
            tm_blk,
            pl.BlockSpec((None, tm, o.shape[-1]), tok),
            pl.BlockSpec((None, tm, d), tok),
            pl.BlockSpec((None, 1, d), per_b),
            pl.BlockSpec((None, 1, d), per_b),
            pl.BlockSpec((None, 1, d), per_b),
            pl.BlockSpec(wglu.shape, const2),
            pl.BlockSpec(wout.shape, const2),
            pl.BlockSpec(wr.shape, const2),
            pl.BlockSpec((1, LANE), const2),
        ],
        out_specs=[
            pl.BlockSpec((None, tm, d), tok),
            pl.BlockSpec((tm * TOKEN_ROWS, LANE), lambda j, i: (i * (l // tm) + j, 0)),
            pl.BlockSpec((None, tm, LANE), tok),
        ],
        out_shape=[
            jax.ShapeDtypeStruct((b, l, d), F32),
            jax.ShapeDtypeStruct((b * l * TOKEN_ROWS, LANE), F32),
            jax.ShapeDtypeStruct((b, l, LANE), F32),
        ],
        compiler_params=_cparams(("arbitrary", "arbitrary")),
        name="mix_cd",
    )(yf, yb, o, x, g1, sh2, sc2, wglu, wout, wr, br)


def _router_weights(w_grp, b_grp, w_exp, b_exp):
    d = w_grp.shape[0]
    pad = LANE - N_GROUPS - N_EXPERTS
    wr = jnp.concatenate([w_grp, w_exp, jnp.zeros((d, pad), F32)], axis=1)
    br = jnp.concatenate([b_grp, b_exp, jnp.zeros((pad,), F32)]).reshape(1, LANE)
    return wr, br


def _pad_heads(w, width):
    k = w.shape[0]
    w = w.reshape(k, N_HEADS, width)
    return jnp.pad(w, ((0, 0), (0, 0), (0, HEAD_PAD - width))).reshape(k, N_HEADS * HEAD_PAD)


def _rope_tables(l):
    rows = l // GRID_W
    row = jnp.repeat(jnp.arange(rows, dtype=F32), GRID_W)
    col = jnp.tile(jnp.arange(GRID_W, dtype=F32), rows)
    inv_freq = ROPE_BASE ** (-jnp.arange(AXIS_PAIRS, dtype=F32) / AXIS_PAIRS)
    ar, ac = row[:, None] * inv_freq, col[:, None] * inv_freq
    one = jnp.ones((l, QK_NOPE), F32)
    zn = jnp.zeros((l, QK_NOPE), F32)
    zp = jnp.zeros((l, HEAD_PAD - QK_HEAD), F32)
    cos = jnp.concatenate([one, jnp.cos(ar), jnp.cos(ar), jnp.cos(ac), jnp.cos(ac), zp], axis=1)
    sin = jnp.concatenate([zn, jnp.sin(ar), jnp.sin(ar), jnp.sin(ac), jnp.sin(ac), zp], axis=1)
    return cos, sin


def _rot_swap(t, signed):
    a = AXIS_PAIRS
    s = -1.0 if signed else 1.0
    return jnp.concatenate([s * t[..., a:2 * a], t[..., 0:a], s * t[..., 3 * a:4 * a], t[..., 2 * a:3 * a]], axis=-1)


def _head_lanes(rot):
    pad = [(0, 0)] * (rot.ndim - 1) + [(QK_NOPE, HEAD_PAD - QK_HEAD)]
    return jnp.pad(rot, pad)


def _split_mod(mod_row, b, d):
    parts = jnp.split(mod_row, 6, axis=-1)
    return [jnp.broadcast_to(p.reshape(-1, 1, d), (b, 1, d)) for p in parts]


def kernel(x, c, ctx, c_ctx, ada_w, ada_b, ab_w_in, sgu_norm, sgu_w, sgu_b, conv_w, ab_w_out, cd_w_in, s5_a_re, s5_a_im, s5_log_dt, s5_b_re, s5_b_im, s5_c_re, s5_c_im, s5_d, s5_w_glu, mla_q_norm, mla_kv_norm, mla_w_uq, mla_w_uk, mla_w_uv, mla_qn_gain, mla_kn_gain, cd_w_out, moe_w_grp, moe_b_grp, moe_w_exp, moe_b_exp, moe_w_gate, moe_w_up, moe_w_down):
    b, l, d = x.shape
    lc = ctx.shape[1]
    depth = ada_w.shape[0]
    assert depth == 2, "layer 0 = gated-MLP/conv mixers, layer 1 = S5/attention mixers"
    assert d == TOKEN_ROWS * LANE, "token-tile layout: one vreg tile per token"
    tm = min(TOKEN_TILE, l)
    tmc = min(TOKEN_TILE, lc)

    mod_rows = 2 * SUBLANE
    cvec = jnp.zeros((mod_rows, d), F32).at[:b].set(c).at[b].set(c_ctx)
    mod = _ada(cvec, ada_w, ada_b)

    sh1, sc1, g1, sh2, sc2, g2 = _split_mod(mod[0, :b], b, d)
    csh1, csc1, cg1, csh2, csc2, cg2 = _split_mod(mod[0, b:b + 1], b, d)
    w_in = ab_w_in[0].astype(BF16)
    wr, br = _router_weights(moe_w_grp[0], moe_b_grp[0], moe_w_exp[0], moe_b_exp[0])
    sgub = jnp.repeat(sgu_b[0].T, d // G_A, axis=1)
    mix_args = (sgu_norm[0].reshape(1, d), sgu_w[0].astype(BF16), sgub, conv_w[0],
                ab_w_out[0].astype(BF16), wr, br)
    z, p = _inproj_ab(x, sh1, sc1, w_in, tm)
    xl, xm, rt = _mix_ab(z, p, x, g1, sh2, sc2, *mix_args, tm)
    zc, pc = _inproj_ab(ctx, csh1, csc1, w_in, tmc)
    xc, xmc, rtc = _mix_ab(zc, pc, ctx, cg1, csh2, csc2, *mix_args, tmc)
    xl, xc = _hier_moe([(xl, xm, rt, g2), (xc, xmc, rtc, cg2)], 0, moe_w_gate, moe_w_up, moe_w_down)

    sh1, sc1, g1, sh2, sc2, g2 = _split_mod(mod[1, :b], b, d)
    csh1, csc1, _, _, _, _ = _split_mod(mod[1, b:b + 1], b, d)
    d_c = s5_d.shape[-1]
    q_lora = mla_q_norm.shape[-1]
    kv_lora = mla_kv_norm.shape[-1]
    w_cd = cd_w_in[0]
    o_kr = d_c + q_lora + kv_lora
    w_kr = _head_lanes(w_cd[:, o_kr:])
    w_krs = _head_lanes(_rot_swap(w_cd[:, o_kr:], True))
    w_cd = jnp.concatenate([w_cd[:, :o_kr], w_kr, w_krs], axis=1).astype(BF16)
    widths = (d_c, q_lora, kv_lora, HEAD_PAD, HEAD_PAD)
    u_l, cq_l, ckv_l, krp_l, krs_l = _inproj_cd(xl, sh1, sc1, w_cd, widths, tm)
    u_c, _, ckv_c, krp_c, _ = _inproj_cd(xc, csh1, csc1, w_cd, widths, tmc)

    ab_re, ab_im, bb_re, bb_im = _s5_params(s5_a_re[0], s5_a_im[0], s5_log_dt[0], s5_b_re[0], s5_b_im[0])
    a_bc, bsb, csb = _s5_block_matrices(ab_re, ab_im, bb_re, bb_im, s5_c_re[0], s5_c_im[0], b)
    yf, yb = _s5_scan(u_c, u_l, a_bc, bsb, csb, s5_d[0].reshape(1, d_c), b)

    wq = _pad_heads(mla_w_uq[0], QK_HEAD).astype(BF16)
    wk = _pad_heads(mla_w_uk[0], QK_NOPE).astype(BF16)
    wv = mla_w_uv[0].astype(BF16)
    w_uq = mla_w_uq[0].reshape(q_lora, N_HEADS, QK_HEAD)
    wqs = _head_lanes(_rot_swap(w_uq[..., QK_NOPE:], True)).reshape(q_lora, N_HEADS * HEAD_PAD).astype(BF16)
    qg, kg = mla_qn_gain[0], mla_kn_gain[0]
    full = lambda g: jnp.pad(g, (0, HEAD_PAD - QK_HEAD)).reshape(1, HEAD_PAD)
    partner = lambda g: _head_lanes(_rot_swap(g[QK_NOPE:], False)).reshape(1, HEAD_PAD)
    cos, sin = _rope_tables(l)
    tables = (cos * full(qg), sin * partner(qg), cos * full(kg), sin * partner(kg))
    qn = mla_q_norm[0].reshape(1, q_lora)
    kvn = mla_kv_norm[0].reshape(1, kv_lora)
    q_l, k_l, v_l = _mla_prep(_mla_prep_lat_kernel, "mla_prep_lat", (cq_l, ckv_l, krp_l, krs_l),
                              (qn, kvn, wq, wqs, wk, wv), tables, 1, tm)
    k_c, v_c = _mla_prep(_mla_prep_ctx_kernel, "mla_prep_ctx", (ckv_c, krp_c), (kvn, wk, wv, full(kg)), (), 0, tmc)
    o_l = _attention(q_l, k_l, k_c, v_l, v_c, ATTN_Q_TILE, ATTN_HEADS)

    wr, br = _router_weights(moe_w_grp[1], moe_b_grp[1], moe_w_exp[1], moe_b_exp[1])
    xl, xm, rt = _mix_cd(yf, yb, o_l, xl, g1, sh2, sc2,
                         s5_w_glu[0].astype(BF16), cd_w_out[0].astype(BF16), wr, br, MIX_CD_TILE)
    (xl,) = _hier_moe([(xl, xm, rt, g2)], 1, moe_w_gate, moe_w_up, moe_w_down)
    return xl
```

```python
import functools
import math

import jax
import jax.numpy as jnp
from jax import lax
from jax.experimental import pallas as pl
from jax.experimental.pallas import tpu as pltpu

F32 = jnp.float32
BF16 = jnp.bfloat16
I32 = jnp.int32
HIGHEST = lax.Precision.HIGHEST

EPS = 1e-6
LOG2_E = math.log2(math.e)
GRID_W = 64
CHUNK = 128
G_A = 8
CONV_W = 3
S5_GROUP = 16
S5_STATE = 64
N_HEADS = 8
QK_NOPE = 64
QK_ROPE = 32
QK_HEAD = QK_NOPE + QK_ROPE
V_HEAD = 64
AXIS_PAIRS = QK_ROPE // 4
ROPE_BASE = 10000.0
N_GROUPS = 4
EXPERTS_PER_GROUP = 8
N_EXPERTS = N_GROUPS * EXPERTS_PER_GROUP
TOP_K = 2

LANE = 128
SUBLANE = 8
HEAD_PAD = LANE
TOKEN_TILE = 512
MIX_SUBTILES = 2
MIX_CD_TILE = 256
MOE_TILE = 512
W_SLOTS = 3
COMBINE_TILE = 512
PLAN_BLOCK = 1024
MAX_TILES = 256
TOKEN_ROWS = SUBLANE
ATTN_Q_TILE = 512
ATTN_HEADS = 8
S5_STEPS = 64
S5_SUPER = LANE // S5_GROUP
VMEM_LIMIT = 56 * 1024 * 1024


def _cparams(sem):
    return pltpu.CompilerParams(dimension_semantics=sem, vmem_limit_bytes=VMEM_LIMIT)


def _rms(x):
    return x * lax.rsqrt(jnp.mean(x * x, axis=-1, keepdims=True) + EPS)


def _sigmoid(x):
    return 1.0 / (1.0 + jnp.exp(-x))


def _dot(a, b):
    return jnp.dot(a, b, preferred_element_type=F32)


def _ada_kernel(c_ref, w_ref, b_ref, o_ref):
    c = c_ref[...]
    s = c * _sigmoid(c)
    o_ref[...] = jnp.dot(s, w_ref[...], precision=HIGHEST, preferred_element_type=F32) + b_ref[...]


def _ada(cvec, ada_w, ada_b):
    depth, d, n = ada_w.shape
    rows = cvec.shape[0]
    tn = 1024
    return pl.pallas_call(
        _ada_kernel,
        grid=(depth, n // tn),
        in_specs=[
            pl.BlockSpec((rows, d), lambda i, j: (0, 0)),
            pl.BlockSpec((None, d, tn), lambda i, j: (i, 0, j)),
            pl.BlockSpec((None, 1, tn), lambda i, j: (i, 0, j)),
        ],
        out_specs=pl.BlockSpec((None, rows, tn), lambda i, j: (i, 0, j)),
        out_shape=jax.ShapeDtypeStruct((depth, rows, n), F32),
        compiler_params=_cparams(("arbitrary", "arbitrary")),
        name="ada_mod",
    )(cvec, ada_w, ada_b.reshape(depth, 1, n))


def _route(xm, wr_ref, br_ref):
    hi = xm.astype(BF16)
    lo = (xm - hi.astype(F32)).astype(BF16)
    w = wr_ref[...]
    w_hi = w.astype(BF16)
    w_lo = (w - w_hi.astype(F32)).astype(BF16)
    logits = _dot(jnp.concatenate([hi, lo, hi], axis=1), jnp.concatenate([w_hi, w_hi, w_lo], axis=0)) + br_ref[...]
    lane = lax.broadcasted_iota(I32, logits.shape, 1).astype(F32)
    neg = jnp.float32(-1e30)
    big = jnp.float32(1e6)
    is_grp = lane < N_GROUPS
    gl = jnp.where(is_grp, logits, neg)
    gmax = jnp.max(gl, axis=-1, keepdims=True)
    gsum = jnp.sum(jnp.where(is_grp, jnp.exp(gl - gmax), 0.0), axis=-1, keepdims=True)
    p_top = 1.0 / gsum
    grp = jnp.min(jnp.where(gl == gmax, lane, big), axis=-1, keepdims=True)
    eidx = lane - N_GROUPS
    in_grp = (eidx >= grp * EXPERTS_PER_GROUP) & (eidx < (grp + 1.0) * EXPERTS_PER_GROUP)
    el = jnp.where(in_grp, logits, neg)
    m1 = jnp.max(el, axis=-1, keepdims=True)
    i1 = jnp.min(jnp.where(el == m1, lane, big), axis=-1, keepdims=True)
    el2 = jnp.where(lane == i1, neg, el)
    m2 = jnp.max(el2, axis=-1, keepdims=True)
    i2 = jnp.min(jnp.where(el2 == m2, lane, big), axis=-1, keepdims=True)
    t = jnp.exp(m2 - m1)
    w1 = p_top / (1.0 + t)
    w2 = p_top * t / (1.0 + t)
    e1 = i1 - N_GROUPS
    e2 = i2 - N_GROUPS
    return jnp.where(lane == 0, e1, jnp.where(lane == 1, e2, jnp.where(lane == 2, w1, jnp.where(lane == 3, w2, 0.0))))


def _residual_and_route(x, y, g1, sh2, sc2, wr_ref, br_ref, xl_ref, xm_ref, rt_ref):
    xl = x + g1 * y
    xl_ref[...] = xl
    xm = _rms(xl) * (1.0 + sc2) + sh2
    _store_token_tiles(xm_ref, xm)
    rt_ref[...] = _route(xm, wr_ref, br_ref)


def _inproj_ab_kernel(x_ref, sh_ref, sc_ref, w_ref, z_ref, p_ref):
    d = x_ref.shape[-1]
    h = (_rms(x_ref[...]) * (1.0 + sc_ref[...]) + sh_ref[...]).astype(BF16)
    for j in range(3):
        z_ref[:, j * d:(j + 1) * d] = _dot(h, w_ref[:, j * d:(j + 1) * d]).astype(BF16)
    gate_c = _dot(h, w_ref[:, 3 * d:4 * d])
    xb = _dot(h, w_ref[:, 4 * d:5 * d])
    p_ref[...] = gate_c * xb


def _inproj_ab(x, sh, sc, w_bf16, tm):
    b, l, d = x.shape
    n = w_bf16.shape[1]
    return pl.pallas_call(
        _inproj_ab_kernel,
        grid=(b, l // tm),
        in_specs=[
            pl.BlockSpec((None, tm, d), lambda i, j: (i, j, 0)),
            pl.BlockSpec((None, 1, d), lambda i, j: (i, 0, 0)),
            pl.BlockSpec((None, 1, d), lambda i, j: (i, 0, 0)),
            pl.BlockSpec((d, n), lambda i, j: (0, 0)),
        ],
        out_specs=[
            pl.BlockSpec((None, tm, 3 * d), lambda i, j: (i, j, 0)),
            pl.BlockSpec((None, tm, d), lambda i, j: (i, j, 0)),
        ],
        out_shape=[
            jax.ShapeDtypeStruct((b, l, 3 * d), BF16),
            jax.ShapeDtypeStruct((b, l, d), F32),
        ],
        compiler_params=_cparams(("arbitrary", "arbitrary")),
        name="inproj_ab",
    )(x, sh, sc, w_bf16)


def _mix_ab_kernel(z_ref, p_ref, pprev_ref, pnext_ref, x_ref, g1_ref, sh2_ref, sc2_ref,
                   sgun_ref, sguw_ref, sgub_ref, convw_ref, wout_ref, wr_ref, br_ref,
                   xl_ref, xm_ref, rt_ref, pbuf, ycat):
    i = pl.program_id(1)
    nt = pl.num_programs(1)
    tm, d = x_ref.shape
    cg = d // G_A
    pbuf[SUBLANE:SUBLANE + tm, :] = p_ref[...]
    pbuf[0:SUBLANE, :] = jnp.where(i > 0, pprev_ref[...], 0.0)
    pbuf[SUBLANE + tm:2 * SUBLANE + tm, :] = jnp.where(i < nt - 1, pnext_ref[...], 0.0)
    ts = tm // (MIX_SUBTILES if tm % (MIX_SUBTILES * CHUNK) == 0 else 1)
    for t0 in range(0, tm, ts):
        rows = slice(t0, t0 + ts)
        v = jax.nn.gelu(z_ref[rows, d:2 * d].astype(F32))
        vc = (_rms(v) * sgun_ref[...]).astype(BF16)
        for c in range(ts // CHUNK):
            lr = c * CHUNK
            r0 = t0 + lr
            cols = [_dot(sguw_ref[g], vc[lr:lr + CHUNK, g * cg:(g + 1) * cg]) for g in range(G_A)]
            s = jnp.concatenate(cols, axis=1) + sgub_ref[...]
            u = jax.nn.gelu(z_ref[r0:r0 + CHUNK, 0:d].astype(F32))
            ycat[r0:r0 + CHUNK, 0:d] = (u * s).astype(BF16)
        conv = (convw_ref[0:1, :] * pbuf[SUBLANE - 1 + t0:SUBLANE - 1 + t0 + ts, :]
                + convw_ref[1:2, :] * pbuf[SUBLANE + t0:SUBLANE + t0 + ts, :]
                + convw_ref[2:3, :] * pbuf[SUBLANE + 1 + t0:SUBLANE + 1 + t0 + ts, :])
        ycat[rows, d:2 * d] = (z_ref[rows, 2 * d:3 * d].astype(F32) * conv).astype(BF16)
        y = _dot(ycat[rows, :], wout_ref[...])
        _residual_and_route(x_ref[rows, :], y, g1_ref[...], sh2_ref[...], sc2_ref[...], wr_ref, br_ref,
                            xl_ref.at[rows, :], xm_ref.at[pl.ds(t0 * TOKEN_ROWS, ts * TOKEN_ROWS), :],
                            rt_ref.at[rows, :])


def _mix_ab(z, p, x, g1, sh2, sc2, sgun, sguw, sgub, convw, wout, wr, br, tm):
    b, l, d = x.shape
    hb = tm // SUBLANE
    nhb = l // SUBLANE
    tok = lambda i, j: (i, j, 0)
    per_b = lambda i, j: (i, 0, 0)
    const2 = lambda i, j: (0, 0)
    return pl.pallas_call(
        _mix_ab_kernel,
        grid=(b, l // tm),
        in_specs=[
            pl.BlockSpec((None, tm, 3 * d), tok),
            pl.BlockSpec((None, tm, d), tok),
            pl.BlockSpec((None, SUBLANE, d), lambda i, j: (i, jnp.maximum(j * hb - 1, 0), 0)),
            pl.BlockSpec((None, SUBLANE, d), lambda i, j: (i, jnp.minimum((j + 1) * hb, nhb - 1), 0)),
            pl.BlockSpec((None, tm, d), tok),
            pl.BlockSpec((None, 1, d), per_b),
            pl.BlockSpec((None, 1, d), per_b),
            pl.BlockSpec((None, 1, d), per_b),
            pl.BlockSpec((1, d), const2),
            pl.BlockSpec((G_A, CHUNK, CHUNK), lambda i, j: (0, 0, 0)),
            pl.BlockSpec((CHUNK, d), const2),
            pl.BlockSpec((CONV_W, d), const2),
            pl.BlockSpec((2 * d, d), const2),
            pl.BlockSpec(wr.shape, const2),
            pl.BlockSpec((1, LANE), const2),
        ],
        out_specs=[
            pl.BlockSpec((None, tm, d), tok),
            pl.BlockSpec((tm * TOKEN_ROWS, LANE), lambda i, j: (i * (l // tm) + j, 0)),
            pl.BlockSpec((None, tm, LANE), tok),
        ],
        out_shape=[
            jax.ShapeDtypeStruct((b, l, d), F32),
            jax.ShapeDtypeStruct((b * l * TOKEN_ROWS, LANE), F32),
            jax.ShapeDtypeStruct((b, l, LANE), F32),
        ],
        scratch_shapes=[
            pltpu.VMEM((tm + 2 * SUBLANE, d), F32),
            pltpu.VMEM((tm, 2 * d), BF16),
        ],
        compiler_params=_cparams(("arbitrary", "arbitrary")),
        name="mix_ab",
    )(z, p, p, p, x, g1, sh2, sc2, sgun, sguw, sgub, convw, wout, wr, br)


def _store_token_tiles(ref, val):
    tm, d = val.shape
    rpt = d // LANE
    for j in range(rpt):
        ref[pl.ds(j, tm, stride=rpt), :] = val[:, j * LANE:(j + 1) * LANE]


def _load_token_tile_col(ref, tm, rpt, j):
    return ref[pl.ds(j, tm, stride=rpt), :]


def _plan_kernel(rt_ref, tri_ref, pos_ref, te_ref, tv_ref, pad_ref, run, *, tm):
    ps = pl.program_id(0)
    i = pl.program_id(1)
    blk = rt_ref.shape[0]
    ne = N_EXPERTS

    @pl.when(jnp.logical_and(ps == 0, i == 0))
    def _():
        run[...] = jnp.zeros_like(run)

    @pl.when(jnp.logical_and(ps == 1, i == 0))
    def _():
        counts = run[...]
        padded = jnp.floor((counts + (tm - 1)) * (1.0 / tm)) * tm
        r = lax.broadcasted_iota(I32, (ne, ne), 0)
        c = lax.broadcasted_iota(I32, (ne, ne), 1)
        starts = jnp.dot((c < r).astype(F32), padded, precision=HIGHEST, preferred_element_type=F32)
        nt = te_ref.shape[1]
        ends = jnp.concatenate([(starts + padded) * (1.0 / tm)] * (nt // LANE), axis=1)
        t = lax.broadcasted_iota(I32, (ne, nt), 1).astype(F32)
        te = jnp.sum((t >= ends).astype(F32), axis=0, keepdims=True)
        total = jnp.max(ends, axis=0, keepdims=True)
        te_ref[...] = jnp.minimum(te, ne - 1.0).astype(I32)
        tv_ref[...] = (t[0:1, :] < total).astype(I32)
        lane = lax.broadcasted_iota(I32, (ne, LANE), 1)
        pad_ref[...] = jnp.where(lane == 0, starts + counts,
                                 jnp.where(lane == 1, padded - counts, 0.0)).astype(I32)
        run[...] = starts

    slab_t = rt_ref[...].T
    ex = lax.broadcasted_iota(I32, (ne, blk), 0).astype(F32)
    oh1 = ex == slab_t[0:1, :]
    oh2 = ex == slab_t[1:2, :]
    oh = jnp.where(jnp.logical_or(oh1, oh2), 1.0, 0.0)
    @pl.when(ps == 0)
    def _():
        pos_ref[...] = jnp.zeros_like(pos_ref)

    @pl.when(ps == 1)
    def _():
        prefix = _dot(oh.astype(BF16), tri_ref[...]) + run[:, 0:1]
        pos1 = jnp.sum(jnp.where(oh1, prefix, 0.0), axis=0, keepdims=True)
        pos2 = jnp.sum(jnp.where(oh2, prefix, 0.0), axis=0, keepdims=True)
        pos_ref[...] = jnp.concatenate([pos1, pos2], axis=0).astype(I32)

    run[...] = run[...] + jnp.sum(oh, axis=1, keepdims=True)


def _moe_plan(rt_all, tm):
    n = rt_all.shape[0]
    blk = PLAN_BLOCK
    nb = n // blk
    tri = jnp.triu(jnp.ones((blk, blk), BF16), k=1)
    return pl.pallas_call(
        functools.partial(_plan_kernel, tm=tm),
        grid=(2, nb),
        in_specs=[
            pl.BlockSpec((blk, LANE), lambda p, i: (i, 0)),
            pl.BlockSpec((blk, blk), lambda p, i: (0, 0)),
        ],
        out_specs=[
            pl.BlockSpec((None, TOP_K, blk), lambda p, i: (jnp.where(p == 0, nb, i), 0, 0)),
            pl.BlockSpec((1, MAX_TILES), lambda p, i: (0, 0)),
            pl.BlockSpec((1, MAX_TILES), lambda p, i: (0, 0)),
            pl.BlockSpec((N_EXPERTS, LANE), lambda p, i: (0, 0)),
        ],
        out_shape=[
            jax.ShapeDtypeStruct((nb + 1, TOP_K, blk), I32),
            jax.ShapeDtypeStruct((1, MAX_TILES), I32),
            jax.ShapeDtypeStruct((1, MAX_TILES), I32),
            jax.ShapeDtypeStruct((N_EXPERTS, LANE), I32),
        ],
        scratch_shapes=[pltpu.VMEM((N_EXPERTS, LANE), F32)],
        compiler_params=_cparams(("arbitrary", "arbitrary")),
        name="moe_plan",
    )(rt_all, tri)


def _dispatch_kernel(ps_ref, pl_ref, tv_ref, pos_ref, *rest, blocks, n_tiles):
    xm_refs = rest[:len(blocks)]
    xs_hbm, zbuf, sem = rest[len(blocks):]
    i = pl.program_id(0)
    tt = pos_ref.shape[1]
    rpt = xm_refs[0].shape[0] // tt
    tile_rows = zbuf.shape[0]

    @pl.when(i == 0)
    def _():
        zbuf[...] = jnp.zeros_like(zbuf)

        def gap_copies(e, wait):
            base = ps_ref[e]
            n = pl_ref[e]
            bit = tile_rows // (2 * rpt)
            while bit >= 1:
                lower = n & (bit - 1)

                @pl.when((n & bit) != 0)
                def _(lower=lower, bit=bit):
                    dst = xs_hbm.at[pl.ds((base + lower) * rpt, bit * rpt)]
                    cp = pltpu.make_async_copy(zbuf.at[pl.ds(0, bit * rpt)], dst, sem.at[1])
                    cp.wait() if wait else cp.start()
                bit //= 2

        def tile_copy(t, wait):
            @pl.when(tv_ref[t] != 1)
            def _():
                dst = xs_hbm.at[pl.ds(pl.multiple_of(t * tile_rows, tile_rows), tile_rows)]
                cp = pltpu.make_async_copy(zbuf, dst, sem.at[1])
                cp.wait() if wait else cp.start()

        for wait in (False, True):
            lax.fori_loop(0, N_EXPERTS, lambda e, c, w=wait: (gap_copies(e, w), c)[1], 0)
            lax.fori_loop(0, n_tiles, lambda t, c, w=wait: (tile_copy(t, w), c)[1], 0)

    def scatter(xm_ref):
        def issue(r, carry):
            src = xm_ref.at[pl.ds(pl.multiple_of(r * rpt, rpt), rpt)]
            for k in range(TOP_K):
                dst = xs_hbm.at[pl.ds(pl.multiple_of(pos_ref[k, r] * rpt, rpt), rpt)]
                pltpu.make_async_copy(src, dst, sem.at[0]).start(priority=k)
            return carry
        lax.fori_loop(0, tt, issue, 0, unroll=8)
        for k in range(TOP_K):
            pltpu.make_async_copy(xm_ref, xs_hbm.at[pl.ds(0, tt * rpt)], sem.at[0]).wait()

    lo = 0
    for xm_ref, nblk in zip(xm_refs, blocks):
        pl.when(jnp.logical_and(i >= lo, i < lo + nblk))(functools.partial(scatter, xm_ref))
        lo += nblk


def _dispatch(pad_start, pad_len, valid, pos, xm_streams, n_tiles, tm):
    tt = pos.shape[-1]
    rpt = TOKEN_ROWS
    blocks = tuple(xm.shape[0] // (rpt * tt) for xm in xm_streams)
    in_specs = [pl.BlockSpec((None, TOP_K, tt), lambda i, a, b, c: (i, 0, 0), memory_space=pltpu.SMEM)]
    lo = 0
    for nblk in blocks:
        in_specs.append(pl.BlockSpec(
            (tt * rpt, LANE), lambda i, a, b, c, lo=lo, nblk=nblk: (jnp.clip(i - lo, 0, nblk - 1), 0)))
        lo += nblk
    return pl.pallas_call(
        functools.partial(_dispatch_kernel, blocks=blocks, n_tiles=n_tiles),
        grid_spec=pltpu.PrefetchScalarGridSpec(
            num_scalar_prefetch=3,
            grid=(sum(blocks),),
            in_specs=in_specs,
            out_specs=pl.BlockSpec(memory_space=pl.ANY),
            scratch_shapes=[pltpu.VMEM((tm * rpt, LANE), F32), pltpu.SemaphoreType.DMA((2,))],
        ),
        out_shape=jax.ShapeDtypeStruct((n_tiles * tm * rpt, LANE), F32),
        compiler_params=_cparams(("arbitrary",)),
        name="moe_dispatch",
    )(pad_start, pad_len, valid, pos, *xm_streams)


def _moe_kernel(te_ref, tv_ref, x_ref, wg_hbm, wu_hbm, wd_hbm, y_ref,
                wg_f, wu_f, wd_f, wg_s, wu_s, wd_s, slot_ref, sem, *, layer, n_tiles):
    t = pl.program_id(0)
    rpt = TOKEN_ROWS
    tm = x_ref.shape[0] // rpt

    def weight_copies(e, slot):
        return [pltpu.make_async_copy(src.at[layer, e], dst.at[slot], sem.at[slot, k])
                for k, (src, dst) in enumerate(((wg_hbm, wg_f), (wu_hbm, wu_f), (wd_hbm, wd_f)))]

    @pl.when(tv_ref[t] == 1)
    def _():
        e = te_ref[t]
        first = jnp.logical_or(t == 0, te_ref[jnp.maximum(t - 1, 0)] != e)

        @pl.when(first)
        def _():
            at = lambda ref, i: ref[jnp.minimum(i, n_tiles - 1)]
            has = lambda i: (i < n_tiles) & (at(tv_ref, i) == 1)

            def next_start(t0, e0):
                same = lambda i: has(i) & (at(te_ref, i) == e0)
                return lax.while_loop(same, lambda i: i + 1, t0 + 1)

            t1 = next_start(t, e)
            e1 = at(te_ref, t1)
            t2 = next_start(t1, e1)

            @pl.when(t == 0)
            def _():
                slot_ref[0] = 0
                for cp in weight_copies(e, 0):
                    cp.start()

                @pl.when(has(t1))
                def _():
                    for cp in weight_copies(e1, 1):
                        cp.start()

            slot = slot_ref[0]
            for cp in weight_copies(e, slot):
                cp.wait()

            @pl.when(has(t1) & has(t2))
            def _():
                for cp in weight_copies(at(te_ref, t2), lax.rem(slot + 2, W_SLOTS)):
                    cp.start()

            wg_s[...] = wg_f[slot].astype(BF16)
            wu_s[...] = wu_f[slot].astype(BF16)
            wd_s[...] = wd_f[slot].astype(BF16)
            slot_ref[0] = lax.rem(slot + 1, W_SLOTS)

        x = jnp.concatenate([_load_token_tile_col(x_ref, tm, rpt, j) for j in range(rpt)], axis=1).astype(BF16)
        g = _dot(x, wg_s[...])
        u = _dot(x, wu_s[...])
        h = (g * _sigmoid(g) * u).astype(BF16)
        _store_token_tiles(y_ref, _dot(h, wd_s[...]))

    @pl.when(tv_ref[t] != 1)
    def _():
        y_ref[...] = jnp.zeros_like(y_ref)


def _moe_experts(xs, tile_e, valid, n_tiles, layer, w_gate, w_up, w_down, tm):
    _, _, d, de = w_gate.shape
    rpt = TOKEN_ROWS
    grid_spec = pltpu.PrefetchScalarGridSpec(
        num_scalar_prefetch=2,
        grid=(n_tiles,),
        in_specs=[
            pl.BlockSpec((tm * rpt, LANE), lambda t, te, tv: (t * tv[t], 0)),
            pl.BlockSpec(memory_space=pl.ANY),
            pl.BlockSpec(memory_space=pl.ANY),
            pl.BlockSpec(memory_space=pl.ANY),
        ],
        out_specs=pl.BlockSpec((tm * rpt, LANE), lambda t, te, tv: (t, 0)),
        scratch_shapes=[
            pltpu.VMEM((W_SLOTS, d, de), F32),
            pltpu.VMEM((W_SLOTS, d, de), F32),
            pltpu.VMEM((W_SLOTS, de, d), F32),
            pltpu.VMEM((d, de), BF16),
            pltpu.VMEM((d, de), BF16),
            pltpu.VMEM((de, d), BF16),
            pltpu.SMEM((1,), I32),
            pltpu.SemaphoreType.DMA((W_SLOTS, 3)),
        ],
    )
    return pl.pallas_call(
        functools.partial(_moe_kernel, layer=layer, n_tiles=n_tiles),
        grid_spec=grid_spec,
        out_shape=jax.ShapeDtypeStruct((n_tiles * tm * rpt, LANE), F32),
        compiler_params=_cparams(("arbitrary",)),
        name="moe_experts",
    )(tile_e, valid, xs, w_gate, w_up, w_down)


def _combine_kernel(pos0_ref, posn_ref, x_ref, g_ref, rt_ref, ys_hbm, o_ref, buf, sem, *, n):
    i = pl.program_id(0)
    tt, d = x_ref.shape
    rpt = TOKEN_ROWS

    def gather(pos_ref, slot):
        def issue(r, carry):
            for k in range(TOP_K):
                src = ys_hbm.at[pl.ds(pl.multiple_of(pos_ref[k, r] * rpt, rpt), rpt)]
                dst = buf.at[slot, k, pl.ds(pl.multiple_of(r * rpt, rpt), rpt)]
                pltpu.make_async_copy(src, dst, sem.at[slot, k]).start(priority=k)
            return carry
        lax.fori_loop(0, tt, issue, 0, unroll=8)

    slot = lax.rem(i, 2)

    @pl.when(i == 0)
    def _():
        gather(pos0_ref, 0)

    @pl.when(i + 1 < n)
    def _():
        gather(posn_ref, 1 - slot)

    for k in range(TOP_K):
        pltpu.make_async_copy(ys_hbm.at[pl.ds(0, tt * rpt)], buf.at[slot, k], sem.at[slot, k]).wait()
    w1 = rt_ref[:, 2:3]
    w2 = rt_ref[:, 3:4]
    for j in range(rpt):
        r1 = buf[slot, 0, pl.ds(j, tt, stride=rpt), :]
        r2 = buf[slot, 1, pl.ds(j, tt, stride=rpt), :]
        sl = slice(j * LANE, (j + 1) * LANE)
        o_ref[:, sl] = x_ref[:, sl] + g_ref[:, sl] * (w1 * r1 + w2 * r2)


def _moe_combine(x, g2, rt, pos, tok_off, ys, tt):
    b, l, d = x.shape
    rpt = TOKEN_ROWS
    n = b * l
    per_b = l // tt
    per_blk = PLAN_BLOCK // tt
    blk0 = tok_off // PLAN_BLOCK

    def pos_map(step):
        return lambda i: (blk0 + step(i) // per_blk, 0, step(i) % per_blk)

    return pl.pallas_call(
        functools.partial(_combine_kernel, n=n // tt),
        grid=(n // tt,),
        in_specs=[
            pl.BlockSpec((None, TOP_K, tt), pos_map(lambda i: i), memory_space=pltpu.SMEM),
            pl.BlockSpec((None, TOP_K, tt), pos_map(lambda i: jnp.minimum(i + 1, n // tt - 1)),
                         memory_space=pltpu.SMEM),
            pl.BlockSpec((tt, d), lambda i: (i, 0)),
            pl.BlockSpec((None, 1, d), lambda i: (i // per_b, 0, 0)),
            pl.BlockSpec((tt, LANE), lambda i: (i, 0)),
            pl.BlockSpec(memory_space=pl.ANY),
        ],
        out_specs=pl.BlockSpec((tt, d), lambda i: (i, 0)),
        out_shape=jax.ShapeDtypeStruct((n, d), F32),
        scratch_shapes=[
            pltpu.VMEM((2, TOP_K, tt * rpt, LANE), F32),
            pltpu.SemaphoreType.DMA((2, TOP_K)),
        ],
        compiler_params=_cparams(("arbitrary",)),
        name="moe_combine",
    )(pos, pos, x.reshape(n, d), g2, rt.reshape(n, LANE), ys).reshape(b, l, d)


def _hier_moe(streams, layer, w_gate, w_up, w_down):
    sizes = [s[0].shape[0] * s[0].shape[1] for s in streams]
    n = sum(sizes)
    tm = MOE_TILE
    n_tiles = (TOP_K * n) // tm + N_EXPERTS
    assert n_tiles <= MAX_TILES and all(sz % PLAN_BLOCK == 0 for sz in sizes)
    rt_all = jnp.concatenate([s[2].reshape(-1, LANE) for s in streams], axis=0)
    pos, tile_e, valid, pad = _moe_plan(rt_all, tm)
    tile_e, valid = tile_e.reshape(-1), valid.reshape(-1)
    xs = _dispatch(pad[:, 0], pad[:, 1], valid, pos, [s[1] for s in streams], n_tiles, tm)
    ys = _moe_experts(xs, tile_e, valid, n_tiles, layer, w_gate, w_up, w_down, tm)
    outs = []
    off = 0
    for (xl, _, rt, g2), sz in zip(streams, sizes):
        outs.append(_moe_combine(xl, g2, rt, pos, off, ys, min(COMBINE_TILE, xl.shape[1])))
        off += sz
    return outs


def _inproj_cd_kernel(x_ref, sh_ref, sc_ref, w_ref, u_ref, *out_refs):
    h = (_rms(x_ref[...]) * (1.0 + sc_ref[...]) + sh_ref[...]).astype(BF16)
    tm = x_ref.shape[0]
    n_slab, rows, cw = u_ref.shape
    nb = rows // tm
    bi = pl.program_id(1)
    u = _dot(h, w_ref[:, 0:n_slab * cw])
    for s in range(n_slab):
        u_ref[s, pl.ds(bi, tm, stride=nb), :] = u[:, s * cw:(s + 1) * cw]
    off = n_slab * cw
    for ref in out_refs:
        n = ref.shape[-1]
        ref[...] = _dot(h, w_ref[:, off:off + n]).astype(ref.dtype)
        off += n


def _inproj_cd(x, sh, sc, w_bf16, widths, tm):
    b, l, d = x.shape
    n = w_bf16.shape[1]
    n_slab = widths[0] // LANE
    tok = lambda j, i: (i, j, 0)
    return pl.pallas_call(
        _inproj_cd_kernel,
        grid=(l // tm, b),
        in_specs=[
            pl.BlockSpec((None, tm, d), tok),
            pl.BlockSpec((None, 1, d), lambda j, i: (i, 0, 0)),
            pl.BlockSpec((None, 1, d), lambda j, i: (i, 0, 0)),
            pl.BlockSpec((d, n), lambda j, i: (0, 0)),
        ],
        out_specs=([pl.BlockSpec((n_slab, tm * b, LANE), lambda j, i: (0, j, 0))]
                   + [pl.BlockSpec((None, tm, w), tok) for w in widths[1:]]),
        out_shape=([jax.ShapeDtypeStruct((n_slab, l * b, LANE), F32)]
                   + [jax.ShapeDtypeStruct((b, l, w), F32) for w in widths[1:]]),
        compiler_params=_cparams(("arbitrary", "arbitrary")),
        name="inproj_cd",
    )(x, sh, sc, w_bf16)


def _s5_param_kernel(are_ref, aim_ref, ldt_ref, bre_ref, bim_ref, abre_ref, abim_ref, bbre_ref, bbim_ref):
    a_re = are_ref[...]
    a_im = aim_ref[...]
    dt = jnp.exp(ldt_ref[...])
    mag = jnp.exp(dt * a_re)
    ab_re = mag * jnp.cos(dt * a_im)
    ab_im = mag * jnp.sin(dt * a_im)
    den = a_re * a_re + a_im * a_im
    nr = ab_re - 1.0
    f_re = (nr * a_re + ab_im * a_im) / den
    f_im = (ab_im * a_re - nr * a_im) / den
    abre_ref[...] = ab_re
    abim_ref[...] = ab_im
    b_re = bre_ref[...]
    b_im = bim_ref[...]
    bbre_ref[...] = f_re[None] * b_re - f_im[None] * b_im
    bbim_ref[...] = f_re[None] * b_im + f_im[None] * b_re


def _s5_params(a_re, a_im, log_dt, b_re, b_im):
    nd, g, p = a_re.shape
    c = b_re.shape[-1]
    rows = nd * g * p // LANE
    flat = lambda t: t.reshape(rows, LANE)
    chan_major = lambda t: jnp.moveaxis(t, -1, 0).reshape(c, rows, LANE)
    ldt = jnp.broadcast_to(log_dt[:, :, None], (nd, g, p))
    shapes = [jax.ShapeDtypeStruct((rows, LANE), F32)] * 2 + [jax.ShapeDtypeStruct((c, rows, LANE), F32)] * 2
    ab_re, ab_im, bb_re, bb_im = pl.pallas_call(
        _s5_param_kernel, out_shape=shapes, name="s5_params",
    )(flat(a_re), flat(a_im), flat(ldt), chan_major(b_re), chan_major(b_im))
    unflat = lambda t: jnp.moveaxis(t.reshape(c, nd, g, p), 0, -1)
    return ab_re.reshape(nd, g, p), ab_im.reshape(nd, g, p), unflat(bb_re), unflat(bb_im)


def _s5_kernel(ufc_ref, ubc_ref, ufl_ref, ubl_ref, a_ref, bsb_ref, csb_ref, d_ref, yf_ref, yb_ref,
               hf, hb, cf, cb, *, n_ctx):
    i = pl.program_id(0)
    nb = cf.shape[0]
    steps = hf.shape[0] // nb
    n_super = bsb_ref.shape[1]
    cw = bsb_ref.shape[2]
    sw = bsb_ref.shape[3]
    half = sw // 2

    @pl.when(i == 0)
    def _():
        cf[...] = jnp.zeros_like(cf)
        cb[...] = jnp.zeros_like(cb)

    def drive(uf_ref, ub_ref):
        for s in range(n_super):
            hf[:, s * sw:(s + 1) * sw] = _dot(uf_ref[s].astype(BF16), bsb_ref[0, s])
            hb[:, s * sw:(s + 1) * sw] = _dot(ub_ref[s].astype(BF16), bsb_ref[1, s])

    pl.when(i < n_ctx)(functools.partial(drive, ufc_ref, ubc_ref))
    pl.when(i >= n_ctx)(functools.partial(drive, ufl_ref, ubl_ref))

    for s in range(n_super):
        re = slice(s * sw, s * sw + half)
        im = slice(s * sw + half, (s + 1) * sw)
        st = slice(s * half, (s + 1) * half)
        arf, aif = a_ref[0, 0, :, st], a_ref[0, 1, :, st]
        arb, aib = a_ref[1, 0, :, st], a_ref[1, 1, :, st]

        def step(t, carry):
            hfr, hfi, hbr, hbi = carry
            rf = pl.multiple_of(t * nb, nb)
            nfr = arf * hfr - aif * hfi + hf[pl.ds(rf, nb), re]
            nfi = arf * hfi + aif * hfr + hf[pl.ds(rf, nb), im]
            hf[pl.ds(rf, nb), re] = nfr
            hf[pl.ds(rf, nb), im] = nfi
            rb = pl.multiple_of((steps - 1 - t) * nb, nb)
            nbr = arb * hbr - aib * hbi + hb[pl.ds(rb, nb), re]
            nbi = arb * hbi + aib * hbr + hb[pl.ds(rb, nb), im]
            hb[pl.ds(rb, nb), re] = nbr
            hb[pl.ds(rb, nb), im] = nbi
            return nfr, nfi, nbr, nbi

        out = lax.fori_loop(0, steps, step, (cf[:, re], cf[:, im], cb[:, re], cb[:, im]), unroll=4)
        cf[:, re], cf[:, im], cb[:, re], cb[:, im] = out

    @pl.when(i >= n_ctx)
    def _():
        for s in range(n_super):
            skip = ufl_ref[s] * d_ref[:, s * cw:(s + 1) * cw]
            yf_ref[s] = _dot(hf[:, s * sw:(s + 1) * sw].astype(BF16), csb_ref[0, s]) + skip
            yb_ref[s] = _dot(hb[:, s * sw:(s + 1) * sw].astype(BF16), csb_ref[1, s])


def _s5_scan(u_c, u_l, a_bc, bsb, csb, d_skip, nb):
    n_slab, rows_l, _ = u_l.shape
    blk = S5_STEPS * nb
    nc = u_c.shape[1] // blk
    nl = rows_l // blk
    n_state = bsb.shape[1] * bsb.shape[3]
    clip = jnp.clip
    ctx_f = lambda i: (0, clip(i, 0, nc - 1), 0)
    ctx_b = lambda i: (0, clip(nc - 1 - i, 0, nc - 1), 0)
    lat_f = lambda i: (0, clip(i - nc, 0, nl - 1), 0)
    lat_b = lambda i: (0, clip(nl - 1 - (i - nc), 0, nl - 1), 0)
    whole = lambda a: pl.BlockSpec(a.shape, lambda i: (0,) * a.ndim)
    slab = lambda index_map: pl.BlockSpec((n_slab, blk, LANE), index_map)
    return pl.pallas_call(
        functools.partial(_s5_kernel, n_ctx=nc),
        grid=(nc + nl,),
        in_specs=[slab(ctx_f), slab(ctx_b), slab(lat_f), slab(lat_b), whole(a_bc), whole(bsb), whole(csb),
                  whole(d_skip)],
        out_specs=[slab(lat_f), slab(lat_b)],
        out_shape=[jax.ShapeDtypeStruct(u_l.shape, F32)] * 2,
        scratch_shapes=[
            pltpu.VMEM((blk, n_state), F32),
            pltpu.VMEM((blk, n_state), F32),
            pltpu.VMEM((nb, n_state), F32),
            pltpu.VMEM((nb, n_state), F32),
        ],
        compiler_params=_cparams(("arbitrary",)),
        name="s5_scan",
    )(u_c, u_c, u_l, u_l, a_bc, bsb, csb, d_skip)


def _s5_block_matrices(ab_re, ab_im, bb_re, bb_im, c_re, c_im, nb):
    nd, g, p = ab_re.shape
    c = bb_re.shape[-1]
    ns = g // S5_SUPER
    eye = jnp.eye(S5_SUPER, dtype=F32)

    def in_mat(bb):
        t = bb.reshape(nd, ns, S5_SUPER, p, c)
        return jnp.einsum('dsgpc,gh->dsgchp', t, eye).reshape(nd, ns, S5_SUPER * c, S5_SUPER * p)

    def out_mat(cc):
        t = cc.reshape(nd, ns, S5_SUPER, c, p)
        return jnp.einsum('dsgcp,gh->dsgphc', t, eye).reshape(nd, ns, S5_SUPER * p, S5_SUPER * c)

    bsb = jnp.concatenate([in_mat(bb_re), in_mat(bb_im)], axis=-1).astype(BF16)
    csb = jnp.concatenate([out_mat(c_re), out_mat(-c_im)], axis=-2).astype(BF16)
    a_bc = jnp.stack([ab_re.reshape(nd, g * p), ab_im.reshape(nd, g * p)], axis=1)
    a_bc = jnp.broadcast_to(a_bc[:, :, None, :], (nd, 2, nb, g * p))
    return a_bc, bsb, csb


def _head_scale(x):
    return lax.rsqrt(jnp.sum(x * x, axis=-1, keepdims=True) * (1.0 / QK_HEAD) + EPS)


def _mla_prep_lat_kernel(cq_ref, ckv_ref, krp_ref, krs_ref, qn_ref, kvn_ref, wq_ref, wqs_ref, wk_ref, wv_ref,
                         qc_ref, qs_ref, kc_ref, ks_ref, q_ref, k_ref, v_ref):
    ckvn = (_rms(ckv_ref[...]) * kvn_ref[...]).astype(BF16)
    v_ref[...] = _dot(ckvn, wv_ref[...]).astype(v_ref.dtype)
    kn = _dot(ckvn, wk_ref[...])
    krp = krp_ref[...]
    k_swap = krs_ref[...] * ks_ref[...]
    kc = kc_ref[...]
    for h in range(N_HEADS):
        sl = slice(h * HEAD_PAD, (h + 1) * HEAD_PAD)
        kh = kn[:, sl] + krp
        k_ref[:, sl] = (_head_scale(kh) * (kh * kc + k_swap)).astype(k_ref.dtype)
    cqn = (_rms(cq_ref[...]) * qn_ref[...]).astype(BF16)
    qn = _dot(cqn, wq_ref[...])
    q_swap = _dot(cqn, wqs_ref[...])
    qc = qc_ref[...]
    qs = qs_ref[...]
    for h in range(N_HEADS):
        sl = slice(h * HEAD_PAD, (h + 1) * HEAD_PAD)
        qh = qn[:, sl]
        scale = _head_scale(qh) * (QK_HEAD ** -0.5 * LOG2_E)
        q_ref[:, sl] = (scale * (qh * qc + q_swap[:, sl] * qs)).astype(q_ref.dtype)


def _mla_prep_ctx_kernel(ckv_ref, krp_ref, kvn_ref, wk_ref, wv_ref, kg_ref, k_ref, v_ref):
    ckvn = (_rms(ckv_ref[...]) * kvn_ref[...]).astype(BF16)
    v_ref[...] = _dot(ckvn, wv_ref[...]).astype(v_ref.dtype)
    kn = _dot(ckvn, wk_ref[...])
    krp = krp_ref[...]
    kg = kg_ref[...]
    for h in range(N_HEADS):
        sl = slice(h * HEAD_PAD, (h + 1) * HEAD_PAD)
        kh = kn[:, sl] + krp
        k_ref[:, sl] = (_head_scale(kh) * (kh * kg)).astype(k_ref.dtype)


def _mla_prep(kernel_fn, name, tok_args, const_args, table_args, n_q_out, tm):
    b, l, _ = tok_args[0].shape
    tok = lambda i, j: (i, j, 0)
    specs = [pl.BlockSpec((None, tm, a.shape[-1]), tok) for a in tok_args]
    specs += [pl.BlockSpec(a.shape, lambda i, j: (0, 0)) for a in const_args]
    specs += [pl.BlockSpec((tm, HEAD_PAD), lambda i, j: (j, 0)) for _ in table_args]
    hk = N_HEADS * HEAD_PAD
    hv = N_HEADS * V_HEAD
    widths = [hk] * n_q_out + [hk, hv]
    return pl.pallas_call(
        kernel_fn,
        grid=(b, l // tm),
        in_specs=specs,
        out_specs=[pl.BlockSpec((None, tm, w), tok) for w in widths],
        out_shape=[jax.ShapeDtypeStruct((b, l, w), BF16) for w in widths],
        compiler_params=_cparams(("arbitrary", "arbitrary")),
        name=name,
    )(*tok_args, *const_args, *table_args)


def _attn_kernel(q_ref, kl_ref, kc_ref, vl_ref, vc_ref, o_ref, *, heads):
    nt = (((1,), (1,)), ((), ()))
    for pr in range(heads // 2):
        vsl = slice(pr * 2 * V_HEAD, (pr + 1) * 2 * V_HEAD)
        outs = []
        for hh in range(2):
            h = 2 * pr + hh
            sl = slice(h * HEAD_PAD, (h + 1) * HEAD_PAD)
            q = q_ref[:, sl]
            s_l = lax.dot_general(q, kl_ref[:, sl], nt, preferred_element_type=F32)
            s_c = lax.dot_general(q, kc_ref[:, sl], nt, preferred_element_type=F32)
            m = jnp.maximum(jnp.max(s_l, axis=-1, keepdims=True), jnp.max(s_c, axis=-1, keepdims=True))
            p_l = jnp.exp2(s_l - m)
            p_c = jnp.exp2(s_c - m)
            den = jnp.sum(p_l, axis=-1, keepdims=True) + jnp.sum(p_c, axis=-1, keepdims=True)
            o = _dot(p_l.astype(BF16), vl_ref[:, vsl]) + _dot(p_c.astype(BF16), vc_ref[:, vsl])
            outs.append(o / den)
        lane = lax.broadcasted_iota(I32, outs[0].shape, 1)
        o_ref[:, vsl] = jnp.where(lane < V_HEAD, outs[0], outs[1]).astype(o_ref.dtype)


def _attention(q, k_l, k_c, v_l, v_c, tq, heads):
    b, l, _ = q.shape
    lc = k_c.shape[1]
    hp = N_HEADS // heads
    kw = heads * HEAD_PAD
    vw = heads * V_HEAD
    return pl.pallas_call(
        functools.partial(_attn_kernel, heads=heads),
        grid=(b, hp, l // tq),
        in_specs=[
            pl.BlockSpec((None, tq, kw), lambda i, h, j: (i, j, h)),
            pl.BlockSpec((None, l, kw), lambda i, h, j: (i, 0, h)),
            pl.BlockSpec((None, lc, kw), lambda i, h, j: (i, 0, h)),
            pl.BlockSpec((None, l, vw), lambda i, h, j: (i, 0, h)),
            pl.BlockSpec((None, lc, vw), lambda i, h, j: (i, 0, h)),
        ],
        out_specs=pl.BlockSpec((None, tq, vw), lambda i, h, j: (i, j, h)),
        out_shape=jax.ShapeDtypeStruct((b, l, N_HEADS * V_HEAD), BF16),
        compiler_params=_cparams(("arbitrary", "arbitrary", "arbitrary")),
        name="mla_attention",
    )(q, k_l, k_c, v_l, v_c)


def _mix_cd_kernel(yf_ref, yb_ref, o_ref, x_ref, g1_ref, sh2_ref, sc2_ref,
                   wglu_ref, wout_ref, wr_ref, br_ref, xl_ref, xm_ref, rt_ref):
    tm = x_ref.shape[0]
    n_slab, rows, _ = yf_ref.shape
    nb = rows // tm
    bi = pl.program_id(1)
    dc = n_slab * LANE
    pick = lambda ref, s: ref[s, pl.ds(bi, tm, stride=nb), :]
    y = jnp.concatenate([pick(yf_ref, s) + pick(yb_ref, s) for s in range(n_slab)], axis=1)
    g = jax.nn.gelu(y)
    s5 = (g * _sigmoid(_dot(g.astype(BF16), wglu_ref[...]))).astype(BF16)
    y_mix = _dot(s5, wout_ref[0:dc, :]) + _dot(o_ref[...], wout_ref[dc:, :])
    _residual_and_route(x_ref[...], y_mix, g1_ref[...], sh2_ref[...], sc2_ref[...], wr_ref, br_ref,
                        xl_ref, xm_ref, rt_ref)


def _mix_cd(yf, yb, o, x, g1, sh2, sc2, wglu, wout, wr, br, tm):
    b, l, d = x.shape
    n_slab = yf.shape[0]
    tok = lambda j, i: (i, j, 0)
    per_b = lambda j, i: (i, 0, 0)
    const2 = lambda j, i: (0, 0)
    tm_blk = pl.BlockSpec((n_slab, tm * b, LANE), lambda j, i: (0, j, 0))
    return pl.pallas_call(
        _mix_cd_kernel,
        grid=(l // tm, b),
        in_specs=[
            tm_blk,
            tm_blk,
            pl.BlockSpec((None, tm, o.shape[-1]), tok),
            pl.BlockSpec((None, tm, d), tok),
            pl.BlockSpec((None, 1, d), per_b),
            pl.BlockSpec((None, 1, d), per_b),
            pl.BlockSpec((None, 1, d), per_b),
            pl.BlockSpec(wglu.shape, const2),
            pl.BlockSpec(wout.shape, const2),
            pl.BlockSpec(wr.shape, const2),
            pl.BlockSpec((1, LANE), const2),
        ],
        out_specs=[
            pl.BlockSpec((None, tm, d), tok),
            pl.BlockSpec((tm * TOKEN_ROWS, LANE), lambda j, i: (i * (l // tm) + j, 0)),
            pl.BlockSpec((None, tm, LANE), tok),
        ],
        out_shape=[
            jax.ShapeDtypeStruct((b, l, d), F32),
            jax.ShapeDtypeStruct((b * l * TOKEN_ROWS, LANE), F32),
            jax.ShapeDtypeStruct((b, l, LANE), F32),
        ],
        compiler_params=_cparams(("arbitrary", "arbitrary")),
        name="mix_cd",
    )(yf, yb, o, x, g1, sh2, sc2, wglu, wout, wr, br)


def _router_weights(w_grp, b_grp, w_exp, b_exp):
    d = w_grp.shape[0]
    pad = LANE - N_GROUPS - N_EXPERTS
    wr = jnp.concatenate([w_grp, w_exp, jnp.zeros((d, pad), F32)], axis=1)
    br = jnp.concatenate([b_grp, b_exp, jnp.zeros((pad,), F32)]).reshape(1, LANE)
    return wr, br


def _pad_heads(w, width):
    k = w.shape[0]
    w = w.reshape(k, N_HEADS, width)
    return jnp.pad(w, ((0, 0), (0, 0), (0, HEAD_PAD - width))).reshape(k, N_HEADS * HEAD_PAD)


def _rope_tables(l):
    rows = l // GRID_W
    row = jnp.repeat(jnp.arange(rows, dtype=F32), GRID_W)
    col = jnp.tile(jnp.arange(GRID_W, dtype=F32), rows)
    inv_freq = ROPE_BASE ** (-jnp.arange(AXIS_PAIRS, dtype=F32) / AXIS_PAIRS)
    ar, ac = row[:, None] * inv_freq, col[:, None] * inv_freq
    one = jnp.ones((l, QK_NOPE), F32)
    zn = jnp.zeros((l, QK_NOPE), F32)
    zp = jnp.zeros((l, HEAD_PAD - QK_HEAD), F32)
    cos = jnp.concatenate([one, jnp.cos(ar), jnp.cos(ar), jnp.cos(ac), jnp.cos(ac), zp], axis=1)
    sin = jnp.concatenate([zn, jnp.sin(ar), jnp.sin(ar), jnp.sin(ac), jnp.sin(ac), zp], axis=1)
    return cos, sin


def _rot_swap(t, signed):
    a = AXIS_PAIRS
    s = -1.0 if signed else 1.0
    return jnp.concatenate([s * t[..., a:2 * a], t[..., 0:a], s * t[..., 3 * a:4 * a], t[..., 2 * a:3 * a]], axis=-1)


def _head_lanes(rot):
    pad = [(0, 0)] * (rot.ndim - 1) + [(QK_NOPE, HEAD_PAD - QK_HEAD)]
    return jnp.pad(rot, pad)


def _split_mod(mod_row, b, d):
    parts = jnp.split(mod_row, 6, axis=-1)
    return [jnp.broadcast_to(p.reshape(-1, 1, d), (b, 1, d)) for p in parts]


def kernel(x, c, ctx, c_ctx, ada_w, ada_b, ab_w_in, sgu_norm, sgu_w, sgu_b, conv_w, ab_w_out, cd_w_in, s5_a_re, s5_a_im, s5_log_dt, s5_b_re, s5_b_im, s5_c_re, s5_c_im, s5_d, s5_w_glu, mla_q_norm, mla_kv_norm, mla_w_uq, mla_w_uk, mla_w_uv, mla_qn_gain, mla_kn_gain, cd_w_out, moe_w_grp, moe_b_grp, moe_w_exp, moe_b_exp, moe_w_gate, moe_w_up, moe_w_down):
    b, l, d = x.shape
    lc = ctx.shape[1]
    depth = ada_w.shape[0]
    assert depth == 2, "layer 0 = gated-MLP/conv mixers, layer 1 = S5/attention mixers"
    assert d == TOKEN_ROWS * LANE, "token-tile layout: one vreg tile per token"
    tm = min(TOKEN_TILE, l)
    tmc = min(TOKEN_TILE, lc)

    mod_rows = 2 * SUBLANE
    cvec = jnp.zeros((mod_rows, d), F32).at[:b].set(c).at[b].set(c_ctx)
    mod = _ada(cvec, ada_w, ada_b)

    sh1, sc1, g1, sh2, sc2, g2 = _split_mod(mod[0, :b], b, d)
    csh1, csc1, cg1, csh2, csc2, cg2 = _split_mod(mod[0, b:b + 1], b, d)
    w_in = ab_w_in[0].astype(BF16)
    wr, br = _router_weights(moe_w_grp[0], moe_b_grp[0], moe_w_exp[0], moe_b_exp[0])
    sgub = jnp.repeat(sgu_b[0].T, d // G_A, axis=1)
    mix_args = (sgu_norm[0].reshape(1, d), sgu_w[0].astype(BF16), sgub, conv_w[0],
                ab_w_out[0].astype(BF16), wr, br)
    z, p = _inproj_ab(x, sh1, sc1, w_in, tm)
    xl, xm, rt = _mix_ab(z, p, x, g1, sh2, sc2, *mix_args, tm)
    zc, pc = _inproj_ab(ctx, csh1, csc1, w_in, tmc)
    xc, xmc, rtc = _mix_ab(zc, pc, ctx, cg1, csh2, csc2, *mix_args, tmc)
    xl, xc = _hier_moe([(xl, xm, rt, g2), (xc, xmc, rtc, cg2)], 0, moe_w_gate, moe_w_up, moe_w_down)

    sh1, sc1, g1, sh2, sc2, g2 = _split_mod(mod[1, :b], b, d)
    csh1, csc1, _, _, _, _ = _split_mod(mod[1, b:b + 1], b, d)
    d_c = s5_d.shape[-1]
    q_lora = mla_q_norm.shape[-1]
    kv_lora = mla_kv_norm.shape[-1]
    w_cd = cd_w_in[0]
    o_kr = d_c + q_lora + kv_lora
    w_kr = _head_lanes(w_cd[:, o_kr:])
    w_krs = _head_lanes(_rot_swap(w_cd[:, o_kr:], True))
    w_cd = jnp.concatenate([w_cd[:, :o_kr], w_kr, w_krs], axis=1).astype(BF16)
    widths = (d_c, q_lora, kv_lora, HEAD_PAD, HEAD_PAD)
    u_l, cq_l, ckv_l, krp_l, krs_l = _inproj_cd(xl, sh1, sc1, w_cd, widths, tm)
    u_c, _, ckv_c, krp_c, _ = _inproj_cd(xc, csh1, csc1, w_cd, widths, tmc)

    ab_re, ab_im, bb_re, bb_im = _s5_params(s5_a_re[0], s5_a_im[0], s5_log_dt[0], s5_b_re[0], s5_b_im[0])
    a_bc, bsb, csb = _s5_block_matrices(ab_re, ab_im, bb_re, bb_im, s5_c_re[0], s5_c_im[0], b)
    yf, yb = _s5_scan(u_c, u_l, a_bc, bsb, csb, s5_d[0].reshape(1, d_c), b)

    wq = _pad_heads(mla_w_uq[0], QK_HEAD).astype(BF16)
    wk = _pad_heads(mla_w_uk[0], QK_NOPE).astype(BF16)
    wv = mla_w_uv[0].astype(BF16)
    w_uq = mla_w_uq[0].reshape(q_lora, N_HEADS, QK_HEAD)
    wqs = _head_lanes(_rot_swap(w_uq[..., QK_NOPE:], True)).reshape(q_lora, N_HEADS * HEAD_PAD).astype(BF16)
    qg, kg = mla_qn_gain[0], mla_kn_gain[0]
    full = lambda g: jnp.pad(g, (0, HEAD_PAD - QK_HEAD)).reshape(1, HEAD_PAD)
    partner = lambda g: _head_lanes(_rot_swap(g[QK_NOPE:], False)).reshape(1, HEAD_PAD)
    cos, sin = _rope_tables(l)
    tables = (cos * full(qg), sin * partner(qg), cos * full(kg), sin * partner(kg))
    qn = mla_q_norm[0].reshape(1, q_lora)
    kvn = mla_kv_norm[0].reshape(1, kv_lora)
    q_l, k_l, v_l = _mla_prep(_mla_prep_lat_kernel, "mla_prep_lat", (cq_l, ckv_l, krp_l, krs_l),
                              (qn, kvn, wq, wqs, wk, wv), tables, 1, tm)
    k_c, v_c = _mla_prep(_mla_prep_ctx_kernel, "mla_prep_ctx", (ckv_c, krp_c), (kvn, wk, wv, full(kg)), (), 0, tmc)
    o_l = _attention(q_l, k_l, k_c, v_l, v_c, ATTN_Q_TILE, ATTN_HEADS)

    wr, br = _router_weights(moe_w_grp[1], moe_b_grp[1], moe_w_exp[1], moe_b_exp[1])
    xl, xm, rt = _mix_cd(yf, yb, o_l, xl, g1, sh2, sc2,
                         s5_w_glu[0].astype(BF16), cd_w_out[0].astype(BF16), wr, br, MIX_CD_TILE)
    (xl,) = _hier_moe([(xl, xm, rt, g2)], 1, moe_w_gate, moe_w_up, moe_w_down)
    return xl
```
